```python
import jax, jax.numpy as jnp
from jax import lax
import numpy as np

D_MODEL = 1024
BATCH = 8
SEQ = 8192
DEPTH = 2

N_MEM = 256
D_MIX = D_MODEL
POOL_WIDTH = D_MIX // 2
POOL_WINDOWS = (2, 4, 8, 16)
POOL_GROUPS = len(POOL_WINDOWS)
POOL_GROUP_DIM = POOL_WIDTH // POOL_GROUPS
SGU_WIDTH = D_MIX - POOL_WIDTH
SGU_HEADS = 4
SGU_HEAD_DIM = SGU_WIDTH // SGU_HEADS
CHUNK = 128
D_IN_PROJ = POOL_WIDTH + 2 * SGU_WIDTH
XATTN_HEADS = 4
XATTN_HEAD_DIM = D_MODEL // XATTN_HEADS
D_FF = 2816
CONV_WIDTH = 3
EPS = 1e-6

kernel_name = "hybrid_pool_sgu_memxattn_convffn"


def rmsnorm(x, g):
    xf = x.astype(jnp.float32)
    y = xf * lax.rsqrt(jnp.mean(xf * xf, axis=-1, keepdims=True) + EPS)
    return (y * g.astype(jnp.float32)).astype(x.dtype)


def layernorm_nobias(x, g):
    xf = x.astype(jnp.float32)
    mu = jnp.mean(xf, axis=-1, keepdims=True)
    xc = xf - mu
    y = xc * lax.rsqrt(jnp.mean(xc * xc, axis=-1, keepdims=True) + EPS)
    return (y * g.astype(jnp.float32)).astype(x.dtype)


def pool_mixer(p, pool_w, pool_scale):
    B, S, _ = p.shape
    pf = p.astype(jnp.float32)
    c = jnp.pad(jnp.cumsum(pf, axis=1), ((0, 0), (1, 0), (0, 0)))
    t = jnp.arange(S)
    diffs = []
    for gi, win in enumerate(POOL_WINDOWS):
        sl = slice(gi * POOL_GROUP_DIM, (gi + 1) * POOL_GROUP_DIM)
        cg = c[..., sl]
        prev = jnp.pad(cg, ((0, 0), (win - 1, 0), (0, 0)))[:, :S]
        count = jnp.minimum(t + 1, win).astype(jnp.float32)[None, :, None]
        diffs.append((cg[:, 1:] - prev) / count - pf[..., sl])
    d = jnp.stack(diffs, axis=2).astype(p.dtype)
    y = jnp.einsum('bsgc,gcd->bsgd', d, pool_w).reshape(B, S, POOL_WIDTH)
    return y * pool_scale


def sgu_mixer(u, v, sgu_g, sgu_w, sgu_b):
    B, S, _ = u.shape
    vn = layernorm_nobias(v, sgu_g)
    vc = vn.reshape(B, S // CHUNK, CHUNK, SGU_HEADS, SGU_HEAD_DIM)
    mask = jnp.tril(jnp.ones((CHUNK, CHUNK), dtype=bool))
    w_masked = jnp.where(mask[None], sgu_w, jnp.zeros_like(sgu_w))
    z = jnp.einsum('hts,bnshd->bnthd', w_masked, vc) + sgu_b.T[:, :, None]
    return u * z.reshape(B, S, SGU_WIDTH)


def mem_cross_attention(xn, mem, mem_g, wq, wk, wv, wo):
    B, S, _ = xn.shape
    memn = rmsnorm(mem, mem_g)
    q = (xn @ wq).reshape(B, S, XATTN_HEADS, XATTN_HEAD_DIM)
    k = (memn @ wk).reshape(B, N_MEM, XATTN_HEADS, XATTN_HEAD_DIM)
    v = (memn @ wv).reshape(B, N_MEM, XATTN_HEADS, XATTN_HEAD_DIM)
    s = jnp.einsum('bshd,bmhd->bhsm', q, k).astype(jnp.float32) * (XATTN_HEAD_DIM ** -0.5)
    pr = jax.nn.softmax(s, axis=-1).astype(v.dtype)
    o = jnp.einsum('bhsm,bmhd->bshd', pr, v).reshape(B, S, D_MODEL)
    return o @ wo


def conv_ffn(xn, w_up, conv_w, conv_b, w_down):
    S = xn.shape[1]
    h = xn @ w_up
    hp = jnp.pad(h, ((0, 0), (CONV_WIDTH - 1, 0), (0, 0)))
    hc = conv_b + sum(conv_w[k] * hp[:, k:k + S] for k in range(CONV_WIDTH))
    gate, val = jnp.split(hc, 2, axis=-1)
    return (jax.nn.silu(gate) * val) @ w_down


def _fwd_setup_inputs(seed: int = 0) -> dict:
    key = jax.random.key(seed)
    ks = jax.random.split(key, 24)
    f32 = jnp.float32
    n = lambda k, shape, s: (jax.random.normal(k, shape, f32) * s)
    gain = lambda k, shape: 1.0 + 0.05 * jax.random.normal(k, shape, f32)
    L = DEPTH
    return {
        "x": jax.random.normal(ks[0], (BATCH, SEQ, D_MODEL), f32),
        "mem": jax.random.normal(ks[1], (BATCH, N_MEM, D_MODEL), f32),
        "norm_mix_g": gain(ks[2], (L, D_MODEL)),
        "w_in": n(ks[3], (L, D_MODEL, D_IN_PROJ), D_MODEL ** -0.5),
        "pool_w": n(ks[4], (L, POOL_GROUPS, POOL_GROUP_DIM, POOL_GROUP_DIM), POOL_GROUP_DIM ** -0.5),
        "pool_scale": 1.0 + 0.1 * jax.random.normal(ks[5], (L, POOL_WIDTH), f32),
        "sgu_g": gain(ks[6], (L, SGU_WIDTH)),
        "sgu_w": n(ks[7], (L, SGU_HEADS, CHUNK, CHUNK), CHUNK ** -0.5),
        "sgu_b": 1.0 + 0.05 * jax.random.normal(ks[8], (L, SGU_HEADS, CHUNK), f32),
        "w_out": n(ks[9], (L, D_MIX, D_MODEL), D_MIX ** -0.5),
        "norm_xattn_g": gain(ks[10], (L, D_MODEL)),
        "mem_norm_g": gain(ks[11], (L, D_MODEL)),
        "wq": n(ks[12], (L, D_MODEL, D_MODEL), D_MODEL ** -0.5),
        "wk": n(ks[13], (L, D_MODEL, D_MODEL), D_MODEL ** -0.5),
        "wv": n(ks[14], (L, D_MODEL, D_MODEL), D_MODEL ** -0.5),
        "wo": n(ks[15], (L, D_MODEL, D_MODEL), D_MODEL ** -0.5),
        "norm_ffn_g": gain(ks[16], (L, D_MODEL)),
        "w_up": n(ks[17], (L, D_MODEL, 2 * D_FF), D_MODEL ** -0.5),
        "conv_w": n(ks[18], (L, CONV_WIDTH, 2 * D_FF), CONV_WIDTH ** -0.5),
        "conv_b": n(ks[19], (L, 2 * D_FF), 0.02),
        "w_down": n(ks[20], (L, D_FF, D_MODEL), D_FF ** -0.5),
        "final_norm_g": gain(ks[21], (D_MODEL,)),
    }


def _fwd_reference(x, mem, norm_mix_g, w_in, pool_w, pool_scale, sgu_g, sgu_w, sgu_b, w_out,
              norm_xattn_g, mem_norm_g, wq, wk, wv, wo,
              norm_ffn_g, w_up, conv_w, conv_b, w_down, final_norm_g):
    h = x
    for l in range(DEPTH):
        xn = rmsnorm(h, norm_mix_g[l])
        proj = xn @ w_in[l]
        p = proj[..., :POOL_WIDTH]
        uv = jax.nn.gelu(proj[..., POOL_WIDTH:], approximate=False)
        u, v = uv[..., :SGU_WIDTH], uv[..., SGU_WIDTH:]
        y_pool = pool_mixer(p, pool_w[l], pool_scale[l])
        y_sgu = sgu_mixer(u, v, sgu_g[l], sgu_w[l], sgu_b[l])
        h = h + jnp.concatenate([y_pool, y_sgu], axis=-1) @ w_out[l]
        xn = rmsnorm(h, norm_xattn_g[l])
        h = h + mem_cross_attention(xn, mem, mem_norm_g[l], wq[l], wk[l], wv[l], wo[l])
        xn = rmsnorm(h, norm_ffn_g[l])
        h = h + conv_ffn(xn, w_up[l], conv_w[l], conv_b[l], w_down[l])
    return rmsnorm(h, final_norm_g)


import jax as _jax
import jax.numpy as _jnp

TWIN_FORMAT = 'train_step'
FWD_PARAMS = ['x', 'mem', 'norm_mix_g', 'w_in', 'pool_w', 'pool_scale', 'sgu_g', 'sgu_w', 'sgu_b', 'w_out', 'norm_xattn_g', 'mem_norm_g', 'wq', 'wk', 'wv', 'wo', 'norm_ffn_g', 'w_up', 'conv_w', 'conv_b', 'w_down', 'final_norm_g']
TWIN_WEIGHTS = ['norm_mix_g', 'w_in', 'pool_w', 'pool_scale', 'sgu_g', 'sgu_w', 'sgu_b', 'w_out', 'norm_xattn_g', 'mem_norm_g', 'wq', 'wk', 'wv', 'wo', 'norm_ffn_g', 'w_up', 'conv_w', 'conv_b', 'w_down', 'final_norm_g']
TWIN_DIFF_INPUT = 'x'
TWIN_INPUTS = ['x', 'mem', 'norm_mix_g', 'w_in', 'pool_w', 'pool_scale', 'sgu_g', 'sgu_w', 'sgu_b', 'w_out', 'norm_xattn_g', 'mem_norm_g', 'wq', 'wk', 'wv', 'wo', 'norm_ffn_g', 'w_up', 'conv_w', 'conv_b', 'w_down', 'final_norm_g', 'loss_target', 'm_norm_mix_g', 'm_w_in', 'm_pool_w', 'm_pool_scale', 'm_sgu_g', 'm_sgu_w', 'm_sgu_b', 'm_w_out', 'm_norm_xattn_g', 'm_mem_norm_g', 'm_wq', 'm_wk', 'm_wv', 'm_wo', 'm_norm_ffn_g', 'm_w_up', 'm_conv_w', 'm_conv_b', 'm_w_down', 'm_final_norm_g', 'v_norm_mix_g', 'v_w_in', 'v_pool_w', 'v_pool_scale', 'v_sgu_g', 'v_sgu_w', 'v_sgu_b', 'v_w_out', 'v_norm_xattn_g', 'v_mem_norm_g', 'v_wq', 'v_wk', 'v_wv', 'v_wo', 'v_norm_ffn_g', 'v_w_up', 'v_conv_w', 'v_conv_b', 'v_w_down', 'v_final_norm_g']
TWIN_OUTPUTS = ['loss', 'grad_x', 'grad_norm_mix_g', 'grad_w_in', 'grad_pool_w', 'grad_pool_scale', 'grad_sgu_g', 'grad_sgu_w', 'grad_sgu_b', 'grad_w_out', 'grad_norm_xattn_g', 'grad_mem_norm_g', 'grad_wq', 'grad_wk', 'grad_wv', 'grad_wo', 'grad_norm_ffn_g', 'grad_w_up', 'grad_conv_w', 'grad_conv_b', 'grad_w_down', 'grad_final_norm_g', 'delta_norm_mix_g', 'delta_w_in', 'delta_pool_w', 'delta_pool_scale', 'delta_sgu_g', 'delta_sgu_w', 'delta_sgu_b', 'delta_w_out', 'delta_norm_xattn_g', 'delta_mem_norm_g', 'delta_wq', 'delta_wk', 'delta_wv', 'delta_wo', 'delta_norm_ffn_g', 'delta_w_up', 'delta_conv_w', 'delta_conv_b', 'delta_w_down', 'delta_final_norm_g', 'new_m_norm_mix_g', 'new_m_w_in', 'new_m_pool_w', 'new_m_pool_scale', 'new_m_sgu_g', 'new_m_sgu_w', 'new_m_sgu_b', 'new_m_w_out', 'new_m_norm_xattn_g', 'new_m_mem_norm_g', 'new_m_wq', 'new_m_wk', 'new_m_wv', 'new_m_wo', 'new_m_norm_ffn_g', 'new_m_w_up', 'new_m_conv_w', 'new_m_conv_b', 'new_m_w_down', 'new_m_final_norm_g', 'new_v_norm_mix_g', 'new_v_w_in', 'new_v_pool_w', 'new_v_pool_scale', 'new_v_sgu_g', 'new_v_sgu_w', 'new_v_sgu_b', 'new_v_w_out', 'new_v_norm_xattn_g', 'new_v_mem_norm_g', 'new_v_wq', 'new_v_wk', 'new_v_wv', 'new_v_wo', 'new_v_norm_ffn_g', 'new_v_w_up', 'new_v_conv_w', 'new_v_conv_b', 'new_v_w_down', 'new_v_final_norm_g']
TWIN_LEAF_KINDS = {'loss': 'loss', 'grad_x': 'grad_x', 'grad_norm_mix_g': 'grad_w', 'grad_w_in': 'grad_w', 'grad_pool_w': 'grad_w', 'grad_pool_scale': 'grad_w', 'grad_sgu_g': 'grad_w', 'grad_sgu_w': 'grad_w', 'grad_sgu_b': 'grad_w', 'grad_w_out': 'grad_w', 'grad_norm_xattn_g': 'grad_w', 'grad_mem_norm_g': 'grad_w', 'grad_wq': 'grad_w', 'grad_wk': 'grad_w', 'grad_wv': 'grad_w', 'grad_wo': 'grad_w', 'grad_norm_ffn_g': 'grad_w', 'grad_w_up': 'grad_w', 'grad_conv_w': 'grad_w', 'grad_conv_b': 'grad_w', 'grad_w_down': 'grad_w', 'grad_final_norm_g': 'grad_w', 'delta_norm_mix_g': 'delta_w', 'delta_w_in': 'delta_w', 'delta_pool_w': 'delta_w', 'delta_pool_scale': 'delta_w', 'delta_sgu_g': 'delta_w', 'delta_sgu_w': 'delta_w', 'delta_sgu_b': 'delta_w', 'delta_w_out': 'delta_w', 'delta_norm_xattn_g': 'delta_w', 'delta_mem_norm_g': 'delta_w', 'delta_wq': 'delta_w', 'delta_wk': 'delta_w', 'delta_wv': 'delta_w', 'delta_wo': 'delta_w', 'delta_norm_ffn_g': 'delta_w', 'delta_w_up': 'delta_w', 'delta_conv_w': 'delta_w', 'delta_conv_b': 'delta_w', 'delta_w_down': 'delta_w', 'delta_final_norm_g': 'delta_w', 'new_m_norm_mix_g': 'new_m', 'new_m_w_in': 'new_m', 'new_m_pool_w': 'new_m', 'new_m_pool_scale': 'new_m', 'new_m_sgu_g': 'new_m', 'new_m_sgu_w': 'new_m', 'new_m_sgu_b': 'new_m', 'new_m_w_out': 'new_m', 'new_m_norm_xattn_g': 'new_m', 'new_m_mem_norm_g': 'new_m', 'new_m_wq': 'new_m', 'new_m_wk': 'new_m', 'new_m_wv': 'new_m', 'new_m_wo': 'new_m', 'new_m_norm_ffn_g': 'new_m', 'new_m_w_up': 'new_m', 'new_m_conv_w': 'new_m', 'new_m_conv_b': 'new_m', 'new_m_w_down': 'new_m', 'new_m_final_norm_g': 'new_m', 'new_v_norm_mix_g': 'new_v', 'new_v_w_in': 'new_v', 'new_v_pool_w': 'new_v', 'new_v_pool_scale': 'new_v', 'new_v_sgu_g': 'new_v', 'new_v_sgu_w': 'new_v', 'new_v_sgu_b': 'new_v', 'new_v_w_out': 'new_v', 'new_v_norm_xattn_g': 'new_v', 'new_v_mem_norm_g': 'new_v', 'new_v_wq': 'new_v', 'new_v_wk': 'new_v', 'new_v_wv': 'new_v', 'new_v_wo': 'new_v', 'new_v_norm_ffn_g': 'new_v', 'new_v_w_up': 'new_v', 'new_v_conv_w': 'new_v', 'new_v_conv_b': 'new_v', 'new_v_w_down': 'new_v', 'new_v_final_norm_g': 'new_v'}


def _forward(args):
    return _fwd_reference(*[args[k] for k in FWD_PARAMS])


def _output_shape():
    out = _jax.eval_shape(lambda: _forward(_fwd_setup_inputs(0)))
    return out.shape, out.dtype

N_MICROBATCH = 1
ADAM_LR = 0.001
ADAM_B1 = 0.9
ADAM_B2 = 0.999
ADAM_EPS = 1e-08
ADAM_WD = 0.01
ADAM_STEP = 10
PER_EXAMPLE_BATCH_AXIS = {'x': 0, 'mem': 0, 'loss_target': 0}
SHARED_INPUTS = []
_WEIGHT_DTYPES = {'norm_mix_g': _jnp.float32, 'w_in': _jnp.float32, 'pool_w': _jnp.float32, 'pool_scale': _jnp.float32, 'sgu_g': _jnp.float32, 'sgu_w': _jnp.float32, 'sgu_b': _jnp.float32, 'w_out': _jnp.float32, 'norm_xattn_g': _jnp.float32, 'mem_norm_g': _jnp.float32, 'wq': _jnp.float32, 'wk': _jnp.float32, 'wv': _jnp.float32, 'wo': _jnp.float32, 'norm_ffn_g': _jnp.float32, 'w_up': _jnp.float32, 'conv_w': _jnp.float32, 'conv_b': _jnp.float32, 'w_down': _jnp.float32, 'final_norm_g': _jnp.float32}
MOMENT_SCALE = {'norm_mix_g': 2.036666e-01, 'w_in': 1.533935e-01, 'pool_w': 1.816773e-01, 'pool_scale': 1.837844e-01, 'sgu_g': 9.427709e-02, 'sgu_w': 9.087397e-02, 'sgu_b': 1.269970e-01, 'w_out': 1.788793e-01, 'norm_xattn_g': 2.068739e-02, 'mem_norm_g': 3.272175e-02, 'wq': 2.103559e-02, 'wk': 2.108768e-02, 'wv': 2.273615e-02, 'wo': 2.209776e-02, 'norm_ffn_g': 1.420869e-01, 'w_up': 6.286212e-02, 'conv_w': 6.390211e-02, 'conv_b': 6.854509e-02, 'w_down': 1.031312e-01, 'final_norm_g': 6.421224e+01}


def _to_microbatches(a, axis):
    t = _jnp.moveaxis(a, axis, 0)
    t = t.reshape((N_MICROBATCH, t.shape[0] // N_MICROBATCH) + t.shape[1:])
    return _jnp.moveaxis(t, 1, axis + 1)


def setup_inputs(seed: int = 0) -> dict:
    inp = _fwd_setup_inputs(seed)
    key = _jax.random.fold_in(_jax.random.key(seed), 7919)
    shape, _ = _output_shape()
    out = dict(inp)
    out["loss_target"] = _jax.random.normal(_jax.random.fold_in(key, 0), shape, _jnp.float32)
    for i, name in enumerate(TWIN_WEIGHTS):
        w = inp[name].astype(_jnp.float32)
        if MOMENT_SCALE is None:
            s = _jnp.sqrt(_jnp.mean(_jnp.square(w)) + 1e-30)
        else:
            s = MOMENT_SCALE[name]
        km, kv = _jax.random.split(_jax.random.fold_in(key, i + 1))
        out[name] = w
        out["m_" + name] = s * _jax.random.normal(km, w.shape, _jnp.float32)
        out["v_" + name] = (s * s) * _jax.random.uniform(kv, w.shape, _jnp.float32, 0.5, 1.5)
    if N_MICROBATCH > 1:
        for name, axis in PER_EXAMPLE_BATCH_AXIS.items():
            out[name] = _to_microbatches(out[name], axis)
    return {'x': out['x'], 'mem': out['mem'], 'norm_mix_g': out['norm_mix_g'], 'w_in': out['w_in'], 'pool_w': out['pool_w'], 'pool_scale': out['pool_scale'], 'sgu_g': out['sgu_g'], 'sgu_w': out['sgu_w'], 'sgu_b': out['sgu_b'], 'w_out': out['w_out'], 'norm_xattn_g': out['norm_xattn_g'], 'mem_norm_g': out['mem_norm_g'], 'wq': out['wq'], 'wk': out['wk'], 'wv': out['wv'], 'wo': out['wo'], 'norm_ffn_g': out['norm_ffn_g'], 'w_up': out['w_up'], 'conv_w': out['conv_w'], 'conv_b': out['conv_b'], 'w_down': out['w_down'], 'final_norm_g': out['final_norm_g'], 'loss_target': out['loss_target'], 'm_norm_mix_g': out['m_norm_mix_g'], 'm_w_in': out['m_w_in'], 'm_pool_w': out['m_pool_w'], 'm_pool_scale': out['m_pool_scale'], 'm_sgu_g': out['m_sgu_g'], 'm_sgu_w': out['m_sgu_w'], 'm_sgu_b': out['m_sgu_b'], 'm_w_out': out['m_w_out'], 'm_norm_xattn_g': out['m_norm_xattn_g'], 'm_mem_norm_g': out['m_mem_norm_g'], 'm_wq': out['m_wq'], 'm_wk': out['m_wk'], 'm_wv': out['m_wv'], 'm_wo': out['m_wo'], 'm_norm_ffn_g': out['m_norm_ffn_g'], 'm_w_up': out['m_w_up'], 'm_conv_w': out['m_conv_w'], 'm_conv_b': out['m_conv_b'], 'm_w_down': out['m_w_down'], 'm_final_norm_g': out['m_final_norm_g'], 'v_norm_mix_g': out['v_norm_mix_g'], 'v_w_in': out['v_w_in'], 'v_pool_w': out['v_pool_w'], 'v_pool_scale': out['v_pool_scale'], 'v_sgu_g': out['v_sgu_g'], 'v_sgu_w': out['v_sgu_w'], 'v_sgu_b': out['v_sgu_b'], 'v_w_out': out['v_w_out'], 'v_norm_xattn_g': out['v_norm_xattn_g'], 'v_mem_norm_g': out['v_mem_norm_g'], 'v_wq': out['v_wq'], 'v_wk': out['v_wk'], 'v_wv': out['v_wv'], 'v_wo': out['v_wo'], 'v_norm_ffn_g': out['v_norm_ffn_g'], 'v_w_up': out['v_w_up'], 'v_conv_w': out['v_conv_w'], 'v_conv_b': out['v_conv_b'], 'v_w_down': out['v_w_down'], 'v_final_norm_g': out['v_final_norm_g']}


def _loss(weights, diff, rest, loss_target):
    with _jax.named_scope("forward"):
        args = {**rest, TWIN_DIFF_INPUT: diff, **{k: w.astype(_WEIGHT_DTYPES[k]) for k, w in weights.items()}}
        y = _forward(args)
    with _jax.named_scope("loss_head"):
        err = _jnp.square(y.astype(_jnp.float32) - loss_target)
        return 0.5 * _jnp.sum(_jnp.mean(err, axis=-1)) if err.ndim else 0.5 * err


def _adamw(w, g, m, v):
    m = ADAM_B1 * m + (1.0 - ADAM_B1) * g
    v = ADAM_B2 * v + (1.0 - ADAM_B2) * _jnp.square(g)
    m_hat = m / (1.0 - ADAM_B1 ** ADAM_STEP)
    v_hat = v / (1.0 - ADAM_B2 ** ADAM_STEP)
    delta = -ADAM_LR * (m_hat / (_jnp.sqrt(v_hat) + ADAM_EPS) + ADAM_WD * w)
    return delta, m, v


def reference(x, mem, norm_mix_g, w_in, pool_w, pool_scale, sgu_g, sgu_w, sgu_b, w_out, norm_xattn_g, mem_norm_g, wq, wk, wv, wo, norm_ffn_g, w_up, conv_w, conv_b, w_down, final_norm_g, loss_target, m_norm_mix_g, m_w_in, m_pool_w, m_pool_scale, m_sgu_g, m_sgu_w, m_sgu_b, m_w_out, m_norm_xattn_g, m_mem_norm_g, m_wq, m_wk, m_wv, m_wo, m_norm_ffn_g, m_w_up, m_conv_w, m_conv_b, m_w_down, m_final_norm_g, v_norm_mix_g, v_w_in, v_pool_w, v_pool_scale, v_sgu_g, v_sgu_w, v_sgu_b, v_w_out, v_norm_xattn_g, v_mem_norm_g, v_wq, v_wk, v_wv, v_wo, v_norm_ffn_g, v_w_up, v_conv_w, v_conv_b, v_w_down, v_final_norm_g):
    given = dict(x=x, mem=mem, norm_mix_g=norm_mix_g, w_in=w_in, pool_w=pool_w, pool_scale=pool_scale, sgu_g=sgu_g, sgu_w=sgu_w, sgu_b=sgu_b, w_out=w_out, norm_xattn_g=norm_xattn_g, mem_norm_g=mem_norm_g, wq=wq, wk=wk, wv=wv, wo=wo, norm_ffn_g=norm_ffn_g, w_up=w_up, conv_w=conv_w, conv_b=conv_b, w_down=w_down, final_norm_g=final_norm_g, loss_target=loss_target, m_norm_mix_g=m_norm_mix_g, m_w_in=m_w_in, m_pool_w=m_pool_w, m_pool_scale=m_pool_scale, m_sgu_g=m_sgu_g, m_sgu_w=m_sgu_w, m_sgu_b=m_sgu_b, m_w_out=m_w_out, m_norm_xattn_g=m_norm_xattn_g, m_mem_norm_g=m_mem_norm_g, m_wq=m_wq, m_wk=m_wk, m_wv=m_wv, m_wo=m_wo, m_norm_ffn_g=m_norm_ffn_g, m_w_up=m_w_up, m_conv_w=m_conv_w, m_conv_b=m_conv_b, m_w_down=m_w_down, m_final_norm_g=m_final_norm_g, v_norm_mix_g=v_norm_mix_g, v_w_in=v_w_in, v_pool_w=v_pool_w, v_pool_scale=v_pool_scale, v_sgu_g=v_sgu_g, v_sgu_w=v_sgu_w, v_sgu_b=v_sgu_b, v_w_out=v_w_out, v_norm_xattn_g=v_norm_xattn_g, v_mem_norm_g=v_mem_norm_g, v_wq=v_wq, v_wk=v_wk, v_wv=v_wv, v_wo=v_wo, v_norm_ffn_g=v_norm_ffn_g, v_w_up=v_w_up, v_conv_w=v_conv_w, v_conv_b=v_conv_b, v_w_down=v_w_down, v_final_norm_g=v_final_norm_g)
    weights = {n: given[n] for n in TWIN_WEIGHTS}
    shared = {n: given[n] for n in SHARED_INPUTS}
    per_example = {n: given[n] for n in ['x', 'mem']}
    grad_fn = _jax.value_and_grad(_loss, argnums=(0, 1))

    def one_microbatch(ex, loss_target):
        ex = dict(ex)
        diff = ex.pop(TWIN_DIFF_INPUT)
        return grad_fn(weights, diff, {**shared, **ex}, loss_target)

    if N_MICROBATCH == 1:
        loss, (grad_w, grad_x) = one_microbatch(per_example, given["loss_target"])
    else:
        def body(carry, xs):
            loss_sum, grad_sum = carry
            l_k, (gw_k, gx_k) = one_microbatch(xs[0], xs[1])
            with _jax.named_scope("update"):
                return (loss_sum + l_k, _jax.tree.map(_jnp.add, grad_sum, gw_k)), gx_k

        init = (_jnp.zeros((), _jnp.float32), _jax.tree.map(_jnp.zeros_like, weights))
        (loss, grad_w), grad_x = _jax.lax.scan(body, init, (per_example, given["loss_target"]))
    with _jax.named_scope("update"):
        delta_w, new_m, new_v = {}, {}, {}
        for n in TWIN_WEIGHTS:
            delta_w[n], new_m[n], new_v[n] = _adamw(weights[n], grad_w[n], given["m_" + n], given["v_" + n])
    return (loss, grad_x, *[grad_w[n] for n in TWIN_WEIGHTS], *[delta_w[n] for n in TWIN_WEIGHTS],
            *[new_m[n] for n in TWIN_WEIGHTS], *[new_v[n] for n in TWIN_WEIGHTS])
```

```python
import functools
import math

import jax
import jax.numpy as jnp
from jax import lax
from jax.experimental import pallas as pl
from jax.experimental.pallas import tpu as pltpu

F32 = jnp.float32
BF16 = jnp.bfloat16
MESH = pl.DeviceIdType.MESH

EPS = 1e-6
N_DEV = 8
DEPTH = 2
POOL_WINDOWS = (2, 4, 8, 16)
GROUP = 128
POOL_WIDTH = 512
SGU_WIDTH = 512
HEADS = 4
HEAD_DIM = 256
POOL_HALO = 16
CONV_HALO = 8

ADAM_LR = 0.001
ADAM_B1 = 0.9
ADAM_B2 = 0.999
ADAM_EPS = 1e-08
ADAM_WD = 0.01
ADAM_STEP = 10

VMEM_LIMIT_BYTES = 52 * 1024 * 1024


def _params(*semantics):
    return pltpu.CompilerParams(dimension_semantics=semantics, vmem_limit_bytes=VMEM_LIMIT_BYTES)


def _tile(n, want):
    t = min(n, want)
    assert n % t == 0, (n, want)
    return t


_DOT_DIMS = {
    "nn": (((1,), (0,)), ((), ())),
    "nt": (((1,), (1,)), ((), ())),
    "tn": (((0,), (0,)), ((), ())),
}


def _mm(a, b, *, dims, grid, a_spec, b_spec, o_spec, out_shape, out_dtype, acc_shape, name, res=None, res_spec=None):
    nk = grid[2]
    dn = _DOT_DIMS[dims]
    has_res = res is not None

    def body(*refs):
        if has_res:
            a_ref, b_ref, r_ref, o_ref = refs[:4]
        else:
            a_ref, b_ref, o_ref = refs[:3]
            r_ref = None
        p = lax.dot_general(a_ref[...], b_ref[...], dn, preferred_element_type=F32)

        def finish(r):
            if r_ref is not None:
                r = r + r_ref[...]
            o_ref[...] = r.astype(o_ref.dtype)

        if nk == 1:
            finish(p)
        else:
            acc_ref = refs[-1]
            k = pl.program_id(2)

            @pl.when(k == 0)
            def _():
                acc_ref[...] = p

            @pl.when(k > 0)
            def _():
                acc_ref[...] += p

            @pl.when(k == nk - 1)
            def _():
                finish(acc_ref[...])

    in_specs = [a_spec, b_spec] + ([res_spec] if has_res else [])
    args = (a, b) + ((res,) if has_res else ())
    scratch = [pltpu.VMEM(acc_shape, F32)] if nk > 1 else []
    return pl.pallas_call(
        body, grid=grid, in_specs=in_specs, out_specs=o_spec,
        out_shape=jax.ShapeDtypeStruct(out_shape, out_dtype), scratch_shapes=scratch, name=name,
        compiler_params=_params("parallel", "parallel", "arbitrary"),
    )(*args)


def _mm_nn(a, b, *, out_dtype, name, res=None, tm=1024):
    m, k = a.shape
    n = b.shape[1]
    tm = _tile(m, tm)
    spec_o = pl.BlockSpec((tm, n), lambda i, j, kk: (i, 0))
    return _mm(a, b, dims="nn", grid=(m // tm, 1, 1),
               a_spec=pl.BlockSpec((tm, k), lambda i, j, kk: (i, 0)),
               b_spec=pl.BlockSpec((k, n), lambda i, j, kk: (0, 0)),
               o_spec=spec_o, out_shape=(m, n), out_dtype=out_dtype, acc_shape=None, name=name,
               res=res, res_spec=spec_o if res is not None else None)


def _mm_nt(a, b, *, out_dtype, name, res=None, tm=1024):
    m, k = a.shape
    n = b.shape[0]
    tm = _tile(m, tm)
    spec_o = pl.BlockSpec((tm, n), lambda i, j, kk: (i, 0))
    return _mm(a, b, dims="nt", grid=(m // tm, 1, 1),
               a_spec=pl.BlockSpec((tm, k), lambda i, j, kk: (i, 0)),
               b_spec=pl.BlockSpec((n, k), lambda i, j, kk: (0, 0)),
               o_spec=spec_o, out_shape=(m, n), out_dtype=out_dtype, acc_shape=None, name=name,
               res=res, res_spec=spec_o if res is not None else None)


def _mm_tn(a, b, *, name, tm=None, tn=None, ts=512, out_dtype=BF16):
    s, m = a.shape
    n = b.shape[1]
    tm = m if tm is None else tm
    tn = n if tn is None else tn
    ts = _tile(s, ts)
    return _mm(a, b, dims="tn", grid=(m // tm, n // tn, s // ts),
               a_spec=pl.BlockSpec((ts, tm), lambda i, j, kk: (kk, i)),
               b_spec=pl.BlockSpec((ts, tn), lambda i, j, kk: (kk, j)),
               o_spec=pl.BlockSpec((tm, tn), lambda i, j, kk: (i, j)),
               out_shape=(m, n), out_dtype=out_dtype, acc_shape=(tm, tn), name=name)


def _mm_up(xn, w_up, *, name, tm=512):
    s, d = xn.shape
    f = w_up.shape[1] // 2
    tm = _tile(s, tm)
    return _mm(xn, w_up, dims="nn", grid=(2, s // tm, 1),
               a_spec=pl.BlockSpec((tm, d), lambda j, i, kk: (i, 0)),
               b_spec=pl.BlockSpec((d, f), lambda j, i, kk: (0, j)),
               o_spec=pl.BlockSpec((None, tm, f), lambda j, i, kk: (j, i, 0)),
               out_shape=(2, s, f), out_dtype=F32, acc_shape=None, name=name)


def _mm_up_nt(dhh, w_up, *, name, tm=1024):
    _, s, f = dhh.shape
    d = w_up.shape[0]
    tm = _tile(s, tm)
    return _mm(dhh, w_up, dims="nt", grid=(s // tm, 1, 2),
               a_spec=pl.BlockSpec((None, tm, f), lambda i, j, kk: (kk, i, 0)),
               b_spec=pl.BlockSpec((d, f), lambda i, j, kk: (0, kk)),
               o_spec=pl.BlockSpec((tm, d), lambda i, j, kk: (i, 0)),
               out_shape=(s, d), out_dtype=F32, acc_shape=(tm, d), name=name)


def _mm_up_tn(xn, dhh, *, name, ts=512):
    s, d = xn.shape
    f = dhh.shape[2]
    tn = f // 2
    ts = _tile(s, ts)
    return _mm(xn, dhh, dims="tn", grid=(1, 4, s // ts),
               a_spec=pl.BlockSpec((ts, d), lambda i, j, kk: (kk, 0)),
               b_spec=pl.BlockSpec((None, ts, tn), lambda i, j, kk: (j // 2, kk, j % 2)),
               o_spec=pl.BlockSpec((d, tn), lambda i, j, kk: (0, j)),
               out_shape=(d, 2 * f), out_dtype=BF16, acc_shape=(d, tn), name=name)


def _rms_fwd(h, g, *, name, tr=512):
    s, d = h.shape
    tr = _tile(s, tr)

    def body(h_ref, g_ref, o_ref):
        x = h_ref[...]
        r = lax.rsqrt(jnp.mean(x * x, axis=-1, keepdims=True) + EPS)
        o_ref[...] = ((x * r) * g_ref[...]).astype(o_ref.dtype)

    row = pl.BlockSpec((tr, d), lambda i: (i, 0))
    return pl.pallas_call(
        body, grid=(s // tr,), in_specs=[row, pl.BlockSpec((1, d), lambda i: (0, 0))], out_specs=row,
        out_shape=jax.ShapeDtypeStruct((s, d), BF16), name=name, compiler_params=_params("parallel"),
    )(h, g.reshape(1, d))


def _rms_bwd(h, dxn, g, dres, *, name, tr=512):
    s, d = h.shape
    tr = _tile(s, tr)
    has_res = dres is not None

    def body(*refs):
        if has_res:
            h_ref, dxn_ref, g_ref, dres_ref, dh_ref, dhb_ref, dg_ref = refs
        else:
            h_ref, dxn_ref, g_ref, dh_ref, dhb_ref, dg_ref = refs
        i = pl.program_id(0)
        x = h_ref[...]
        dy = dxn_ref[...].astype(F32)
        r = lax.rsqrt(jnp.mean(x * x, axis=-1, keepdims=True) + EPS)
        a = dy * g_ref[...]
        m = jnp.mean(a * x, axis=-1, keepdims=True)
        dh = r * a - x * (r * r * r * m)
        if has_res:
            dh = dh + dres_ref[...]
        dh_ref[...] = dh
        dhb_ref[...] = dh.astype(BF16)
        part = jnp.sum(dy * (x * r), axis=0, keepdims=True)

        @pl.when(i == 0)
        def _():
            dg_ref[...] = part

        @pl.when(i > 0)
        def _():
            dg_ref[...] += part

    row = pl.BlockSpec((tr, d), lambda i: (i, 0))
    vec = pl.BlockSpec((1, d), lambda i: (0, 0))
    in_specs = [row, row, vec] + ([row] if has_res else [])
    args = (h, dxn, g.reshape(1, d)) + ((dres,) if has_res else ())
    return pl.pallas_call(
        body, grid=(s // tr,), in_specs=in_specs, out_specs=[row, row, vec],
        out_shape=[jax.ShapeDtypeStruct((s, d), F32), jax.ShapeDtypeStruct((s, d), BF16),
                   jax.ShapeDtypeStruct((1, d), F32)],
        name=name, compiler_params=_params("arbitrary"),
    )(*args)


def _loss_head(h, g, target, *, name, tr=512):
    s, d = h.shape
    tr = _tile(s, tr)
    nt = s // tr

    def body(h_ref, g_ref, t_ref, dh_ref, dhb_ref, dg_ref, loss_ref, sq_ref):
        i = pl.program_id(0)
        x = h_ref[...]
        gain = g_ref[...]
        r = lax.rsqrt(jnp.mean(x * x, axis=-1, keepdims=True) + EPS)
        xh = x * r
        err = xh * gain - t_ref[...]
        dy = err * (1.0 / d)
        a = dy * gain
        m = jnp.mean(a * x, axis=-1, keepdims=True)
        dh = r * a - x * (r * r * r * m)
        dh_ref[...] = dh
        dhb_ref[...] = dh.astype(BF16)
        dg_part = jnp.sum(dy * xh, axis=0, keepdims=True)
        sq_part = jnp.sum(err * err, axis=0, keepdims=True)

        @pl.when(i == 0)
        def _():
            dg_ref[...] = dg_part
            sq_ref[...] = sq_part

        @pl.when(i > 0)
        def _():
            dg_ref[...] += dg_part
            sq_ref[...] += sq_part

        @pl.when(i == nt - 1)
        def _():
            total = jnp.sum(sq_ref[...], axis=1, keepdims=True) * (0.5 / d)
            loss_ref[...] = jnp.broadcast_to(total, loss_ref.shape)

    row = pl.BlockSpec((tr, d), lambda i: (i, 0))
    vec = pl.BlockSpec((1, d), lambda i: (0, 0))
    return pl.pallas_call(
        body, grid=(nt,), in_specs=[row, vec, row],
        out_specs=[row, row, vec, pl.BlockSpec((1, 128), lambda i: (0, 0))],
        out_shape=[jax.ShapeDtypeStruct((s, d), F32), jax.ShapeDtypeStruct((s, d), BF16),
                   jax.ShapeDtypeStruct((1, d), F32), jax.ShapeDtypeStruct((1, 128), F32)],
        scratch_shapes=[pltpu.VMEM((1, d), F32)], name=name, compiler_params=_params("arbitrary"),
    )(h, g.reshape(1, d), target)


_SQRT_HALF = 0.7071067811865476
_INV_SQRT_2PI = 0.3989422804014327


def _gelu(x):
    return 0.5 * x * (1.0 + lax.erf(x * _SQRT_HALF))


def _gelu_grad(x):
    return 0.5 * (1.0 + lax.erf(x * _SQRT_HALF)) + x * (jnp.exp(-0.5 * x * x) * _INV_SQRT_2PI)


def _trailing_sums(xe, win):
    s = xe
    sh = 1
    while sh < win:
        s = s + pltpu.roll(s, sh, 0)
        sh *= 2
    return s


def _leading_sums(xe, win):
    n = xe.shape[0]
    s = xe
    sh = 1
    while sh < win:
        s = s + pltpu.roll(s, n - sh, 0)
        sh *= 2
    return s


def _tril_mask():
    return lax.broadcasted_iota(jnp.int32, (GROUP, GROUP), 0) >= lax.broadcasted_iota(jnp.int32, (GROUP, GROUP), 1)


def _layernorm_stats(v):
    mu = jnp.mean(v, axis=-1, keepdims=True)
    xc = v - mu
    rstd = lax.rsqrt(jnp.mean(xc * xc, axis=-1, keepdims=True) + EPS)
    return xc * rstd, rstd


def _mixer_specs(s, t):
    halo_blocks = t // POOL_HALO
    tile = lambda w: pl.BlockSpec((t, w), lambda i: (i, 0))
    prev = pl.BlockSpec((POOL_HALO, POOL_WIDTH), lambda i: (jnp.maximum(i * halo_blocks - 1, 0), 0))
    nxt = pl.BlockSpec((POOL_HALO, POOL_WIDTH),
                       lambda i: (jnp.minimum((i + 1) * halo_blocks, s // POOL_HALO - 1), 0))
    const3 = pl.BlockSpec((HEADS, GROUP, GROUP), lambda i: (0, 0, 0))
    vec = pl.BlockSpec((1, POOL_WIDTH), lambda i: (0, 0))
    bias = pl.BlockSpec((GROUP, SGU_WIDTH), lambda i: (0, 0))
    return tile, prev, nxt, const3, vec, bias


def _mixer_fwd(proj, pool_w, pool_scale, sgu_g, sgu_w, sgu_bias, *, name, t=256):
    s = proj.shape[0]
    t = _tile(s, t)
    tile, prev, _, const3, vec, bias = _mixer_specs(s, t)

    def body(proj_ref, halo_ref, pw_ref, ps_ref, sg_ref, sw_ref, sb_ref, cat_ref):
        i = pl.program_id(0)
        row = i * t + lax.broadcasted_iota(jnp.int32, (t, 1), 0)
        p = proj_ref[:, 0:POOL_WIDTH]
        pe = jnp.concatenate([jnp.where(i > 0, halo_ref[...], 0.0), p], axis=0)
        for gi, win in enumerate(POOL_WINDOWS):
            cols = slice(gi * GROUP, (gi + 1) * GROUP)
            count = jnp.minimum(row + 1, win).astype(F32)
            d = _trailing_sums(pe[:, cols], win)[POOL_HALO:] / count - p[:, cols]
            y = jnp.dot(d.astype(BF16), pw_ref[gi].astype(BF16), preferred_element_type=F32) * ps_ref[:, cols]
            cat_ref[:, cols] = y.astype(BF16)

        u = _gelu(proj_ref[:, POOL_WIDTH:POOL_WIDTH + SGU_WIDTH])
        xhat, _ = _layernorm_stats(_gelu(proj_ref[:, POOL_WIDTH + SGU_WIDTH:]))
        vn = (xhat * sg_ref[...]).astype(BF16)
        tri = _tril_mask()
        for h in range(HEADS):
            cols = slice(h * GROUP, (h + 1) * GROUP)
            w = jnp.where(tri, sw_ref[h], 0.0).astype(BF16)
            for c in range(t // GROUP):
                rows = slice(c * GROUP, (c + 1) * GROUP)
                z = jnp.dot(w, vn[rows, cols], preferred_element_type=F32) + sb_ref[:, cols]
                cat_ref[rows, POOL_WIDTH + h * GROUP:POOL_WIDTH + (h + 1) * GROUP] = (u[rows, cols] * z).astype(BF16)

    return pl.pallas_call(
        body, grid=(s // t,),
        in_specs=[tile(POOL_WIDTH + 2 * SGU_WIDTH), prev, const3, vec, vec, const3, bias],
        out_specs=tile(POOL_WIDTH + SGU_WIDTH),
        out_shape=jax.ShapeDtypeStruct((s, POOL_WIDTH + SGU_WIDTH), BF16), name=name,
        compiler_params=_params("parallel"),
    )(proj, proj, pool_w, pool_scale, sgu_g, sgu_w, sgu_bias)


def _mixer_bwd(proj, dcat, pool_w, pool_scale, sgu_g, sgu_w, sgu_bias, *, name, t=256):
    s = proj.shape[0]
    t = _tile(s, t)
    nt = s // t
    tile, prev, nxt, const3, vec, bias = _mixer_specs(s, t)

    def body(proj_ref, halo_ref, dcat_ref, dnext_ref, pw_ref, ps_ref, sg_ref, sw_ref, sb_ref,
             dproj_ref, dpw_ref, dps_ref, dsg_ref, dsw_ref, dsb_ref, du_ref, dvn_ref, dz_ref):
        i = pl.program_id(0)

        @pl.when(i == 0)
        def _():
            dpw_ref[...] = jnp.zeros_like(dpw_ref)
            dps_ref[...] = jnp.zeros_like(dps_ref)
            dsg_ref[...] = jnp.zeros_like(dsg_ref)
            dsw_ref[...] = jnp.zeros_like(dsw_ref)
            dz_ref[...] = jnp.zeros_like(dz_ref)

        row = i * t + lax.broadcasted_iota(jnp.int32, (t, 1), 0)
        row_e = i * t + lax.broadcasted_iota(jnp.int32, (t + POOL_HALO, 1), 0)
        p = proj_ref[:, 0:POOL_WIDTH]
        pe = jnp.concatenate([jnp.where(i > 0, halo_ref[...], 0.0), p], axis=0)
        dyp = dcat_ref[:, 0:POOL_WIDTH]
        dye = jnp.concatenate([dyp, jnp.where(i < nt - 1, dnext_ref[...], 0.0)], axis=0)
        for gi, win in enumerate(POOL_WINDOWS):
            cols = slice(gi * GROUP, (gi + 1) * GROUP)
            count = jnp.minimum(row + 1, win).astype(F32)
            d = (_trailing_sums(pe[:, cols], win)[POOL_HALO:] / count - p[:, cols]).astype(BF16)
            pw = pw_ref[gi].astype(BF16)
            pre = jnp.dot(d, pw, preferred_element_type=F32)
            dps_ref[:, cols] += jnp.sum(dyp[:, cols] * pre, axis=0, keepdims=True)
            ys = (dye[:, cols] * ps_ref[:, cols]).astype(BF16)
            dpw_ref[gi] += lax.dot_general(d, ys[:t], _DOT_DIMS["tn"], preferred_element_type=F32)
            dd = lax.dot_general(ys, pw, _DOT_DIMS["nt"], preferred_element_type=F32)
            count_e = jnp.minimum(row_e + 1, win).astype(F32)
            dp = _leading_sums(dd / count_e, win)[:t] - dd[:t]
            dproj_ref[:, cols] = dp.astype(BF16)

        xu = proj_ref[:, POOL_WIDTH:POOL_WIDTH + SGU_WIDTH]
        xv = proj_ref[:, POOL_WIDTH + SGU_WIDTH:]
        u = _gelu(xu)
        xhat, rstd = _layernorm_stats(_gelu(xv))
        gain = sg_ref[...]
        vn = (xhat * gain).astype(BF16)
        tri = _tril_mask()
        for h in range(HEADS):
            cols = slice(h * GROUP, (h + 1) * GROUP)
            wf = jnp.where(tri, sw_ref[h], 0.0)
            w, wt = wf.astype(BF16), wf.T.astype(BF16)
            for c in range(t // GROUP):
                rows = slice(c * GROUP, (c + 1) * GROUP)
                vch = vn[rows, cols]
                z = jnp.dot(w, vch, preferred_element_type=F32) + sb_ref[:, cols]
                dy = dcat_ref[rows, POOL_WIDTH + h * GROUP:POOL_WIDTH + (h + 1) * GROUP]
                du_ref[rows, cols] = dy * z
                dz = dy * u[rows, cols]
                dz_ref[:, cols] += dz
                dzb = dz.astype(BF16)
                dsw_ref[h] += lax.dot_general(dzb, vch, _DOT_DIMS["nt"], preferred_element_type=F32)
                dvn_ref[rows, cols] = jnp.dot(wt, dzb, preferred_element_type=F32)
        dvn = dvn_ref[...]
        dsg_ref[...] += jnp.sum(dvn * xhat, axis=0, keepdims=True)
        dxh = dvn * gain
        dv = rstd * (dxh - jnp.mean(dxh, axis=-1, keepdims=True)
                     - xhat * jnp.mean(dxh * xhat, axis=-1, keepdims=True))
        dproj_ref[:, POOL_WIDTH:POOL_WIDTH + SGU_WIDTH] = (du_ref[...] * _gelu_grad(xu)).astype(BF16)
        dproj_ref[:, POOL_WIDTH + SGU_WIDTH:] = (dv * _gelu_grad(xv)).astype(BF16)

        @pl.when(i == nt - 1)
        def _():
            for h in range(HEADS):
                dsw_ref[h] = jnp.where(tri, dsw_ref[h], 0.0)
            lane = lax.broadcasted_iota(jnp.int32, (GROUP, GROUP), 1)
            out = jnp.zeros((GROUP, GROUP), F32)
            for h in range(HEADS):
                sh = jnp.sum(dz_ref[:, h * GROUP:(h + 1) * GROUP], axis=1, keepdims=True)
                out = jnp.where(lane == h, sh, out)
            dsb_ref[...] = out

    outs = pl.pallas_call(
        body, grid=(nt,),
        in_specs=[tile(POOL_WIDTH + 2 * SGU_WIDTH), prev, tile(POOL_WIDTH + SGU_WIDTH), nxt,
                  const3, vec, vec, const3, bias],
        out_specs=[tile(POOL_WIDTH + 2 * SGU_WIDTH), const3, vec, vec, const3,
                   pl.BlockSpec((GROUP, GROUP), lambda i: (0, 0))],
        out_shape=[jax.ShapeDtypeStruct((s, POOL_WIDTH + 2 * SGU_WIDTH), BF16),
                   jax.ShapeDtypeStruct((HEADS, GROUP, GROUP), F32),
                   jax.ShapeDtypeStruct((1, POOL_WIDTH), F32),
                   jax.ShapeDtypeStruct((1, SGU_WIDTH), F32),
                   jax.ShapeDtypeStruct((HEADS, GROUP, GROUP), F32),
                   jax.ShapeDtypeStruct((GROUP, GROUP), F32)],
        scratch_shapes=[pltpu.VMEM((t, SGU_WIDTH), F32), pltpu.VMEM((t, SGU_WIDTH), F32),
                        pltpu.VMEM((GROUP, SGU_WIDTH), F32)],
        name=name, compiler_params=_params("arbitrary"),
    )(proj, proj, dcat, dcat, pool_w, pool_scale, sgu_g, sgu_w, sgu_bias)
    dproj, dpw, dps, dsg, dsw, dsb = outs
    return dproj, dpw, dps, dsg, dsw, dsb[:, :HEADS].T


def _attn_probs(q, k, scale):
    sc = lax.dot_general(q, k, _DOT_DIMS["nt"], preferred_element_type=F32) * scale
    sc = sc - jnp.max(sc, axis=-1, keepdims=True)
    e = jnp.exp(sc)
    return e / jnp.sum(e, axis=-1, keepdims=True)


def _attn_fwd(q, k, v, *, name, t=512):
    s, d = q.shape
    nm = k.shape[0]
    t = _tile(s, t)
    scale = HEAD_DIM ** -0.5

    def body(q_ref, k_ref, v_ref, o_ref):
        for h in range(HEADS):
            cols = slice(h * HEAD_DIM, (h + 1) * HEAD_DIM)
            pr = _attn_probs(q_ref[:, cols], k_ref[:, cols], scale)
            o_ref[:, cols] = jnp.dot(pr.astype(BF16), v_ref[:, cols], preferred_element_type=F32).astype(BF16)

    row = pl.BlockSpec((t, d), lambda i: (i, 0))
    kv = pl.BlockSpec((nm, d), lambda i: (0, 0))
    return pl.pallas_call(
        body, grid=(s // t,), in_specs=[row, kv, kv], out_specs=row,
        out_shape=jax.ShapeDtypeStruct((s, d), BF16), name=name, compiler_params=_params("parallel"),
    )(q, k, v)


def _attn_bwd(q, k, v, do, *, name, t=512):
    s, d = q.shape
    nm = k.shape[0]
    t = _tile(s, t)
    scale = HEAD_DIM ** -0.5

    def body(q_ref, k_ref, v_ref, do_ref, dq_ref, dk_ref, dv_ref):
        i = pl.program_id(0)

        @pl.when(i == 0)
        def _():
            dk_ref[...] = jnp.zeros_like(dk_ref)
            dv_ref[...] = jnp.zeros_like(dv_ref)

        for h in range(HEADS):
            cols = slice(h * HEAD_DIM, (h + 1) * HEAD_DIM)
            qh, kh, vh, doh = q_ref[:, cols], k_ref[:, cols], v_ref[:, cols], do_ref[:, cols]
            pr = _attn_probs(qh, kh, scale)
            dpr = lax.dot_general(doh, vh, _DOT_DIMS["nt"], preferred_element_type=F32)
            ds = (pr * (dpr - jnp.sum(dpr * pr, axis=-1, keepdims=True)) * scale).astype(BF16)
            dv_ref[:, cols] += lax.dot_general(pr.astype(BF16), doh, _DOT_DIMS["tn"], preferred_element_type=F32)
            dk_ref[:, cols] += lax.dot_general(ds, qh, _DOT_DIMS["tn"], preferred_element_type=F32)
            dq_ref[:, cols] = jnp.dot(ds, kh, preferred_element_type=F32).astype(BF16)

    row = pl.BlockSpec((t, d), lambda i: (i, 0))
    kv = pl.BlockSpec((nm, d), lambda i: (0, 0))
    return pl.pallas_call(
        body, grid=(s // t,), in_specs=[row, kv, kv, row], out_specs=[row, kv, kv],
        out_shape=[jax.ShapeDtypeStruct((s, d), BF16), jax.ShapeDtypeStruct((nm, d), F32),
                   jax.ShapeDtypeStruct((nm, d), F32)],
        name=name, compiler_params=_params("arbitrary"),
    )(q, k, v, do)


def _conv_specs(s, f, t, tc, swap):
    hb = t // CONV_HALO
    order = (lambda fn: (lambda j, i: fn(i, j))) if swap else (lambda fn: fn)
    tile3 = pl.BlockSpec((2, t, tc), order(lambda i, j: (0, i, j)))
    prev3 = pl.BlockSpec((2, CONV_HALO, tc), order(lambda i, j: (0, jnp.maximum(i * hb - 1, 0), j)))
    next3 = pl.BlockSpec((2, CONV_HALO, tc),
                         order(lambda i, j: (0, jnp.minimum((i + 1) * hb, s // CONV_HALO - 1), j)))
    tile2 = pl.BlockSpec((t, tc), order(lambda i, j: (i, j)))
    next2 = pl.BlockSpec((CONV_HALO, tc), order(lambda i, j: (jnp.minimum((i + 1) * hb, s // CONV_HALO - 1), j)))
    wspec = pl.BlockSpec((2, 3, tc), order(lambda i, j: (0, 0, j)))
    bspec = pl.BlockSpec((2, 1, tc), order(lambda i, j: (0, 0, j)))
    return tile3, prev3, next3, tile2, next2, wspec, bspec


def _conv3(w_ref, p, x2, x1, x0, b):
    return (w_ref[p, 0:1, :] * x2 + w_ref[p, 1:2, :] * x1 + w_ref[p, 2:3, :] * x0) + b


def _convgate_fwd(hh, cw, cb, *, name, t=256, tc=1408):
    _, s, f = hh.shape
    t = _tile(s, t)
    tile3, prev3, _, tile2, _, wspec, bspec = _conv_specs(s, f, t, tc, swap=False)

    def body(hh_ref, prev_ref, cw_ref, cb_ref, act_ref):
        i = pl.program_id(0)
        hc = []
        for p in range(2):
            xe = jnp.concatenate([jnp.where(i > 0, prev_ref[p], 0.0), hh_ref[p]], axis=0)
            hc.append(_conv3(cw_ref, p, pltpu.roll(xe, 2, 0), pltpu.roll(xe, 1, 0), xe, cb_ref[p])[CONV_HALO:])
        gate, val = hc
        act_ref[...] = ((gate * jax.nn.sigmoid(gate)) * val).astype(BF16)

    return pl.pallas_call(
        body, grid=(s // t, f // tc), in_specs=[tile3, prev3, wspec, bspec], out_specs=tile2,
        out_shape=jax.ShapeDtypeStruct((s, f), BF16), name=name, compiler_params=_params("parallel", "parallel"),
    )(hh, hh, cw, cb)


def _convgate_bwd(hh, dact, cw, cb, *, name, t=128, tc=1408):
    _, s, f = hh.shape
    t = _tile(s, t)
    nt = s // t
    tile3, prev3, next3, tile2, next2, wspec, bspec = _conv_specs(s, f, t, tc, swap=True)

    def body(hh_ref, prev_ref, next_ref, da_ref, danext_ref, cw_ref, cb_ref, dhh_ref, dcw_ref, dcb_ref):
        i = pl.program_id(1)
        is_last = i == nt - 1

        @pl.when(i == 0)
        def _():
            dcw_ref[...] = jnp.zeros_like(dcw_ref)
            dcb_ref[...] = jnp.zeros_like(dcb_ref)

        taps, hc = [], []
        for p in range(2):
            xe = jnp.concatenate([jnp.where(i > 0, prev_ref[p], 0.0), hh_ref[p],
                                  jnp.where(is_last, 0.0, next_ref[p])], axis=0)
            x2, x1 = pltpu.roll(xe, 2, 0), pltpu.roll(xe, 1, 0)
            hc.append(_conv3(cw_ref, p, x2, x1, xe, cb_ref[p])[CONV_HALO:])
            taps.append((x2[CONV_HALO:CONV_HALO + t], x1[CONV_HALO:CONV_HALO + t], xe[CONV_HALO:CONV_HALO + t]))
        gate, val = hc
        dae = jnp.concatenate([da_ref[...], jnp.where(is_last, 0.0, danext_ref[...])], axis=0)
        sg = jax.nn.sigmoid(gate)
        dval = dae * (gate * sg)
        dgate = dae * val * (sg * (1.0 + gate * (1.0 - sg)))
        m = t + CONV_HALO
        for p, dhc in enumerate((dgate, dval)):
            dh = (cw_ref[p, 2:3, :] * dhc + cw_ref[p, 1:2, :] * pltpu.roll(dhc, m - 1, 0)
                  + cw_ref[p, 0:1, :] * pltpu.roll(dhc, m - 2, 0))
            dhh_ref[p] = dh[:t].astype(BF16)
            d0 = dhc[:t]
            for kk, tap in enumerate(taps[p]):
                dcw_ref[p, kk:kk + 1, :] += jnp.sum(d0 * tap, axis=0, keepdims=True)
            dcb_ref[p] += jnp.sum(d0, axis=0, keepdims=True)

    return pl.pallas_call(
        body, grid=(f // tc, nt), in_specs=[tile3, prev3, next3, tile2, next2, wspec, bspec],
        out_specs=[tile3, wspec, bspec],
        out_shape=[jax.ShapeDtypeStruct((2, s, f), BF16), jax.ShapeDtypeStruct((2, 3, f), F32),
                   jax.ShapeDtypeStruct((2, 1, f), F32)],
        name=name, compiler_params=_params("parallel", "arbitrary"),
    )(hh, hh, hh, dact, dact, cw, cb)


def _position():
    return lax.axis_index("x"), lax.axis_index("y"), lax.axis_index("c")


def _linear(px, py, pc):
    return 4 * px + 2 * py + pc


_ANY = pl.BlockSpec(memory_space=pl.ANY)


def _all_gather(shards, *, name):
    n = len(shards)

    def body(*refs):
        x_refs, o_refs = refs[:n], refs[n:2 * n]
        send_sems, recv_sems, local_sems = refs[2 * n:]
        x, y, c = _position()
        me, sibling = (x, y, c), (x, y, 1 - c)
        chips = [(1 - x, y), (x, 1 - y), (1 - x, 1 - y)]

        def copy(ti, k, block, to, src=None):
            dst = o_refs[ti].at[_linear(*block)]
            return pltpu.make_async_remote_copy(
                src_ref=dst if src is None else src, dst_ref=dst,
                send_sem=send_sems.at[ti, k], recv_sem=recv_sems.at[ti, k],
                device_id=to, device_id_type=MESH)

        mine = [pltpu.make_async_copy(x_refs[ti], o_refs[ti].at[_linear(*me)], local_sems.at[ti]) for ti in range(n)]
        for cp in mine:
            cp.start()
        first = []
        for ti in range(n):
            first.append(copy(ti, 0, me, sibling, src=x_refs[ti]))
            for j, chip in enumerate(chips):
                first.append(copy(ti, 1 + j, me, (*chip, c), src=x_refs[ti]))
        for cp in first:
            cp.start()
        passed = []
        for j, chip in enumerate(chips):
            for ti in range(n):
                copy(ti, 1 + j, (*chip, c), me).wait_recv()
                fwd = copy(ti, 4 + j, (*chip, c), sibling)
                fwd.start()
                passed.append(fwd)
        for ti in range(n):
            copy(ti, 0, sibling, me).wait_recv()
            for j, chip in enumerate(chips):
                copy(ti, 4 + j, (*chip, 1 - c), me).wait_recv()
        for cp in first + passed:
            cp.wait_send()
        for cp in mine:
            cp.wait()

    return pl.pallas_call(
        body, in_specs=[_ANY] * n, out_specs=[_ANY] * n,
        out_shape=[jax.ShapeDtypeStruct((N_DEV,) + a.shape, a.dtype) for a in shards],
        scratch_shapes=[pltpu.SemaphoreType.DMA((n, 7)), pltpu.SemaphoreType.DMA((n, 7)),
                        pltpu.SemaphoreType.DMA((n,))],
        name=name,
    )(*shards)


def _scatter_partials(parts, *, name):
    depth = len(parts)
    nw = len(parts[0])
    flat = [parts[l][w] for l in range(depth) for w in range(nw)]

    def body(*refs):
        p_refs, o_refs = refs[:depth * nw], refs[depth * nw:depth * nw + nw]
        send_sems, recv_sems, local_sems = refs[depth * nw + nw:]
        x, y, c = _position()
        mine = _linear(x, y, c)
        peers = []
        for mask in range(1, N_DEV):
            px = 1 - x if mask & 4 else x
            py = 1 - y if mask & 2 else y
            pc = 1 - c if mask & 1 else c
            peers.append((px, py, pc))

        local, remote = [], []
        for l in range(depth):
            for w in range(nw):
                ti = l * nw + w
                local.append(pltpu.make_async_copy(p_refs[ti].at[mine], o_refs[w].at[l, mine], local_sems.at[ti]))
                for k, peer in enumerate(peers):
                    remote.append(pltpu.make_async_remote_copy(
                        src_ref=p_refs[ti].at[_linear(*peer)], dst_ref=o_refs[w].at[l, mine],
                        send_sem=send_sems.at[ti, k], recv_sem=recv_sems.at[ti, k],
                        device_id=peer, device_id_type=MESH))
        for cp in local + remote:
            cp.start()
        for l in range(depth):
            for w in range(nw):
                ti = l * nw + w
                for k, peer in enumerate(peers):
                    landing = o_refs[w].at[l, _linear(*peer)]
                    pltpu.make_async_remote_copy(
                        src_ref=landing, dst_ref=landing, send_sem=send_sems.at[ti, k], recv_sem=recv_sems.at[ti, k],
                        device_id=peer, device_id_type=MESH).wait_recv()
        for cp in remote:
            cp.wait_send()
        for cp in local:
            cp.wait()

    nf = depth * nw
    return pl.pallas_call(
        body, in_specs=[_ANY] * nf, out_specs=[_ANY] * nw,
        out_shape=[jax.ShapeDtypeStruct((depth, N_DEV) + parts[0][w].shape[1:], parts[0][w].dtype) for w in range(nw)],
        scratch_shapes=[pltpu.SemaphoreType.DMA((nf, 7)), pltpu.SemaphoreType.DMA((nf, 7)),
                        pltpu.SemaphoreType.DMA((nf,))],
        name=name,
    )(*flat)


def _adamw_math(g, w, m, v):
    m2 = ADAM_B1 * m + (1.0 - ADAM_B1) * g
    v2 = ADAM_B2 * v + (1.0 - ADAM_B2) * (g * g)
    m_hat = m2 / (1.0 - ADAM_B1 ** ADAM_STEP)
    v_hat = v2 / (1.0 - ADAM_B2 ** ADAM_STEP)
    delta = -ADAM_LR * (m_hat / (jnp.sqrt(v_hat) + ADAM_EPS) + ADAM_WD * w)
    return delta, m2, v2


def _adamw(slots, w, m, v, *, name, tr=256):
    depth, _, r, c = slots.shape
    tr = next((cand for cand in range(min(r, tr), 15, -1) if r % cand == 0 and cand % 16 == 0), r)

    def body(s_ref, w_ref, m_ref, v_ref, g_ref, d_ref, m2_ref, v2_ref):
        g = s_ref[0].astype(F32)
        for d in range(1, N_DEV):
            g = g + s_ref[d].astype(F32)
        delta, m2, v2 = _adamw_math(g, w_ref[...], m_ref[...], v_ref[...])
        g_ref[...] = g
        d_ref[...] = delta
        m2_ref[...] = m2
        v2_ref[...] = v2

    blk = pl.BlockSpec((None, tr, c), lambda l, i: (l, i, 0))
    sblk = pl.BlockSpec((None, N_DEV, tr, c), lambda l, i: (l, 0, i, 0))
    shape = jax.ShapeDtypeStruct((depth, r, c), F32)
    return pl.pallas_call(
        body, grid=(depth, r // tr), in_specs=[sblk, blk, blk, blk], out_specs=[blk] * 4,
        out_shape=[shape] * 4, name=name, compiler_params=_params("parallel", "parallel"),
    )(slots, w, m, v)


_SHARDED = ("w_in", "w_out", "wq", "wk", "wv", "wo", "w_up", "conv_w", "w_down")
_SMALL = ("norm_mix_g", "pool_w", "pool_scale", "sgu_g", "sgu_w", "sgu_b", "norm_xattn_g", "mem_norm_g",
          "norm_ffn_g", "conv_b", "final_norm_g")
_WEIGHTS = ("norm_mix_g", "w_in", "pool_w", "pool_scale", "sgu_g", "sgu_w", "sgu_b", "w_out", "norm_xattn_g",
            "mem_norm_g", "wq", "wk", "wv", "wo", "norm_ffn_g", "w_up", "conv_w", "conv_b", "w_down",
            "final_norm_g")
_PACK_LANES = 128


def _cols_to_blocks(a):
    r, c8 = a.shape
    return a.reshape(r, N_DEV, c8 // N_DEV).transpose(1, 0, 2)


def _blocks_to_cols(a):
    n, r, c = a.shape
    return a.transpose(1, 0, 2).reshape(r, n * c)


def _pack(arrays):
    flat = jnp.concatenate([a.reshape(-1) for a in arrays])
    assert flat.shape[0] % (8 * _PACK_LANES) == 0
    return flat.reshape(-1, _PACK_LANES)


def _unpack(packed, like):
    flat = packed.reshape(-1)
    out, off = [], 0
    for a in like:
        out.append(flat[off:off + a.size].reshape(a.shape))
        off += a.size
    return out


def kernel(x, mem, norm_mix_g, w_in, pool_w, pool_scale, sgu_g, sgu_w, sgu_b, w_out, norm_xattn_g, mem_norm_g, wq, wk, wv, wo, norm_ffn_g, w_up, conv_w, conv_b, w_down, final_norm_g, loss_target, m_norm_mix_g, m_w_in, m_pool_w, m_pool_scale, m_sgu_g, m_sgu_w, m_sgu_b, m_w_out, m_norm_xattn_g, m_mem_norm_g, m_wq, m_wk, m_wv, m_wo, m_norm_ffn_g, m_w_up, m_conv_w, m_conv_b, m_w_down, m_final_norm_g, v_norm_mix_g, v_w_in, v_pool_w, v_pool_scale, v_sgu_g, v_sgu_w, v_sgu_b, v_w_out, v_norm_xattn_g, v_mem_norm_g, v_wq, v_wk, v_wv, v_wo, v_norm_ffn_g, v_w_up, v_conv_w, v_conv_b, v_w_down, v_final_norm_g):
    W = dict(norm_mix_g=norm_mix_g, w_in=w_in, pool_w=pool_w, pool_scale=pool_scale, sgu_g=sgu_g, sgu_w=sgu_w,
             sgu_b=sgu_b, w_out=w_out, norm_xattn_g=norm_xattn_g, mem_norm_g=mem_norm_g, wq=wq, wk=wk, wv=wv, wo=wo,
             norm_ffn_g=norm_ffn_g, w_up=w_up, conv_w=conv_w, conv_b=conv_b, w_down=w_down,
             final_norm_g=final_norm_g)
    M = dict(norm_mix_g=m_norm_mix_g, w_in=m_w_in, pool_w=m_pool_w, pool_scale=m_pool_scale, sgu_g=m_sgu_g,
             sgu_w=m_sgu_w, sgu_b=m_sgu_b, w_out=m_w_out, norm_xattn_g=m_norm_xattn_g, mem_norm_g=m_mem_norm_g,
             wq=m_wq, wk=m_wk, wv=m_wv, wo=m_wo, norm_ffn_g=m_norm_ffn_g, w_up=m_w_up, conv_w=m_conv_w,
             conv_b=m_conv_b, w_down=m_w_down, final_norm_g=m_final_norm_g)
    V = dict(norm_mix_g=v_norm_mix_g, w_in=v_w_in, pool_w=v_pool_w, pool_scale=v_pool_scale, sgu_g=v_sgu_g,
             sgu_w=v_sgu_w, sgu_b=v_sgu_b, w_out=v_w_out, norm_xattn_g=v_norm_xattn_g, mem_norm_g=v_mem_norm_g,
             wq=v_wq, wk=v_wk, wv=v_wv, wo=v_wo, norm_ffn_g=v_norm_ffn_g, w_up=v_w_up, conv_w=v_conv_w,
             conv_b=v_conv_b, w_down=v_w_down, final_norm_g=v_final_norm_g)

    s, d = x.shape[1], x.shape[2]
    f = w_down.shape[1] * N_DEV
    h = x.reshape(s, d)
    memx = mem.reshape(mem.shape[1], d)
    target = loss_target.reshape(s, d)

    shards = []
    for l in range(DEPTH):
        for nme in _SHARDED:
            a = W[nme][l]
            shards.append(a if nme == "conv_w" else a.astype(BF16))
    gathered = _all_gather(shards, name="gather_weights")
    full = []
    for l in range(DEPTH):
        g = dict(zip(_SHARDED, gathered[l * len(_SHARDED):(l + 1) * len(_SHARDED)]))
        cwl = _blocks_to_cols(g["conv_w"])
        full.append(dict(
            w_in=_blocks_to_cols(g["w_in"]), w_up=_blocks_to_cols(g["w_up"]),
            conv_w=cwl.reshape(3, 2, f).transpose(1, 0, 2),
            w_out=g["w_out"].reshape(-1, d), wq=g["wq"].reshape(-1, d), wk=g["wk"].reshape(-1, d),
            wv=g["wv"].reshape(-1, d), wo=g["wo"].reshape(-1, d), w_down=g["w_down"].reshape(-1, d)))

    saved = []
    for l in range(DEPTH):
        fw = full[l]
        sgu_bias = jnp.repeat(sgu_b[l].T, GROUP, axis=1)
        cb = conv_b[l].reshape(2, 1, f)
        xn1 = _rms_fwd(h, norm_mix_g[l], name=f"norm_mix_{l}")
        proj = _mm_nn(xn1, fw["w_in"], out_dtype=F32, name=f"proj_in_{l}")
        cat = _mixer_fwd(proj, pool_w[l], pool_scale[l].reshape(1, -1), sgu_g[l].reshape(1, -1), sgu_w[l], sgu_bias,
                         name=f"mixer_{l}")
        h1 = _mm_nn(cat, fw["w_out"], out_dtype=F32, res=h, name=f"proj_out_{l}")
        xn2 = _rms_fwd(h1, norm_xattn_g[l], name=f"norm_xattn_{l}")
        q = _mm_nn(xn2, fw["wq"], out_dtype=BF16, name=f"q_{l}")
        memn = _rms_fwd(memx, mem_norm_g[l], name=f"norm_mem_{l}")
        k = _mm_nn(memn, fw["wk"], out_dtype=BF16, name=f"k_{l}")
        v = _mm_nn(memn, fw["wv"], out_dtype=BF16, name=f"v_{l}")
        o = _attn_fwd(q, k, v, name=f"attn_{l}")
        h2 = _mm_nn(o, fw["wo"], out_dtype=F32, res=h1, name=f"attn_out_{l}")
        xn3 = _rms_fwd(h2, norm_ffn_g[l], name=f"norm_ffn_{l}")
        hh = _mm_up(xn3, fw["w_up"], name=f"ffn_up_{l}")
        act = _convgate_fwd(hh, fw["conv_w"], cb, name=f"convgate_{l}")
        h3 = _mm_nn(act, fw["w_down"], out_dtype=F32, res=h2, tm=512, name=f"ffn_down_{l}")
        saved.append(dict(h0=h, xn1=xn1, proj=proj, cat=cat, h1=h1, xn2=xn2, q=q, memn=memn, k=k, v=v, o=o, h2=h2,
                          xn3=xn3, hh=hh, act=act, sgu_bias=sgu_bias, cb=cb))
        h = h3

    dh, dhb, dg_final, loss_row = _loss_head(h, final_norm_g, target, name="loss_head")
    loss = lax.psum(loss_row[0, 0], ("x", "y", "c"))

    big = [None] * DEPTH
    small = [None] * DEPTH
    for l in reversed(range(DEPTH)):
        fw, sv = full[l], saved[l]
        dact = _mm_nt(dhb, fw["w_down"], out_dtype=F32, tm=512, name=f"d_act_{l}")
        g_w_down = _mm_tn(sv["act"], dhb, tm=f // 2, name=f"g_w_down_{l}")
        dhh, g_conv_w, g_conv_b = _convgate_bwd(sv["hh"], dact, fw["conv_w"], sv["cb"], name=f"d_convgate_{l}")
        dxn3 = _mm_up_nt(dhh, fw["w_up"], name=f"d_xn_ffn_{l}")
        g_w_up = _mm_up_tn(sv["xn3"], dhh, name=f"g_w_up_{l}")
        dh2, dh2b, g_norm_ffn = _rms_bwd(sv["h2"], dxn3, norm_ffn_g[l], dh, name=f"d_norm_ffn_{l}")

        do = _mm_nt(dh2b, fw["wo"], out_dtype=BF16, name=f"d_o_{l}")
        g_wo = _mm_tn(sv["o"], dh2b, name=f"g_wo_{l}")
        dq, dk, dv = _attn_bwd(sv["q"], sv["k"], sv["v"], do, name=f"d_attn_{l}")
        dkb, dvb = dk.astype(BF16), dv.astype(BF16)
        dxn2 = _mm_nt(dq, fw["wq"], out_dtype=F32, name=f"d_xn_xattn_{l}")
        g_wq = _mm_tn(sv["xn2"], dq, name=f"g_wq_{l}")
        g_wk = _mm_tn(sv["memn"], dkb, name=f"g_wk_{l}")
        g_wv = _mm_tn(sv["memn"], dvb, name=f"g_wv_{l}")
        dmemn = _mm_nt(dkb, fw["wk"], out_dtype=F32, name=f"d_memn_k_{l}")
        dmemn = _mm_nt(dvb, fw["wv"], out_dtype=F32, res=dmemn, name=f"d_memn_v_{l}")
        _, _, g_mem_norm = _rms_bwd(memx, dmemn, mem_norm_g[l], None, name=f"d_norm_mem_{l}")
        dh1, dh1b, g_norm_xattn = _rms_bwd(sv["h1"], dxn2, norm_xattn_g[l], dh2, name=f"d_norm_xattn_{l}")

        dcat = _mm_nt(dh1b, fw["w_out"], out_dtype=F32, name=f"d_cat_{l}")
        g_w_out = _mm_tn(sv["cat"], dh1b, name=f"g_w_out_{l}")
        dproj, g_pool_w, g_pool_scale, g_sgu_g, g_sgu_w, g_sgu_b = _mixer_bwd(
            sv["proj"], dcat, pool_w[l], pool_scale[l].reshape(1, -1), sgu_g[l].reshape(1, -1), sgu_w[l],
            sv["sgu_bias"], name=f"d_mixer_{l}")
        dxn1 = _mm_nt(dproj, fw["w_in"], out_dtype=F32, name=f"d_xn_mix_{l}")
        g_w_in = _mm_tn(sv["xn1"], dproj, name=f"g_w_in_{l}")
        dh, dhb, g_norm_mix = _rms_bwd(sv["h0"], dxn1, norm_mix_g[l], dh1, name=f"d_norm_mix_{l}")

        g_conv_w_cols = g_conv_w.transpose(1, 0, 2).reshape(3, 2 * f)
        big[l] = [
            _cols_to_blocks(g_w_in), g_w_out.reshape(N_DEV, -1, d), g_wq.reshape(N_DEV, -1, d),
            g_wk.reshape(N_DEV, -1, d), g_wv.reshape(N_DEV, -1, d), g_wo.reshape(N_DEV, -1, d),
            _cols_to_blocks(g_w_up), _cols_to_blocks(g_conv_w_cols), g_w_down.reshape(N_DEV, -1, d)]
        small[l] = dict(norm_mix_g=g_norm_mix.reshape(-1), pool_w=g_pool_w, pool_scale=g_pool_scale.reshape(-1),
                        sgu_g=g_sgu_g.reshape(-1), sgu_w=g_sgu_w, sgu_b=g_sgu_b, norm_xattn_g=g_norm_xattn.reshape(-1),
                        mem_norm_g=g_mem_norm.reshape(-1), norm_ffn_g=g_norm_ffn.reshape(-1),
                        conv_b=g_conv_b.reshape(-1))
    grad_x = dh.reshape(x.shape)

    slots = _scatter_partials(big, name="scatter_grads")
    out = {}
    for wi, nme in enumerate(_SHARDED):
        w3 = W[nme].reshape(DEPTH, -1, W[nme].shape[-1])
        res = _adamw(slots[wi].reshape((DEPTH, N_DEV) + w3.shape[1:]), w3, M[nme].reshape(w3.shape),
                     V[nme].reshape(w3.shape), name=f"adamw_{nme}")
        out[nme] = [r.reshape(W[nme].shape) for r in res]

    small_names = [n for n in _SMALL]
    contrib = []
    for nme in small_names:
        if nme == "final_norm_g":
            contrib.append(dg_final.reshape(-1))
        else:
            contrib.append(jnp.stack([small[l][nme] for l in range(DEPTH)]))
    packed_g = _pack(contrib)
    (all_g,) = _all_gather([packed_g], name="gather_small_grads")
    rows = packed_g.shape[0]
    res = _adamw(all_g.reshape(1, N_DEV, rows, _PACK_LANES), _pack([W[n] for n in small_names]).reshape(1, rows, -1),
                 _pack([M[n] for n in small_names]).reshape(1, rows, -1),
                 _pack([V[n] for n in small_names]).reshape(1, rows, -1), name="adamw_small", tr=rows // 3)
    unpacked = [_unpack(r, [W[n] for n in small_names]) for r in res]
    for i, nme in enumerate(small_names):
        out[nme] = [unpacked[j][i] for j in range(4)]

    grads = [out[n][0] for n in _WEIGHTS]
    deltas = [out[n][1] for n in _WEIGHTS]
    new_m = [out[n][2] for n in _WEIGHTS]
    new_v = [out[n][3] for n in _WEIGHTS]
    return (loss, grad_x, *grads, *deltas, *new_m, *new_v)
```

```python
import functools
import math

import jax
import jax.numpy as jnp
from jax import lax
from jax.experimental import pallas as pl
from jax.experimental.pallas import tpu as pltpu

F32 = jnp.float32
BF16 = jnp.bfloat16
MESH = pl.DeviceIdType.MESH

EPS = 1e-6
N_DEV = 8
DEPTH = 2
POOL_WINDOWS = (2, 4, 8, 16)
GROUP = 128
POOL_WIDTH = 512
SGU_WIDTH = 512
HEADS = 4
HEAD_DIM = 256
POOL_HALO = 16
CONV_HALO = 8

ADAM_LR = 0.001
ADAM_B1 = 0.9
ADAM_B2 = 0.999
ADAM_EPS = 1e-08
ADAM_WD = 0.01
ADAM_STEP = 10

VMEM_LIMIT_BYTES = 52 * 1024 * 1024


def _params(*semantics):
    return pltpu.CompilerParams(dimension_semantics=semantics, vmem_limit_bytes=VMEM_LIMIT_BYTES)


def _tile(n, want):
    t = min(n, want)
    assert n % t == 0, (n, want)
    return t


_DOT_DIMS = {
    "nn": (((1,), (0,)), ((), ())),
    "nt": (((1,), (1,)), ((), ())),
    "tn": (((0,), (0,)), ((), ())),
}


def _mm(a, b, *, dims, grid, a_spec, b_spec, o_spec, out_shape, out_dtype, acc_shape, name, res=None, res_spec=None):
    nk = grid[2]
    dn = _DOT_DIMS[dims]
    has_res = res is not None

    def body(*refs):
        if has_res:
            a_ref, b_ref, r_ref, o_ref = refs[:4]
        else:
            a_ref, b_ref, o_ref = refs[:3]
            r_ref = None
        p = lax.dot_general(a_ref[...], b_ref[...], dn, preferred_element_type=F32)

        def finish(r):
            if r_ref is not None:
                r = r + r_ref[...]
            o_ref[...] = r.astype(o_ref.dtype)

        if nk == 1:
            finish(p)
        else:
            acc_ref = refs[-1]
            k = pl.program_id(2)

            @pl.when(k == 0)
            def _():
                acc_ref[...] = p

            @pl.when(k > 0)
            def _():
                acc_ref[...] += p

            @pl.when(k == nk - 1)
            def _():
                finish(acc_ref[...])

    in_specs = [a_spec, b_spec] + ([res_spec] if has_res else [])
    args = (a, b) + ((res,) if has_res else ())
    scratch = [pltpu.VMEM(acc_shape, F32)] if nk > 1 else []
    return pl.pallas_call(
        body, grid=grid, in_specs=in_specs, out_specs=o_spec,
        out_shape=jax.ShapeDtypeStruct(out_shape, out_dtype), scratch_shapes=scratch, name=name,
        compiler_params=_params("parallel", "parallel", "arbitrary"),
    )(*args)


def _mm_nn(a, b, *, out_dtype, name, res=None, tm=1024):
    m, k = a.shape
    n = b.shape[1]
    tm = _tile(m, tm)
    spec_o = pl.BlockSpec((tm, n), lambda i, j, kk: (i, 0))
    return _mm(a, b, dims="nn", grid=(m // tm, 1, 1),
               a_spec=pl.BlockSpec((tm, k), lambda i, j, kk: (i, 0)),
               b_spec=pl.BlockSpec((k, n), lambda i, j, kk: (0, 0)),
               o_spec=spec_o, out_shape=(m, n), out_dtype=out_dtype, acc_shape=None, name=name,
               res=res, res_spec=spec_o if res is not None else None)


def _mm_nt(a, b, *, out_dtype, name, res=None, tm=1024):
    m, k = a.shape
    n = b.shape[0]
    tm = _tile(m, tm)
    spec_o = pl.BlockSpec((tm, n), lambda i, j, kk: (i, 0))
    return _mm(a, b, dims="nt", grid=(m // tm, 1, 1),
               a_spec=pl.BlockSpec((tm, k), lambda i, j, kk: (i, 0)),
               b_spec=pl.BlockSpec((n, k), lambda i, j, kk: (0, 0)),
               o_spec=spec_o, out_shape=(m, n), out_dtype=out_dtype, acc_shape=None, name=name,
               res=res, res_spec=spec_o if res is not None else None)


def _mm_tn(a, b, *, name, tm=None, tn=None, ts=512, out_dtype=BF16):
    s, m = a.shape
    n = b.shape[1]
    tm = m if tm is None else tm
    tn = n if tn is None else tn
    ts = _tile(s, ts)
    return _mm(a, b, dims="tn", grid=(m // tm, n // tn, s // ts),
               a_spec=pl.BlockSpec((ts, tm), lambda i, j, kk: (kk, i)),
               b_spec=pl.BlockSpec((ts, tn), lambda i, j, kk: (kk, j)),
               o_spec=pl.BlockSpec((tm, tn), lambda i, j, kk: (i, j)),
               out_shape=(m, n), out_dtype=out_dtype, acc_shape=(tm, tn), name=name)


def _mm_up(xn, w_up, *, name, tm=512):
    s, d = xn.shape
    f = w_up.shape[1] // 2
    tm = _tile(s, tm)
    return _mm(xn, w_up, dims="nn", grid=(2, s // tm, 1),
               a_spec=pl.BlockSpec((tm, d), lambda j, i, kk: (i, 0)),
               b_spec=pl.BlockSpec((d, f), lambda j, i, kk: (0, j)),
               o_spec=pl.BlockSpec((None, tm, f), lambda j, i, kk: (j, i, 0)),
               out_shape=(2, s, f), out_dtype=F32, acc_shape=None, name=name)


def _mm_up_nt(dhh, w_up, *, name, tm=1024):
    _, s, f = dhh.shape
    d = w_up.shape[0]
    tm = _tile(s, tm)
    return _mm(dhh, w_up, dims="nt", grid=(s // tm, 1, 2),
               a_spec=pl.BlockSpec((None, tm, f), lambda i, j, kk: (kk, i, 0)),
               b_spec=pl.BlockSpec((d, f), lambda i, j, kk: (0, kk)),
               o_spec=pl.BlockSpec((tm, d), lambda i, j, kk: (i, 0)),
               out_shape=(s, d), out_dtype=F32, acc_shape=(tm, d), name=name)


def _mm_up_tn(xn, dhh, *, name, ts=512):
    s, d = xn.shape
    f = dhh.shape[2]
    tn = f // 2
    ts = _tile(s, ts)
    return _mm(xn, dhh, dims="tn", grid=(1, 4, s // ts),
               a_spec=pl.BlockSpec((ts, d), lambda i, j, kk: (kk, 0)),
               b_spec=pl.BlockSpec((None, ts, tn), lambda i, j, kk: (j // 2, kk, j % 2)),
               o_spec=pl.BlockSpec((d, tn), lambda i, j, kk: (0, j)),
               out_shape=(d, 2 * f), out_dtype=BF16, acc_shape=(d, tn), name=name)


def _rms_fwd(h, g, *, name, tr=512):
    s, d = h.shape
    tr = _tile(s, tr)

    def body(h_ref, g_ref, o_ref):
        x = h_ref[...]
        r = lax.rsqrt(jnp.mean(x * x, axis=-1, keepdims=True) + EPS)
        o_ref[...] = ((x * r) * g_ref[...]).astype(o_ref.dtype)

    row = pl.BlockSpec((tr, d), lambda i: (i, 0))
    return pl.pallas_call(
        body, grid=(s // tr,), in_specs=[row, pl.BlockSpec((1, d), lambda i: (0, 0))], out_specs=row,
        out_shape=jax.ShapeDtypeStruct((s, d), BF16), name=name, compiler_params=_params("parallel"),
    )(h, g.reshape(1, d))


def _rms_bwd(h, dxn, g, dres, *, name, tr=512):
    s, d = h.shape
    tr = _tile(s, tr)
    has_res = dres is not None

    def body(*refs):
        if has_res:
            h_ref, dxn_ref, g_ref, dres_ref, dh_ref, dhb_ref, dg_ref = refs
        else:
            h_ref, dxn_ref, g_ref, dh_ref, dhb_ref, dg_ref = refs
        i = pl.program_id(0)
        x = h_ref[...]
        dy = dxn_ref[...].astype(F32)
        r = lax.rsqrt(jnp.mean(x * x, axis=-1, keepdims=True) + EPS)
        a = dy * g_ref[...]
        m = jnp.mean(a * x, axis=-1, keepdims=True)
        dh = r * a - x * (r * r * r * m)
        if has_res:
            dh = dh + dres_ref[...]
        dh_ref[...] = dh
        dhb_ref[...] = dh.astype(BF16)
        part = jnp.sum(dy * (x * r), axis=0, keepdims=True)

        @pl.when(i == 0)
        def _():
            dg_ref[...] = part

        @pl.when(i > 0)
        def _():
            dg_ref[...] += part

    row = pl.BlockSpec((tr, d), lambda i: (i, 0))
    vec = pl.BlockSpec((1, d), lambda i: (0, 0))
    in_specs = [row, row, vec] + ([row] if has_res else [])
    args = (h, dxn, g.reshape(1, d)) + ((dres,) if has_res else ())
    return pl.pallas_call(
        body, grid=(s // tr,), in_specs=in_specs, out_specs=[row, row, vec],
        out_shape=[jax.ShapeDtypeStruct((s, d), F32), jax.ShapeDtypeStruct((s, d), BF16),
                   jax.ShapeDtypeStruct((1, d), F32)],
        name=name, compiler_params=_params("arbitrary"),
    )(*args)


def _loss_head(h, g, target, *, name, tr=512):
    s, d = h.shape
    tr = _tile(s, tr)
    nt = s // tr

    def body(h_ref, g_ref, t_ref, dh_ref, dhb_ref, dg_ref, loss_ref, sq_ref):
        i = pl.program_id(0)
        x = h_ref[...]
        gain = g_ref[...]
        r = lax.rsqrt(jnp.mean(x * x, axis=-1, keepdims=True) + EPS)
        xh = x * r
        err = xh * gain - t_ref[...]
        dy = err * (1.0 / d)
        a = dy * gain
        m = jnp.mean(a * x, axis=-1, keepdims=True)
        dh = r * a - x * (r * r * r * m)
        dh_ref[...] = dh
        dhb_ref[...] = dh.astype(BF16)
        dg_part = jnp.sum(dy * xh, axis=0, keepdims=True)
        sq_part = jnp.sum(err * err, axis=0, keepdims=True)

        @pl.when(i == 0)
        def _():
            dg_ref[...] = dg_part
            sq_ref[...] = sq_part

        @pl.when(i > 0)
        def _():
            dg_ref[...] += dg_part
            sq_ref[...] += sq_part

        @pl.when(i == nt - 1)
        def _():
            total = jnp.sum(sq_ref[...], axis=1, keepdims=True) * (0.5 / d)
            loss_ref[...] = jnp.broadcast_to(total, loss_ref.shape)

    row = pl.BlockSpec((tr, d), lambda i: (i, 0))
    vec = pl.BlockSpec((1, d), lambda i: (0, 0))
    return pl.pallas_call(
        body, grid=(nt,), in_specs=[row, vec, row],
        out_specs=[row, row, vec, pl.BlockSpec((1, 128), lambda i: (0, 0))],
        out_shape=[jax.ShapeDtypeStruct((s, d), F32), jax.ShapeDtypeStruct((s, d), BF16),
                   jax.ShapeDtypeStruct((1, d), F32), jax.ShapeDtypeStruct((1, 128), F32)],
        scratch_shapes=[pltpu.VMEM((1, d), F32)], name=name, compiler_params=_params("arbitrary"),
    )(h, g.reshape(1, d), target)


_SQRT_HALF = 0.7071067811865476
_INV_SQRT_2PI = 0.3989422804014327


def _gelu(x):
    return 0.5 * x * (1.0 + lax.erf(x * _SQRT_HALF))


def _gelu_grad(x):
    return 0.5 * (1.0 + lax.erf(x * _SQRT_HALF)) + x * (jnp.exp(-0.5 * x * x) * _INV_SQRT_2PI)


def _trailing_sums(xe, win):
    s = xe
    sh = 1
    while sh < win:
        s = s + pltpu.roll(s, sh, 0)
        sh *= 2
    return s


def _leading_sums(xe, win):
    n = xe.shape[0]
    s = xe
    sh = 1
    while sh < win:
        s = s + pltpu.roll(s, n - sh, 0)
        sh *= 2
    return s


def _tril_mask():
    return lax.broadcasted_iota(jnp.int32, (GROUP, GROUP), 0) >= lax.broadcasted_iota(jnp.int32, (GROUP, GROUP), 1)


def _layernorm_stats(v):
    mu = jnp.mean(v, axis=-1, keepdims=True)
    xc = v - mu
    rstd = lax.rsqrt(jnp.mean(xc * xc, axis=-1, keepdims=True) + EPS)
    return xc * rstd, rstd


def _mixer_specs(s, t):
    halo_blocks = t // POOL_HALO
    tile = lambda w: pl.BlockSpec((t, w), lambda i: (i, 0))
    prev = pl.BlockSpec((POOL_HALO, POOL_WIDTH), lambda i: (jnp.maximum(i * halo_blocks - 1, 0), 0))
    nxt = pl.BlockSpec((POOL_HALO, POOL_WIDTH),
                       lambda i: (jnp.minimum((i + 1) * halo_blocks, s // POOL_HALO - 1), 0))
    const3 = pl.BlockSpec((HEADS, GROUP, GROUP), lambda i: (0, 0, 0))
    vec = pl.BlockSpec((1, POOL_WIDTH), lambda i: (0, 0))
    bias = pl.BlockSpec((GROUP, SGU_WIDTH), lambda i: (0, 0))
    return tile, prev, nxt, const3, vec, bias


def _mixer_fwd(proj, pool_w, pool_scale, sgu_g, sgu_w, sgu_bias, *, name, t=256):
    s = proj.shape[0]
    t = _tile(s, t)
    tile, prev, _, const3, vec, bias = _mixer_specs(s, t)

    def body(proj_ref, halo_ref, pw_ref, ps_ref, sg_ref, sw_ref, sb_ref, cat_ref):
        i = pl.program_id(0)
        row = i * t + lax.broadcasted_iota(jnp.int32, (t, 1), 0)
        p = proj_ref[:, 0:POOL_WIDTH]
        pe = jnp.concatenate([jnp.where(i > 0, halo_ref[...], 0.0), p], axis=0)
        for gi, win in enumerate(POOL_WINDOWS):
            cols = slice(gi * GROUP, (gi + 1) * GROUP)
            count = jnp.minimum(row + 1, win).astype(F32)
            d = _trailing_sums(pe[:, cols], win)[POOL_HALO:] / count - p[:, cols]
            y = jnp.dot(d.astype(BF16), pw_ref[gi].astype(BF16), preferred_element_type=F32) * ps_ref[:, cols]
            cat_ref[:, cols] = y.astype(BF16)

        u = _gelu(proj_ref[:, POOL_WIDTH:POOL_WIDTH + SGU_WIDTH])
        xhat, _ = _layernorm_stats(_gelu(proj_ref[:, POOL_WIDTH + SGU_WIDTH:]))
        vn = (xhat * sg_ref[...]).astype(BF16)
        tri = _tril_mask()
        for h in range(HEADS):
            cols = slice(h * GROUP, (h + 1) * GROUP)
            w = jnp.where(tri, sw_ref[h], 0.0).astype(BF16)
            for c in range(t // GROUP):
                rows = slice(c * GROUP, (c + 1) * GROUP)
                z = jnp.dot(w, vn[rows, cols], preferred_element_type=F32) + sb_ref[:, cols]
                cat_ref[rows, POOL_WIDTH + h * GROUP:POOL_WIDTH + (h + 1) * GROUP] = (u[rows, cols] * z).astype(BF16)

    return pl.pallas_call(
        body, grid=(s // t,),
        in_specs=[tile(POOL_WIDTH + 2 * SGU_WIDTH), prev, const3, vec, vec, const3, bias],
        out_specs=tile(POOL_WIDTH + SGU_WIDTH),
        out_shape=jax.ShapeDtypeStruct((s, POOL_WIDTH + SGU_WIDTH), BF16), name=name,
        compiler_params=_params("parallel"),
    )(proj, proj, pool_w, pool_scale, sgu_g, sgu_w, sgu_bias)


def _mixer_bwd(proj, dcat, pool_w, pool_scale, sgu_g, sgu_w, sgu_bias, *, name, t=256):
    s = proj.shape[0]
    t = _tile(s, t)
    nt = s // t
    tile, prev, nxt, const3, vec, bias = _mixer_specs(s, t)

    def body(proj_ref, halo_ref, dcat_ref, dnext_ref, pw_ref, ps_ref, sg_ref, sw_ref, sb_ref,
             dproj_ref, dpw_ref, dps_ref, dsg_ref, dsw_ref, dsb_ref, du_ref, dvn_ref, dz_ref):
        i = pl.program_id(0)

        @pl.when(i == 0)
        def _():
            dpw_ref[...] = jnp.zeros_like(dpw_ref)
            dps_ref[...] = jnp.zeros_like(dps_ref)
            dsg_ref[...] = jnp.zeros_like(dsg_ref)
            dsw_ref[...] = jnp.zeros_like(dsw_ref)
            dz_ref[...] = jnp.zeros_like(dz_ref)

        row = i * t + lax.broadcasted_iota(jnp.int32, (t, 1), 0)
        row_e = i * t + lax.broadcasted_iota(jnp.int32, (t + POOL_HALO, 1), 0)
        p = proj_ref[:, 0:POOL_WIDTH]
        pe = jnp.concatenate([jnp.where(i > 0, halo_ref[...], 0.0), p], axis=0)
        dyp = dcat_ref[:, 0:POOL_WIDTH]
        dye = jnp.concatenate([dyp, jnp.where(i < nt - 1, dnext_ref[...], 0.0)], axis=0)
        for gi, win in enumerate(POOL_WINDOWS):
            cols = slice(gi * GROUP, (gi + 1) * GROUP)
            count = jnp.minimum(row + 1, win).astype(F32)
            d = (_trailing_sums(pe[:, cols], win)[POOL_HALO:] / count - p[:, cols]).astype(BF16)
            pw = pw_ref[gi].astype(BF16)
            pre = jnp.dot(d, pw, preferred_element_type=F32)
            dps_ref[:, cols] += jnp.sum(dyp[:, cols] * pre, axis=0, keepdims=True)
            ys = (dye[:, cols] * ps_ref[:, cols]).astype(BF16)
            dpw_ref[gi] += lax.dot_general(d, ys[:t], _DOT_DIMS["tn"], preferred_element_type=F32)
            dd = lax.dot_general(ys, pw, _DOT_DIMS["nt"], preferred_element_type=F32)
            count_e = jnp.minimum(row_e + 1, win).astype(F32)
            dp = _leading_sums(dd / count_e, win)[:t] - dd[:t]
            dproj_ref[:, cols] = dp.astype(BF16)

        xu = proj_ref[:, POOL_WIDTH:POOL_WIDTH + SGU_WIDTH]
        xv = proj_ref[:, POOL_WIDTH + SGU_WIDTH:]
        u = _gelu(xu)
        xhat, rstd = _layernorm_stats(_gelu(xv))
        gain = sg_ref[...]
        vn = (xhat * gain).astype(BF16)
        tri = _tril_mask()
        for h in range(HEADS):
            cols = slice(h * GROUP, (h + 1) * GROUP)
            wf = jnp.where(tri, sw_ref[h], 0.0)
            w, wt = wf.astype(BF16), wf.T.astype(BF16)
            for c in range(t // GROUP):
                rows = slice(c * GROUP, (c + 1) * GROUP)
                vch = vn[rows, cols]
                z = jnp.dot(w, vch, preferred_element_type=F32) + sb_ref[:, cols]
                dy = dcat_ref[rows, POOL_WIDTH + h * GROUP:POOL_WIDTH + (h + 1) * GROUP]
                du_ref[rows, cols] = dy * z
                dz = dy * u[rows, cols]
                dz_ref[:, cols] += dz
                dzb = dz.astype(BF16)
                dsw_ref[h] += lax.dot_general(dzb, vch, _DOT_DIMS["nt"], preferred_element_type=F32)
                dvn_ref[rows, cols] = jnp.dot(wt, dzb, preferred_element_type=F32)
        dvn = dvn_ref[...]
        dsg_ref[...] += jnp.sum(dvn * xhat, axis=0, keepdims=True)
        dxh = dvn * gain
        dv = rstd * (dxh - jnp.mean(dxh, axis=-1, keepdims=True)
                     - xhat * jnp.mean(dxh * xhat, axis=-1, keepdims=True))
        dproj_ref[:, POOL_WIDTH:POOL_WIDTH + SGU_WIDTH] = (du_ref[...] * _gelu_grad(xu)).astype(BF16)
        dproj_ref[:, POOL_WIDTH + SGU_WIDTH:] = (dv * _gelu_grad(xv)).astype(BF16)

        @pl.when(i == nt - 1)
        def _():
            for h in range(HEADS):
                dsw_ref[h] = jnp.where(tri, dsw_ref[h], 0.0)
            lane = lax.broadcasted_iota(jnp.int32, (GROUP, GROUP), 1)
            out = jnp.zeros((GROUP, GROUP), F32)
            for h in range(HEADS):
                sh = jnp.sum(dz_ref[:, h * GROUP:(h + 1) * GROUP], axis=1, keepdims=True)
                out = jnp.where(lane == h, sh, out)
            dsb_ref[...] = out

    outs = pl.pallas_call(
        body, grid=(nt,),
        in_specs=[tile(POOL_WIDTH + 2 * SGU_WIDTH), prev, tile(POOL_WIDTH + SGU_WIDTH), nxt,
                  const3, vec, vec, const3, bias],
        out_specs=[tile(POOL_WIDTH + 2 * SGU_WIDTH), const3, vec, vec, const3,
                   pl.BlockSpec((GROUP, GROUP), lambda i: (0, 0))],
        out_shape=[jax.ShapeDtypeStruct((s, POOL_WIDTH + 2 * SGU_WIDTH), BF16),
                   jax.ShapeDtypeStruct((HEADS, GROUP, GROUP), F32),
                   jax.ShapeDtypeStruct((1, POOL_WIDTH), F32),
                   jax.ShapeDtypeStruct((1, SGU_WIDTH), F32),
                   jax.ShapeDtypeStruct((HEADS, GROUP, GROUP), F32),
                   jax.ShapeDtypeStruct((GROUP, GROUP), F32)],
        scratch_shapes=[pltpu.VMEM((t, SGU_WIDTH), F32), pltpu.VMEM((t, SGU_WIDTH), F32),
                        pltpu.VMEM((GROUP, SGU_WIDTH), F32)],
        name=name, compiler_params=_params("arbitrary"),
    )(proj, proj, dcat, dcat, pool_w, pool_scale, sgu_g, sgu_w, sgu_bias)
    dproj, dpw, dps, dsg, dsw, dsb = outs
    return dproj, dpw, dps, dsg, dsw, dsb[:, :HEADS].T


def _attn_probs(q, k, scale):
    sc = lax.dot_general(q, k, _DOT_DIMS["nt"], preferred_element_type=F32) * scale
    sc = sc - jnp.max(sc, axis=-1, keepdims=True)
    e = jnp.exp(sc)
    return e / jnp.sum(e, axis=-1, keepdims=True)


def _attn_fwd(q, k, v, *, name, t=512):
    s, d = q.shape
    nm = k.shape[0]
    t = _tile(s, t)
    scale = HEAD_DIM ** -0.5

    def body(q_ref, k_ref, v_ref, o_ref):
        for h in range(HEADS):
            cols = slice(h * HEAD_DIM, (h + 1) * HEAD_DIM)
            pr = _attn_probs(q_ref[:, cols], k_ref[:, cols], scale)
            o_ref[:, cols] = jnp.dot(pr.astype(BF16), v_ref[:, cols], preferred_element_type=F32).astype(BF16)

    row = pl.BlockSpec((t, d), lambda i: (i, 0))
    kv = pl.BlockSpec((nm, d), lambda i: (0, 0))
    return pl.pallas_call(
        body, grid=(s // t,), in_specs=[row, kv, kv], out_specs=row,
        out_shape=jax.ShapeDtypeStruct((s, d), BF16), name=name, compiler_params=_params("parallel"),
    )(q, k, v)


def _attn_bwd(q, k, v, do, *, name, t=512):
    s, d = q.shape
    nm = k.shape[0]
    t = _tile(s, t)
    scale = HEAD_DIM ** -0.5

    def body(q_ref, k_ref, v_ref, do_ref, dq_ref, dk_ref, dv_ref):
        i = pl.program_id(0)

        @pl.when(i == 0)
        def _():
            dk_ref[...] = jnp.zeros_like(dk_ref)
            dv_ref[...] = jnp.zeros_like(dv_ref)

        for h in range(HEADS):
            cols = slice(h * HEAD_DIM, (h + 1) * HEAD_DIM)
            qh, kh, vh, doh = q_ref[:, cols], k_ref[:, cols], v_ref[:, cols], do_ref[:, cols]
            pr = _attn_probs(qh, kh, scale)
            dpr = lax.dot_general(doh, vh, _DOT_DIMS["nt"], preferred_element_type=F32)
            ds = (pr * (dpr - jnp.sum(dpr * pr, axis=-1, keepdims=True)) * scale).astype(BF16)
            dv_ref[:, cols] += lax.dot_general(pr.astype(BF16), doh, _DOT_DIMS["tn"], preferred_element_type=F32)
            dk_ref[:, cols] += lax.dot_general(ds, qh, _DOT_DIMS["tn"], preferred_element_type=F32)
            dq_ref[:, cols] = jnp.dot(ds, kh, preferred_element_type=F32).astype(BF16)

    row = pl.BlockSpec((t, d), lambda i: (i, 0))
    kv = pl.BlockSpec((nm, d), lambda i: (0, 0))
    return pl.pallas_call(
        body, grid=(s // t,), in_specs=[row, kv, kv, row], out_specs=[row, kv, kv],
        out_shape=[jax.ShapeDtypeStruct((s, d), BF16), jax.ShapeDtypeStruct((nm, d), F32),
                   jax.ShapeDtypeStruct((nm, d), F32)],
        name=name, compiler_params=_params("arbitrary"),
    )(q, k, v, do)


def _conv_specs(s, f, t, tc, swap):
    hb = t // CONV_HALO
    order = (lambda fn: (lambda j, i: fn(i, j))) if swap else (lambda fn: fn)
    tile3 = pl.BlockSpec((2, t, tc), order(lambda i, j: (0, i, j)))
    prev3 = pl.BlockSpec((2, CONV_HALO, tc), order(lambda i, j: (0, jnp.maximum(i * hb - 1, 0), j)))
    next3 = pl.BlockSpec((2, CONV_HALO, tc),
                         order(lambda i, j: (0, jnp.minimum((i + 1) * hb, s // CONV_HALO - 1), j)))
    tile2 = pl.BlockSpec((t, tc), order(lambda i, j: (i, j)))
    next2 = pl.BlockSpec((CONV_HALO, tc), order(lambda i, j: (jnp.minimum((i + 1) * hb, s // CONV_HALO - 1), j)))
    wspec = pl.BlockSpec((2, 3, tc), order(lambda i, j: (0, 0, j)))
    bspec = pl.BlockSpec((2, 1, tc), order(lambda i, j: (0, 0, j)))
    return tile3, prev3, next3, tile2, next2, wspec, bspec


def _conv3(w_ref, p, x2, x1, x0, b):
    return (w_ref[p, 0:1, :] * x2 + w_ref[p, 1:2, :] * x1 + w_ref[p, 2:3, :] * x0) + b


def _convgate_fwd(hh, cw, cb, *, name, t=256, tc=1408):
    _, s, f = hh.shape
    t = _tile(s, t)
    tile3, prev3, _, tile2, _, wspec, bspec = _conv_specs(s, f, t, tc, swap=False)

    def body(hh_ref, prev_ref, cw_ref, cb_ref, act_ref):
        i = pl.program_id(0)
        hc = []
        for p in range(2):
            xe = jnp.concatenate([jnp.where(i > 0, prev_ref[p], 0.0), hh_ref[p]], axis=0)
            hc.append(_conv3(cw_ref, p, pltpu.roll(xe, 2, 0), pltpu.roll(xe, 1, 0), xe, cb_ref[p])[CONV_HALO:])
        gate, val = hc
        act_ref[...] = ((gate * jax.nn.sigmoid(gate)) * val).astype(BF16)

    return pl.pallas_call(
        body, grid=(s // t, f // tc), in_specs=[tile3, prev3, wspec, bspec], out_specs=tile2,
        out_shape=jax.ShapeDtypeStruct((s, f), BF16), name=name, compiler_params=_params("parallel", "parallel"),
    )(hh, hh, cw, cb)


def _convgate_bwd(hh, dact, cw, cb, *, name, t=128, tc=1408):
    _, s, f = hh.shape
    t = _tile(s, t)
    nt = s // t
    tile3, prev3, next3, tile2, next2, wspec, bspec = _conv_specs(s, f, t, tc, swap=True)

    def body(hh_ref, prev_ref, next_ref, da_ref, danext_ref, cw_ref, cb_ref, dhh_ref, dcw_ref, dcb_ref):
        i = pl.program_id(1)
        is_last = i == nt - 1

        @pl.when(i == 0)
        def _():
            dcw_ref[...] = jnp.zeros_like(dcw_ref)
            dcb_ref[...] = jnp.zeros_like(dcb_ref)

        taps, hc = [], []
        for p in range(2):
            xe = jnp.concatenate([jnp.where(i > 0, prev_ref[p], 0.0), hh_ref[p],
                                  jnp.where(is_last, 0.0, next_ref[p])], axis=0)
            x2, x1 = pltpu.roll(xe, 2, 0), pltpu.roll(xe, 1, 0)
            hc.append(_conv3(cw_ref, p, x2, x1, xe, cb_ref[p])[CONV_HALO:])
            taps.append((x2[CONV_HALO:CONV_HALO + t], x1[CONV_HALO:CONV_HALO + t], xe[CONV_HALO:CONV_HALO + t]))
        gate, val = hc
        dae = jnp.concatenate([da_ref[...], jnp.where(is_last, 0.0, danext_ref[...])], axis=0)
        sg = jax.nn.sigmoid(gate)
        dval = dae * (gate * sg)
        dgate = dae * val * (sg * (1.0 + gate * (1.0 - sg)))
        m = t + CONV_HALO
        for p, dhc in enumerate((dgate, dval)):
            dh = (cw_ref[p, 2:3, :] * dhc + cw_ref[p, 1:2, :] * pltpu.roll(dhc, m - 1, 0)
                  + cw_ref[p, 0:1, :] * pltpu.roll(dhc, m - 2, 0))
            dhh_ref[p] = dh[:t].astype(BF16)
            d0 = dhc[:t]
            for kk, tap in enumerate(taps[p]):
                dcw_ref[p, kk:kk + 1, :] += jnp.sum(d0 * tap, axis=0, keepdims=True)
            dcb_ref[p] += jnp.sum(d0, axis=0, keepdims=True)

    return pl.pallas_call(
        body, grid=(f // tc, nt), in_specs=[tile3, prev3, next3, tile2, next2, wspec, bspec],
        out_specs=[tile3, wspec, bspec],
        out_shape=[jax.ShapeDtypeStruct((2, s, f), BF16), jax.ShapeDtypeStruct((2, 3, f), F32),
                   jax.ShapeDtypeStruct((2, 1, f), F32)],
        name=name, compiler_params=_params("parallel", "arbitrary"),
    )(hh, hh, hh, dact, dact, cw, cb)


def _position():
    return lax.axis_index("x"), lax.axis_index("y"), lax.axis_index("c")


def _linear(px, py, pc):
    return 4 * px + 2 * py + pc


_ANY = pl.BlockSpec(memory_space=pl.ANY)


def _all_gather(shards, *, name):
    n = len(shards)

    def body(*refs):
        x_refs, o_refs = refs[:n], refs[n:2 * n]
        send_sems, recv_sems, local_sems = refs[2 * n:]
        x, y, c = _position()
        me, sibling = (x, y, c), (x, y, 1 - c)
        chips = [(1 - x, y), (x, 1 - y), (1 - x, 1 - y)]

        def copy(ti, k, block, to, src=None):
            dst = o_refs[ti].at[_linear(*block)]
            return pltpu.make_async_remote_copy(
                src_ref=dst if src is None else src, dst_ref=dst,
                send_sem=send_sems.at[ti, k], recv_sem=recv_sems.at[ti, k],
                device_id=to, device_id_type=MESH)

        mine = [pltpu.make_async_copy(x_refs[ti], o_refs[ti].at[_linear(*me)], local_sems.at[ti]) for ti in range(n)]
        for cp in mine:
            cp.start()
        first = []
        for ti in range(n):
            first.append(copy(ti, 0, me, sibling, src=x_refs[ti]))
            for j, chip in enumerate(chips):
                first.append(copy(ti, 1 + j, me, (*chip, c), src=x_refs[ti]))
        for cp in first:
            cp.start()
        passed = []
        for j, chip in enumerate(chips):
            for ti in range(n):
                copy(ti, 1 + j, (*chip, c), me).wait_recv()
                fwd = copy(ti, 4 + j, (*chip, c), sibling)
                fwd.start()
                passed.append(fwd)
        for ti in range(n):
            copy(ti, 0, sibling, me).wait_recv()
            for j, chip in enumerate(chips):
                copy(ti, 4 + j, (*chip, 1 - c), me).wait_recv()
        for cp in first + passed:
            cp.wait_send()
        for cp in mine:
            cp.wait()

    return pl.pallas_call(
        body, in_specs=[_ANY] * n, out_specs=[_ANY] * n,
        out_shape=[jax.ShapeDtypeStruct((N_DEV,) + a.shape, a.dtype) for a in shards],
        scratch_shapes=[pltpu.SemaphoreType.DMA((n, 7)), pltpu.SemaphoreType.DMA((n, 7)),
                        pltpu.SemaphoreType.DMA((n,))],
        name=name,
    )(*shards)


def _peers_of(x, y, c):
    peers = []
    for mask in range(1, N_DEV):
        peers.append((1 - x if mask & 4 else x, 1 - y if mask & 2 else y, 1 - c if mask & 1 else c))
    return peers


_HBM = pl.BlockSpec(memory_space=pltpu.HBM)
_SEM = pl.BlockSpec(memory_space=pltpu.SEMAPHORE)
_EFFECT = pltpu.SideEffectType.DATAFLOW_SIDE_EFFECTING


def _exchange_copy(src_ref, land_ref, send_sem, recv_sem, peer, mine, scatter, arriving):
    src = src_ref.at[_linear(*peer)] if scatter else src_ref
    dst = land_ref.at[_linear(*peer) if arriving else mine]
    return pltpu.make_async_remote_copy(src_ref=src, dst_ref=dst, send_sem=send_sem, recv_sem=recv_sem,
                                        device_id=peer, device_id_type=MESH)


def _exchange_start(srcs, groups, *, scatter, name):
    n = len(srcs)
    ng = len(groups)
    lands = [lax.empty(a.shape if scatter else (N_DEV,) + a.shape, a.dtype) for a in srcs]

    def body(*refs):
        src_refs, land_refs = refs[:n], refs[n:2 * n]
        sem_refs = refs[2 * n:2 * n + 2 * ng]
        token_ref = refs[2 * n + 2 * ng + 2 * n]
        local_sems = refs[-1]
        x, y, c = _position()
        mine = _linear(x, y, c)
        peers = _peers_of(x, y, c)
        for gi, group in enumerate(groups):
            for pos, t in enumerate(group):
                for k, peer in enumerate(peers):
                    _exchange_copy(src_refs[t], land_refs[t], sem_refs[2 * gi].at[7 * pos + k],
                                   sem_refs[2 * gi + 1].at[7 * pos + k], peer, mine, scatter, arriving=False).start()
        local = [pltpu.make_async_copy(src_refs[t].at[mine] if scatter else src_refs[t], land_refs[t].at[mine],
                                       local_sems.at[t]) for t in range(n)]
        for cp in local:
            cp.start()
        for cp in local:
            cp.wait()
        token_ref[...] = jnp.zeros_like(token_ref)

    sem_shapes = []
    for group in groups:
        sem_shapes += [pltpu.SemaphoreType.DMA((7 * len(group),))] * 2
    outs = pl.pallas_call(
        body, name=name,
        out_shape=tuple(sem_shapes) + tuple(pltpu.HBM(a.shape, a.dtype) for a in srcs)
        + tuple(pltpu.HBM(a.shape, a.dtype) for a in lands) + (jax.ShapeDtypeStruct((8, 128), F32),),
        in_specs=[_HBM] * (2 * n),
        out_specs=(_SEM,) * (2 * ng) + (_HBM,) * (2 * n) + (pl.BlockSpec(memory_space=pltpu.VMEM),),
        input_output_aliases={i: 2 * ng + i for i in range(2 * n)},
        scratch_shapes=[pltpu.SemaphoreType.DMA((n,))],
        compiler_params=pltpu.CompilerParams(has_side_effects=_EFFECT),
    )(*[pltpu.with_memory_space_constraint(a, pltpu.HBM) for a in list(srcs) + lands])
    sems = [(outs[2 * gi], outs[2 * gi + 1]) for gi in range(ng)]
    return sems, list(outs[2 * ng:2 * ng + n]), list(outs[2 * ng + n:2 * ng + 2 * n]), outs[-1]


def _exchange_wait(sems, srcs, lands, after, *, scatter, name):
    n = len(srcs)
    send_sems, recv_sems = sems

    def body(*refs):
        src_refs, land_refs = refs[:n], refs[n:2 * n]
        send_ref, recv_ref = refs[2 * n], refs[2 * n + 1]
        x, y, c = _position()
        mine = _linear(x, y, c)
        for pos in range(n):
            for k, peer in enumerate(_peers_of(x, y, c)):
                cp = _exchange_copy(src_refs[pos], land_refs[pos], send_ref.at[7 * pos + k], recv_ref.at[7 * pos + k],
                                    peer, mine, scatter, arriving=True)
                cp.wait_send()
                cp.wait_recv()

    outs = pl.pallas_call(
        body, name=name,
        out_shape=tuple(pltpu.HBM(a.shape, a.dtype) for a in list(srcs) + list(lands)),
        in_specs=[_HBM] * (2 * n) + [_SEM, _SEM, _ANY], out_specs=(_HBM,) * (2 * n),
        input_output_aliases={i: i for i in range(2 * n)},
        compiler_params=pltpu.CompilerParams(has_side_effects=_EFFECT),
    )(*srcs, *lands, send_sems, recv_sems, after)
    return list(outs[n:])


def _adamw_math(g, w, m, v):
    m2 = ADAM_B1 * m + (1.0 - ADAM_B1) * g
    v2 = ADAM_B2 * v + (1.0 - ADAM_B2) * (g * g)
    m_hat = m2 / (1.0 - ADAM_B1 ** ADAM_STEP)
    v_hat = v2 / (1.0 - ADAM_B2 ** ADAM_STEP)
    delta = -ADAM_LR * (m_hat / (jnp.sqrt(v_hat) + ADAM_EPS) + ADAM_WD * w)
    return delta, m2, v2


def _adamw(slots, w, m, v, *, name, tr=256):
    depth = len(slots)
    _, r, c = slots[0].shape
    tr = next((cand for cand in range(min(r, tr), 15, -1) if r % cand == 0 and cand % 16 == 0), r)

    def body(*refs):
        s_refs = refs[:depth]
        w_ref, m_ref, v_ref, g_ref, d_ref, m2_ref, v2_ref = refs[depth:]
        layer = pl.program_id(0)
        for l in range(depth):
            @pl.when(layer == l)
            def _():
                g = s_refs[l][0].astype(F32)
                for d in range(1, N_DEV):
                    g = g + s_refs[l][d].astype(F32)
                delta, m2, v2 = _adamw_math(g, w_ref[...], m_ref[...], v_ref[...])
                g_ref[...] = g
                d_ref[...] = delta
                m2_ref[...] = m2
                v2_ref[...] = v2

    blk = pl.BlockSpec((None, tr, c), lambda layer, i: (layer, i, 0))
    sblks = [pl.BlockSpec((N_DEV, tr, c), lambda layer, i, l=l: (0, jnp.where(layer == l, i, 0), 0))
             for l in range(depth)]
    shape = jax.ShapeDtypeStruct((depth, r, c), F32)
    return pl.pallas_call(
        body, grid=(depth, r // tr), in_specs=sblks + [blk, blk, blk], out_specs=[blk] * 4,
        out_shape=[shape] * 4, name=name, compiler_params=_params("arbitrary", "arbitrary"),
    )(*slots, w, m, v)


_SHARDED = ("w_in", "w_out", "wq", "wk", "wv", "wo", "w_up", "conv_w", "w_down")
_SMALL = ("norm_mix_g", "pool_w", "pool_scale", "sgu_g", "sgu_w", "sgu_b", "norm_xattn_g", "mem_norm_g",
          "norm_ffn_g", "conv_b", "final_norm_g")
_WEIGHTS = ("norm_mix_g", "w_in", "pool_w", "pool_scale", "sgu_g", "sgu_w", "sgu_b", "w_out", "norm_xattn_g",
            "mem_norm_g", "wq", "wk", "wv", "wo", "norm_ffn_g", "w_up", "conv_w", "conv_b", "w_down",
            "final_norm_g")
_PACK_LANES = 128
_GATHER_GROUPS = (("w_in",), ("w_out",), ("wq", "wk", "wv", "wo"), ("w_up", "conv_w", "w_down"))


def _cols_to_blocks(a):
    r, c8 = a.shape
    return a.reshape(r, N_DEV, c8 // N_DEV).transpose(1, 0, 2)


def _blocks_to_cols(a):
    n, r, c = a.shape
    return a.transpose(1, 0, 2).reshape(r, n * c)


def _pack(arrays):
    flat = jnp.concatenate([a.reshape(-1) for a in arrays])
    assert flat.shape[0] % (8 * _PACK_LANES) == 0
    return flat.reshape(-1, _PACK_LANES)


def _unpack(packed, like):
    flat = packed.reshape(-1)
    out, off = [], 0
    for a in like:
        out.append(flat[off:off + a.size].reshape(a.shape))
        off += a.size
    return out


def kernel(x, mem, norm_mix_g, w_in, pool_w, pool_scale, sgu_g, sgu_w, sgu_b, w_out, norm_xattn_g, mem_norm_g, wq, wk, wv, wo, norm_ffn_g, w_up, conv_w, conv_b, w_down, final_norm_g, loss_target, m_norm_mix_g, m_w_in, m_pool_w, m_pool_scale, m_sgu_g, m_sgu_w, m_sgu_b, m_w_out, m_norm_xattn_g, m_mem_norm_g, m_wq, m_wk, m_wv, m_wo, m_norm_ffn_g, m_w_up, m_conv_w, m_conv_b, m_w_down, m_final_norm_g, v_norm_mix_g, v_w_in, v_pool_w, v_pool_scale, v_sgu_g, v_sgu_w, v_sgu_b, v_w_out, v_norm_xattn_g, v_mem_norm_g, v_wq, v_wk, v_wv, v_wo, v_norm_ffn_g, v_w_up, v_conv_w, v_conv_b, v_w_down, v_final_norm_g):
    W = dict(norm_mix_g=norm_mix_g, w_in=w_in, pool_w=pool_w, pool_scale=pool_scale, sgu_g=sgu_g, sgu_w=sgu_w,
             sgu_b=sgu_b, w_out=w_out, norm_xattn_g=norm_xattn_g, mem_norm_g=mem_norm_g, wq=wq, wk=wk, wv=wv, wo=wo,
             norm_ffn_g=norm_ffn_g, w_up=w_up, conv_w=conv_w, conv_b=conv_b, w_down=w_down,
             final_norm_g=final_norm_g)
    M = dict(norm_mix_g=m_norm_mix_g, w_in=m_w_in, pool_w=m_pool_w, pool_scale=m_pool_scale, sgu_g=m_sgu_g,
             sgu_w=m_sgu_w, sgu_b=m_sgu_b, w_out=m_w_out, norm_xattn_g=m_norm_xattn_g, mem_norm_g=m_mem_norm_g,
             wq=m_wq, wk=m_wk, wv=m_wv, wo=m_wo, norm_ffn_g=m_norm_ffn_g, w_up=m_w_up, conv_w=m_conv_w,
             conv_b=m_conv_b, w_down=m_w_down, final_norm_g=m_final_norm_g)
    V = dict(norm_mix_g=v_norm_mix_g, w_in=v_w_in, pool_w=v_pool_w, pool_scale=v_pool_scale, sgu_g=v_sgu_g,
             sgu_w=v_sgu_w, sgu_b=v_sgu_b, w_out=v_w_out, norm_xattn_g=v_norm_xattn_g, mem_norm_g=v_mem_norm_g,
             wq=v_wq, wk=v_wk, wv=v_wv, wo=v_wo, norm_ffn_g=v_norm_ffn_g, w_up=v_w_up, conv_w=v_conv_w,
             conv_b=v_conv_b, w_down=v_w_down, final_norm_g=v_final_norm_g)

    s, d = x.shape[1], x.shape[2]
    f = w_down.shape[1] * N_DEV
    h = x.reshape(s, d)
    memx = mem.reshape(mem.shape[1], d)
    target = loss_target.reshape(s, d)

    shards, groups = [], []
    for l in range(DEPTH):
        for names in _GATHER_GROUPS:
            groups.append(list(range(len(shards), len(shards) + len(names))))
            for nme in names:
                a = W[nme][l]
                shards.append(a if nme == "conv_w" else a.astype(BF16))
    gather_sems, shard_thru, land_thru, token = _exchange_start(shards, groups, scatter=False, name="gather_start")

    def fetch(l, gi, after):
        idx = groups[l * len(_GATHER_GROUPS) + gi]
        lands = _exchange_wait(gather_sems[l * len(_GATHER_GROUPS) + gi], [shard_thru[t] for t in idx],
                               [land_thru[t] for t in idx], after, scatter=False, name=f"gather_wait_{l}_{gi}")
        return dict(zip(_GATHER_GROUPS[gi], lands))

    saved, full = [], []
    for l in range(DEPTH):
        sgu_bias = jnp.repeat(sgu_b[l].T, GROUP, axis=1)
        cb = conv_b[l].reshape(2, 1, f)
        xn1 = _rms_fwd(h, norm_mix_g[l] + token[0, 0] if l == 0 else norm_mix_g[l], name=f"norm_mix_{l}")
        w_in_f = _blocks_to_cols(fetch(l, 0, xn1)["w_in"])
        proj = _mm_nn(xn1, w_in_f, out_dtype=F32, name=f"proj_in_{l}")
        cat = _mixer_fwd(proj, pool_w[l], pool_scale[l].reshape(1, -1), sgu_g[l].reshape(1, -1), sgu_w[l], sgu_bias,
                         name=f"mixer_{l}")
        w_out_f = fetch(l, 1, cat)["w_out"].reshape(-1, d)
        h1 = _mm_nn(cat, w_out_f, out_dtype=F32, res=h, name=f"proj_out_{l}")
        xn2 = _rms_fwd(h1, norm_xattn_g[l], name=f"norm_xattn_{l}")
        g = fetch(l, 2, xn2)
        wq_f, wk_f, wv_f, wo_f = (g[nme].reshape(-1, d) for nme in ("wq", "wk", "wv", "wo"))
        q = _mm_nn(xn2, wq_f, out_dtype=BF16, name=f"q_{l}")
        memn = _rms_fwd(memx, mem_norm_g[l], name=f"norm_mem_{l}")
        k = _mm_nn(memn, wk_f, out_dtype=BF16, name=f"k_{l}")
        v = _mm_nn(memn, wv_f, out_dtype=BF16, name=f"v_{l}")
        o = _attn_fwd(q, k, v, name=f"attn_{l}")
        h2 = _mm_nn(o, wo_f, out_dtype=F32, res=h1, name=f"attn_out_{l}")
        xn3 = _rms_fwd(h2, norm_ffn_g[l], name=f"norm_ffn_{l}")
        g = fetch(l, 3, xn3)
        w_up_f = _blocks_to_cols(g["w_up"])
        conv_w_f = _blocks_to_cols(g["conv_w"]).reshape(3, 2, f).transpose(1, 0, 2)
        w_down_f = g["w_down"].reshape(-1, d)
        hh = _mm_up(xn3, w_up_f, name=f"ffn_up_{l}")
        act = _convgate_fwd(hh, conv_w_f, cb, name=f"convgate_{l}")
        h3 = _mm_nn(act, w_down_f, out_dtype=F32, res=h2, tm=512, name=f"ffn_down_{l}")
        full.append(dict(w_in=w_in_f, w_out=w_out_f, wq=wq_f, wk=wk_f, wv=wv_f, wo=wo_f, w_up=w_up_f,
                         conv_w=conv_w_f, w_down=w_down_f))
        saved.append(dict(h0=h, xn1=xn1, proj=proj, cat=cat, h1=h1, xn2=xn2, q=q, memn=memn, k=k, v=v, o=o, h2=h2,
                          xn3=xn3, hh=hh, act=act, sgu_bias=sgu_bias, cb=cb))
        h = h3

    dh, dhb, dg_final, loss_row = _loss_head(h, final_norm_g, target, name="loss_head")
    loss = lax.psum(loss_row[0, 0], ("x", "y", "c"))

    pending = []
    small = [None] * DEPTH

    def scatter(l, tag, names, parts):
        sems, srcs, lands, tok = _exchange_start(parts, [list(range(len(parts)))], scatter=True,
                                                 name=f"scatter_start_{tag}_{l}")
        pending.append((l, tag, names, sems[0], srcs, lands))
        return tok[0, 0]

    for l in reversed(range(DEPTH)):
        fw, sv = full[l], saved[l]
        dact = _mm_nt(dhb, fw["w_down"], out_dtype=F32, tm=512, name=f"d_act_{l}")
        g_w_down = _mm_tn(sv["act"], dhb, tm=f // 2, name=f"g_w_down_{l}")
        dhh, g_conv_w, g_conv_b = _convgate_bwd(sv["hh"], dact, fw["conv_w"], sv["cb"], name=f"d_convgate_{l}")
        dxn3 = _mm_up_nt(dhh, fw["w_up"], name=f"d_xn_ffn_{l}")
        g_w_up = _mm_up_tn(sv["xn3"], dhh, name=f"g_w_up_{l}")
        g_conv_w_cols = g_conv_w.transpose(1, 0, 2).reshape(3, 2 * f)
        tok = scatter(l, "ffn", ("w_up", "conv_w", "w_down"),
                      [_cols_to_blocks(g_w_up), _cols_to_blocks(g_conv_w_cols), g_w_down.reshape(N_DEV, -1, d)])
        dh2, dh2b, g_norm_ffn = _rms_bwd(sv["h2"], dxn3, norm_ffn_g[l] + tok, dh, name=f"d_norm_ffn_{l}")

        do = _mm_nt(dh2b, fw["wo"], out_dtype=BF16, name=f"d_o_{l}")
        g_wo = _mm_tn(sv["o"], dh2b, name=f"g_wo_{l}")
        dq, dk, dv = _attn_bwd(sv["q"], sv["k"], sv["v"], do, name=f"d_attn_{l}")
        dkb, dvb = dk.astype(BF16), dv.astype(BF16)
        g_wq = _mm_tn(sv["xn2"], dq, name=f"g_wq_{l}")
        g_wk = _mm_tn(sv["memn"], dkb, name=f"g_wk_{l}")
        g_wv = _mm_tn(sv["memn"], dvb, name=f"g_wv_{l}")
        tok = scatter(l, "attn", ("wq", "wk", "wv", "wo"),
                      [g.reshape(N_DEV, -1, d) for g in (g_wq, g_wk, g_wv, g_wo)])
        dxn2 = _mm_nt(dq, fw["wq"], out_dtype=F32, name=f"d_xn_xattn_{l}")
        dmemn = _mm_nt(dkb, fw["wk"], out_dtype=F32, name=f"d_memn_k_{l}")
        dmemn = _mm_nt(dvb, fw["wv"], out_dtype=F32, res=dmemn, name=f"d_memn_v_{l}")
        _, _, g_mem_norm = _rms_bwd(memx, dmemn, mem_norm_g[l], None, name=f"d_norm_mem_{l}")
        dh1, dh1b, g_norm_xattn = _rms_bwd(sv["h1"], dxn2, norm_xattn_g[l] + tok, dh2, name=f"d_norm_xattn_{l}")

        dcat = _mm_nt(dh1b, fw["w_out"], out_dtype=F32, name=f"d_cat_{l}")
        g_w_out = _mm_tn(sv["cat"], dh1b, name=f"g_w_out_{l}")
        dproj, g_pool_w, g_pool_scale, g_sgu_g, g_sgu_w, g_sgu_b = _mixer_bwd(
            sv["proj"], dcat, pool_w[l], pool_scale[l].reshape(1, -1), sgu_g[l].reshape(1, -1), sgu_w[l],
            sv["sgu_bias"], name=f"d_mixer_{l}")
        g_w_in = _mm_tn(sv["xn1"], dproj, name=f"g_w_in_{l}")
        tok = scatter(l, "mix", ("w_in", "w_out"), [_cols_to_blocks(g_w_in), g_w_out.reshape(N_DEV, -1, d)])
        dxn1 = _mm_nt(dproj, fw["w_in"], out_dtype=F32, name=f"d_xn_mix_{l}")
        dh, dhb, g_norm_mix = _rms_bwd(sv["h0"], dxn1, norm_mix_g[l] + tok, dh1, name=f"d_norm_mix_{l}")

        small[l] = dict(norm_mix_g=g_norm_mix.reshape(-1), pool_w=g_pool_w, pool_scale=g_pool_scale.reshape(-1),
                        sgu_g=g_sgu_g.reshape(-1), sgu_w=g_sgu_w, sgu_b=g_sgu_b, norm_xattn_g=g_norm_xattn.reshape(-1),
                        mem_norm_g=g_mem_norm.reshape(-1), norm_ffn_g=g_norm_ffn.reshape(-1),
                        conv_b=g_conv_b.reshape(-1))
    grad_x = dh.reshape(x.shape)

    slots = {nme: [None] * DEPTH for nme in _SHARDED}
    for l, tag, names, sems, srcs, lands in pending:
        arrived = _exchange_wait(sems, srcs, lands, dh, scatter=True, name=f"scatter_wait_{tag}_{l}")
        for nme, land in zip(names, arrived):
            slots[nme][l] = land
    out = {}
    for nme in _SHARDED:
        w3 = W[nme].reshape(DEPTH, -1, W[nme].shape[-1])
        res = _adamw([sl.reshape((N_DEV,) + w3.shape[1:]) for sl in slots[nme]], w3, M[nme].reshape(w3.shape),
                     V[nme].reshape(w3.shape), name=f"adamw_{nme}")
        out[nme] = [r.reshape(W[nme].shape) for r in res]

    small_names = [n for n in _SMALL]
    contrib = []
    for nme in small_names:
        if nme == "final_norm_g":
            contrib.append(dg_final.reshape(-1))
        else:
            contrib.append(jnp.stack([small[l][nme] for l in range(DEPTH)]))
    packed_g = _pack(contrib)
    (all_g,) = _all_gather([packed_g], name="gather_small_grads")
    rows = packed_g.shape[0]
    res = _adamw([all_g], _pack([W[n] for n in small_names]).reshape(1, rows, -1),
                 _pack([M[n] for n in small_names]).reshape(1, rows, -1),
                 _pack([V[n] for n in small_names]).reshape(1, rows, -1), name="adamw_small", tr=rows // 3)
    unpacked = [_unpack(r, [W[n] for n in small_names]) for r in res]
    for i, nme in enumerate(small_names):
        out[nme] = [unpacked[j][i] for j in range(4)]

    grads = [out[n][0] for n in _WEIGHTS]
    deltas = [out[n][1] for n in _WEIGHTS]
    new_m = [out[n][2] for n in _WEIGHTS]
    new_v = [out[n][3] for n in _WEIGHTS]
    return (loss, grad_x, *grads, *deltas, *new_m, *new_v)
```

```python
import functools
import math

import jax
import jax.numpy as jnp
from jax import lax
from jax.experimental import pallas as pl
from jax.experimental.pallas import tpu as pltpu

F32 = jnp.float32
BF16 = jnp.bfloat16
MESH = pl.DeviceIdType.MESH

EPS = 1e-6
N_DEV = 8
DEPTH = 2
POOL_WINDOWS = (2, 4, 8, 16)
GROUP = 128
POOL_WIDTH = 512
SGU_WIDTH = 512
HEADS = 4
HEAD_DIM = 256
POOL_HALO = 16
CONV_HALO = 8

ADAM_LR = 0.001
ADAM_B1 = 0.9
ADAM_B2 = 0.999
ADAM_EPS = 1e-08
ADAM_WD = 0.01
ADAM_STEP = 10

VMEM_LIMIT_BYTES = 52 * 1024 * 1024


def _params(*semantics):
    return pltpu.CompilerParams(dimension_semantics=semantics, vmem_limit_bytes=VMEM_LIMIT_BYTES)


def _tile(n, want):
    t = min(n, want)
    assert n % t == 0, (n, want)
    return t


_DOT_DIMS = {
    "nn": (((1,), (0,)), ((), ())),
    "nt": (((1,), (1,)), ((), ())),
    "tn": (((0,), (0,)), ((), ())),
}


def _mm(a, b, *, dims, grid, a_spec, b_spec, o_spec, out_shape, out_dtype, acc_shape, name, res=None, res_spec=None,
        norm_bwd=None):
    nk = grid[2]
    dn = _DOT_DIMS[dims]
    extras, extra_specs = [], []
    if res is not None:
        extras, extra_specs = [res], [res_spec]
    if norm_bwd is not None:
        h, gain, dres, row_spec, gain_spec = norm_bwd
        extras = [h, gain] + ([dres] if dres is not None else [])
        extra_specs = [row_spec, gain_spec] + ([row_spec] if dres is not None else [])
        out_specs = [row_spec, row_spec, gain_spec]
        out_shapes = [jax.ShapeDtypeStruct(h.shape, F32), jax.ShapeDtypeStruct(h.shape, BF16),
                      jax.ShapeDtypeStruct(gain.shape, F32)]
    else:
        out_specs, out_shapes = o_spec, jax.ShapeDtypeStruct(out_shape, out_dtype)
    n_extra = len(extras)

    def body(*refs):
        a_ref, b_ref = refs[:2]
        extra_refs = refs[2:2 + n_extra]
        out_refs = refs[2 + n_extra:len(refs) - (1 if nk > 1 else 0)]
        p = lax.dot_general(a_ref[...], b_ref[...], dn, preferred_element_type=F32)

        def finish(r):
            if norm_bwd is not None:
                _rms_bwd_math(r, extra_refs[0], extra_refs[1], extra_refs[2] if n_extra == 3 else None,
                              *out_refs, first=pl.program_id(0) == 0)
                return
            if res is not None:
                r = r + extra_refs[0][...]
            out_refs[0][...] = r.astype(out_refs[0].dtype)

        if nk == 1:
            finish(p)
        else:
            acc_ref = refs[-1]
            k = pl.program_id(2)

            @pl.when(k == 0)
            def _():
                acc_ref[...] = p

            @pl.when(k > 0)
            def _():
                acc_ref[...] += p

            @pl.when(k == nk - 1)
            def _():
                finish(acc_ref[...])

    scratch = [pltpu.VMEM(acc_shape, F32)] if nk > 1 else []
    return pl.pallas_call(
        body, grid=grid, in_specs=[a_spec, b_spec] + extra_specs, out_specs=out_specs,
        out_shape=out_shapes, scratch_shapes=scratch, name=name,
        compiler_params=_params("arbitrary" if norm_bwd is not None else "parallel", "parallel", "arbitrary"),
    )(a, b, *extras)


def _rms_bwd_math(dy, h_ref, g_ref, dres_ref, dh_ref, dhb_ref, dg_ref, *, first):
    x = h_ref[...]
    r = lax.rsqrt(jnp.mean(x * x, axis=-1, keepdims=True) + EPS)
    a = dy * g_ref[...]
    m = jnp.mean(a * x, axis=-1, keepdims=True)
    dh = r * a - x * (r * r * r * m)
    if dres_ref is not None:
        dh = dh + dres_ref[...]
    dh_ref[...] = dh
    dhb_ref[...] = dh.astype(BF16)
    part = jnp.sum(dy * (x * r), axis=0, keepdims=True)

    @pl.when(first)
    def _():
        dg_ref[...] = part

    @pl.when(jnp.logical_not(first))
    def _():
        dg_ref[...] += part


def _mm_nn(a, b, *, out_dtype, name, res=None, tm=1024):
    m, k = a.shape
    n = b.shape[1]
    tm = _tile(m, tm)
    spec_o = pl.BlockSpec((tm, n), lambda i, j, kk: (i, 0))
    return _mm(a, b, dims="nn", grid=(m // tm, 1, 1),
               a_spec=pl.BlockSpec((tm, k), lambda i, j, kk: (i, 0)),
               b_spec=pl.BlockSpec((k, n), lambda i, j, kk: (0, 0)),
               o_spec=spec_o, out_shape=(m, n), out_dtype=out_dtype, acc_shape=None, name=name,
               res=res, res_spec=spec_o if res is not None else None)


def _norm_bwd_arg(h, gain, dres, tm):
    d = h.shape[1]
    return (h, gain.reshape(1, d), dres, pl.BlockSpec((tm, d), lambda i, j, kk: (i, 0)),
            pl.BlockSpec((1, d), lambda i, j, kk: (0, 0)))


def _mm_nt(a, b, *, out_dtype=F32, name, res=None, tm=1024, norm_bwd=None):
    m, k = a.shape
    n = b.shape[0]
    tm = _tile(m, tm)
    spec_o = pl.BlockSpec((tm, n), lambda i, j, kk: (i, 0))
    return _mm(a, b, dims="nt", grid=(m // tm, 1, 1),
               a_spec=pl.BlockSpec((tm, k), lambda i, j, kk: (i, 0)),
               b_spec=pl.BlockSpec((n, k), lambda i, j, kk: (0, 0)),
               o_spec=spec_o, out_shape=(m, n), out_dtype=out_dtype, acc_shape=None, name=name,
               res=res, res_spec=spec_o if res is not None else None,
               norm_bwd=None if norm_bwd is None else _norm_bwd_arg(*norm_bwd, tm))


_TN_ROWS = 2048


def _mm_tn(a, b, *, name, tm=None, tn=None, ts=_TN_ROWS, out_dtype=BF16):
    s, m = a.shape
    n = b.shape[1]
    tm = m if tm is None else tm
    tn = n if tn is None else tn
    ts = _tile(s, ts)
    return _mm(a, b, dims="tn", grid=(m // tm, n // tn, s // ts),
               a_spec=pl.BlockSpec((ts, tm), lambda i, j, kk: (kk, i)),
               b_spec=pl.BlockSpec((ts, tn), lambda i, j, kk: (kk, j)),
               o_spec=pl.BlockSpec((tm, tn), lambda i, j, kk: (i, j)),
               out_shape=(m, n), out_dtype=out_dtype, acc_shape=(tm, tn), name=name)


def _mm_up(xn, w_up, *, name, tm=512):
    s, d = xn.shape
    f = w_up.shape[1] // 2
    tm = _tile(s, tm)
    return _mm(xn, w_up, dims="nn", grid=(2, s // tm, 1),
               a_spec=pl.BlockSpec((tm, d), lambda j, i, kk: (i, 0)),
               b_spec=pl.BlockSpec((d, f), lambda j, i, kk: (0, j)),
               o_spec=pl.BlockSpec((None, tm, f), lambda j, i, kk: (j, i, 0)),
               out_shape=(2, s, f), out_dtype=F32, acc_shape=None, name=name)


def _mm_up_nt(dhh, w_up, *, name, tm=1024, norm_bwd=None):
    _, s, f = dhh.shape
    d = w_up.shape[0]
    tm = _tile(s, tm)
    return _mm(dhh, w_up, dims="nt", grid=(s // tm, 1, 2),
               a_spec=pl.BlockSpec((None, tm, f), lambda i, j, kk: (kk, i, 0)),
               b_spec=pl.BlockSpec((d, f), lambda i, j, kk: (0, kk)),
               o_spec=pl.BlockSpec((tm, d), lambda i, j, kk: (i, 0)),
               out_shape=(s, d), out_dtype=F32, acc_shape=(tm, d), name=name,
               norm_bwd=None if norm_bwd is None else _norm_bwd_arg(*norm_bwd, tm))


def _mm_up_tn(xn, dhh, *, name, ts=_TN_ROWS):
    s, d = xn.shape
    f = dhh.shape[2]
    tn = f // 2
    ts = _tile(s, ts)
    return _mm(xn, dhh, dims="tn", grid=(1, 4, s // ts),
               a_spec=pl.BlockSpec((ts, d), lambda i, j, kk: (kk, 0)),
               b_spec=pl.BlockSpec((None, ts, tn), lambda i, j, kk: (j // 2, kk, j % 2)),
               o_spec=pl.BlockSpec((d, tn), lambda i, j, kk: (0, j)),
               out_shape=(d, 2 * f), out_dtype=BF16, acc_shape=(d, tn), name=name)


def _rms_fwd(h, g, *, name, tr=512):
    s, d = h.shape
    tr = _tile(s, tr)

    def body(h_ref, g_ref, o_ref):
        x = h_ref[...]
        r = lax.rsqrt(jnp.mean(x * x, axis=-1, keepdims=True) + EPS)
        o_ref[...] = ((x * r) * g_ref[...]).astype(o_ref.dtype)

    row = pl.BlockSpec((tr, d), lambda i: (i, 0))
    return pl.pallas_call(
        body, grid=(s // tr,), in_specs=[row, pl.BlockSpec((1, d), lambda i: (0, 0))], out_specs=row,
        out_shape=jax.ShapeDtypeStruct((s, d), BF16), name=name, compiler_params=_params("parallel"),
    )(h, g.reshape(1, d))


def _rms_bwd(h, dxn, g, dres, *, name, tr=512):
    s, d = h.shape
    tr = _tile(s, tr)
    has_res = dres is not None

    def body(*refs):
        if has_res:
            h_ref, dxn_ref, g_ref, dres_ref, dh_ref, dhb_ref, dg_ref = refs
        else:
            h_ref, dxn_ref, g_ref, dh_ref, dhb_ref, dg_ref = refs
            dres_ref = None
        _rms_bwd_math(dxn_ref[...].astype(F32), h_ref, g_ref, dres_ref, dh_ref, dhb_ref, dg_ref,
                      first=pl.program_id(0) == 0)

    row = pl.BlockSpec((tr, d), lambda i: (i, 0))
    vec = pl.BlockSpec((1, d), lambda i: (0, 0))
    in_specs = [row, row, vec] + ([row] if has_res else [])
    args = (h, dxn, g.reshape(1, d)) + ((dres,) if has_res else ())
    return pl.pallas_call(
        body, grid=(s // tr,), in_specs=in_specs, out_specs=[row, row, vec],
        out_shape=[jax.ShapeDtypeStruct((s, d), F32), jax.ShapeDtypeStruct((s, d), BF16),
                   jax.ShapeDtypeStruct((1, d), F32)],
        name=name, compiler_params=_params("arbitrary"),
    )(*args)


def _loss_head(h, g, target, *, name, tr=512):
    s, d = h.shape
    tr = _tile(s, tr)
    nt = s // tr

    def body(h_ref, g_ref, t_ref, dh_ref, dhb_ref, dg_ref, loss_ref, sq_ref):
        i = pl.program_id(0)
        x = h_ref[...]
        gain = g_ref[...]
        r = lax.rsqrt(jnp.mean(x * x, axis=-1, keepdims=True) + EPS)
        xh = x * r
        err = xh * gain - t_ref[...]
        dy = err * (1.0 / d)
        a = dy * gain
        m = jnp.mean(a * x, axis=-1, keepdims=True)
        dh = r * a - x * (r * r * r * m)
        dh_ref[...] = dh
        dhb_ref[...] = dh.astype(BF16)
        dg_part = jnp.sum(dy * xh, axis=0, keepdims=True)
        sq_part = jnp.sum(err * err, axis=0, keepdims=True)

        @pl.when(i == 0)
        def _():
            dg_ref[...] = dg_part
            sq_ref[...] = sq_part

        @pl.when(i > 0)
        def _():
            dg_ref[...] += dg_part
            sq_ref[...] += sq_part

        @pl.when(i == nt - 1)
        def _():
            total = jnp.sum(sq_ref[...], axis=1, keepdims=True) * (0.5 / d)
            loss_ref[...] = jnp.broadcast_to(total, loss_ref.shape)

    row = pl.BlockSpec((tr, d), lambda i: (i, 0))
    vec = pl.BlockSpec((1, d), lambda i: (0, 0))
    return pl.pallas_call(
        body, grid=(nt,), in_specs=[row, vec, row],
        out_specs=[row, row, vec, pl.BlockSpec((1, 128), lambda i: (0, 0))],
        out_shape=[jax.ShapeDtypeStruct((s, d), F32), jax.ShapeDtypeStruct((s, d), BF16),
                   jax.ShapeDtypeStruct((1, d), F32), jax.ShapeDtypeStruct((1, 128), F32)],
        scratch_shapes=[pltpu.VMEM((1, d), F32)], name=name, compiler_params=_params("arbitrary"),
    )(h, g.reshape(1, d), target)


_SQRT_HALF = 0.7071067811865476
_INV_SQRT_2PI = 0.3989422804014327


def _gelu(x):
    return 0.5 * x * (1.0 + lax.erf(x * _SQRT_HALF))


def _gelu_grad(x):
    return 0.5 * (1.0 + lax.erf(x * _SQRT_HALF)) + x * (jnp.exp(-0.5 * x * x) * _INV_SQRT_2PI)


def _trailing_sums(xe, win):
    s = xe
    sh = 1
    while sh < win:
        s = s + pltpu.roll(s, sh, 0)
        sh *= 2
    return s


def _leading_sums(xe, win):
    n = xe.shape[0]
    s = xe
    sh = 1
    while sh < win:
        s = s + pltpu.roll(s, n - sh, 0)
        sh *= 2
    return s


def _tril_mask():
    return lax.broadcasted_iota(jnp.int32, (GROUP, GROUP), 0) >= lax.broadcasted_iota(jnp.int32, (GROUP, GROUP), 1)


def _layernorm_stats(v):
    mu = jnp.mean(v, axis=-1, keepdims=True)
    xc = v - mu
    rstd = lax.rsqrt(jnp.mean(xc * xc, axis=-1, keepdims=True) + EPS)
    return xc * rstd, rstd


def _mixer_specs(s, t):
    halo_blocks = t // POOL_HALO
    tile = lambda w: pl.BlockSpec((t, w), lambda i: (i, 0))
    prev = pl.BlockSpec((POOL_HALO, POOL_WIDTH), lambda i: (jnp.maximum(i * halo_blocks - 1, 0), 0))
    nxt = pl.BlockSpec((POOL_HALO, POOL_WIDTH),
                       lambda i: (jnp.minimum((i + 1) * halo_blocks, s // POOL_HALO - 1), 0))
    const3 = pl.BlockSpec((HEADS, GROUP, GROUP), lambda i: (0, 0, 0))
    vec = pl.BlockSpec((1, POOL_WIDTH), lambda i: (0, 0))
    bias = pl.BlockSpec((GROUP, SGU_WIDTH), lambda i: (0, 0))
    return tile, prev, nxt, const3, vec, bias


def _mixer_fwd(proj, pool_w, pool_scale, sgu_g, sgu_w, sgu_bias, *, name, t=256):
    s = proj.shape[0]
    t = _tile(s, t)
    tile, prev, _, const3, vec, bias = _mixer_specs(s, t)

    def body(proj_ref, halo_ref, pw_ref, ps_ref, sg_ref, sw_ref, sb_ref, cat_ref):
        i = pl.program_id(0)
        row = i * t + lax.broadcasted_iota(jnp.int32, (t, 1), 0)
        p = proj_ref[:, 0:POOL_WIDTH]
        pe = jnp.concatenate([jnp.where(i > 0, halo_ref[...], 0.0), p], axis=0)
        for gi, win in enumerate(POOL_WINDOWS):
            cols = slice(gi * GROUP, (gi + 1) * GROUP)
            count = jnp.minimum(row + 1, win).astype(F32)
            d = _trailing_sums(pe[:, cols], win)[POOL_HALO:] / count - p[:, cols]
            y = jnp.dot(d.astype(BF16), pw_ref[gi].astype(BF16), preferred_element_type=F32) * ps_ref[:, cols]
            cat_ref[:, cols] = y.astype(BF16)

        u = _gelu(proj_ref[:, POOL_WIDTH:POOL_WIDTH + SGU_WIDTH])
        xhat, _ = _layernorm_stats(_gelu(proj_ref[:, POOL_WIDTH + SGU_WIDTH:]))
        vn = (xhat * sg_ref[...]).astype(BF16)
        tri = _tril_mask()
        for h in range(HEADS):
            cols = slice(h * GROUP, (h + 1) * GROUP)
            w = jnp.where(tri, sw_ref[h], 0.0).astype(BF16)
            for c in range(t // GROUP):
                rows = slice(c * GROUP, (c + 1) * GROUP)
                z = jnp.dot(w, vn[rows, cols], preferred_element_type=F32) + sb_ref[:, cols]
                cat_ref[rows, POOL_WIDTH + h * GROUP:POOL_WIDTH + (h + 1) * GROUP] = (u[rows, cols] * z).astype(BF16)

    return pl.pallas_call(
        body, grid=(s // t,),
        in_specs=[tile(POOL_WIDTH + 2 * SGU_WIDTH), prev, const3, vec, vec, const3, bias],
        out_specs=tile(POOL_WIDTH + SGU_WIDTH),
        out_shape=jax.ShapeDtypeStruct((s, POOL_WIDTH + SGU_WIDTH), BF16), name=name,
        compiler_params=_params("parallel"),
    )(proj, proj, pool_w, pool_scale, sgu_g, sgu_w, sgu_bias)


def _mixer_bwd(proj, dcat, pool_w, pool_scale, sgu_g, sgu_w, sgu_bias, *, name, t=256):
    s = proj.shape[0]
    t = _tile(s, t)
    nt = s // t
    tile, prev, nxt, const3, vec, bias = _mixer_specs(s, t)

    def body(proj_ref, halo_ref, dcat_ref, dnext_ref, pw_ref, ps_ref, sg_ref, sw_ref, sb_ref,
             dproj_ref, dpw_ref, dps_ref, dsg_ref, dsw_ref, dsb_ref, du_ref, dvn_ref, dz_ref):
        i = pl.program_id(0)

        @pl.when(i == 0)
        def _():
            dpw_ref[...] = jnp.zeros_like(dpw_ref)
            dps_ref[...] = jnp.zeros_like(dps_ref)
            dsg_ref[...] = jnp.zeros_like(dsg_ref)
            dsw_ref[...] = jnp.zeros_like(dsw_ref)
            dz_ref[...] = jnp.zeros_like(dz_ref)

        row = i * t + lax.broadcasted_iota(jnp.int32, (t, 1), 0)
        row_e = i * t + lax.broadcasted_iota(jnp.int32, (t + POOL_HALO, 1), 0)
        p = proj_ref[:, 0:POOL_WIDTH]
        pe = jnp.concatenate([jnp.where(i > 0, halo_ref[...], 0.0), p], axis=0)
        dyp = dcat_ref[:, 0:POOL_WIDTH]
        dye = jnp.concatenate([dyp, jnp.where(i < nt - 1, dnext_ref[...], 0.0)], axis=0)
        for gi, win in enumerate(POOL_WINDOWS):
            cols = slice(gi * GROUP, (gi + 1) * GROUP)
            count = jnp.minimum(row + 1, win).astype(F32)
            d = (_trailing_sums(pe[:, cols], win)[POOL_HALO:] / count - p[:, cols]).astype(BF16)
            pw = pw_ref[gi].astype(BF16)
            pre = jnp.dot(d, pw, preferred_element_type=F32)
            dps_ref[:, cols] += jnp.sum(dyp[:, cols] * pre, axis=0, keepdims=True)
            ys = (dye[:, cols] * ps_ref[:, cols]).astype(BF16)
            dpw_ref[gi] += lax.dot_general(d, ys[:t], _DOT_DIMS["tn"], preferred_element_type=F32)
            dd = lax.dot_general(ys, pw, _DOT_DIMS["nt"], preferred_element_type=F32)
            count_e = jnp.minimum(row_e + 1, win).astype(F32)
            dp = _leading_sums(dd / count_e, win)[:t] - dd[:t]
            dproj_ref[:, cols] = dp.astype(BF16)

        xu = proj_ref[:, POOL_WIDTH:POOL_WIDTH + SGU_WIDTH]
        xv = proj_ref[:, POOL_WIDTH + SGU_WIDTH:]
        u = _gelu(xu)
        xhat, rstd = _layernorm_stats(_gelu(xv))
        gain = sg_ref[...]
        vn = (xhat * gain).astype(BF16)
        tri = _tril_mask()
        for h in range(HEADS):
            cols = slice(h * GROUP, (h + 1) * GROUP)
            wf = jnp.where(tri, sw_ref[h], 0.0)
            w, wt = wf.astype(BF16), wf.T.astype(BF16)
            for c in range(t // GROUP):
                rows = slice(c * GROUP, (c + 1) * GROUP)
                vch = vn[rows, cols]
                z = jnp.dot(w, vch, preferred_element_type=F32) + sb_ref[:, cols]
                dy = dcat_ref[rows, POOL_WIDTH + h * GROUP:POOL_WIDTH + (h + 1) * GROUP]
                du_ref[rows, cols] = dy * z
                dz = dy * u[rows, cols]
                dz_ref[:, cols] += dz
                dzb = dz.astype(BF16)
                dsw_ref[h] += lax.dot_general(dzb, vch, _DOT_DIMS["nt"], preferred_element_type=F32)
                dvn_ref[rows, cols] = jnp.dot(wt, dzb, preferred_element_type=F32)
        dvn = dvn_ref[...]
        dsg_ref[...] += jnp.sum(dvn * xhat, axis=0, keepdims=True)
        dxh = dvn * gain
        dv = rstd * (dxh - jnp.mean(dxh, axis=-1, keepdims=True)
                     - xhat * jnp.mean(dxh * xhat, axis=-1, keepdims=True))
        dproj_ref[:, POOL_WIDTH:POOL_WIDTH + SGU_WIDTH] = (du_ref[...] * _gelu_grad(xu)).astype(BF16)
        dproj_ref[:, POOL_WIDTH + SGU_WIDTH:] = (dv * _gelu_grad(xv)).astype(BF16)

        @pl.when(i == nt - 1)
        def _():
            for h in range(HEADS):
                dsw_ref[h] = jnp.where(tri, dsw_ref[h], 0.0)
            lane = lax.broadcasted_iota(jnp.int32, (GROUP, GROUP), 1)
            out = jnp.zeros((GROUP, GROUP), F32)
            for h in range(HEADS):
                sh = jnp.sum(dz_ref[:, h * GROUP:(h + 1) * GROUP], axis=1, keepdims=True)
                out = jnp.where(lane == h, sh, out)
            dsb_ref[...] = out

    outs = pl.pallas_call(
        body, grid=(nt,),
        in_specs=[tile(POOL_WIDTH + 2 * SGU_WIDTH), prev, tile(POOL_WIDTH + SGU_WIDTH), nxt,
                  const3, vec, vec, const3, bias],
        out_specs=[tile(POOL_WIDTH + 2 * SGU_WIDTH), const3, vec, vec, const3,
                   pl.BlockSpec((GROUP, GROUP), lambda i: (0, 0))],
        out_shape=[jax.ShapeDtypeStruct((s, POOL_WIDTH + 2 * SGU_WIDTH), BF16),
                   jax.ShapeDtypeStruct((HEADS, GROUP, GROUP), F32),
                   jax.ShapeDtypeStruct((1, POOL_WIDTH), F32),
                   jax.ShapeDtypeStruct((1, SGU_WIDTH), F32),
                   jax.ShapeDtypeStruct((HEADS, GROUP, GROUP), F32),
                   jax.ShapeDtypeStruct((GROUP, GROUP), F32)],
        scratch_shapes=[pltpu.VMEM((t, SGU_WIDTH), F32), pltpu.VMEM((t, SGU_WIDTH), F32),
                        pltpu.VMEM((GROUP, SGU_WIDTH), F32)],
        name=name, compiler_params=_params("arbitrary"),
    )(proj, proj, dcat, dcat, pool_w, pool_scale, sgu_g, sgu_w, sgu_bias)
    dproj, dpw, dps, dsg, dsw, dsb = outs
    return dproj, dpw, dps, dsg, dsw, dsb[:, :HEADS].T


def _attn_probs(q, k, scale):
    sc = lax.dot_general(q, k, _DOT_DIMS["nt"], preferred_element_type=F32) * scale
    sc = sc - jnp.max(sc, axis=-1, keepdims=True)
    e = jnp.exp(sc)
    return e / jnp.sum(e, axis=-1, keepdims=True)


def _attn_fwd(q, k, v, *, name, t=512):
    s, d = q.shape
    nm = k.shape[0]
    t = _tile(s, t)
    scale = HEAD_DIM ** -0.5

    def body(q_ref, k_ref, v_ref, o_ref):
        for h in range(HEADS):
            cols = slice(h * HEAD_DIM, (h + 1) * HEAD_DIM)
            pr = _attn_probs(q_ref[:, cols], k_ref[:, cols], scale)
            o_ref[:, cols] = jnp.dot(pr.astype(BF16), v_ref[:, cols], preferred_element_type=F32).astype(BF16)

    row = pl.BlockSpec((t, d), lambda i: (i, 0))
    kv = pl.BlockSpec((nm, d), lambda i: (0, 0))
    return pl.pallas_call(
        body, grid=(s // t,), in_specs=[row, kv, kv], out_specs=row,
        out_shape=jax.ShapeDtypeStruct((s, d), BF16), name=name, compiler_params=_params("parallel"),
    )(q, k, v)


def _attn_bwd(q, k, v, do, *, name, t=512):
    s, d = q.shape
    nm = k.shape[0]
    t = _tile(s, t)
    scale = HEAD_DIM ** -0.5

    def body(q_ref, k_ref, v_ref, do_ref, dq_ref, dk_ref, dv_ref):
        i = pl.program_id(0)

        @pl.when(i == 0)
        def _():
            dk_ref[...] = jnp.zeros_like(dk_ref)
            dv_ref[...] = jnp.zeros_like(dv_ref)

        for h in range(HEADS):
            cols = slice(h * HEAD_DIM, (h + 1) * HEAD_DIM)
            qh, kh, vh, doh = q_ref[:, cols], k_ref[:, cols], v_ref[:, cols], do_ref[:, cols]
            pr = _attn_probs(qh, kh, scale)
            dpr = lax.dot_general(doh, vh, _DOT_DIMS["nt"], preferred_element_type=F32)
            ds = (pr * (dpr - jnp.sum(dpr * pr, axis=-1, keepdims=True)) * scale).astype(BF16)
            dv_ref[:, cols] += lax.dot_general(pr.astype(BF16), doh, _DOT_DIMS["tn"], preferred_element_type=F32)
            dk_ref[:, cols] += lax.dot_general(ds, qh, _DOT_DIMS["tn"], preferred_element_type=F32)
            dq_ref[:, cols] = jnp.dot(ds, kh, preferred_element_type=F32).astype(BF16)

    row = pl.BlockSpec((t, d), lambda i: (i, 0))
    kv = pl.BlockSpec((nm, d), lambda i: (0, 0))
    return pl.pallas_call(
        body, grid=(s // t,), in_specs=[row, kv, kv, row], out_specs=[row, kv, kv],
        out_shape=[jax.ShapeDtypeStruct((s, d), BF16), jax.ShapeDtypeStruct((nm, d), F32),
                   jax.ShapeDtypeStruct((nm, d), F32)],
        name=name, compiler_params=_params("arbitrary"),
    )(q, k, v, do)


def _conv_specs(s, f, t, tc, swap):
    hb = t // CONV_HALO
    order = (lambda fn: (lambda j, i: fn(i, j))) if swap else (lambda fn: fn)
    tile3 = pl.BlockSpec((2, t, tc), order(lambda i, j: (0, i, j)))
    prev3 = pl.BlockSpec((2, CONV_HALO, tc), order(lambda i, j: (0, jnp.maximum(i * hb - 1, 0), j)))
    next3 = pl.BlockSpec((2, CONV_HALO, tc),
                         order(lambda i, j: (0, jnp.minimum((i + 1) * hb, s // CONV_HALO - 1), j)))
    tile2 = pl.BlockSpec((t, tc), order(lambda i, j: (i, j)))
    next2 = pl.BlockSpec((CONV_HALO, tc), order(lambda i, j: (jnp.minimum((i + 1) * hb, s // CONV_HALO - 1), j)))
    wspec = pl.BlockSpec((2, 3, tc), order(lambda i, j: (0, 0, j)))
    bspec = pl.BlockSpec((2, 1, tc), order(lambda i, j: (0, 0, j)))
    return tile3, prev3, next3, tile2, next2, wspec, bspec


def _conv3(w_ref, p, x2, x1, x0, b):
    return (w_ref[p, 0:1, :] * x2 + w_ref[p, 1:2, :] * x1 + w_ref[p, 2:3, :] * x0) + b


def _convgate_fwd(hh, cw, cb, *, name, t=256, tc=1408):
    _, s, f = hh.shape
    t = _tile(s, t)
    tile3, prev3, _, tile2, _, wspec, bspec = _conv_specs(s, f, t, tc, swap=False)

    def body(hh_ref, prev_ref, cw_ref, cb_ref, act_ref):
        i = pl.program_id(0)
        hc = []
        for p in range(2):
            xe = jnp.concatenate([jnp.where(i > 0, prev_ref[p], 0.0), hh_ref[p]], axis=0)
            hc.append(_conv3(cw_ref, p, pltpu.roll(xe, 2, 0), pltpu.roll(xe, 1, 0), xe, cb_ref[p])[CONV_HALO:])
        gate, val = hc
        act_ref[...] = ((gate * jax.nn.sigmoid(gate)) * val).astype(BF16)

    return pl.pallas_call(
        body, grid=(s // t, f // tc), in_specs=[tile3, prev3, wspec, bspec], out_specs=tile2,
        out_shape=jax.ShapeDtypeStruct((s, f), BF16), name=name, compiler_params=_params("parallel", "parallel"),
    )(hh, hh, cw, cb)


def _convgate_bwd(hh, dact, cw, cb, *, name, t=128, tc=1408):
    _, s, f = hh.shape
    t = _tile(s, t)
    nt = s // t
    tile3, prev3, next3, tile2, next2, wspec, bspec = _conv_specs(s, f, t, tc, swap=True)

    def body(hh_ref, prev_ref, next_ref, da_ref, danext_ref, cw_ref, cb_ref, dhh_ref, dcw_ref, dcb_ref):
        i = pl.program_id(1)
        is_last = i == nt - 1

        @pl.when(i == 0)
        def _():
            dcw_ref[...] = jnp.zeros_like(dcw_ref)
            dcb_ref[...] = jnp.zeros_like(dcb_ref)

        taps, hc = [], []
        for p in range(2):
            xe = jnp.concatenate([jnp.where(i > 0, prev_ref[p], 0.0), hh_ref[p],
                                  jnp.where(is_last, 0.0, next_ref[p])], axis=0)
            x2, x1 = pltpu.roll(xe, 2, 0), pltpu.roll(xe, 1, 0)
            hc.append(_conv3(cw_ref, p, x2, x1, xe, cb_ref[p])[CONV_HALO:])
            taps.append((x2[CONV_HALO:CONV_HALO + t], x1[CONV_HALO:CONV_HALO + t], xe[CONV_HALO:CONV_HALO + t]))
        gate, val = hc
        dae = jnp.concatenate([da_ref[...], jnp.where(is_last, 0.0, danext_ref[...])], axis=0)
        sg = jax.nn.sigmoid(gate)
        dval = dae * (gate * sg)
        dgate = dae * val * (sg * (1.0 + gate * (1.0 - sg)))
        m = t + CONV_HALO
        for p, dhc in enumerate((dgate, dval)):
            dh = (cw_ref[p, 2:3, :] * dhc + cw_ref[p, 1:2, :] * pltpu.roll(dhc, m - 1, 0)
                  + cw_ref[p, 0:1, :] * pltpu.roll(dhc, m - 2, 0))
            dhh_ref[p] = dh[:t].astype(BF16)
            d0 = dhc[:t]
            for kk, tap in enumerate(taps[p]):
                dcw_ref[p, kk:kk + 1, :] += jnp.sum(d0 * tap, axis=0, keepdims=True)
            dcb_ref[p] += jnp.sum(d0, axis=0, keepdims=True)

    return pl.pallas_call(
        body, grid=(f // tc, nt), in_specs=[tile3, prev3, next3, tile2, next2, wspec, bspec],
        out_specs=[tile3, wspec, bspec],
        out_shape=[jax.ShapeDtypeStruct((2, s, f), BF16), jax.ShapeDtypeStruct((2, 3, f), F32),
                   jax.ShapeDtypeStruct((2, 1, f), F32)],
        name=name, compiler_params=_params("parallel", "arbitrary"),
    )(hh, hh, hh, dact, dact, cw, cb)


def _position():
    return lax.axis_index("x"), lax.axis_index("y"), lax.axis_index("c")


def _linear(px, py, pc):
    return 4 * px + 2 * py + pc


_ANY = pl.BlockSpec(memory_space=pl.ANY)


def _all_gather(shards, *, name):
    n = len(shards)

    def body(*refs):
        x_refs, o_refs = refs[:n], refs[n:2 * n]
        send_sems, recv_sems, local_sems = refs[2 * n:]
        x, y, c = _position()
        me, sibling = (x, y, c), (x, y, 1 - c)
        chips = [(1 - x, y), (x, 1 - y), (1 - x, 1 - y)]

        def copy(ti, k, block, to, src=None):
            dst = o_refs[ti].at[_linear(*block)]
            return pltpu.make_async_remote_copy(
                src_ref=dst if src is None else src, dst_ref=dst,
                send_sem=send_sems.at[ti, k], recv_sem=recv_sems.at[ti, k],
                device_id=to, device_id_type=MESH)

        mine = [pltpu.make_async_copy(x_refs[ti], o_refs[ti].at[_linear(*me)], local_sems.at[ti]) for ti in range(n)]
        for cp in mine:
            cp.start()
        first = []
        for ti in range(n):
            first.append(copy(ti, 0, me, sibling, src=x_refs[ti]))
            for j, chip in enumerate(chips):
                first.append(copy(ti, 1 + j, me, (*chip, c), src=x_refs[ti]))
        for cp in first:
            cp.start()
        passed = []
        for j, chip in enumerate(chips):
            for ti in range(n):
                copy(ti, 1 + j, (*chip, c), me).wait_recv()
                fwd = copy(ti, 4 + j, (*chip, c), sibling)
                fwd.start()
                passed.append(fwd)
        for ti in range(n):
            copy(ti, 0, sibling, me).wait_recv()
            for j, chip in enumerate(chips):
                copy(ti, 4 + j, (*chip, 1 - c), me).wait_recv()
        for cp in first + passed:
            cp.wait_send()
        for cp in mine:
            cp.wait()

    return pl.pallas_call(
        body, in_specs=[_ANY] * n, out_specs=[_ANY] * n,
        out_shape=[jax.ShapeDtypeStruct((N_DEV,) + a.shape, a.dtype) for a in shards],
        scratch_shapes=[pltpu.SemaphoreType.DMA((n, 7)), pltpu.SemaphoreType.DMA((n, 7)),
                        pltpu.SemaphoreType.DMA((n,))],
        name=name,
    )(*shards)


def _peers_of(x, y, c):
    peers = []
    for mask in range(1, N_DEV):
        peers.append((1 - x if mask & 4 else x, 1 - y if mask & 2 else y, 1 - c if mask & 1 else c))
    return peers


_HBM = pl.BlockSpec(memory_space=pltpu.HBM)
_SEM = pl.BlockSpec(memory_space=pltpu.SEMAPHORE)
_EFFECT = pltpu.SideEffectType.DATAFLOW_SIDE_EFFECTING


def _exchange_copy(src_ref, land_ref, send_sem, recv_sem, peer, mine, scatter, arriving):
    src = src_ref.at[_linear(*peer)] if scatter else src_ref
    dst = land_ref.at[_linear(*peer) if arriving else mine]
    return pltpu.make_async_remote_copy(src_ref=src, dst_ref=dst, send_sem=send_sem, recv_sem=recv_sem,
                                        device_id=peer, device_id_type=MESH)


def _exchange_start(srcs, groups, *, scatter, name):
    n = len(srcs)
    ng = len(groups)
    lands = [lax.empty(a.shape if scatter else (N_DEV,) + a.shape, a.dtype) for a in srcs]

    def body(*refs):
        src_refs, land_refs = refs[:n], refs[n:2 * n]
        sem_refs = refs[2 * n:2 * n + 2 * ng]
        token_ref = refs[2 * n + 2 * ng + 2 * n]
        local_sems = refs[-1]
        x, y, c = _position()
        mine = _linear(x, y, c)
        peers = _peers_of(x, y, c)
        local = [pltpu.make_async_copy(src_refs[t].at[mine] if scatter else src_refs[t], land_refs[t].at[mine],
                                       local_sems.at[t]) for t in range(n)]
        for cp in local:
            cp.start()
        for cp in local:
            cp.wait()
        for gi, group in enumerate(groups):
            for pos, t in enumerate(group):
                for k, peer in enumerate(peers):
                    _exchange_copy(src_refs[t], land_refs[t], sem_refs[2 * gi].at[7 * pos + k],
                                   sem_refs[2 * gi + 1].at[7 * pos + k], peer, mine, scatter, arriving=False).start()
        token_ref[...] = jnp.zeros_like(token_ref)

    sem_shapes = []
    for group in groups:
        sem_shapes += [pltpu.SemaphoreType.DMA((7 * len(group),))] * 2
    outs = pl.pallas_call(
        body, name=name,
        out_shape=tuple(sem_shapes) + tuple(pltpu.HBM(a.shape, a.dtype) for a in srcs)
        + tuple(pltpu.HBM(a.shape, a.dtype) for a in lands) + (jax.ShapeDtypeStruct((8, 128), F32),),
        in_specs=[_HBM] * (2 * n),
        out_specs=(_SEM,) * (2 * ng) + (_HBM,) * (2 * n) + (pl.BlockSpec(memory_space=pltpu.VMEM),),
        input_output_aliases={i: 2 * ng + i for i in range(2 * n)},
        scratch_shapes=[pltpu.SemaphoreType.DMA((n,))],
        compiler_params=pltpu.CompilerParams(has_side_effects=_EFFECT),
    )(*[pltpu.with_memory_space_constraint(a, pltpu.HBM) for a in list(srcs) + lands])
    sems = [(outs[2 * gi], outs[2 * gi + 1]) for gi in range(ng)]
    return sems, list(outs[2 * ng:2 * ng + n]), list(outs[2 * ng + n:2 * ng + 2 * n]), outs[-1]


def _exchange_wait(sems, srcs, lands, after, *, scatter, name):
    n = len(srcs)
    send_sems, recv_sems = sems

    def body(*refs):
        src_refs, land_refs = refs[:n], refs[n:2 * n]
        send_ref, recv_ref = refs[2 * n], refs[2 * n + 1]
        x, y, c = _position()
        mine = _linear(x, y, c)
        for pos in range(n):
            for k, peer in enumerate(_peers_of(x, y, c)):
                cp = _exchange_copy(src_refs[pos], land_refs[pos], send_ref.at[7 * pos + k], recv_ref.at[7 * pos + k],
                                    peer, mine, scatter, arriving=True)
                cp.wait_send()
                cp.wait_recv()

    outs = pl.pallas_call(
        body, name=name,
        out_shape=tuple(pltpu.HBM(a.shape, a.dtype) for a in list(srcs) + list(lands)),
        in_specs=[_HBM] * (2 * n) + [_SEM, _SEM, _ANY], out_specs=(_HBM,) * (2 * n),
        input_output_aliases={i: i for i in range(2 * n)},
        compiler_params=pltpu.CompilerParams(has_side_effects=_EFFECT),
    )(*srcs, *lands, send_sems, recv_sems, after)
    return list(outs[n:])


def _adamw_math(g, w, m, v):
    m2 = ADAM_B1 * m + (1.0 - ADAM_B1) * g
    v2 = ADAM_B2 * v + (1.0 - ADAM_B2) * (g * g)
    m_hat = m2 / (1.0 - ADAM_B1 ** ADAM_STEP)
    v_hat = v2 / (1.0 - ADAM_B2 ** ADAM_STEP)
    delta = -ADAM_LR * (m_hat / (jnp.sqrt(v_hat) + ADAM_EPS) + ADAM_WD * w)
    return delta, m2, v2


def _adamw(slots, w, m, v, *, name, tr=256):
    depth = len(slots)
    _, r, c = slots[0].shape
    tr = next((cand for cand in range(min(r, tr), 15, -1) if r % cand == 0 and cand % 16 == 0), r)

    def body(*refs):
        s_refs = refs[:depth]
        w_ref, m_ref, v_ref, g_ref, d_ref, m2_ref, v2_ref = refs[depth:]
        layer = pl.program_id(0)
        for l in range(depth):
            @pl.when(layer == l)
            def _():
                g = s_refs[l][0].astype(F32)
                for d in range(1, N_DEV):
                    g = g + s_refs[l][d].astype(F32)
                delta, m2, v2 = _adamw_math(g, w_ref[...], m_ref[...], v_ref[...])
                g_ref[...] = g
                d_ref[...] = delta
                m2_ref[...] = m2
                v2_ref[...] = v2

    blk = pl.BlockSpec((None, tr, c), lambda layer, i: (layer, i, 0))
    sblks = [pl.BlockSpec((N_DEV, tr, c), lambda layer, i, l=l: (0, jnp.where(layer == l, i, 0), 0))
             for l in range(depth)]
    shape = jax.ShapeDtypeStruct((depth, r, c), F32)
    return pl.pallas_call(
        body, grid=(depth, r // tr), in_specs=sblks + [blk, blk, blk], out_specs=[blk] * 4,
        out_shape=[shape] * 4, name=name, compiler_params=_params("arbitrary", "arbitrary"),
    )(*slots, w, m, v)


_SHARDED = ("w_in", "w_out", "wq", "wk", "wv", "wo", "w_up", "conv_w", "w_down")
_SMALL = ("norm_mix_g", "pool_w", "pool_scale", "sgu_g", "sgu_w", "sgu_b", "norm_xattn_g", "mem_norm_g",
          "norm_ffn_g", "conv_b", "final_norm_g")
_WEIGHTS = ("norm_mix_g", "w_in", "pool_w", "pool_scale", "sgu_g", "sgu_w", "sgu_b", "w_out", "norm_xattn_g",
            "mem_norm_g", "wq", "wk", "wv", "wo", "norm_ffn_g", "w_up", "conv_w", "conv_b", "w_down",
            "final_norm_g")
_PACK_LANES = 128
_GATHER_GROUPS = (("w_in",), ("w_out",), ("wq", "wk", "wv", "wo"), ("w_up", "conv_w", "w_down"))


def _cols_to_blocks(a):
    r, c8 = a.shape
    return a.reshape(r, N_DEV, c8 // N_DEV).transpose(1, 0, 2)


def _blocks_to_cols(a):
    n, r, c = a.shape
    return a.transpose(1, 0, 2).reshape(r, n * c)


def _pack(arrays):
    flat = jnp.concatenate([a.reshape(-1) for a in arrays])
    assert flat.shape[0] % (8 * _PACK_LANES) == 0
    return flat.reshape(-1, _PACK_LANES)


def _unpack(packed, like):
    flat = packed.reshape(-1)
    out, off = [], 0
    for a in like:
        out.append(flat[off:off + a.size].reshape(a.shape))
        off += a.size
    return out


def kernel(x, mem, norm_mix_g, w_in, pool_w, pool_scale, sgu_g, sgu_w, sgu_b, w_out, norm_xattn_g, mem_norm_g, wq, wk, wv, wo, norm_ffn_g, w_up, conv_w, conv_b, w_down, final_norm_g, loss_target, m_norm_mix_g, m_w_in, m_pool_w, m_pool_scale, m_sgu_g, m_sgu_w, m_sgu_b, m_w_out, m_norm_xattn_g, m_mem_norm_g, m_wq, m_wk, m_wv, m_wo, m_norm_ffn_g, m_w_up, m_conv_w, m_conv_b, m_w_down, m_final_norm_g, v_norm_mix_g, v_w_in, v_pool_w, v_pool_scale, v_sgu_g, v_sgu_w, v_sgu_b, v_w_out, v_norm_xattn_g, v_mem_norm_g, v_wq, v_wk, v_wv, v_wo, v_norm_ffn_g, v_w_up, v_conv_w, v_conv_b, v_w_down, v_final_norm_g):
    W = dict(norm_mix_g=norm_mix_g, w_in=w_in, pool_w=pool_w, pool_scale=pool_scale, sgu_g=sgu_g, sgu_w=sgu_w,
             sgu_b=sgu_b, w_out=w_out, norm_xattn_g=norm_xattn_g, mem_norm_g=mem_norm_g, wq=wq, wk=wk, wv=wv, wo=wo,
             norm_ffn_g=norm_ffn_g, w_up=w_up, conv_w=conv_w, conv_b=conv_b, w_down=w_down,
             final_norm_g=final_norm_g)
    M = dict(norm_mix_g=m_norm_mix_g, w_in=m_w_in, pool_w=m_pool_w, pool_scale=m_pool_scale, sgu_g=m_sgu_g,
             sgu_w=m_sgu_w, sgu_b=m_sgu_b, w_out=m_w_out, norm_xattn_g=m_norm_xattn_g, mem_norm_g=m_mem_norm_g,
             wq=m_wq, wk=m_wk, wv=m_wv, wo=m_wo, norm_ffn_g=m_norm_ffn_g, w_up=m_w_up, conv_w=m_conv_w,
             conv_b=m_conv_b, w_down=m_w_down, final_norm_g=m_final_norm_g)
    V = dict(norm_mix_g=v_norm_mix_g, w_in=v_w_in, pool_w=v_pool_w, pool_scale=v_pool_scale, sgu_g=v_sgu_g,
             sgu_w=v_sgu_w, sgu_b=v_sgu_b, w_out=v_w_out, norm_xattn_g=v_norm_xattn_g, mem_norm_g=v_mem_norm_g,
             wq=v_wq, wk=v_wk, wv=v_wv, wo=v_wo, norm_ffn_g=v_norm_ffn_g, w_up=v_w_up, conv_w=v_conv_w,
             conv_b=v_conv_b, w_down=v_w_down, final_norm_g=v_final_norm_g)

    s, d = x.shape[1], x.shape[2]
    f = w_down.shape[1] * N_DEV
    h = x.reshape(s, d)
    memx = mem.reshape(mem.shape[1], d)
    target = loss_target.reshape(s, d)

    shards, groups = [], []
    for l in range(DEPTH):
        for names in _GATHER_GROUPS:
            groups.append(list(range(len(shards), len(shards) + len(names))))
            for nme in names:
                a = W[nme][l]
                shards.append(a if nme == "conv_w" else a.astype(BF16))
    gather_sems, shard_thru, land_thru, token = _exchange_start(shards, groups, scatter=False, name="gather_start")

    def fetch(l, gi, after):
        idx = groups[l * len(_GATHER_GROUPS) + gi]
        lands = _exchange_wait(gather_sems[l * len(_GATHER_GROUPS) + gi], [shard_thru[t] for t in idx],
                               [land_thru[t] for t in idx], after, scatter=False, name=f"gather_wait_{l}_{gi}")
        return dict(zip(_GATHER_GROUPS[gi], lands))

    saved, full = [], []
    for l in range(DEPTH):
        sgu_bias = jnp.repeat(sgu_b[l].T, GROUP, axis=1)
        cb = conv_b[l].reshape(2, 1, f)
        xn1 = _rms_fwd(h, norm_mix_g[l] + token[0, 0] if l == 0 else norm_mix_g[l], name=f"norm_mix_{l}")
        w_in_f = _blocks_to_cols(fetch(l, 0, xn1)["w_in"])
        proj = _mm_nn(xn1, w_in_f, out_dtype=F32, name=f"proj_in_{l}")
        cat = _mixer_fwd(proj, pool_w[l], pool_scale[l].reshape(1, -1), sgu_g[l].reshape(1, -1), sgu_w[l], sgu_bias,
                         name=f"mixer_{l}")
        w_out_f = fetch(l, 1, cat)["w_out"].reshape(-1, d)
        h1 = _mm_nn(cat, w_out_f, out_dtype=F32, res=h, name=f"proj_out_{l}")
        xn2 = _rms_fwd(h1, norm_xattn_g[l], name=f"norm_xattn_{l}")
        g = fetch(l, 2, xn2)
        wq_f, wk_f, wv_f, wo_f = (g[nme].reshape(-1, d) for nme in ("wq", "wk", "wv", "wo"))
        q = _mm_nn(xn2, wq_f, out_dtype=BF16, name=f"q_{l}")
        memn = _rms_fwd(memx, mem_norm_g[l], name=f"norm_mem_{l}")
        k = _mm_nn(memn, wk_f, out_dtype=BF16, name=f"k_{l}")
        v = _mm_nn(memn, wv_f, out_dtype=BF16, name=f"v_{l}")
        o = _attn_fwd(q, k, v, name=f"attn_{l}")
        h2 = _mm_nn(o, wo_f, out_dtype=F32, res=h1, name=f"attn_out_{l}")
        xn3 = _rms_fwd(h2, norm_ffn_g[l], name=f"norm_ffn_{l}")
        g = fetch(l, 3, xn3)
        w_up_f = _blocks_to_cols(g["w_up"])
        conv_w_f = _blocks_to_cols(g["conv_w"]).reshape(3, 2, f).transpose(1, 0, 2)
        w_down_f = g["w_down"].reshape(-1, d)
        hh = _mm_up(xn3, w_up_f, name=f"ffn_up_{l}")
        act = _convgate_fwd(hh, conv_w_f, cb, name=f"convgate_{l}")
        h3 = _mm_nn(act, w_down_f, out_dtype=F32, res=h2, tm=512, name=f"ffn_down_{l}")
        full.append(dict(w_in=w_in_f, w_out=w_out_f, wq=wq_f, wk=wk_f, wv=wv_f, wo=wo_f, w_up=w_up_f,
                         conv_w=conv_w_f, w_down=w_down_f))
        saved.append(dict(h0=h, xn1=xn1, proj=proj, cat=cat, h1=h1, xn2=xn2, q=q, memn=memn, k=k, v=v, o=o, h2=h2,
                          xn3=xn3, hh=hh, act=act, sgu_bias=sgu_bias, cb=cb))
        h = h3

    dh, dhb, dg_final, loss_row = _loss_head(h, final_norm_g, target, name="loss_head")
    loss = lax.psum(loss_row[0, 0], ("x", "y", "c"))

    pending = []
    small = [None] * DEPTH

    def scatter(l, tag, names, parts):
        sems, srcs, lands, tok = _exchange_start(parts, [list(range(len(parts)))], scatter=True,
                                                 name=f"scatter_start_{tag}_{l}")
        pending.append((l, tag, names, sems[0], srcs, lands))
        return tok[0, 0]

    for l in reversed(range(DEPTH)):
        fw, sv = full[l], saved[l]
        dact = _mm_nt(dhb, fw["w_down"], out_dtype=F32, tm=512, name=f"d_act_{l}")
        g_w_down = _mm_tn(sv["act"], dhb, tm=f // 2, name=f"g_w_down_{l}")
        dhh, g_conv_w, g_conv_b = _convgate_bwd(sv["hh"], dact, fw["conv_w"], sv["cb"], name=f"d_convgate_{l}")
        g_w_up = _mm_up_tn(sv["xn3"], dhh, name=f"g_w_up_{l}")
        g_conv_w_cols = g_conv_w.transpose(1, 0, 2).reshape(3, 2 * f)
        tok = scatter(l, "ffn", ("w_up", "conv_w", "w_down"),
                      [_cols_to_blocks(g_w_up), _cols_to_blocks(g_conv_w_cols), g_w_down.reshape(N_DEV, -1, d)])
        dxn3 = _mm_up_nt(dhh, fw["w_up"], name=f"d_xn_ffn_{l}")
        dh2, dh2b, g_norm_ffn = _rms_bwd(sv["h2"], dxn3, norm_ffn_g[l] + tok, dh, name=f"d_norm_ffn_{l}")

        do = _mm_nt(dh2b, fw["wo"], out_dtype=BF16, name=f"d_o_{l}")
        g_wo = _mm_tn(sv["o"], dh2b, name=f"g_wo_{l}")
        dq, dk, dv = _attn_bwd(sv["q"], sv["k"], sv["v"], do, name=f"d_attn_{l}")
        dkb, dvb = dk.astype(BF16), dv.astype(BF16)
        g_wq = _mm_tn(sv["xn2"], dq, name=f"g_wq_{l}")
        g_wk = _mm_tn(sv["memn"], dkb, name=f"g_wk_{l}")
        g_wv = _mm_tn(sv["memn"], dvb, name=f"g_wv_{l}")
        tok = scatter(l, "attn", ("wq", "wk", "wv", "wo"),
                      [g.reshape(N_DEV, -1, d) for g in (g_wq, g_wk, g_wv, g_wo)])
        dmemn = _mm_nt(dkb, fw["wk"], out_dtype=F32, name=f"d_memn_k_{l}")
        dmemn = _mm_nt(dvb, fw["wv"], out_dtype=F32, res=dmemn, name=f"d_memn_v_{l}")
        _, _, g_mem_norm = _rms_bwd(memx, dmemn, mem_norm_g[l], None, name=f"d_norm_mem_{l}")
        dh1, dh1b, g_norm_xattn = _mm_nt(dq, fw["wq"], name=f"d_norm_xattn_{l}",
                                         norm_bwd=(sv["h1"], norm_xattn_g[l] + tok, dh2))

        dcat = _mm_nt(dh1b, fw["w_out"], out_dtype=F32, name=f"d_cat_{l}")
        g_w_out = _mm_tn(sv["cat"], dh1b, name=f"g_w_out_{l}")
        dproj, g_pool_w, g_pool_scale, g_sgu_g, g_sgu_w, g_sgu_b = _mixer_bwd(
            sv["proj"], dcat, pool_w[l], pool_scale[l].reshape(1, -1), sgu_g[l].reshape(1, -1), sgu_w[l],
            sv["sgu_bias"], name=f"d_mixer_{l}")
        g_w_in = _mm_tn(sv["xn1"], dproj, name=f"g_w_in_{l}")
        tok = scatter(l, "mix", ("w_in", "w_out"), [_cols_to_blocks(g_w_in), g_w_out.reshape(N_DEV, -1, d)])
        dh, dhb, g_norm_mix = _mm_nt(dproj, fw["w_in"], name=f"d_norm_mix_{l}",
                                     norm_bwd=(sv["h0"], norm_mix_g[l] + tok, dh1))

        small[l] = dict(norm_mix_g=g_norm_mix.reshape(-1), pool_w=g_pool_w, pool_scale=g_pool_scale.reshape(-1),
                        sgu_g=g_sgu_g.reshape(-1), sgu_w=g_sgu_w, sgu_b=g_sgu_b, norm_xattn_g=g_norm_xattn.reshape(-1),
                        mem_norm_g=g_mem_norm.reshape(-1), norm_ffn_g=g_norm_ffn.reshape(-1),
                        conv_b=g_conv_b.reshape(-1))
    grad_x = dh.reshape(x.shape)

    slots = {nme: [None] * DEPTH for nme in _SHARDED}
    for l, tag, names, sems, srcs, lands in pending:
        arrived = _exchange_wait(sems, srcs, lands, dh, scatter=True, name=f"scatter_wait_{tag}_{l}")
        for nme, land in zip(names, arrived):
            slots[nme][l] = land
    out = {}
    for nme in _SHARDED:
        w3 = W[nme].reshape(DEPTH, -1, W[nme].shape[-1])
        res = _adamw([sl.reshape((N_DEV,) + w3.shape[1:]) for sl in slots[nme]], w3, M[nme].reshape(w3.shape),
                     V[nme].reshape(w3.shape), name=f"adamw_{nme}")
        out[nme] = [r.reshape(W[nme].shape) for r in res]

    small_names = [n for n in _SMALL]
    contrib = []
    for nme in small_names:
        if nme == "final_norm_g":
            contrib.append(dg_final.reshape(-1))
        else:
            contrib.append(jnp.stack([small[l][nme] for l in range(DEPTH)]))
    packed_g = _pack(contrib)
    (all_g,) = _all_gather([packed_g], name="gather_small_grads")
    rows = packed_g.shape[0]
    res = _adamw([all_g], _pack([W[n] for n in small_names]).reshape(1, rows, -1),
                 _pack([M[n] for n in small_names]).reshape(1, rows, -1),
                 _pack([V[n] for n in small_names]).reshape(1, rows, -1), name="adamw_small", tr=rows // 3)
    unpacked = [_unpack(r, [W[n] for n in small_names]) for r in res]
    for i, nme in enumerate(small_names):
        out[nme] = [unpacked[j][i] for j in range(4)]

    grads = [out[n][0] for n in _WEIGHTS]
    deltas = [out[n][1] for n in _WEIGHTS]
    new_m = [out[n][2] for n in _WEIGHTS]
    new_v = [out[n][3] for n in _WEIGHTS]
    return (loss, grad_x, *grads, *deltas, *new_m, *new_v)
```

```python
import functools
import math

import jax
import jax.numpy as jnp
from jax import lax
from jax.experimental import pallas as pl
from jax.experimental.pallas import tpu as pltpu
from jax.experimental.pallas import tpu_sc as plsc

F32 = jnp.float32
BF16 = jnp.bfloat16
MESH = pl.DeviceIdType.MESH

EPS = 1e-6
N_DEV = 8
DEPTH = 2
POOL_WINDOWS = (2, 4, 8, 16)
GROUP = 128
POOL_WIDTH = 512
SGU_WIDTH = 512
HEADS = 4
HEAD_DIM = 256
POOL_HALO = 16
CONV_HALO = 8

ADAM_LR = 0.001
ADAM_B1 = 0.9
ADAM_B2 = 0.999
ADAM_EPS = 1e-08
ADAM_WD = 0.01
ADAM_STEP = 10

VMEM_LIMIT_BYTES = 52 * 1024 * 1024


def _params(*semantics):
    return pltpu.CompilerParams(dimension_semantics=semantics, vmem_limit_bytes=VMEM_LIMIT_BYTES)


def _tile(n, want):
    t = min(n, want)
    assert n % t == 0, (n, want)
    return t


_DOT_DIMS = {
    "nn": (((1,), (0,)), ((), ())),
    "nt": (((1,), (1,)), ((), ())),
    "tn": (((0,), (0,)), ((), ())),
}


def _mm(a, b, *, dims, grid, a_spec, b_spec, o_spec, out_shape, out_dtype, acc_shape, name, res=None, res_spec=None,
        norm_bwd=None):
    nk = grid[2]
    dn = _DOT_DIMS[dims]
    extras, extra_specs = [], []
    if res is not None:
        extras, extra_specs = [res], [res_spec]
    if norm_bwd is not None:
        h, gain, dres, row_spec, gain_spec = norm_bwd
        extras = [h, gain] + ([dres] if dres is not None else [])
        extra_specs = [row_spec, gain_spec] + ([row_spec] if dres is not None else [])
        out_specs = [row_spec, row_spec, gain_spec]
        out_shapes = [jax.ShapeDtypeStruct(h.shape, F32), jax.ShapeDtypeStruct(h.shape, BF16),
                      jax.ShapeDtypeStruct(gain.shape, F32)]
    else:
        out_specs, out_shapes = o_spec, jax.ShapeDtypeStruct(out_shape, out_dtype)
    n_extra = len(extras)

    def body(*refs):
        a_ref, b_ref = refs[:2]
        extra_refs = refs[2:2 + n_extra]
        out_refs = refs[2 + n_extra:len(refs) - (1 if nk > 1 else 0)]
        p = lax.dot_general(a_ref[...], b_ref[...], dn, preferred_element_type=F32)

        def finish(r):
            if norm_bwd is not None:
                _rms_bwd_math(r, extra_refs[0], extra_refs[1], extra_refs[2] if n_extra == 3 else None,
                              *out_refs, first=pl.program_id(0) == 0)
                return
            if res is not None:
                r = r + extra_refs[0][...]
            out_refs[0][...] = r.astype(out_refs[0].dtype)

        if nk == 1:
            finish(p)
        else:
            acc_ref = refs[-1]
            k = pl.program_id(2)

            @pl.when(k == 0)
            def _():
                acc_ref[...] = p

            @pl.when(k > 0)
            def _():
                acc_ref[...] += p

            @pl.when(k == nk - 1)
            def _():
                finish(acc_ref[...])

    scratch = [pltpu.VMEM(acc_shape, F32)] if nk > 1 else []
    return pl.pallas_call(
        body, grid=grid, in_specs=[a_spec, b_spec] + extra_specs, out_specs=out_specs,
        out_shape=out_shapes, scratch_shapes=scratch, name=name,
        compiler_params=_params("arbitrary" if norm_bwd is not None else "parallel", "parallel", "arbitrary"),
    )(a, b, *extras)


def _rms_bwd_math(dy, h_ref, g_ref, dres_ref, dh_ref, dhb_ref, dg_ref, *, first):
    x = h_ref[...]
    r = lax.rsqrt(jnp.mean(x * x, axis=-1, keepdims=True) + EPS)
    a = dy * g_ref[...]
    m = jnp.mean(a * x, axis=-1, keepdims=True)
    dh = r * a - x * (r * r * r * m)
    if dres_ref is not None:
        dh = dh + dres_ref[...]
    dh_ref[...] = dh
    dhb_ref[...] = dh.astype(BF16)
    part = jnp.sum(dy * (x * r), axis=0, keepdims=True)

    @pl.when(first)
    def _():
        dg_ref[...] = part

    @pl.when(jnp.logical_not(first))
    def _():
        dg_ref[...] += part


def _mm_nn(a, b, *, out_dtype, name, res=None, tm=1024):
    m, k = a.shape
    n = b.shape[1]
    tm = _tile(m, tm)
    spec_o = pl.BlockSpec((tm, n), lambda i, j, kk: (i, 0))
    return _mm(a, b, dims="nn", grid=(m // tm, 1, 1),
               a_spec=pl.BlockSpec((tm, k), lambda i, j, kk: (i, 0)),
               b_spec=pl.BlockSpec((k, n), lambda i, j, kk: (0, 0)),
               o_spec=spec_o, out_shape=(m, n), out_dtype=out_dtype, acc_shape=None, name=name,
               res=res, res_spec=spec_o if res is not None else None)


def _norm_bwd_arg(h, gain, dres, tm):
    d = h.shape[1]
    return (h, gain.reshape(1, d), dres, pl.BlockSpec((tm, d), lambda i, j, kk: (i, 0)),
            pl.BlockSpec((1, d), lambda i, j, kk: (0, 0)))


def _mm_nt(a, b, *, out_dtype=F32, name, res=None, tm=1024, norm_bwd=None):
    m, k = a.shape
    n = b.shape[0]
    tm = _tile(m, tm)
    spec_o = pl.BlockSpec((tm, n), lambda i, j, kk: (i, 0))
    return _mm(a, b, dims="nt", grid=(m // tm, 1, 1),
               a_spec=pl.BlockSpec((tm, k), lambda i, j, kk: (i, 0)),
               b_spec=pl.BlockSpec((n, k), lambda i, j, kk: (0, 0)),
               o_spec=spec_o, out_shape=(m, n), out_dtype=out_dtype, acc_shape=None, name=name,
               res=res, res_spec=spec_o if res is not None else None,
               norm_bwd=None if norm_bwd is None else _norm_bwd_arg(*norm_bwd, tm))


_TN_ROWS = 2048


def _mm_tn(a, b, *, name, tm=None, tn=None, ts=_TN_ROWS, out_dtype=BF16):
    s, m = a.shape
    n = b.shape[1]
    tm = m if tm is None else tm
    tn = n if tn is None else tn
    ts = _tile(s, ts)
    return _mm(a, b, dims="tn", grid=(m // tm, n // tn, s // ts),
               a_spec=pl.BlockSpec((ts, tm), lambda i, j, kk: (kk, i)),
               b_spec=pl.BlockSpec((ts, tn), lambda i, j, kk: (kk, j)),
               o_spec=pl.BlockSpec((tm, tn), lambda i, j, kk: (i, j)),
               out_shape=(m, n), out_dtype=out_dtype, acc_shape=(tm, tn), name=name)


def _mm_up(xn, w_up, *, name, tm=512):
    s, d = xn.shape
    f = w_up.shape[1] // 2
    tm = _tile(s, tm)
    return _mm(xn, w_up, dims="nn", grid=(2, s // tm, 1),
               a_spec=pl.BlockSpec((tm, d), lambda j, i, kk: (i, 0)),
               b_spec=pl.BlockSpec((d, f), lambda j, i, kk: (0, j)),
               o_spec=pl.BlockSpec((None, tm, f), lambda j, i, kk: (j, i, 0)),
               out_shape=(2, s, f), out_dtype=F32, acc_shape=None, name=name)


def _mm_up_nt(dhh, w_up, *, name, tm=1024, norm_bwd=None):
    _, s, f = dhh.shape
    d = w_up.shape[0]
    tm = _tile(s, tm)
    return _mm(dhh, w_up, dims="nt", grid=(s // tm, 1, 2),
               a_spec=pl.BlockSpec((None, tm, f), lambda i, j, kk: (kk, i, 0)),
               b_spec=pl.BlockSpec((d, f), lambda i, j, kk: (0, kk)),
               o_spec=pl.BlockSpec((tm, d), lambda i, j, kk: (i, 0)),
               out_shape=(s, d), out_dtype=F32, acc_shape=(tm, d), name=name,
               norm_bwd=None if norm_bwd is None else _norm_bwd_arg(*norm_bwd, tm))


def _mm_up_tn(xn, dhh, *, name, ts=_TN_ROWS):
    s, d = xn.shape
    f = dhh.shape[2]
    tn = f // 2
    ts = _tile(s, ts)
    return _mm(xn, dhh, dims="tn", grid=(1, 4, s // ts),
               a_spec=pl.BlockSpec((ts, d), lambda i, j, kk: (kk, 0)),
               b_spec=pl.BlockSpec((None, ts, tn), lambda i, j, kk: (j // 2, kk, j % 2)),
               o_spec=pl.BlockSpec((d, tn), lambda i, j, kk: (0, j)),
               out_shape=(d, 2 * f), out_dtype=BF16, acc_shape=(d, tn), name=name)


def _rms_fwd(h, g, *, name, tr=512):
    s, d = h.shape
    tr = _tile(s, tr)

    def body(h_ref, g_ref, o_ref):
        x = h_ref[...]
        r = lax.rsqrt(jnp.mean(x * x, axis=-1, keepdims=True) + EPS)
        o_ref[...] = ((x * r) * g_ref[...]).astype(o_ref.dtype)

    row = pl.BlockSpec((tr, d), lambda i: (i, 0))
    return pl.pallas_call(
        body, grid=(s // tr,), in_specs=[row, pl.BlockSpec((1, d), lambda i: (0, 0))], out_specs=row,
        out_shape=jax.ShapeDtypeStruct((s, d), BF16), name=name, compiler_params=_params("parallel"),
    )(h, g.reshape(1, d))


def _rms_bwd(h, dxn, g, dres, *, name, tr=512):
    s, d = h.shape
    tr = _tile(s, tr)
    has_res = dres is not None

    def body(*refs):
        if has_res:
            h_ref, dxn_ref, g_ref, dres_ref, dh_ref, dhb_ref, dg_ref = refs
        else:
            h_ref, dxn_ref, g_ref, dh_ref, dhb_ref, dg_ref = refs
            dres_ref = None
        _rms_bwd_math(dxn_ref[...].astype(F32), h_ref, g_ref, dres_ref, dh_ref, dhb_ref, dg_ref,
                      first=pl.program_id(0) == 0)

    row = pl.BlockSpec((tr, d), lambda i: (i, 0))
    vec = pl.BlockSpec((1, d), lambda i: (0, 0))
    in_specs = [row, row, vec] + ([row] if has_res else [])
    args = (h, dxn, g.reshape(1, d)) + ((dres,) if has_res else ())
    return pl.pallas_call(
        body, grid=(s // tr,), in_specs=in_specs, out_specs=[row, row, vec],
        out_shape=[jax.ShapeDtypeStruct((s, d), F32), jax.ShapeDtypeStruct((s, d), BF16),
                   jax.ShapeDtypeStruct((1, d), F32)],
        name=name, compiler_params=_params("arbitrary"),
    )(*args)


def _loss_head(h, g, target, *, name, tr=512):
    s, d = h.shape
    tr = _tile(s, tr)
    nt = s // tr

    def body(h_ref, g_ref, t_ref, dh_ref, dhb_ref, dg_ref, loss_ref, sq_ref):
        i = pl.program_id(0)
        x = h_ref[...]
        gain = g_ref[...]
        r = lax.rsqrt(jnp.mean(x * x, axis=-1, keepdims=True) + EPS)
        xh = x * r
        err = xh * gain - t_ref[...]
        dy = err * (1.0 / d)
        a = dy * gain
        m = jnp.mean(a * x, axis=-1, keepdims=True)
        dh = r * a - x * (r * r * r * m)
        dh_ref[...] = dh
        dhb_ref[...] = dh.astype(BF16)
        dg_part = jnp.sum(dy * xh, axis=0, keepdims=True)
        sq_part = jnp.sum(err * err, axis=0, keepdims=True)

        @pl.when(i == 0)
        def _():
            dg_ref[...] = dg_part
            sq_ref[...] = sq_part

        @pl.when(i > 0)
        def _():
            dg_ref[...] += dg_part
            sq_ref[...] += sq_part

        @pl.when(i == nt - 1)
        def _():
            total = jnp.sum(sq_ref[...], axis=1, keepdims=True) * (0.5 / d)
            loss_ref[...] = jnp.broadcast_to(total, loss_ref.shape)

    row = pl.BlockSpec((tr, d), lambda i: (i, 0))
    vec = pl.BlockSpec((1, d), lambda i: (0, 0))
    return pl.pallas_call(
        body, grid=(nt,), in_specs=[row, vec, row],
        out_specs=[row, row, vec, pl.BlockSpec((1, 128), lambda i: (0, 0))],
        out_shape=[jax.ShapeDtypeStruct((s, d), F32), jax.ShapeDtypeStruct((s, d), BF16),
                   jax.ShapeDtypeStruct((1, d), F32), jax.ShapeDtypeStruct((1, 128), F32)],
        scratch_shapes=[pltpu.VMEM((1, d), F32)], name=name, compiler_params=_params("arbitrary"),
    )(h, g.reshape(1, d), target)


_SQRT_HALF = 0.7071067811865476
_INV_SQRT_2PI = 0.3989422804014327


def _gelu(x):
    return 0.5 * x * (1.0 + lax.erf(x * _SQRT_HALF))


def _gelu_grad(x):
    return 0.5 * (1.0 + lax.erf(x * _SQRT_HALF)) + x * (jnp.exp(-0.5 * x * x) * _INV_SQRT_2PI)


def _trailing_sums(xe, win):
    s = xe
    sh = 1
    while sh < win:
        s = s + pltpu.roll(s, sh, 0)
        sh *= 2
    return s


def _leading_sums(xe, win):
    n = xe.shape[0]
    s = xe
    sh = 1
    while sh < win:
        s = s + pltpu.roll(s, n - sh, 0)
        sh *= 2
    return s


def _tril_mask():
    return lax.broadcasted_iota(jnp.int32, (GROUP, GROUP), 0) >= lax.broadcasted_iota(jnp.int32, (GROUP, GROUP), 1)


def _layernorm_stats(v):
    mu = jnp.mean(v, axis=-1, keepdims=True)
    xc = v - mu
    rstd = lax.rsqrt(jnp.mean(xc * xc, axis=-1, keepdims=True) + EPS)
    return xc * rstd, rstd


def _mixer_specs(s, t):
    halo_blocks = t // POOL_HALO
    tile = lambda w: pl.BlockSpec((t, w), lambda i: (i, 0))
    prev = pl.BlockSpec((POOL_HALO, POOL_WIDTH), lambda i: (jnp.maximum(i * halo_blocks - 1, 0), 0))
    nxt = pl.BlockSpec((POOL_HALO, POOL_WIDTH),
                       lambda i: (jnp.minimum((i + 1) * halo_blocks, s // POOL_HALO - 1), 0))
    const3 = pl.BlockSpec((HEADS, GROUP, GROUP), lambda i: (0, 0, 0))
    vec = pl.BlockSpec((1, POOL_WIDTH), lambda i: (0, 0))
    bias = pl.BlockSpec((GROUP, SGU_WIDTH), lambda i: (0, 0))
    return tile, prev, nxt, const3, vec, bias


def _mixer_fwd(proj, pool_w, pool_scale, sgu_g, sgu_w, sgu_bias, *, name, t=256):
    s = proj.shape[0]
    t = _tile(s, t)
    tile, prev, _, const3, vec, bias = _mixer_specs(s, t)

    def body(proj_ref, halo_ref, pw_ref, ps_ref, sg_ref, sw_ref, sb_ref, cat_ref):
        i = pl.program_id(0)
        row = i * t + lax.broadcasted_iota(jnp.int32, (t, 1), 0)
        p = proj_ref[:, 0:POOL_WIDTH]
        pe = jnp.concatenate([jnp.where(i > 0, halo_ref[...], 0.0), p], axis=0)
        for gi, win in enumerate(POOL_WINDOWS):
            cols = slice(gi * GROUP, (gi + 1) * GROUP)
            count = jnp.minimum(row + 1, win).astype(F32)
            d = _trailing_sums(pe[:, cols], win)[POOL_HALO:] / count - p[:, cols]
            y = jnp.dot(d.astype(BF16), pw_ref[gi].astype(BF16), preferred_element_type=F32) * ps_ref[:, cols]
            cat_ref[:, cols] = y.astype(BF16)

        u = _gelu(proj_ref[:, POOL_WIDTH:POOL_WIDTH + SGU_WIDTH])
        xhat, _ = _layernorm_stats(_gelu(proj_ref[:, POOL_WIDTH + SGU_WIDTH:]))
        vn = (xhat * sg_ref[...]).astype(BF16)
        tri = _tril_mask()
        for h in range(HEADS):
            cols = slice(h * GROUP, (h + 1) * GROUP)
            w = jnp.where(tri, sw_ref[h], 0.0).astype(BF16)
            for c in range(t // GROUP):
                rows = slice(c * GROUP, (c + 1) * GROUP)
                z = jnp.dot(w, vn[rows, cols], preferred_element_type=F32) + sb_ref[:, cols]
                cat_ref[rows, POOL_WIDTH + h * GROUP:POOL_WIDTH + (h + 1) * GROUP] = (u[rows, cols] * z).astype(BF16)

    return pl.pallas_call(
        body, grid=(s // t,),
        in_specs=[tile(POOL_WIDTH + 2 * SGU_WIDTH), prev, const3, vec, vec, const3, bias],
        out_specs=tile(POOL_WIDTH + SGU_WIDTH),
        out_shape=jax.ShapeDtypeStruct((s, POOL_WIDTH + SGU_WIDTH), BF16), name=name,
        compiler_params=_params("parallel"),
    )(proj, proj, pool_w, pool_scale, sgu_g, sgu_w, sgu_bias)


def _mixer_bwd(proj, dcat, pool_w, pool_scale, sgu_g, sgu_w, sgu_bias, *, name, t=256):
    s = proj.shape[0]
    t = _tile(s, t)
    nt = s // t
    tile, prev, nxt, const3, vec, bias = _mixer_specs(s, t)

    def body(proj_ref, halo_ref, dcat_ref, dnext_ref, pw_ref, ps_ref, sg_ref, sw_ref, sb_ref,
             dproj_ref, dpw_ref, dps_ref, dsg_ref, dsw_ref, dsb_ref, du_ref, dvn_ref, dz_ref):
        i = pl.program_id(0)

        @pl.when(i == 0)
        def _():
            dpw_ref[...] = jnp.zeros_like(dpw_ref)
            dps_ref[...] = jnp.zeros_like(dps_ref)
            dsg_ref[...] = jnp.zeros_like(dsg_ref)
            dsw_ref[...] = jnp.zeros_like(dsw_ref)
            dz_ref[...] = jnp.zeros_like(dz_ref)

        row = i * t + lax.broadcasted_iota(jnp.int32, (t, 1), 0)
        row_e = i * t + lax.broadcasted_iota(jnp.int32, (t + POOL_HALO, 1), 0)
        p = proj_ref[:, 0:POOL_WIDTH]
        pe = jnp.concatenate([jnp.where(i > 0, halo_ref[...], 0.0), p], axis=0)
        dyp = dcat_ref[:, 0:POOL_WIDTH]
        dye = jnp.concatenate([dyp, jnp.where(i < nt - 1, dnext_ref[...], 0.0)], axis=0)
        for gi, win in enumerate(POOL_WINDOWS):
            cols = slice(gi * GROUP, (gi + 1) * GROUP)
            count = jnp.minimum(row + 1, win).astype(F32)
            d = (_trailing_sums(pe[:, cols], win)[POOL_HALO:] / count - p[:, cols]).astype(BF16)
            pw = pw_ref[gi].astype(BF16)
            pre = jnp.dot(d, pw, preferred_element_type=F32)
            dps_ref[:, cols] += jnp.sum(dyp[:, cols] * pre, axis=0, keepdims=True)
            ys = (dye[:, cols] * ps_ref[:, cols]).astype(BF16)
            dpw_ref[gi] += lax.dot_general(d, ys[:t], _DOT_DIMS["tn"], preferred_element_type=F32)
            dd = lax.dot_general(ys, pw, _DOT_DIMS["nt"], preferred_element_type=F32)
            count_e = jnp.minimum(row_e + 1, win).astype(F32)
            dp = _leading_sums(dd / count_e, win)[:t] - dd[:t]
            dproj_ref[:, cols] = dp.astype(BF16)

        xu = proj_ref[:, POOL_WIDTH:POOL_WIDTH + SGU_WIDTH]
        xv = proj_ref[:, POOL_WIDTH + SGU_WIDTH:]
        u = _gelu(xu)
        xhat, rstd = _layernorm_stats(_gelu(xv))
        gain = sg_ref[...]
        vn = (xhat * gain).astype(BF16)
        tri = _tril_mask()
        for h in range(HEADS):
            cols = slice(h * GROUP, (h + 1) * GROUP)
            wf = jnp.where(tri, sw_ref[h], 0.0)
            w, wt = wf.astype(BF16), wf.T.astype(BF16)
            for c in range(t // GROUP):
                rows = slice(c * GROUP, (c + 1) * GROUP)
                vch = vn[rows, cols]
                z = jnp.dot(w, vch, preferred_element_type=F32) + sb_ref[:, cols]
                dy = dcat_ref[rows, POOL_WIDTH + h * GROUP:POOL_WIDTH + (h + 1) * GROUP]
                du_ref[rows, cols] = dy * z
                dz = dy * u[rows, cols]
                dz_ref[:, cols] += dz
                dzb = dz.astype(BF16)
                dsw_ref[h] += lax.dot_general(dzb, vch, _DOT_DIMS["nt"], preferred_element_type=F32)
                dvn_ref[rows, cols] = jnp.dot(wt, dzb, preferred_element_type=F32)
        dvn = dvn_ref[...]
        dsg_ref[...] += jnp.sum(dvn * xhat, axis=0, keepdims=True)
        dxh = dvn * gain
        dv = rstd * (dxh - jnp.mean(dxh, axis=-1, keepdims=True)
                     - xhat * jnp.mean(dxh * xhat, axis=-1, keepdims=True))
        dproj_ref[:, POOL_WIDTH:POOL_WIDTH + SGU_WIDTH] = (du_ref[...] * _gelu_grad(xu)).astype(BF16)
        dproj_ref[:, POOL_WIDTH + SGU_WIDTH:] = (dv * _gelu_grad(xv)).astype(BF16)

        @pl.when(i == nt - 1)
        def _():
            for h in range(HEADS):
                dsw_ref[h] = jnp.where(tri, dsw_ref[h], 0.0)
            lane = lax.broadcasted_iota(jnp.int32, (GROUP, GROUP), 1)
            out = jnp.zeros((GROUP, GROUP), F32)
            for h in range(HEADS):
                sh = jnp.sum(dz_ref[:, h * GROUP:(h + 1) * GROUP], axis=1, keepdims=True)
                out = jnp.where(lane == h, sh, out)
            dsb_ref[...] = out

    outs = pl.pallas_call(
        body, grid=(nt,),
        in_specs=[tile(POOL_WIDTH + 2 * SGU_WIDTH), prev, tile(POOL_WIDTH + SGU_WIDTH), nxt,
                  const3, vec, vec, const3, bias],
        out_specs=[tile(POOL_WIDTH + 2 * SGU_WIDTH), const3, vec, vec, const3,
                   pl.BlockSpec((GROUP, GROUP), lambda i: (0, 0))],
        out_shape=[jax.ShapeDtypeStruct((s, POOL_WIDTH + 2 * SGU_WIDTH), BF16),
                   jax.ShapeDtypeStruct((HEADS, GROUP, GROUP), F32),
                   jax.ShapeDtypeStruct((1, POOL_WIDTH), F32),
                   jax.ShapeDtypeStruct((1, SGU_WIDTH), F32),
                   jax.ShapeDtypeStruct((HEADS, GROUP, GROUP), F32),
                   jax.ShapeDtypeStruct((GROUP, GROUP), F32)],
        scratch_shapes=[pltpu.VMEM((t, SGU_WIDTH), F32), pltpu.VMEM((t, SGU_WIDTH), F32),
                        pltpu.VMEM((GROUP, SGU_WIDTH), F32)],
        name=name, compiler_params=_params("arbitrary"),
    )(proj, proj, dcat, dcat, pool_w, pool_scale, sgu_g, sgu_w, sgu_bias)
    dproj, dpw, dps, dsg, dsw, dsb = outs
    return dproj, dpw, dps, dsg, dsw, dsb[:, :HEADS].T


def _attn_probs(q, k, scale):
    sc = lax.dot_general(q, k, _DOT_DIMS["nt"], preferred_element_type=F32) * scale
    sc = sc - jnp.max(sc, axis=-1, keepdims=True)
    e = jnp.exp(sc)
    return e / jnp.sum(e, axis=-1, keepdims=True)


def _attn_fwd(q, k, v, *, name, t=512):
    s, d = q.shape
    nm = k.shape[0]
    t = _tile(s, t)
    scale = HEAD_DIM ** -0.5

    def body(q_ref, k_ref, v_ref, o_ref):
        for h in range(HEADS):
            cols = slice(h * HEAD_DIM, (h + 1) * HEAD_DIM)
            pr = _attn_probs(q_ref[:, cols], k_ref[:, cols], scale)
            o_ref[:, cols] = jnp.dot(pr.astype(BF16), v_ref[:, cols], preferred_element_type=F32).astype(BF16)

    row = pl.BlockSpec((t, d), lambda i: (i, 0))
    kv = pl.BlockSpec((nm, d), lambda i: (0, 0))
    return pl.pallas_call(
        body, grid=(s // t,), in_specs=[row, kv, kv], out_specs=row,
        out_shape=jax.ShapeDtypeStruct((s, d), BF16), name=name, compiler_params=_params("parallel"),
    )(q, k, v)


def _attn_bwd(q, k, v, do, *, name, t=512):
    s, d = q.shape
    nm = k.shape[0]
    t = _tile(s, t)
    scale = HEAD_DIM ** -0.5

    def body(q_ref, k_ref, v_ref, do_ref, dq_ref, dk_ref, dv_ref):
        i = pl.program_id(0)

        @pl.when(i == 0)
        def _():
            dk_ref[...] = jnp.zeros_like(dk_ref)
            dv_ref[...] = jnp.zeros_like(dv_ref)

        for h in range(HEADS):
            cols = slice(h * HEAD_DIM, (h + 1) * HEAD_DIM)
            qh, kh, vh, doh = q_ref[:, cols], k_ref[:, cols], v_ref[:, cols], do_ref[:, cols]
            pr = _attn_probs(qh, kh, scale)
            dpr = lax.dot_general(doh, vh, _DOT_DIMS["nt"], preferred_element_type=F32)
            ds = (pr * (dpr - jnp.sum(dpr * pr, axis=-1, keepdims=True)) * scale).astype(BF16)
            dv_ref[:, cols] += lax.dot_general(pr.astype(BF16), doh, _DOT_DIMS["tn"], preferred_element_type=F32)
            dk_ref[:, cols] += lax.dot_general(ds, qh, _DOT_DIMS["tn"], preferred_element_type=F32)
            dq_ref[:, cols] = jnp.dot(ds, kh, preferred_element_type=F32).astype(BF16)

    row = pl.BlockSpec((t, d), lambda i: (i, 0))
    kv = pl.BlockSpec((nm, d), lambda i: (0, 0))
    return pl.pallas_call(
        body, grid=(s // t,), in_specs=[row, kv, kv, row], out_specs=[row, kv, kv],
        out_shape=[jax.ShapeDtypeStruct((s, d), BF16), jax.ShapeDtypeStruct((nm, d), F32),
                   jax.ShapeDtypeStruct((nm, d), F32)],
        name=name, compiler_params=_params("arbitrary"),
    )(q, k, v, do)


def _conv_specs(s, f, t, tc, swap):
    hb = t // CONV_HALO
    order = (lambda fn: (lambda j, i: fn(i, j))) if swap else (lambda fn: fn)
    tile3 = pl.BlockSpec((2, t, tc), order(lambda i, j: (0, i, j)))
    prev3 = pl.BlockSpec((2, CONV_HALO, tc), order(lambda i, j: (0, jnp.maximum(i * hb - 1, 0), j)))
    next3 = pl.BlockSpec((2, CONV_HALO, tc),
                         order(lambda i, j: (0, jnp.minimum((i + 1) * hb, s // CONV_HALO - 1), j)))
    tile2 = pl.BlockSpec((t, tc), order(lambda i, j: (i, j)))
    next2 = pl.BlockSpec((CONV_HALO, tc), order(lambda i, j: (jnp.minimum((i + 1) * hb, s // CONV_HALO - 1), j)))
    wspec = pl.BlockSpec((2, 3, tc), order(lambda i, j: (0, 0, j)))
    bspec = pl.BlockSpec((2, 1, tc), order(lambda i, j: (0, 0, j)))
    return tile3, prev3, next3, tile2, next2, wspec, bspec


def _conv3(w_ref, p, x2, x1, x0, b):
    return (w_ref[p, 0:1, :] * x2 + w_ref[p, 1:2, :] * x1 + w_ref[p, 2:3, :] * x0) + b


def _convgate_fwd(hh, cw, cb, *, name, t=256, tc=1408):
    _, s, f = hh.shape
    t = _tile(s, t)
    tile3, prev3, _, tile2, _, wspec, bspec = _conv_specs(s, f, t, tc, swap=False)

    def body(hh_ref, prev_ref, cw_ref, cb_ref, act_ref):
        i = pl.program_id(0)
        hc = []
        for p in range(2):
            xe = jnp.concatenate([jnp.where(i > 0, prev_ref[p], 0.0), hh_ref[p]], axis=0)
            hc.append(_conv3(cw_ref, p, pltpu.roll(xe, 2, 0), pltpu.roll(xe, 1, 0), xe, cb_ref[p])[CONV_HALO:])
        gate, val = hc
        act_ref[...] = ((gate * jax.nn.sigmoid(gate)) * val).astype(BF16)

    return pl.pallas_call(
        body, grid=(s // t, f // tc), in_specs=[tile3, prev3, wspec, bspec], out_specs=tile2,
        out_shape=jax.ShapeDtypeStruct((s, f), BF16), name=name, compiler_params=_params("parallel", "parallel"),
    )(hh, hh, cw, cb)


def _convgate_bwd(hh, dact, cw, cb, *, name, t=128, tc=1408):
    _, s, f = hh.shape
    t = _tile(s, t)
    nt = s // t
    tile3, prev3, next3, tile2, next2, wspec, bspec = _conv_specs(s, f, t, tc, swap=True)

    def body(hh_ref, prev_ref, next_ref, da_ref, danext_ref, cw_ref, cb_ref, dhh_ref, dcw_ref, dcb_ref):
        i = pl.program_id(1)
        is_last = i == nt - 1

        @pl.when(i == 0)
        def _():
            dcw_ref[...] = jnp.zeros_like(dcw_ref)
            dcb_ref[...] = jnp.zeros_like(dcb_ref)

        taps, hc = [], []
        for p in range(2):
            xe = jnp.concatenate([jnp.where(i > 0, prev_ref[p], 0.0), hh_ref[p],
                                  jnp.where(is_last, 0.0, next_ref[p])], axis=0)
            x2, x1 = pltpu.roll(xe, 2, 0), pltpu.roll(xe, 1, 0)
            hc.append(_conv3(cw_ref, p, x2, x1, xe, cb_ref[p])[CONV_HALO:])
            taps.append((x2[CONV_HALO:CONV_HALO + t], x1[CONV_HALO:CONV_HALO + t], xe[CONV_HALO:CONV_HALO + t]))
        gate, val = hc
        dae = jnp.concatenate([da_ref[...], jnp.where(is_last, 0.0, danext_ref[...])], axis=0)
        sg = jax.nn.sigmoid(gate)
        dval = dae * (gate * sg)
        dgate = dae * val * (sg * (1.0 + gate * (1.0 - sg)))
        m = t + CONV_HALO
        for p, dhc in enumerate((dgate, dval)):
            dh = (cw_ref[p, 2:3, :] * dhc + cw_ref[p, 1:2, :] * pltpu.roll(dhc, m - 1, 0)
                  + cw_ref[p, 0:1, :] * pltpu.roll(dhc, m - 2, 0))
            dhh_ref[p] = dh[:t].astype(BF16)
            d0 = dhc[:t]
            for kk, tap in enumerate(taps[p]):
                dcw_ref[p, kk:kk + 1, :] += jnp.sum(d0 * tap, axis=0, keepdims=True)
            dcb_ref[p] += jnp.sum(d0, axis=0, keepdims=True)

    return pl.pallas_call(
        body, grid=(f // tc, nt), in_specs=[tile3, prev3, next3, tile2, next2, wspec, bspec],
        out_specs=[tile3, wspec, bspec],
        out_shape=[jax.ShapeDtypeStruct((2, s, f), BF16), jax.ShapeDtypeStruct((2, 3, f), F32),
                   jax.ShapeDtypeStruct((2, 1, f), F32)],
        name=name, compiler_params=_params("parallel", "arbitrary"),
    )(hh, hh, hh, dact, dact, cw, cb)


def _position():
    return lax.axis_index("x"), lax.axis_index("y"), lax.axis_index("c")


def _linear(px, py, pc):
    return 4 * px + 2 * py + pc


_ANY = pl.BlockSpec(memory_space=pl.ANY)


def _all_gather(shards, *, name):
    n = len(shards)

    def body(*refs):
        x_refs, o_refs = refs[:n], refs[n:2 * n]
        send_sems, recv_sems, local_sems = refs[2 * n:]
        x, y, c = _position()
        me, sibling = (x, y, c), (x, y, 1 - c)
        chips = [(1 - x, y), (x, 1 - y), (1 - x, 1 - y)]

        def copy(ti, k, block, to, src=None):
            dst = o_refs[ti].at[_linear(*block)]
            return pltpu.make_async_remote_copy(
                src_ref=dst if src is None else src, dst_ref=dst,
                send_sem=send_sems.at[ti, k], recv_sem=recv_sems.at[ti, k],
                device_id=to, device_id_type=MESH)

        mine = [pltpu.make_async_copy(x_refs[ti], o_refs[ti].at[_linear(*me)], local_sems.at[ti]) for ti in range(n)]
        for cp in mine:
            cp.start()
        first = []
        for ti in range(n):
            first.append(copy(ti, 0, me, sibling, src=x_refs[ti]))
            for j, chip in enumerate(chips):
                first.append(copy(ti, 1 + j, me, (*chip, c), src=x_refs[ti]))
        for cp in first:
            cp.start()
        passed = []
        for j, chip in enumerate(chips):
            for ti in range(n):
                copy(ti, 1 + j, (*chip, c), me).wait_recv()
                fwd = copy(ti, 4 + j, (*chip, c), sibling)
                fwd.start()
                passed.append(fwd)
        for ti in range(n):
            copy(ti, 0, sibling, me).wait_recv()
            for j, chip in enumerate(chips):
                copy(ti, 4 + j, (*chip, 1 - c), me).wait_recv()
        for cp in first + passed:
            cp.wait_send()
        for cp in mine:
            cp.wait()

    return pl.pallas_call(
        body, in_specs=[_ANY] * n, out_specs=[_ANY] * n,
        out_shape=[jax.ShapeDtypeStruct((N_DEV,) + a.shape, a.dtype) for a in shards],
        scratch_shapes=[pltpu.SemaphoreType.DMA((n, 7)), pltpu.SemaphoreType.DMA((n, 7)),
                        pltpu.SemaphoreType.DMA((n,))],
        name=name,
    )(*shards)


def _peers_of(x, y, c):
    peers = []
    for mask in range(1, N_DEV):
        peers.append((1 - x if mask & 4 else x, 1 - y if mask & 2 else y, 1 - c if mask & 1 else c))
    return peers


_HBM = pl.BlockSpec(memory_space=pltpu.HBM)
_SEM = pl.BlockSpec(memory_space=pltpu.SEMAPHORE)
_EFFECT = pltpu.SideEffectType.DATAFLOW_SIDE_EFFECTING


def _exchange_copy(src_ref, land_ref, send_sem, recv_sem, peer, mine, scatter, arriving):
    src = src_ref.at[_linear(*peer)] if scatter else src_ref
    dst = land_ref.at[_linear(*peer) if arriving else mine]
    return pltpu.make_async_remote_copy(src_ref=src, dst_ref=dst, send_sem=send_sem, recv_sem=recv_sem,
                                        device_id=peer, device_id_type=MESH)


_EXCHANGE_COLLECTIVE_ID = 7


def _sequencer_exchange(srcs, *, scatter, name):
    n = len(srcs)
    src_refs = [jax.new_ref(a, memory_space=pltpu.MemorySpace.HBM) for a in srcs]
    land_refs = [jax.empty_ref(jax.ShapeDtypeStruct(a.shape if scatter else (N_DEV,) + a.shape, a.dtype),
                               memory_space=pltpu.MemorySpace.HBM) for a in srcs]

    @pl.kernel(mesh=plsc.ScalarSubcoreMesh(axis_name="sequencer", num_cores=1), name=name,
               scratch_types=(pltpu.SemaphoreType.DMA((7 * n,)), pltpu.SemaphoreType.DMA((7 * n,)),
                              pltpu.SemaphoreType.DMA((n,))),
               compiler_params=pltpu.CompilerParams(collective_id=_EXCHANGE_COLLECTIVE_ID))
    def launch(send_sems, recv_sems, local_sems):
        x, y, c = _position()
        mine = _linear(x, y, c)
        peers = _peers_of(x, y, c)
        barrier = pltpu.get_barrier_semaphore()
        for peer in peers:
            pl.semaphore_signal(barrier, inc=1, device_id=peer, device_id_type=MESH)
        pl.semaphore_wait(barrier, N_DEV - 1)
        local = [pltpu.make_async_copy(src_refs[t].at[mine] if scatter else src_refs[t], land_refs[t].at[mine],
                                       local_sems.at[t]) for t in range(n)]
        for cp in local:
            cp.start()
        sends = []
        for t in range(n):
            for k, peer in enumerate(peers):
                cp = _exchange_copy(src_refs[t], land_refs[t], send_sems.at[7 * t + k], recv_sems.at[7 * t + k],
                                    peer, mine, scatter, arriving=False)
                cp.start()
                sends.append(cp)
        for cp in local:
            cp.wait()
        for cp in sends:
            cp.wait_send()
        for t in range(n):
            for k, peer in enumerate(peers):
                _exchange_copy(src_refs[t], land_refs[t], send_sems.at[7 * t + k], recv_sems.at[7 * t + k],
                               peer, mine, scatter, arriving=True).wait_recv()

    launch()
    return [r[...] for r in land_refs]


def _exchange_start(srcs, groups, *, scatter, name):
    n = len(srcs)
    ng = len(groups)
    lands = [lax.empty(a.shape if scatter else (N_DEV,) + a.shape, a.dtype) for a in srcs]

    def body(*refs):
        src_refs, land_refs = refs[:n], refs[n:2 * n]
        sem_refs = refs[2 * n:2 * n + 2 * ng]
        token_ref = refs[2 * n + 2 * ng + 2 * n]
        local_sems = refs[-1]
        x, y, c = _position()
        mine = _linear(x, y, c)
        peers = _peers_of(x, y, c)
        local = [pltpu.make_async_copy(src_refs[t].at[mine] if scatter else src_refs[t], land_refs[t].at[mine],
                                       local_sems.at[t]) for t in range(n)]
        for cp in local:
            cp.start()
        for cp in local:
            cp.wait()
        for gi, group in enumerate(groups):
            for pos, t in enumerate(group):
                for k, peer in enumerate(peers):
                    _exchange_copy(src_refs[t], land_refs[t], sem_refs[2 * gi].at[7 * pos + k],
                                   sem_refs[2 * gi + 1].at[7 * pos + k], peer, mine, scatter, arriving=False).start()
        token_ref[...] = jnp.zeros_like(token_ref)

    sem_shapes = []
    for group in groups:
        sem_shapes += [pltpu.SemaphoreType.DMA((7 * len(group),))] * 2
    outs = pl.pallas_call(
        body, name=name,
        out_shape=tuple(sem_shapes) + tuple(pltpu.HBM(a.shape, a.dtype) for a in srcs)
        + tuple(pltpu.HBM(a.shape, a.dtype) for a in lands) + (jax.ShapeDtypeStruct((8, 128), F32),),
        in_specs=[_HBM] * (2 * n),
        out_specs=(_SEM,) * (2 * ng) + (_HBM,) * (2 * n) + (pl.BlockSpec(memory_space=pltpu.VMEM),),
        input_output_aliases={i: 2 * ng + i for i in range(2 * n)},
        scratch_shapes=[pltpu.SemaphoreType.DMA((n,))],
        compiler_params=pltpu.CompilerParams(has_side_effects=_EFFECT),
    )(*[pltpu.with_memory_space_constraint(a, pltpu.HBM) for a in list(srcs) + lands])
    sems = [(outs[2 * gi], outs[2 * gi + 1]) for gi in range(ng)]
    return sems, list(outs[2 * ng:2 * ng + n]), list(outs[2 * ng + n:2 * ng + 2 * n]), outs[-1]


def _exchange_wait(sems, srcs, lands, after, *, scatter, name):
    n = len(srcs)
    send_sems, recv_sems = sems

    def body(*refs):
        src_refs, land_refs = refs[:n], refs[n:2 * n]
        send_ref, recv_ref = refs[2 * n], refs[2 * n + 1]
        x, y, c = _position()
        mine = _linear(x, y, c)
        for pos in range(n):
            for k, peer in enumerate(_peers_of(x, y, c)):
                cp = _exchange_copy(src_refs[pos], land_refs[pos], send_ref.at[7 * pos + k], recv_ref.at[7 * pos + k],
                                    peer, mine, scatter, arriving=True)
                cp.wait_send()
                cp.wait_recv()

    outs = pl.pallas_call(
        body, name=name,
        out_shape=tuple(pltpu.HBM(a.shape, a.dtype) for a in list(srcs) + list(lands)),
        in_specs=[_HBM] * (2 * n) + [_SEM, _SEM, _ANY], out_specs=(_HBM,) * (2 * n),
        input_output_aliases={i: i for i in range(2 * n)},
        compiler_params=pltpu.CompilerParams(has_side_effects=_EFFECT),
    )(*srcs, *lands, send_sems, recv_sems, after)
    return list(outs[n:])


def _adamw_math(g, w, m, v):
    m2 = ADAM_B1 * m + (1.0 - ADAM_B1) * g
    v2 = ADAM_B2 * v + (1.0 - ADAM_B2) * (g * g)
    m_hat = m2 / (1.0 - ADAM_B1 ** ADAM_STEP)
    v_hat = v2 / (1.0 - ADAM_B2 ** ADAM_STEP)
    delta = -ADAM_LR * (m_hat / (jnp.sqrt(v_hat) + ADAM_EPS) + ADAM_WD * w)
    return delta, m2, v2


def _adamw(slots, w, m, v, *, name, tr=256):
    depth = len(slots)
    _, r, c = slots[0].shape
    tr = next((cand for cand in range(min(r, tr), 15, -1) if r % cand == 0 and cand % 16 == 0), r)

    def body(*refs):
        s_refs = refs[:depth]
        w_ref, m_ref, v_ref, g_ref, d_ref, m2_ref, v2_ref = refs[depth:]
        layer = pl.program_id(0)
        for l in range(depth):
            @pl.when(layer == l)
            def _():
                g = s_refs[l][0].astype(F32)
                for d in range(1, N_DEV):
                    g = g + s_refs[l][d].astype(F32)
                delta, m2, v2 = _adamw_math(g, w_ref[...], m_ref[...], v_ref[...])
                g_ref[...] = g
                d_ref[...] = delta
                m2_ref[...] = m2
                v2_ref[...] = v2

    blk = pl.BlockSpec((None, tr, c), lambda layer, i: (layer, i, 0))
    sblks = [pl.BlockSpec((N_DEV, tr, c), lambda layer, i, l=l: (0, jnp.where(layer == l, i, 0), 0))
             for l in range(depth)]
    shape = jax.ShapeDtypeStruct((depth, r, c), F32)
    return pl.pallas_call(
        body, grid=(depth, r // tr), in_specs=sblks + [blk, blk, blk], out_specs=[blk] * 4,
        out_shape=[shape] * 4, name=name, compiler_params=_params("arbitrary", "arbitrary"),
    )(*slots, w, m, v)


_SHARDED = ("w_in", "w_out", "wq", "wk", "wv", "wo", "w_up", "conv_w", "w_down")
_SMALL = ("norm_mix_g", "pool_w", "pool_scale", "sgu_g", "sgu_w", "sgu_b", "norm_xattn_g", "mem_norm_g",
          "norm_ffn_g", "conv_b", "final_norm_g")
_WEIGHTS = ("norm_mix_g", "w_in", "pool_w", "pool_scale", "sgu_g", "sgu_w", "sgu_b", "w_out", "norm_xattn_g",
            "mem_norm_g", "wq", "wk", "wv", "wo", "norm_ffn_g", "w_up", "conv_w", "conv_b", "w_down",
            "final_norm_g")
_PACK_LANES = 128
_GATHER_GROUPS = (("w_in",), ("w_out",), ("wq", "wk", "wv", "wo"), ("w_up", "conv_w", "w_down"))


def _cols_to_blocks(a, *, name, tr=256):
    r, c8 = a.shape
    c = c8 // N_DEV
    tr = _tile(r, tr)

    def body(a_ref, o_ref):
        for dev in range(N_DEV):
            o_ref[dev] = a_ref[:, dev * c:(dev + 1) * c]

    return pl.pallas_call(
        body, grid=(r // tr,), in_specs=[pl.BlockSpec((tr, c8), lambda i: (i, 0))],
        out_specs=pl.BlockSpec((N_DEV, tr, c), lambda i: (0, i, 0)),
        out_shape=jax.ShapeDtypeStruct((N_DEV, r, c), a.dtype), name=name, compiler_params=_params("parallel"),
    )(a)


def _blocks_to_cols(a, *, name, tr=256):
    n, r, c = a.shape
    tr = _tile(r, tr)

    def body(a_ref, o_ref):
        for dev in range(n):
            o_ref[:, dev * c:(dev + 1) * c] = a_ref[dev]

    return pl.pallas_call(
        body, grid=(r // tr,), in_specs=[pl.BlockSpec((n, tr, c), lambda i: (0, i, 0))],
        out_specs=pl.BlockSpec((tr, n * c), lambda i: (i, 0)),
        out_shape=jax.ShapeDtypeStruct((r, n * c), a.dtype), name=name, compiler_params=_params("parallel"),
    )(a)


def _pin(x, *deps):
    return lax.optimization_barrier((x, *deps))[0]


def _pack(arrays):
    flat = jnp.concatenate([a.reshape(-1) for a in arrays])
    assert flat.shape[0] % (8 * _PACK_LANES) == 0
    return flat.reshape(-1, _PACK_LANES)


def _unpack(packed, like):
    flat = packed.reshape(-1)
    out, off = [], 0
    for a in like:
        out.append(flat[off:off + a.size].reshape(a.shape))
        off += a.size
    return out


def kernel(x, mem, norm_mix_g, w_in, pool_w, pool_scale, sgu_g, sgu_w, sgu_b, w_out, norm_xattn_g, mem_norm_g, wq, wk, wv, wo, norm_ffn_g, w_up, conv_w, conv_b, w_down, final_norm_g, loss_target, m_norm_mix_g, m_w_in, m_pool_w, m_pool_scale, m_sgu_g, m_sgu_w, m_sgu_b, m_w_out, m_norm_xattn_g, m_mem_norm_g, m_wq, m_wk, m_wv, m_wo, m_norm_ffn_g, m_w_up, m_conv_w, m_conv_b, m_w_down, m_final_norm_g, v_norm_mix_g, v_w_in, v_pool_w, v_pool_scale, v_sgu_g, v_sgu_w, v_sgu_b, v_w_out, v_norm_xattn_g, v_mem_norm_g, v_wq, v_wk, v_wv, v_wo, v_norm_ffn_g, v_w_up, v_conv_w, v_conv_b, v_w_down, v_final_norm_g):
    W = dict(norm_mix_g=norm_mix_g, w_in=w_in, pool_w=pool_w, pool_scale=pool_scale, sgu_g=sgu_g, sgu_w=sgu_w,
             sgu_b=sgu_b, w_out=w_out, norm_xattn_g=norm_xattn_g, mem_norm_g=mem_norm_g, wq=wq, wk=wk, wv=wv, wo=wo,
             norm_ffn_g=norm_ffn_g, w_up=w_up, conv_w=conv_w, conv_b=conv_b, w_down=w_down,
             final_norm_g=final_norm_g)
    M = dict(norm_mix_g=m_norm_mix_g, w_in=m_w_in, pool_w=m_pool_w, pool_scale=m_pool_scale, sgu_g=m_sgu_g,
             sgu_w=m_sgu_w, sgu_b=m_sgu_b, w_out=m_w_out, norm_xattn_g=m_norm_xattn_g, mem_norm_g=m_mem_norm_g,
             wq=m_wq, wk=m_wk, wv=m_wv, wo=m_wo, norm_ffn_g=m_norm_ffn_g, w_up=m_w_up, conv_w=m_conv_w,
             conv_b=m_conv_b, w_down=m_w_down, final_norm_g=m_final_norm_g)
    V = dict(norm_mix_g=v_norm_mix_g, w_in=v_w_in, pool_w=v_pool_w, pool_scale=v_pool_scale, sgu_g=v_sgu_g,
             sgu_w=v_sgu_w, sgu_b=v_sgu_b, w_out=v_w_out, norm_xattn_g=v_norm_xattn_g, mem_norm_g=v_mem_norm_g,
             wq=v_wq, wk=v_wk, wv=v_wv, wo=v_wo, norm_ffn_g=v_norm_ffn_g, w_up=v_w_up, conv_w=v_conv_w,
             conv_b=v_conv_b, w_down=v_w_down, final_norm_g=v_final_norm_g)

    s, d = x.shape[1], x.shape[2]
    f = w_down.shape[1] * N_DEV
    h = x.reshape(s, d)
    memx = mem.reshape(mem.shape[1], d)
    target = loss_target.reshape(s, d)

    gathered = {}

    def launch_gather(l, gi, after):
        if l >= DEPTH:
            return
        names = _GATHER_GROUPS[gi]
        shards = [W[nme][l] if nme == "conv_w" else W[nme][l].astype(BF16) for nme in names]
        if after is not None:
            shards[0], _ = lax.optimization_barrier((shards[0], after))
        gathered[l, gi] = dict(zip(names, _sequencer_exchange(shards, scatter=False, name=f"gather_{l}_{gi}")))

    launch_gather(0, 0, None)

    saved, full = [], []
    for l in range(DEPTH):
        sgu_bias = jnp.repeat(sgu_b[l].T, GROUP, axis=1)
        cb = conv_b[l].reshape(2, 1, f)
        xn1 = _rms_fwd(h, norm_mix_g[l], name=f"norm_mix_{l}")
        if l == 0:
            launch_gather(0, 1, xn1)
        w_in_f = _blocks_to_cols(gathered[l, 0]["w_in"], name=f"w_in_cols_{l}")
        proj = _mm_nn(xn1, w_in_f, out_dtype=F32, name=f"proj_in_{l}")
        if l == 0:
            launch_gather(0, 2, proj)
        cat = _mixer_fwd(proj, pool_w[l], pool_scale[l].reshape(1, -1), sgu_g[l].reshape(1, -1), sgu_w[l], sgu_bias,
                         name=f"mixer_{l}")
        if l == 0:
            launch_gather(0, 3, cat)
        w_out_f = gathered[l, 1]["w_out"].reshape(-1, d)
        h1 = _mm_nn(cat, w_out_f, out_dtype=F32, res=h, name=f"proj_out_{l}")
        launch_gather(l + 1, 0, h1)
        xn2 = _rms_fwd(h1, norm_xattn_g[l], name=f"norm_xattn_{l}")
        g = gathered[l, 2]
        wq_f, wk_f, wv_f, wo_f = (g[nme].reshape(-1, d) for nme in ("wq", "wk", "wv", "wo"))
        q = _mm_nn(xn2, wq_f, out_dtype=BF16, name=f"q_{l}")
        launch_gather(l + 1, 1, q)
        memn = _rms_fwd(memx, mem_norm_g[l], name=f"norm_mem_{l}")
        k = _mm_nn(memn, wk_f, out_dtype=BF16, name=f"k_{l}")
        v = _mm_nn(memn, wv_f, out_dtype=BF16, name=f"v_{l}")
        o = _attn_fwd(q, k, v, name=f"attn_{l}")
        h2 = _mm_nn(o, wo_f, out_dtype=F32, res=h1, name=f"attn_out_{l}")
        launch_gather(l + 1, 2, h2)
        xn3 = _rms_fwd(h2, norm_ffn_g[l], name=f"norm_ffn_{l}")
        g = gathered[l, 3]
        w_up_f = _blocks_to_cols(g["w_up"], name=f"w_up_cols_{l}")
        conv_w_f = _blocks_to_cols(g["conv_w"], name=f"conv_w_cols_{l}").reshape(3, 2, f).transpose(1, 0, 2)
        w_down_f = g["w_down"].reshape(-1, d)
        hh = _mm_up(xn3, w_up_f, name=f"ffn_up_{l}")
        launch_gather(l + 1, 3, hh)
        act = _convgate_fwd(hh, conv_w_f, cb, name=f"convgate_{l}")
        h3 = _mm_nn(act, w_down_f, out_dtype=F32, res=h2, tm=512, name=f"ffn_down_{l}")
        full.append(dict(w_in=w_in_f, w_out=w_out_f, wq=wq_f, wk=wk_f, wv=wv_f, wo=wo_f, w_up=w_up_f,
                         conv_w=conv_w_f, w_down=w_down_f))
        saved.append(dict(h0=h, xn1=xn1, proj=proj, cat=cat, h1=h1, xn2=xn2, q=q, memn=memn, k=k, v=v, o=o, h2=h2,
                          xn3=xn3, hh=hh, act=act, sgu_bias=sgu_bias, cb=cb))
        h = h3

    dh, dhb, dg_final, loss_row = _loss_head(h, final_norm_g, target, name="loss_head")
    loss = lax.psum(loss_row[0, 0], ("x", "y", "c"))

    slots = {nme: [None] * DEPTH for nme in _SHARDED}
    small = [None] * DEPTH

    previous = []

    def scatter(l, tag, names, parts):
        parts = [_pin(parts[0], *previous)] + parts[1:]
        arrived = _sequencer_exchange(parts, scatter=True, name=f"scatter_{tag}_{l}")
        previous[:] = arrived[:1]
        for nme, land in zip(names, arrived):
            slots[nme][l] = land
        return parts

    for l in reversed(range(DEPTH)):
        fw, sv = full[l], saved[l]
        dact = _mm_nt(dhb, fw["w_down"], out_dtype=F32, tm=512, name=f"d_act_{l}")
        g_w_down = _mm_tn(sv["act"], dhb, tm=f // 2, name=f"g_w_down_{l}")
        dhh, g_conv_w, g_conv_b = _convgate_bwd(sv["hh"], dact, fw["conv_w"], sv["cb"], name=f"d_convgate_{l}")
        g_w_up = _mm_up_tn(sv["xn3"], dhh, name=f"g_w_up_{l}")
        g_conv_w_cols = g_conv_w.transpose(1, 0, 2).reshape(3, 2 * f)
        parts = [_cols_to_blocks(g_w_up, name=f"g_w_up_blocks_{l}"),
                 _cols_to_blocks(g_conv_w_cols, name=f"g_conv_w_blocks_{l}"), g_w_down.reshape(N_DEV, -1, d)]
        parts = scatter(l, "ffn", ("w_up", "conv_w", "w_down"), parts)
        dxn3 = _mm_up_nt(_pin(dhh, *parts), fw["w_up"], name=f"d_xn_ffn_{l}")
        dh2, dh2b, g_norm_ffn = _rms_bwd(sv["h2"], dxn3, norm_ffn_g[l], dh, name=f"d_norm_ffn_{l}")

        do = _mm_nt(dh2b, fw["wo"], out_dtype=BF16, name=f"d_o_{l}")
        g_wo = _mm_tn(sv["o"], dh2b, name=f"g_wo_{l}")
        dq, dk, dv = _attn_bwd(sv["q"], sv["k"], sv["v"], do, name=f"d_attn_{l}")
        dkb, dvb = dk.astype(BF16), dv.astype(BF16)
        g_wq = _mm_tn(sv["xn2"], dq, name=f"g_wq_{l}")
        g_wk = _mm_tn(sv["memn"], dkb, name=f"g_wk_{l}")
        g_wv = _mm_tn(sv["memn"], dvb, name=f"g_wv_{l}")
        parts = [g.reshape(N_DEV, -1, d) for g in (g_wq, g_wk, g_wv, g_wo)]
        parts = scatter(l, "attn", ("wq", "wk", "wv", "wo"), parts)
        dq = _pin(dq, *parts)
        dmemn = _mm_nt(dkb, fw["wk"], out_dtype=F32, name=f"d_memn_k_{l}")
        dmemn = _mm_nt(dvb, fw["wv"], out_dtype=F32, res=dmemn, name=f"d_memn_v_{l}")
        _, _, g_mem_norm = _rms_bwd(memx, dmemn, mem_norm_g[l], None, name=f"d_norm_mem_{l}")
        dh1, dh1b, g_norm_xattn = _mm_nt(dq, fw["wq"], name=f"d_norm_xattn_{l}",
                                         norm_bwd=(sv["h1"], norm_xattn_g[l], dh2))

        dcat = _mm_nt(dh1b, fw["w_out"], out_dtype=F32, name=f"d_cat_{l}")
        g_w_out = _mm_tn(sv["cat"], dh1b, name=f"g_w_out_{l}")
        dproj, g_pool_w, g_pool_scale, g_sgu_g, g_sgu_w, g_sgu_b = _mixer_bwd(
            sv["proj"], dcat, pool_w[l], pool_scale[l].reshape(1, -1), sgu_g[l].reshape(1, -1), sgu_w[l],
            sv["sgu_bias"], name=f"d_mixer_{l}")
        g_w_in = _mm_tn(sv["xn1"], dproj, name=f"g_w_in_{l}")
        parts = [_cols_to_blocks(g_w_in, name=f"g_w_in_blocks_{l}"), g_w_out.reshape(N_DEV, -1, d)]
        parts = scatter(l, "mix", ("w_in", "w_out"), parts)
        dh, dhb, g_norm_mix = _mm_nt(_pin(dproj, *parts), fw["w_in"], name=f"d_norm_mix_{l}",
                                     norm_bwd=(sv["h0"], norm_mix_g[l], dh1))

        small[l] = dict(norm_mix_g=g_norm_mix.reshape(-1), pool_w=g_pool_w, pool_scale=g_pool_scale.reshape(-1),
                        sgu_g=g_sgu_g.reshape(-1), sgu_w=g_sgu_w, sgu_b=g_sgu_b, norm_xattn_g=g_norm_xattn.reshape(-1),
                        mem_norm_g=g_mem_norm.reshape(-1), norm_ffn_g=g_norm_ffn.reshape(-1),
                        conv_b=g_conv_b.reshape(-1))
    grad_x = dh.reshape(x.shape)

    out = {}
    for nme in _SHARDED:
        w3 = W[nme].reshape(DEPTH, -1, W[nme].shape[-1])
        res = _adamw([sl.reshape((N_DEV,) + w3.shape[1:]) for sl in slots[nme]], w3, M[nme].reshape(w3.shape),
                     V[nme].reshape(w3.shape), name=f"adamw_{nme}")
        out[nme] = [r.reshape(W[nme].shape) for r in res]

    small_names = [n for n in _SMALL]
    contrib = []
    for nme in small_names:
        if nme == "final_norm_g":
            contrib.append(dg_final.reshape(-1))
        else:
            contrib.append(jnp.stack([small[l][nme] for l in range(DEPTH)]))
    packed_g = _pack(contrib)
    (all_g,) = _all_gather([packed_g], name="gather_small_grads")
    rows = packed_g.shape[0]
    res = _adamw([all_g], _pack([W[n] for n in small_names]).reshape(1, rows, -1),
                 _pack([M[n] for n in small_names]).reshape(1, rows, -1),
                 _pack([V[n] for n in small_names]).reshape(1, rows, -1), name="adamw_small", tr=rows // 3)
    unpacked = [_unpack(r, [W[n] for n in small_names]) for r in res]
    for i, nme in enumerate(small_names):
        out[nme] = [unpacked[j][i] for j in range(4)]

    grads = [out[n][0] for n in _WEIGHTS]
    deltas = [out[n][1] for n in _WEIGHTS]
    new_m = [out[n][2] for n in _WEIGHTS]
    new_v = [out[n][3] for n in _WEIGHTS]
    return (loss, grad_x, *grads, *deltas, *new_m, *new_v)
```

```python
import jax
import jax.numpy as jnp
from jax import lax
from jax.experimental import pallas as pl
from jax.experimental.pallas import tpu as pltpu
from jax.experimental.pallas import tpu_sc as plsc

F32 = jnp.float32
BF16 = jnp.bfloat16
MESH = pl.DeviceIdType.MESH

EPS = 1e-6
N_DEV = 8
DEPTH = 2
POOL_WINDOWS = (2, 4, 8, 16)
GROUP = 128
POOL_WIDTH = 512
SGU_WIDTH = 512
HEADS = 4
HEAD_DIM = 256
POOL_HALO = 16
CONV_HALO = 8

ADAM_LR = 0.001
ADAM_B1 = 0.9
ADAM_B2 = 0.999
ADAM_EPS = 1e-08
ADAM_WD = 0.01
ADAM_STEP = 10

VMEM_LIMIT_BYTES = 52 * 1024 * 1024


def _params(*semantics):
    return pltpu.CompilerParams(dimension_semantics=semantics, vmem_limit_bytes=VMEM_LIMIT_BYTES)


def _tile(n, want):
    t = min(n, want)
    assert n % t == 0, (n, want)
    return t


_DOT_DIMS = {
    "nn": (((1,), (0,)), ((), ())),
    "nt": (((1,), (1,)), ((), ())),
    "tn": (((0,), (0,)), ((), ())),
}


def _mm(a, b, *, dims, grid, a_spec, b_spec, o_spec, out_shape, out_dtype, acc_shape, name, res=None, res_spec=None,
        norm_bwd=None):
    nk = grid[2]
    dn = _DOT_DIMS[dims]
    extras, extra_specs = [], []
    if res is not None:
        extras, extra_specs = [res], [res_spec]
    if norm_bwd is not None:
        h, gain, dres, row_spec, gain_spec = norm_bwd
        extras = [h, gain] + ([dres] if dres is not None else [])
        extra_specs = [row_spec, gain_spec] + ([row_spec] if dres is not None else [])
        out_specs = [row_spec, row_spec, gain_spec]
        out_shapes = [jax.ShapeDtypeStruct(h.shape, F32), jax.ShapeDtypeStruct(h.shape, BF16),
                      jax.ShapeDtypeStruct(gain.shape, F32)]
    else:
        out_specs, out_shapes = o_spec, jax.ShapeDtypeStruct(out_shape, out_dtype)
    n_extra = len(extras)

    def body(*refs):
        a_ref, b_ref = refs[:2]
        extra_refs = refs[2:2 + n_extra]
        out_refs = refs[2 + n_extra:len(refs) - (1 if nk > 1 else 0)]
        p = lax.dot_general(a_ref[...], b_ref[...], dn, preferred_element_type=F32)

        def finish(r):
            if norm_bwd is not None:
                _rms_bwd_math(r, extra_refs[0], extra_refs[1], extra_refs[2] if n_extra == 3 else None,
                              *out_refs, first=pl.program_id(0) == 0)
                return
            if res is not None:
                r = r + extra_refs[0][...]
            out_refs[0][...] = r.astype(out_refs[0].dtype)

        if nk == 1:
            finish(p)
        else:
            acc_ref = refs[-1]
            k = pl.program_id(2)

            @pl.when(k == 0)
            def _():
                acc_ref[...] = p

            @pl.when(k > 0)
            def _():
                acc_ref[...] += p

            @pl.when(k == nk - 1)
            def _():
                finish(acc_ref[...])

    scratch = [pltpu.VMEM(acc_shape, F32)] if nk > 1 else []
    return pl.pallas_call(
        body, grid=grid, in_specs=[a_spec, b_spec] + extra_specs, out_specs=out_specs,
        out_shape=out_shapes, scratch_shapes=scratch, name=name,
        compiler_params=_params("arbitrary" if norm_bwd is not None else "parallel", "parallel", "arbitrary"),
    )(a, b, *extras)


def _rms_bwd_math(dy, h_ref, g_ref, dres_ref, dh_ref, dhb_ref, dg_ref, *, first):
    x = h_ref[...]
    r = lax.rsqrt(jnp.mean(x * x, axis=-1, keepdims=True) + EPS)
    a = dy * g_ref[...]
    m = jnp.mean(a * x, axis=-1, keepdims=True)
    dh = r * a - x * (r * r * r * m)
    if dres_ref is not None:
        dh = dh + dres_ref[...]
    dh_ref[...] = dh
    dhb_ref[...] = dh.astype(BF16)
    part = jnp.sum(dy * (x * r), axis=0, keepdims=True)

    @pl.when(first)
    def _():
        dg_ref[...] = part

    @pl.when(jnp.logical_not(first))
    def _():
        dg_ref[...] += part


def _mm_nn(a, b, *, out_dtype, name, res=None, tm=1024):
    m, k = a.shape
    n = b.shape[1]
    tm = _tile(m, tm)
    spec_o = pl.BlockSpec((tm, n), lambda i, j, kk: (i, 0))
    return _mm(a, b, dims="nn", grid=(m // tm, 1, 1),
               a_spec=pl.BlockSpec((tm, k), lambda i, j, kk: (i, 0)),
               b_spec=pl.BlockSpec((k, n), lambda i, j, kk: (0, 0)),
               o_spec=spec_o, out_shape=(m, n), out_dtype=out_dtype, acc_shape=None, name=name,
               res=res, res_spec=spec_o if res is not None else None)


def _norm_bwd_arg(h, gain, dres, tm):
    d = h.shape[1]
    return (h, gain.reshape(1, d), dres, pl.BlockSpec((tm, d), lambda i, j, kk: (i, 0)),
            pl.BlockSpec((1, d), lambda i, j, kk: (0, 0)))


def _mm_nt(a, b, *, out_dtype=F32, name, res=None, tm=1024, norm_bwd=None):
    m, k = a.shape
    n = b.shape[0]
    tm = _tile(m, tm)
    spec_o = pl.BlockSpec((tm, n), lambda i, j, kk: (i, 0))
    return _mm(a, b, dims="nt", grid=(m // tm, 1, 1),
               a_spec=pl.BlockSpec((tm, k), lambda i, j, kk: (i, 0)),
               b_spec=pl.BlockSpec((n, k), lambda i, j, kk: (0, 0)),
               o_spec=spec_o, out_shape=(m, n), out_dtype=out_dtype, acc_shape=None, name=name,
               res=res, res_spec=spec_o if res is not None else None,
               norm_bwd=None if norm_bwd is None else _norm_bwd_arg(*norm_bwd, tm))


_TN_ROWS = 2048


def _mm_tn(a, b, *, name, tm=None, tn=None, ts=_TN_ROWS, out_dtype=BF16):
    s, m = a.shape
    n = b.shape[1]
    tm = m if tm is None else tm
    tn = n if tn is None else tn
    ts = _tile(s, ts)
    return _mm(a, b, dims="tn", grid=(m // tm, n // tn, s // ts),
               a_spec=pl.BlockSpec((ts, tm), lambda i, j, kk: (kk, i)),
               b_spec=pl.BlockSpec((ts, tn), lambda i, j, kk: (kk, j)),
               o_spec=pl.BlockSpec((tm, tn), lambda i, j, kk: (i, j)),
               out_shape=(m, n), out_dtype=out_dtype, acc_shape=(tm, tn), name=name)


def _mm_up(xn, w_up, *, name, tm=512):
    s, d = xn.shape
    f = w_up.shape[1] // 2
    tm = _tile(s, tm)
    return _mm(xn, w_up, dims="nn", grid=(2, s // tm, 1),
               a_spec=pl.BlockSpec((tm, d), lambda j, i, kk: (i, 0)),
               b_spec=pl.BlockSpec((d, f), lambda j, i, kk: (0, j)),
               o_spec=pl.BlockSpec((None, tm, f), lambda j, i, kk: (j, i, 0)),
               out_shape=(2, s, f), out_dtype=F32, acc_shape=None, name=name)


def _mm_up_nt(dhh, w_up, *, name, tm=1024):
    _, s, f = dhh.shape
    d = w_up.shape[0]
    tm = _tile(s, tm)
    return _mm(dhh, w_up, dims="nt", grid=(s // tm, 1, 2),
               a_spec=pl.BlockSpec((None, tm, f), lambda i, j, kk: (kk, i, 0)),
               b_spec=pl.BlockSpec((d, f), lambda i, j, kk: (0, kk)),
               o_spec=pl.BlockSpec((tm, d), lambda i, j, kk: (i, 0)),
               out_shape=(s, d), out_dtype=F32, acc_shape=(tm, d), name=name)


def _mm_up_tn(xn, dhh, *, name, ts=_TN_ROWS):
    s, d = xn.shape
    f = dhh.shape[2]
    tn = f // 2
    ts = _tile(s, ts)
    return _mm(xn, dhh, dims="tn", grid=(1, 4, s // ts),
               a_spec=pl.BlockSpec((ts, d), lambda i, j, kk: (kk, 0)),
               b_spec=pl.BlockSpec((None, ts, tn), lambda i, j, kk: (j // 2, kk, j % 2)),
               o_spec=pl.BlockSpec((d, tn), lambda i, j, kk: (0, j)),
               out_shape=(d, 2 * f), out_dtype=BF16, acc_shape=(d, tn), name=name)


def _rms_fwd(h, g, *, name, tr=512):
    s, d = h.shape
    tr = _tile(s, tr)

    def body(h_ref, g_ref, o_ref):
        x = h_ref[...]
        r = lax.rsqrt(jnp.mean(x * x, axis=-1, keepdims=True) + EPS)
        o_ref[...] = ((x * r) * g_ref[...]).astype(o_ref.dtype)

    row = pl.BlockSpec((tr, d), lambda i: (i, 0))
    return pl.pallas_call(
        body, grid=(s // tr,), in_specs=[row, pl.BlockSpec((1, d), lambda i: (0, 0))], out_specs=row,
        out_shape=jax.ShapeDtypeStruct((s, d), BF16), name=name, compiler_params=_params("parallel"),
    )(h, g.reshape(1, d))


def _rms_bwd(h, dxn, g, dres, *, name, tr=512):
    s, d = h.shape
    tr = _tile(s, tr)
    has_res = dres is not None

    def body(*refs):
        if has_res:
            h_ref, dxn_ref, g_ref, dres_ref, dh_ref, dhb_ref, dg_ref = refs
        else:
            h_ref, dxn_ref, g_ref, dh_ref, dhb_ref, dg_ref = refs
            dres_ref = None
        _rms_bwd_math(dxn_ref[...].astype(F32), h_ref, g_ref, dres_ref, dh_ref, dhb_ref, dg_ref,
                      first=pl.program_id(0) == 0)

    row = pl.BlockSpec((tr, d), lambda i: (i, 0))
    vec = pl.BlockSpec((1, d), lambda i: (0, 0))
    in_specs = [row, row, vec] + ([row] if has_res else [])
    args = (h, dxn, g.reshape(1, d)) + ((dres,) if has_res else ())
    return pl.pallas_call(
        body, grid=(s // tr,), in_specs=in_specs, out_specs=[row, row, vec],
        out_shape=[jax.ShapeDtypeStruct((s, d), F32), jax.ShapeDtypeStruct((s, d), BF16),
                   jax.ShapeDtypeStruct((1, d), F32)],
        name=name, compiler_params=_params("arbitrary"),
    )(*args)


def _loss_head(h, g, target, *, name, tr=512):
    s, d = h.shape
    tr = _tile(s, tr)
    nt = s // tr

    def body(h_ref, g_ref, t_ref, dh_ref, dhb_ref, dg_ref, loss_ref, sq_ref):
        i = pl.program_id(0)
        x = h_ref[...]
        gain = g_ref[...]
        r = lax.rsqrt(jnp.mean(x * x, axis=-1, keepdims=True) + EPS)
        xh = x * r
        err = xh * gain - t_ref[...]
        dy = err * (1.0 / d)
        a = dy * gain
        m = jnp.mean(a * x, axis=-1, keepdims=True)
        dh = r * a - x * (r * r * r * m)
        dh_ref[...] = dh
        dhb_ref[...] = dh.astype(BF16)
        dg_part = jnp.sum(dy * xh, axis=0, keepdims=True)
        sq_part = jnp.sum(err * err, axis=0, keepdims=True)

        @pl.when(i == 0)
        def _():
            dg_ref[...] = dg_part
            sq_ref[...] = sq_part

        @pl.when(i > 0)
        def _():
            dg_ref[...] += dg_part
            sq_ref[...] += sq_part

        @pl.when(i == nt - 1)
        def _():
            total = jnp.sum(sq_ref[...], axis=1, keepdims=True) * (0.5 / d)
            loss_ref[...] = jnp.broadcast_to(total, loss_ref.shape)

    row = pl.BlockSpec((tr, d), lambda i: (i, 0))
    vec = pl.BlockSpec((1, d), lambda i: (0, 0))
    return pl.pallas_call(
        body, grid=(nt,), in_specs=[row, vec, row],
        out_specs=[row, row, vec, pl.BlockSpec((1, 128), lambda i: (0, 0))],
        out_shape=[jax.ShapeDtypeStruct((s, d), F32), jax.ShapeDtypeStruct((s, d), BF16),
                   jax.ShapeDtypeStruct((1, d), F32), jax.ShapeDtypeStruct((1, 128), F32)],
        scratch_shapes=[pltpu.VMEM((1, d), F32)], name=name, compiler_params=_params("arbitrary"),
    )(h, g.reshape(1, d), target)


_SQRT_HALF = 0.7071067811865476
_INV_SQRT_2PI = 0.3989422804014327


def _gelu(x):
    return 0.5 * x * (1.0 + lax.erf(x * _SQRT_HALF))


def _gelu_grad(x):
    return 0.5 * (1.0 + lax.erf(x * _SQRT_HALF)) + x * (jnp.exp(-0.5 * x * x) * _INV_SQRT_2PI)


def _trailing_sums(xe, win):
    s = xe
    sh = 1
    while sh < win:
        s = s + pltpu.roll(s, sh, 0)
        sh *= 2
    return s


def _leading_sums(xe, win):
    n = xe.shape[0]
    s = xe
    sh = 1
    while sh < win:
        s = s + pltpu.roll(s, n - sh, 0)
        sh *= 2
    return s


def _tril_mask():
    return lax.broadcasted_iota(jnp.int32, (GROUP, GROUP), 0) >= lax.broadcasted_iota(jnp.int32, (GROUP, GROUP), 1)


def _layernorm_stats(v):
    mu = jnp.mean(v, axis=-1, keepdims=True)
    xc = v - mu
    rstd = lax.rsqrt(jnp.mean(xc * xc, axis=-1, keepdims=True) + EPS)
    return xc * rstd, rstd


def _mixer_specs(s, t):
    halo_blocks = t // POOL_HALO
    tile = lambda w: pl.BlockSpec((t, w), lambda i: (i, 0))
    prev = pl.BlockSpec((POOL_HALO, POOL_WIDTH), lambda i: (jnp.maximum(i * halo_blocks - 1, 0), 0))
    nxt = pl.BlockSpec((POOL_HALO, POOL_WIDTH),
                       lambda i: (jnp.minimum((i + 1) * halo_blocks, s // POOL_HALO - 1), 0))
    const3 = pl.BlockSpec((HEADS, GROUP, GROUP), lambda i: (0, 0, 0))
    vec = pl.BlockSpec((1, POOL_WIDTH), lambda i: (0, 0))
    bias = pl.BlockSpec((GROUP, SGU_WIDTH), lambda i: (0, 0))
    return tile, prev, nxt, const3, vec, bias


def _mixer_fwd(proj, pool_w, pool_scale, sgu_g, sgu_w, sgu_bias, *, name, t=256):
    s = proj.shape[0]
    t = _tile(s, t)
    tile, prev, _, const3, vec, bias = _mixer_specs(s, t)

    def body(proj_ref, halo_ref, pw_ref, ps_ref, sg_ref, sw_ref, sb_ref, cat_ref):
        i = pl.program_id(0)
        row = i * t + lax.broadcasted_iota(jnp.int32, (t, 1), 0)
        p = proj_ref[:, 0:POOL_WIDTH]
        pe = jnp.concatenate([jnp.where(i > 0, halo_ref[...], 0.0), p], axis=0)
        for gi, win in enumerate(POOL_WINDOWS):
            cols = slice(gi * GROUP, (gi + 1) * GROUP)
            count = jnp.minimum(row + 1, win).astype(F32)
            d = _trailing_sums(pe[:, cols], win)[POOL_HALO:] / count - p[:, cols]
            y = jnp.dot(d.astype(BF16), pw_ref[gi].astype(BF16), preferred_element_type=F32) * ps_ref[:, cols]
            cat_ref[:, cols] = y.astype(BF16)

        u = _gelu(proj_ref[:, POOL_WIDTH:POOL_WIDTH + SGU_WIDTH])
        xhat, _ = _layernorm_stats(_gelu(proj_ref[:, POOL_WIDTH + SGU_WIDTH:]))
        vn = (xhat * sg_ref[...]).astype(BF16)
        tri = _tril_mask()
        for h in range(HEADS):
            cols = slice(h * GROUP, (h + 1) * GROUP)
            w = jnp.where(tri, sw_ref[h], 0.0).astype(BF16)
            for c in range(t // GROUP):
                rows = slice(c * GROUP, (c + 1) * GROUP)
                z = jnp.dot(w, vn[rows, cols], preferred_element_type=F32) + sb_ref[:, cols]
                cat_ref[rows, POOL_WIDTH + h * GROUP:POOL_WIDTH + (h + 1) * GROUP] = (u[rows, cols] * z).astype(BF16)

    return pl.pallas_call(
        body, grid=(s // t,),
        in_specs=[tile(POOL_WIDTH + 2 * SGU_WIDTH), prev, const3, vec, vec, const3, bias],
        out_specs=tile(POOL_WIDTH + SGU_WIDTH),
        out_shape=jax.ShapeDtypeStruct((s, POOL_WIDTH + SGU_WIDTH), BF16), name=name,
        compiler_params=_params("parallel"),
    )(proj, proj, pool_w, pool_scale, sgu_g, sgu_w, sgu_bias)


def _mixer_bwd(proj, dcat, pool_w, pool_scale, sgu_g, sgu_w, sgu_bias, *, name, t=256):
    s = proj.shape[0]
    t = _tile(s, t)
    nt = s // t
    tile, prev, nxt, const3, vec, bias = _mixer_specs(s, t)

    def body(proj_ref, halo_ref, dcat_ref, dnext_ref, pw_ref, ps_ref, sg_ref, sw_ref, sb_ref,
             dproj_ref, dpw_ref, dps_ref, dsg_ref, dsw_ref, dsb_ref, du_ref, dvn_ref, dz_ref):
        i = pl.program_id(0)

        @pl.when(i == 0)
        def _():
            dpw_ref[...] = jnp.zeros_like(dpw_ref)
            dps_ref[...] = jnp.zeros_like(dps_ref)
            dsg_ref[...] = jnp.zeros_like(dsg_ref)
            dsw_ref[...] = jnp.zeros_like(dsw_ref)
            dz_ref[...] = jnp.zeros_like(dz_ref)

        row = i * t + lax.broadcasted_iota(jnp.int32, (t, 1), 0)
        row_e = i * t + lax.broadcasted_iota(jnp.int32, (t + POOL_HALO, 1), 0)
        p = proj_ref[:, 0:POOL_WIDTH]
        pe = jnp.concatenate([jnp.where(i > 0, halo_ref[...], 0.0), p], axis=0)
        dyp = dcat_ref[:, 0:POOL_WIDTH]
        dye = jnp.concatenate([dyp, jnp.where(i < nt - 1, dnext_ref[...], 0.0)], axis=0)
        for gi, win in enumerate(POOL_WINDOWS):
            cols = slice(gi * GROUP, (gi + 1) * GROUP)
            count = jnp.minimum(row + 1, win).astype(F32)
            d = (_trailing_sums(pe[:, cols], win)[POOL_HALO:] / count - p[:, cols]).astype(BF16)
            pw = pw_ref[gi].astype(BF16)
            pre = jnp.dot(d, pw, preferred_element_type=F32)
            dps_ref[:, cols] += jnp.sum(dyp[:, cols] * pre, axis=0, keepdims=True)
            ys = (dye[:, cols] * ps_ref[:, cols]).astype(BF16)
            dpw_ref[gi] += lax.dot_general(d, ys[:t], _DOT_DIMS["tn"], preferred_element_type=F32)
            dd = lax.dot_general(ys, pw, _DOT_DIMS["nt"], preferred_element_type=F32)
            count_e = jnp.minimum(row_e + 1, win).astype(F32)
            dp = _leading_sums(dd / count_e, win)[:t] - dd[:t]
            dproj_ref[:, cols] = dp.astype(BF16)

        xu = proj_ref[:, POOL_WIDTH:POOL_WIDTH + SGU_WIDTH]
        xv = proj_ref[:, POOL_WIDTH + SGU_WIDTH:]
        u = _gelu(xu)
        xhat, rstd = _layernorm_stats(_gelu(xv))
        gain = sg_ref[...]
        vn = (xhat * gain).astype(BF16)
        tri = _tril_mask()
        for h in range(HEADS):
            cols = slice(h * GROUP, (h + 1) * GROUP)
            wf = jnp.where(tri, sw_ref[h], 0.0)
            w, wt = wf.astype(BF16), wf.T.astype(BF16)
            for c in range(t // GROUP):
                rows = slice(c * GROUP, (c + 1) * GROUP)
                vch = vn[rows, cols]
                z = jnp.dot(w, vch, preferred_element_type=F32) + sb_ref[:, cols]
                dy = dcat_ref[rows, POOL_WIDTH + h * GROUP:POOL_WIDTH + (h + 1) * GROUP]
                du_ref[rows, cols] = dy * z
                dz = dy * u[rows, cols]
                dz_ref[:, cols] += dz
                dzb = dz.astype(BF16)
                dsw_ref[h] += lax.dot_general(dzb, vch, _DOT_DIMS["nt"], preferred_element_type=F32)
                dvn_ref[rows, cols] = jnp.dot(wt, dzb, preferred_element_type=F32)
        dvn = dvn_ref[...]
        dsg_ref[...] += jnp.sum(dvn * xhat, axis=0, keepdims=True)
        dxh = dvn * gain
        dv = rstd * (dxh - jnp.mean(dxh, axis=-1, keepdims=True)
                     - xhat * jnp.mean(dxh * xhat, axis=-1, keepdims=True))
        dproj_ref[:, POOL_WIDTH:POOL_WIDTH + SGU_WIDTH] = (du_ref[...] * _gelu_grad(xu)).astype(BF16)
        dproj_ref[:, POOL_WIDTH + SGU_WIDTH:] = (dv * _gelu_grad(xv)).astype(BF16)

        @pl.when(i == nt - 1)
        def _():
            for h in range(HEADS):
                dsw_ref[h] = jnp.where(tri, dsw_ref[h], 0.0)
            lane = lax.broadcasted_iota(jnp.int32, (GROUP, GROUP), 1)
            out = jnp.zeros((GROUP, GROUP), F32)
            for h in range(HEADS):
                sh = jnp.sum(dz_ref[:, h * GROUP:(h + 1) * GROUP], axis=1, keepdims=True)
                out = jnp.where(lane == h, sh, out)
            dsb_ref[...] = out

    outs = pl.pallas_call(
        body, grid=(nt,),
        in_specs=[tile(POOL_WIDTH + 2 * SGU_WIDTH), prev, tile(POOL_WIDTH + SGU_WIDTH), nxt,
                  const3, vec, vec, const3, bias],
        out_specs=[tile(POOL_WIDTH + 2 * SGU_WIDTH), const3, vec, vec, const3,
                   pl.BlockSpec((GROUP, GROUP), lambda i: (0, 0))],
        out_shape=[jax.ShapeDtypeStruct((s, POOL_WIDTH + 2 * SGU_WIDTH), BF16),
                   jax.ShapeDtypeStruct((HEADS, GROUP, GROUP), F32),
                   jax.ShapeDtypeStruct((1, POOL_WIDTH), F32),
                   jax.ShapeDtypeStruct((1, SGU_WIDTH), F32),
                   jax.ShapeDtypeStruct((HEADS, GROUP, GROUP), F32),
                   jax.ShapeDtypeStruct((GROUP, GROUP), F32)],
        scratch_shapes=[pltpu.VMEM((t, SGU_WIDTH), F32), pltpu.VMEM((t, SGU_WIDTH), F32),
                        pltpu.VMEM((GROUP, SGU_WIDTH), F32)],
        name=name, compiler_params=_params("arbitrary"),
    )(proj, proj, dcat, dcat, pool_w, pool_scale, sgu_g, sgu_w, sgu_bias)
    dproj, dpw, dps, dsg, dsw, dsb = outs
    return dproj, dpw, dps, dsg, dsw, dsb[:, :HEADS].T


def _attn_probs(q, k, scale):
    sc = lax.dot_general(q, k, _DOT_DIMS["nt"], preferred_element_type=F32) * scale
    sc = sc - jnp.max(sc, axis=-1, keepdims=True)
    e = jnp.exp(sc)
    return e / jnp.sum(e, axis=-1, keepdims=True)


def _attn_fwd(q, k, v, *, name, t=512):
    s, d = q.shape
    nm = k.shape[0]
    t = _tile(s, t)
    scale = HEAD_DIM ** -0.5

    def body(q_ref, k_ref, v_ref, o_ref):
        for h in range(HEADS):
            cols = slice(h * HEAD_DIM, (h + 1) * HEAD_DIM)
            pr = _attn_probs(q_ref[:, cols], k_ref[:, cols], scale)
            o_ref[:, cols] = jnp.dot(pr.astype(BF16), v_ref[:, cols], preferred_element_type=F32).astype(BF16)

    row = pl.BlockSpec((t, d), lambda i: (i, 0))
    kv = pl.BlockSpec((nm, d), lambda i: (0, 0))
    return pl.pallas_call(
        body, grid=(s // t,), in_specs=[row, kv, kv], out_specs=row,
        out_shape=jax.ShapeDtypeStruct((s, d), BF16), name=name, compiler_params=_params("parallel"),
    )(q, k, v)


def _attn_bwd(q, k, v, do, *, name, t=512):
    s, d = q.shape
    nm = k.shape[0]
    t = _tile(s, t)
    scale = HEAD_DIM ** -0.5

    def body(q_ref, k_ref, v_ref, do_ref, dq_ref, dk_ref, dv_ref):
        i = pl.program_id(0)

        @pl.when(i == 0)
        def _():
            dk_ref[...] = jnp.zeros_like(dk_ref)
            dv_ref[...] = jnp.zeros_like(dv_ref)

        for h in range(HEADS):
            cols = slice(h * HEAD_DIM, (h + 1) * HEAD_DIM)
            qh, kh, vh, doh = q_ref[:, cols], k_ref[:, cols], v_ref[:, cols], do_ref[:, cols]
            pr = _attn_probs(qh, kh, scale)
            dpr = lax.dot_general(doh, vh, _DOT_DIMS["nt"], preferred_element_type=F32)
            ds = (pr * (dpr - jnp.sum(dpr * pr, axis=-1, keepdims=True)) * scale).astype(BF16)
            dv_ref[:, cols] += lax.dot_general(pr.astype(BF16), doh, _DOT_DIMS["tn"], preferred_element_type=F32)
            dk_ref[:, cols] += lax.dot_general(ds, qh, _DOT_DIMS["tn"], preferred_element_type=F32)
            dq_ref[:, cols] = jnp.dot(ds, kh, preferred_element_type=F32).astype(BF16)

    row = pl.BlockSpec((t, d), lambda i: (i, 0))
    kv = pl.BlockSpec((nm, d), lambda i: (0, 0))
    return pl.pallas_call(
        body, grid=(s // t,), in_specs=[row, kv, kv, row], out_specs=[row, kv, kv],
        out_shape=[jax.ShapeDtypeStruct((s, d), BF16), jax.ShapeDtypeStruct((nm, d), F32),
                   jax.ShapeDtypeStruct((nm, d), F32)],
        name=name, compiler_params=_params("arbitrary"),
    )(q, k, v, do)


def _conv_specs(s, f, t, tc, swap):
    hb = t // CONV_HALO
    order = (lambda fn: (lambda j, i: fn(i, j))) if swap else (lambda fn: fn)
    tile3 = pl.BlockSpec((2, t, tc), order(lambda i, j: (0, i, j)))
    prev3 = pl.BlockSpec((2, CONV_HALO, tc), order(lambda i, j: (0, jnp.maximum(i * hb - 1, 0), j)))
    next3 = pl.BlockSpec((2, CONV_HALO, tc),
                         order(lambda i, j: (0, jnp.minimum((i + 1) * hb, s // CONV_HALO - 1), j)))
    tile2 = pl.BlockSpec((t, tc), order(lambda i, j: (i, j)))
    next2 = pl.BlockSpec((CONV_HALO, tc), order(lambda i, j: (jnp.minimum((i + 1) * hb, s // CONV_HALO - 1), j)))
    wspec = pl.BlockSpec((2, 3, tc), order(lambda i, j: (0, 0, j)))
    bspec = pl.BlockSpec((2, 1, tc), order(lambda i, j: (0, 0, j)))
    return tile3, prev3, next3, tile2, next2, wspec, bspec


def _conv3(w_ref, p, x2, x1, x0, b):
    return (w_ref[p, 0:1, :] * x2 + w_ref[p, 1:2, :] * x1 + w_ref[p, 2:3, :] * x0) + b


def _convgate_fwd(hh, cw, cb, *, name, t=256, tc=1408):
    _, s, f = hh.shape
    t = _tile(s, t)
    tile3, prev3, _, tile2, _, wspec, bspec = _conv_specs(s, f, t, tc, swap=False)

    def body(hh_ref, prev_ref, cw_ref, cb_ref, act_ref):
        i = pl.program_id(0)
        hc = []
        for p in range(2):
            xe = jnp.concatenate([jnp.where(i > 0, prev_ref[p], 0.0), hh_ref[p]], axis=0)
            hc.append(_conv3(cw_ref, p, pltpu.roll(xe, 2, 0), pltpu.roll(xe, 1, 0), xe, cb_ref[p])[CONV_HALO:])
        gate, val = hc
        act_ref[...] = ((gate * jax.nn.sigmoid(gate)) * val).astype(BF16)

    return pl.pallas_call(
        body, grid=(s // t, f // tc), in_specs=[tile3, prev3, wspec, bspec], out_specs=tile2,
        out_shape=jax.ShapeDtypeStruct((s, f), BF16), name=name, compiler_params=_params("parallel", "parallel"),
    )(hh, hh, cw, cb)


def _convgate_bwd(hh, dact, cw, cb, *, name, t=128, tc=1408):
    _, s, f = hh.shape
    t = _tile(s, t)
    nt = s // t
    tile3, prev3, next3, tile2, next2, wspec, bspec = _conv_specs(s, f, t, tc, swap=True)

    def body(hh_ref, prev_ref, next_ref, da_ref, danext_ref, cw_ref, cb_ref, dhh_ref, dcw_ref, dcb_ref):
        i = pl.program_id(1)
        is_last = i == nt - 1

        @pl.when(i == 0)
        def _():
            dcw_ref[...] = jnp.zeros_like(dcw_ref)
            dcb_ref[...] = jnp.zeros_like(dcb_ref)

        taps, hc = [], []
        for p in range(2):
            xe = jnp.concatenate([jnp.where(i > 0, prev_ref[p], 0.0), hh_ref[p],
                                  jnp.where(is_last, 0.0, next_ref[p])], axis=0)
            x2, x1 = pltpu.roll(xe, 2, 0), pltpu.roll(xe, 1, 0)
            hc.append(_conv3(cw_ref, p, x2, x1, xe, cb_ref[p])[CONV_HALO:])
            taps.append((x2[CONV_HALO:CONV_HALO + t], x1[CONV_HALO:CONV_HALO + t], xe[CONV_HALO:CONV_HALO + t]))
        gate, val = hc
        dae = jnp.concatenate([da_ref[...], jnp.where(is_last, 0.0, danext_ref[...])], axis=0)
        sg = jax.nn.sigmoid(gate)
        dval = dae * (gate * sg)
        dgate = dae * val * (sg * (1.0 + gate * (1.0 - sg)))
        m = t + CONV_HALO
        for p, dhc in enumerate((dgate, dval)):
            dh = (cw_ref[p, 2:3, :] * dhc + cw_ref[p, 1:2, :] * pltpu.roll(dhc, m - 1, 0)
                  + cw_ref[p, 0:1, :] * pltpu.roll(dhc, m - 2, 0))
            dhh_ref[p] = dh[:t].astype(BF16)
            d0 = dhc[:t]
            for kk, tap in enumerate(taps[p]):
                dcw_ref[p, kk:kk + 1, :] += jnp.sum(d0 * tap, axis=0, keepdims=True)
            dcb_ref[p] += jnp.sum(d0, axis=0, keepdims=True)

    return pl.pallas_call(
        body, grid=(f // tc, nt), in_specs=[tile3, prev3, next3, tile2, next2, wspec, bspec],
        out_specs=[tile3, wspec, bspec],
        out_shape=[jax.ShapeDtypeStruct((2, s, f), BF16), jax.ShapeDtypeStruct((2, 3, f), F32),
                   jax.ShapeDtypeStruct((2, 1, f), F32)],
        name=name, compiler_params=_params("parallel", "arbitrary"),
    )(hh, hh, hh, dact, dact, cw, cb)


def _position():
    return lax.axis_index("x"), lax.axis_index("y"), lax.axis_index("c")


def _linear(px, py, pc):
    return 4 * px + 2 * py + pc


def _peers_of(x, y, c):
    peers = []
    for mask in range(1, N_DEV):
        peers.append((1 - x if mask & 4 else x, 1 - y if mask & 2 else y, 1 - c if mask & 1 else c))
    return peers


def _exchange_copy(src_ref, land_ref, send_sem, recv_sem, peer, mine, scatter, arriving):
    src = src_ref.at[_linear(*peer)] if scatter else src_ref
    dst = land_ref.at[_linear(*peer) if arriving else mine]
    return pltpu.make_async_remote_copy(src_ref=src, dst_ref=dst, send_sem=send_sem, recv_sem=recv_sem,
                                        device_id=peer, device_id_type=MESH)


_EXCHANGE_COLLECTIVE_ID = 7


def _sequencer_exchange(srcs, *, scatter, name):
    n = len(srcs)
    src_refs = [jax.new_ref(a, memory_space=pltpu.MemorySpace.HBM) for a in srcs]
    land_refs = [jax.empty_ref(jax.ShapeDtypeStruct(a.shape if scatter else (N_DEV,) + a.shape, a.dtype),
                               memory_space=pltpu.MemorySpace.HBM) for a in srcs]

    @pl.kernel(mesh=plsc.ScalarSubcoreMesh(axis_name="sequencer", num_cores=1), name=name,
               scratch_types=(pltpu.SemaphoreType.DMA((7 * n,)), pltpu.SemaphoreType.DMA((7 * n,)),
                              pltpu.SemaphoreType.DMA((n,))),
               compiler_params=pltpu.CompilerParams(collective_id=_EXCHANGE_COLLECTIVE_ID))
    def launch(send_sems, recv_sems, local_sems):
        x, y, c = _position()
        mine = _linear(x, y, c)
        peers = _peers_of(x, y, c)
        barrier = pltpu.get_barrier_semaphore()
        for peer in peers:
            pl.semaphore_signal(barrier, inc=1, device_id=peer, device_id_type=MESH)
        pl.semaphore_wait(barrier, N_DEV - 1)
        local = [pltpu.make_async_copy(src_refs[t].at[mine] if scatter else src_refs[t], land_refs[t].at[mine],
                                       local_sems.at[t]) for t in range(n)]
        for cp in local:
            cp.start()
        sends = []
        for t in range(n):
            for k, peer in enumerate(peers):
                cp = _exchange_copy(src_refs[t], land_refs[t], send_sems.at[7 * t + k], recv_sems.at[7 * t + k],
                                    peer, mine, scatter, arriving=False)
                cp.start()
                sends.append(cp)
        for cp in local:
            cp.wait()
        for cp in sends:
            cp.wait_send()
        for t in range(n):
            for k, peer in enumerate(peers):
                _exchange_copy(src_refs[t], land_refs[t], send_sems.at[7 * t + k], recv_sems.at[7 * t + k],
                               peer, mine, scatter, arriving=True).wait_recv()

    launch()
    return [r[...] for r in land_refs]


def _adamw_math(g, w, m, v):
    m2 = ADAM_B1 * m + (1.0 - ADAM_B1) * g
    v2 = ADAM_B2 * v + (1.0 - ADAM_B2) * (g * g)
    m_hat = m2 / (1.0 - ADAM_B1 ** ADAM_STEP)
    v_hat = v2 / (1.0 - ADAM_B2 ** ADAM_STEP)
    delta = -ADAM_LR * (m_hat / (jnp.sqrt(v_hat) + ADAM_EPS) + ADAM_WD * w)
    return delta, m2, v2


def _adamw(slots, w, m, v, *, name, tr=256):
    depth = len(slots)
    _, r, c = slots[0].shape
    tr = next((cand for cand in range(min(r, tr), 15, -1) if r % cand == 0 and cand % 16 == 0), r)

    def body(*refs):
        s_refs = refs[:depth]
        w_ref, m_ref, v_ref, g_ref, d_ref, m2_ref, v2_ref = refs[depth:]
        layer = pl.program_id(0)
        for l in range(depth):
            @pl.when(layer == l)
            def _():
                g = s_refs[l][0].astype(F32)
                for d in range(1, N_DEV):
                    g = g + s_refs[l][d].astype(F32)
                delta, m2, v2 = _adamw_math(g, w_ref[...], m_ref[...], v_ref[...])
                g_ref[...] = g
                d_ref[...] = delta
                m2_ref[...] = m2
                v2_ref[...] = v2

    blk = pl.BlockSpec((None, tr, c), lambda layer, i: (layer, i, 0))
    sblks = [pl.BlockSpec((N_DEV, tr, c), lambda layer, i, l=l: (0, jnp.where(layer == l, i, 0), 0))
             for l in range(depth)]
    shape = jax.ShapeDtypeStruct((depth, r, c), F32)
    return pl.pallas_call(
        body, grid=(depth, r // tr), in_specs=sblks + [blk, blk, blk], out_specs=[blk] * 4,
        out_shape=[shape] * 4, name=name, compiler_params=_params("arbitrary", "arbitrary"),
    )(*slots, w, m, v)


_SHARDED = ("w_in", "w_out", "wq", "wk", "wv", "wo", "w_up", "conv_w", "w_down")
_SMALL = ("norm_mix_g", "pool_w", "pool_scale", "sgu_g", "sgu_w", "sgu_b", "norm_xattn_g", "mem_norm_g",
          "norm_ffn_g", "conv_b", "final_norm_g")
_WEIGHTS = ("norm_mix_g", "w_in", "pool_w", "pool_scale", "sgu_g", "sgu_w", "sgu_b", "w_out", "norm_xattn_g",
            "mem_norm_g", "wq", "wk", "wv", "wo", "norm_ffn_g", "w_up", "conv_w", "conv_b", "w_down",
            "final_norm_g")
_PACK_LANES = 128
_GATHER_GROUPS = (("w_in",), ("w_out",), ("wq", "wk", "wv", "wo"), ("w_up", "conv_w", "w_down"))


def _cols_to_blocks(a, *, name, tr=256):
    r, c8 = a.shape
    c = c8 // N_DEV
    tr = _tile(r, tr)

    def body(a_ref, o_ref):
        for dev in range(N_DEV):
            o_ref[dev] = a_ref[:, dev * c:(dev + 1) * c]

    return pl.pallas_call(
        body, grid=(r // tr,), in_specs=[pl.BlockSpec((tr, c8), lambda i: (i, 0))],
        out_specs=pl.BlockSpec((N_DEV, tr, c), lambda i: (0, i, 0)),
        out_shape=jax.ShapeDtypeStruct((N_DEV, r, c), a.dtype), name=name, compiler_params=_params("parallel"),
    )(a)


def _blocks_to_cols(a, *, name, tr=256):
    n, r, c = a.shape
    tr = _tile(r, tr)

    def body(a_ref, o_ref):
        for dev in range(n):
            o_ref[:, dev * c:(dev + 1) * c] = a_ref[dev]

    return pl.pallas_call(
        body, grid=(r // tr,), in_specs=[pl.BlockSpec((n, tr, c), lambda i: (0, i, 0))],
        out_specs=pl.BlockSpec((tr, n * c), lambda i: (i, 0)),
        out_shape=jax.ShapeDtypeStruct((r, n * c), a.dtype), name=name, compiler_params=_params("parallel"),
    )(a)


def _pin(x, *deps):
    return lax.optimization_barrier((x, *deps))[0]


def _pack(arrays):
    flat = jnp.concatenate([a.reshape(-1) for a in arrays])
    assert flat.shape[0] % (8 * _PACK_LANES) == 0
    return flat.reshape(-1, _PACK_LANES)


def _unpack(packed, like):
    flat = packed.reshape(-1)
    out, off = [], 0
    for a in like:
        out.append(flat[off:off + a.size].reshape(a.shape))
        off += a.size
    return out


def kernel(x, mem, norm_mix_g, w_in, pool_w, pool_scale, sgu_g, sgu_w, sgu_b, w_out, norm_xattn_g, mem_norm_g, wq, wk, wv, wo, norm_ffn_g, w_up, conv_w, conv_b, w_down, final_norm_g, loss_target, m_norm_mix_g, m_w_in, m_pool_w, m_pool_scale, m_sgu_g, m_sgu_w, m_sgu_b, m_w_out, m_norm_xattn_g, m_mem_norm_g, m_wq, m_wk, m_wv, m_wo, m_norm_ffn_g, m_w_up, m_conv_w, m_conv_b, m_w_down, m_final_norm_g, v_norm_mix_g, v_w_in, v_pool_w, v_pool_scale, v_sgu_g, v_sgu_w, v_sgu_b, v_w_out, v_norm_xattn_g, v_mem_norm_g, v_wq, v_wk, v_wv, v_wo, v_norm_ffn_g, v_w_up, v_conv_w, v_conv_b, v_w_down, v_final_norm_g):
    W = dict(norm_mix_g=norm_mix_g, w_in=w_in, pool_w=pool_w, pool_scale=pool_scale, sgu_g=sgu_g, sgu_w=sgu_w,
             sgu_b=sgu_b, w_out=w_out, norm_xattn_g=norm_xattn_g, mem_norm_g=mem_norm_g, wq=wq, wk=wk, wv=wv, wo=wo,
             norm_ffn_g=norm_ffn_g, w_up=w_up, conv_w=conv_w, conv_b=conv_b, w_down=w_down,
             final_norm_g=final_norm_g)
    M = dict(norm_mix_g=m_norm_mix_g, w_in=m_w_in, pool_w=m_pool_w, pool_scale=m_pool_scale, sgu_g=m_sgu_g,
             sgu_w=m_sgu_w, sgu_b=m_sgu_b, w_out=m_w_out, norm_xattn_g=m_norm_xattn_g, mem_norm_g=m_mem_norm_g,
             wq=m_wq, wk=m_wk, wv=m_wv, wo=m_wo, norm_ffn_g=m_norm_ffn_g, w_up=m_w_up, conv_w=m_conv_w,
             conv_b=m_conv_b, w_down=m_w_down, final_norm_g=m_final_norm_g)
    V = dict(norm_mix_g=v_norm_mix_g, w_in=v_w_in, pool_w=v_pool_w, pool_scale=v_pool_scale, sgu_g=v_sgu_g,
             sgu_w=v_sgu_w, sgu_b=v_sgu_b, w_out=v_w_out, norm_xattn_g=v_norm_xattn_g, mem_norm_g=v_mem_norm_g,
             wq=v_wq, wk=v_wk, wv=v_wv, wo=v_wo, norm_ffn_g=v_norm_ffn_g, w_up=v_w_up, conv_w=v_conv_w,
             conv_b=v_conv_b, w_down=v_w_down, final_norm_g=v_final_norm_g)

    s, d = x.shape[1], x.shape[2]
    f = w_down.shape[1] * N_DEV
    h = x.reshape(s, d)
    memx = mem.reshape(mem.shape[1], d)
    target = loss_target.reshape(s, d)

    gathered = {}

    def launch_gather(l, gi, after):
        if l >= DEPTH:
            return
        names = _GATHER_GROUPS[gi]
        shards = [W[nme][l] if nme == "conv_w" else W[nme][l].astype(BF16) for nme in names]
        if after is not None:
            shards[0], _ = lax.optimization_barrier((shards[0], after))
        gathered[l, gi] = dict(zip(names, _sequencer_exchange(shards, scatter=False, name=f"gather_{l}_{gi}")))

    launch_gather(0, 0, None)

    saved, full = [], []
    for l in range(DEPTH):
        sgu_bias = jnp.repeat(sgu_b[l].T, GROUP, axis=1)
        cb = conv_b[l].reshape(2, 1, f)
        xn1 = _rms_fwd(h, norm_mix_g[l], name=f"norm_mix_{l}")
        if l == 0:
            launch_gather(0, 1, xn1)
        w_in_f = _blocks_to_cols(gathered[l, 0]["w_in"], name=f"w_in_cols_{l}")
        proj = _mm_nn(xn1, w_in_f, out_dtype=F32, name=f"proj_in_{l}")
        if l == 0:
            launch_gather(0, 2, proj)
        cat = _mixer_fwd(proj, pool_w[l], pool_scale[l].reshape(1, -1), sgu_g[l].reshape(1, -1), sgu_w[l], sgu_bias,
                         name=f"mixer_{l}")
        if l == 0:
            launch_gather(0, 3, cat)
        w_out_f = gathered[l, 1]["w_out"].reshape(-1, d)
        h1 = _mm_nn(cat, w_out_f, out_dtype=F32, res=h, name=f"proj_out_{l}")
        launch_gather(l + 1, 0, h1)
        xn2 = _rms_fwd(h1, norm_xattn_g[l], name=f"norm_xattn_{l}")
        g = gathered[l, 2]
        wq_f, wk_f, wv_f, wo_f = (g[nme].reshape(-1, d) for nme in ("wq", "wk", "wv", "wo"))
        q = _mm_nn(xn2, wq_f, out_dtype=BF16, name=f"q_{l}")
        launch_gather(l + 1, 1, q)
        memn = _rms_fwd(memx, mem_norm_g[l], name=f"norm_mem_{l}")
        k = _mm_nn(memn, wk_f, out_dtype=BF16, name=f"k_{l}")
        v = _mm_nn(memn, wv_f, out_dtype=BF16, name=f"v_{l}")
        o = _attn_fwd(q, k, v, name=f"attn_{l}")
        h2 = _mm_nn(o, wo_f, out_dtype=F32, res=h1, name=f"attn_out_{l}")
        launch_gather(l + 1, 2, h2)
        xn3 = _rms_fwd(h2, norm_ffn_g[l], name=f"norm_ffn_{l}")
        g = gathered[l, 3]
        w_up_f = _blocks_to_cols(g["w_up"], name=f"w_up_cols_{l}")
        conv_w_f = _blocks_to_cols(g["conv_w"], name=f"conv_w_cols_{l}").reshape(3, 2, f).transpose(1, 0, 2)
        w_down_f = g["w_down"].reshape(-1, d)
        hh = _mm_up(xn3, w_up_f, name=f"ffn_up_{l}")
        launch_gather(l + 1, 3, hh)
        act = _convgate_fwd(hh, conv_w_f, cb, name=f"convgate_{l}")
        h3 = _mm_nn(act, w_down_f, out_dtype=F32, res=h2, tm=512, name=f"ffn_down_{l}")
        full.append(dict(w_in=w_in_f, w_out=w_out_f, wq=wq_f, wk=wk_f, wv=wv_f, wo=wo_f, w_up=w_up_f,
                         conv_w=conv_w_f, w_down=w_down_f))
        saved.append(dict(h0=h, xn1=xn1, proj=proj, cat=cat, h1=h1, xn2=xn2, q=q, memn=memn, k=k, v=v, o=o, h2=h2,
                          xn3=xn3, hh=hh, act=act, sgu_bias=sgu_bias, cb=cb))
        h = h3

    dh, dhb, dg_final, loss_row = _loss_head(h, final_norm_g, target, name="loss_head")

    slots = {nme: [None] * DEPTH for nme in _SHARDED}
    small = [None] * DEPTH

    previous = []

    def scatter(l, tag, names, parts):
        parts = [_pin(parts[0], *previous)] + parts[1:]
        arrived = _sequencer_exchange(parts, scatter=True, name=f"scatter_{tag}_{l}")
        previous[:] = arrived[:1]
        for nme, land in zip(names, arrived):
            slots[nme][l] = land
        return parts

    for l in reversed(range(DEPTH)):
        fw, sv = full[l], saved[l]
        dact = _mm_nt(dhb, fw["w_down"], out_dtype=F32, tm=512, name=f"d_act_{l}")
        g_w_down = _mm_tn(sv["act"], dhb, tm=f // 2, name=f"g_w_down_{l}")
        dhh, g_conv_w, g_conv_b = _convgate_bwd(sv["hh"], dact, fw["conv_w"], sv["cb"], name=f"d_convgate_{l}")
        g_w_up = _mm_up_tn(sv["xn3"], dhh, name=f"g_w_up_{l}")
        g_conv_w_cols = g_conv_w.transpose(1, 0, 2).reshape(3, 2 * f)
        parts = [_cols_to_blocks(g_w_up, name=f"g_w_up_blocks_{l}"),
                 _cols_to_blocks(g_conv_w_cols, name=f"g_conv_w_blocks_{l}"), g_w_down.reshape(N_DEV, -1, d)]
        parts = scatter(l, "ffn", ("w_up", "conv_w", "w_down"), parts)
        dxn3 = _mm_up_nt(_pin(dhh, *parts), fw["w_up"], name=f"d_xn_ffn_{l}")
        dh2, dh2b, g_norm_ffn = _rms_bwd(sv["h2"], dxn3, norm_ffn_g[l], dh, name=f"d_norm_ffn_{l}")

        do = _mm_nt(dh2b, fw["wo"], out_dtype=BF16, name=f"d_o_{l}")
        g_wo = _mm_tn(sv["o"], dh2b, name=f"g_wo_{l}")
        dq, dk, dv = _attn_bwd(sv["q"], sv["k"], sv["v"], do, name=f"d_attn_{l}")
        dkb, dvb = dk.astype(BF16), dv.astype(BF16)
        g_wq = _mm_tn(sv["xn2"], dq, name=f"g_wq_{l}")
        g_wk = _mm_tn(sv["memn"], dkb, name=f"g_wk_{l}")
        g_wv = _mm_tn(sv["memn"], dvb, name=f"g_wv_{l}")
        parts = [g.reshape(N_DEV, -1, d) for g in (g_wq, g_wk, g_wv, g_wo)]
        parts = scatter(l, "attn", ("wq", "wk", "wv", "wo"), parts)
        dq = _pin(dq, *parts)
        dmemn = _mm_nt(dkb, fw["wk"], out_dtype=F32, name=f"d_memn_k_{l}")
        dmemn = _mm_nt(dvb, fw["wv"], out_dtype=F32, res=dmemn, name=f"d_memn_v_{l}")
        _, _, g_mem_norm = _rms_bwd(memx, dmemn, mem_norm_g[l], None, name=f"d_norm_mem_{l}")
        dh1, dh1b, g_norm_xattn = _mm_nt(dq, fw["wq"], name=f"d_norm_xattn_{l}",
                                         norm_bwd=(sv["h1"], norm_xattn_g[l], dh2))

        dcat = _mm_nt(dh1b, fw["w_out"], out_dtype=F32, name=f"d_cat_{l}")
        g_w_out = _mm_tn(sv["cat"], dh1b, name=f"g_w_out_{l}")
        dproj, g_pool_w, g_pool_scale, g_sgu_g, g_sgu_w, g_sgu_b = _mixer_bwd(
            sv["proj"], dcat, pool_w[l], pool_scale[l].reshape(1, -1), sgu_g[l].reshape(1, -1), sgu_w[l],
            sv["sgu_bias"], name=f"d_mixer_{l}")
        g_w_in = _mm_tn(sv["xn1"], dproj, name=f"g_w_in_{l}")
        parts = [_cols_to_blocks(g_w_in, name=f"g_w_in_blocks_{l}"), g_w_out.reshape(N_DEV, -1, d)]
        parts = scatter(l, "mix", ("w_in", "w_out"), parts)
        dh, dhb, g_norm_mix = _mm_nt(_pin(dproj, *parts), fw["w_in"], name=f"d_norm_mix_{l}",
                                     norm_bwd=(sv["h0"], norm_mix_g[l], dh1))

        small[l] = dict(norm_mix_g=g_norm_mix.reshape(-1), pool_w=g_pool_w, pool_scale=g_pool_scale.reshape(-1),
                        sgu_g=g_sgu_g.reshape(-1), sgu_w=g_sgu_w, sgu_b=g_sgu_b, norm_xattn_g=g_norm_xattn.reshape(-1),
                        mem_norm_g=g_mem_norm.reshape(-1), norm_ffn_g=g_norm_ffn.reshape(-1),
                        conv_b=g_conv_b.reshape(-1))
    grad_x = dh.reshape(x.shape)

    out = {}
    for nme in _SHARDED:
        w3 = W[nme].reshape(DEPTH, -1, W[nme].shape[-1])
        res = _adamw([sl.reshape((N_DEV,) + w3.shape[1:]) for sl in slots[nme]], w3, M[nme].reshape(w3.shape),
                     V[nme].reshape(w3.shape), name=f"adamw_{nme}")
        out[nme] = [r.reshape(W[nme].shape) for r in res]

    small_names = [n for n in _SMALL]
    contrib = []
    for nme in small_names:
        if nme == "final_norm_g":
            contrib.append(dg_final.reshape(-1))
        else:
            contrib.append(jnp.stack([small[l][nme] for l in range(DEPTH)]))
    tail = 8 * _PACK_LANES
    packed_g = _pack(contrib + [jnp.pad(loss_row[0, :1], (0, tail - 1))])
    (all_g,) = _sequencer_exchange([_pin(packed_g, *previous)], scatter=False, name="gather_small_grads")
    rows = packed_g.shape[0]
    loss = jnp.sum(all_g[:, rows - 8, 0])
    zeros = jnp.zeros((tail,), F32)
    res = _adamw([all_g], _pack([W[n] for n in small_names] + [zeros]).reshape(1, rows, -1),
                 _pack([M[n] for n in small_names] + [zeros]).reshape(1, rows, -1),
                 _pack([V[n] for n in small_names] + [zeros]).reshape(1, rows, -1), name="adamw_small", tr=rows // 2)
    unpacked = [_unpack(r, [W[n] for n in small_names]) for r in res]
    for i, nme in enumerate(small_names):
        out[nme] = [unpacked[j][i] for j in range(4)]

    grads = [out[n][0] for n in _WEIGHTS]
    deltas = [out[n][1] for n in _WEIGHTS]
    new_m = [out[n][2] for n in _WEIGHTS]
    new_v = [out[n][3] for n in _WEIGHTS]
    return (loss, grad_x, *grads, *deltas, *new_m, *new_v)
```

```python
import jax
import jax.numpy as jnp
from jax import lax
from jax.experimental import pallas as pl
from jax.experimental.pallas import tpu as pltpu
from jax.experimental.pallas import tpu_sc as plsc

F32 = jnp.float32
BF16 = jnp.bfloat16
MESH = pl.DeviceIdType.MESH

EPS = 1e-6
N_DEV = 8
DEPTH = 2
POOL_WINDOWS = (2, 4, 8, 16)
GROUP = 128
POOL_WIDTH = 512
SGU_WIDTH = 512
HEADS = 4
HEAD_DIM = 256
POOL_HALO = 16
CONV_HALO = 8

ADAM_LR = 0.001
ADAM_B1 = 0.9
ADAM_B2 = 0.999
ADAM_EPS = 1e-08
ADAM_WD = 0.01
ADAM_STEP = 10

VMEM_LIMIT_BYTES = 52 * 1024 * 1024


def _params(*semantics):
    return pltpu.CompilerParams(dimension_semantics=semantics, vmem_limit_bytes=VMEM_LIMIT_BYTES)


def _tile(n, want):
    t = min(n, want)
    assert n % t == 0, (n, want)
    return t


_DOT_DIMS = {
    "nn": (((1,), (0,)), ((), ())),
    "nt": (((1,), (1,)), ((), ())),
    "tn": (((0,), (0,)), ((), ())),
}


def _mm(a, b, *, dims, grid, a_spec, b_spec, o_spec, out_shape, out_dtype, acc_shape, name, res=None, res_spec=None,
        norm_bwd=None, norm_out=None):
    nk = grid[2]
    dn = _DOT_DIMS[dims]
    extras, extra_specs = [], []
    if res is not None:
        extras, extra_specs = [res], [res_spec]
    if norm_bwd is not None:
        h, gain, dres, row_spec, gain_spec = norm_bwd
        extras = [h, gain] + ([dres] if dres is not None else [])
        extra_specs = [row_spec, gain_spec] + ([row_spec] if dres is not None else [])
        out_specs = [row_spec, row_spec, gain_spec]
        out_shapes = [jax.ShapeDtypeStruct(h.shape, F32), jax.ShapeDtypeStruct(h.shape, BF16),
                      jax.ShapeDtypeStruct(gain.shape, F32)]
    elif norm_out is not None:
        extras, extra_specs = extras + [norm_out[0]], extra_specs + [norm_out[1]]
        out_specs = [o_spec, o_spec]
        out_shapes = [jax.ShapeDtypeStruct(out_shape, out_dtype), jax.ShapeDtypeStruct(out_shape, BF16)]
    else:
        out_specs, out_shapes = o_spec, jax.ShapeDtypeStruct(out_shape, out_dtype)
    n_extra = len(extras)

    def body(*refs):
        a_ref, b_ref = refs[:2]
        extra_refs = refs[2:2 + n_extra]
        out_refs = refs[2 + n_extra:len(refs) - (1 if nk > 1 else 0)]
        p = lax.dot_general(a_ref[...], b_ref[...], dn, preferred_element_type=F32)

        def finish(r):
            if norm_bwd is not None:
                _rms_bwd_math(r, extra_refs[0], extra_refs[1], extra_refs[2] if n_extra == 3 else None,
                              *out_refs, first=pl.program_id(0) == 0)
                return
            if res is not None:
                r = r + extra_refs[0][...]
            out_refs[0][...] = r.astype(out_refs[0].dtype)
            if norm_out is not None:
                scale = lax.rsqrt(jnp.mean(r * r, axis=-1, keepdims=True) + EPS)
                out_refs[1][...] = ((r * scale) * extra_refs[-1][...]).astype(BF16)

        if nk == 1:
            finish(p)
        else:
            acc_ref = refs[-1]
            k = pl.program_id(2)

            @pl.when(k == 0)
            def _():
                acc_ref[...] = p

            @pl.when(k > 0)
            def _():
                acc_ref[...] += p

            @pl.when(k == nk - 1)
            def _():
                finish(acc_ref[...])

    scratch = [pltpu.VMEM(acc_shape, F32)] if nk > 1 else []
    return pl.pallas_call(
        body, grid=grid, in_specs=[a_spec, b_spec] + extra_specs, out_specs=out_specs,
        out_shape=out_shapes, scratch_shapes=scratch, name=name,
        compiler_params=_params("arbitrary" if norm_bwd is not None else "parallel", "parallel", "arbitrary"),
    )(a, b, *extras)


def _rms_bwd_math(dy, h_ref, g_ref, dres_ref, dh_ref, dhb_ref, dg_ref, *, first):
    x = h_ref[...]
    r = lax.rsqrt(jnp.mean(x * x, axis=-1, keepdims=True) + EPS)
    a = dy * g_ref[...]
    m = jnp.mean(a * x, axis=-1, keepdims=True)
    dh = r * a - x * (r * r * r * m)
    if dres_ref is not None:
        dh = dh + dres_ref[...]
    dh_ref[...] = dh
    dhb_ref[...] = dh.astype(BF16)
    part = jnp.sum(dy * (x * r), axis=0, keepdims=True)

    @pl.when(first)
    def _():
        dg_ref[...] = part

    @pl.when(jnp.logical_not(first))
    def _():
        dg_ref[...] += part


def _mm_nn(a, b, *, out_dtype, name, res=None, tm=1024, norm_gain=None):
    m, k = a.shape
    n = b.shape[1]
    tm = _tile(m, tm)
    spec_o = pl.BlockSpec((tm, n), lambda i, j, kk: (i, 0))
    norm_out = None if norm_gain is None else (norm_gain.reshape(1, n), pl.BlockSpec((1, n), lambda i, j, kk: (0, 0)))
    return _mm(a, b, dims="nn", grid=(m // tm, 1, 1),
               a_spec=pl.BlockSpec((tm, k), lambda i, j, kk: (i, 0)),
               b_spec=pl.BlockSpec((k, n), lambda i, j, kk: (0, 0)),
               o_spec=spec_o, out_shape=(m, n), out_dtype=out_dtype, acc_shape=None, name=name,
               res=res, res_spec=spec_o if res is not None else None, norm_out=norm_out)


def _norm_bwd_arg(h, gain, dres, tm):
    d = h.shape[1]
    return (h, gain.reshape(1, d), dres, pl.BlockSpec((tm, d), lambda i, j, kk: (i, 0)),
            pl.BlockSpec((1, d), lambda i, j, kk: (0, 0)))


def _mm_nt(a, b, *, out_dtype=F32, name, res=None, tm=1024, norm_bwd=None):
    m, k = a.shape
    n = b.shape[0]
    tm = _tile(m, tm)
    spec_o = pl.BlockSpec((tm, n), lambda i, j, kk: (i, 0))
    return _mm(a, b, dims="nt", grid=(m // tm, 1, 1),
               a_spec=pl.BlockSpec((tm, k), lambda i, j, kk: (i, 0)),
               b_spec=pl.BlockSpec((n, k), lambda i, j, kk: (0, 0)),
               o_spec=spec_o, out_shape=(m, n), out_dtype=out_dtype, acc_shape=None, name=name,
               res=res, res_spec=spec_o if res is not None else None,
               norm_bwd=None if norm_bwd is None else _norm_bwd_arg(*norm_bwd, tm))


_TN_ROWS = 2048


def _mm_tn(a, b, *, name, tm=None, tn=None, ts=_TN_ROWS, out_dtype=BF16):
    s, m = a.shape
    n = b.shape[1]
    tm = m if tm is None else tm
    tn = n if tn is None else tn
    ts = _tile(s, ts)
    return _mm(a, b, dims="tn", grid=(m // tm, n // tn, s // ts),
               a_spec=pl.BlockSpec((ts, tm), lambda i, j, kk: (kk, i)),
               b_spec=pl.BlockSpec((ts, tn), lambda i, j, kk: (kk, j)),
               o_spec=pl.BlockSpec((tm, tn), lambda i, j, kk: (i, j)),
               out_shape=(m, n), out_dtype=out_dtype, acc_shape=(tm, tn), name=name)


def _mm_up(xn, w_up, *, name, tm=512):
    s, d = xn.shape
    f = w_up.shape[1] // 2
    tm = _tile(s, tm)
    return _mm(xn, w_up, dims="nn", grid=(2, s // tm, 1),
               a_spec=pl.BlockSpec((tm, d), lambda j, i, kk: (i, 0)),
               b_spec=pl.BlockSpec((d, f), lambda j, i, kk: (0, j)),
               o_spec=pl.BlockSpec((None, tm, f), lambda j, i, kk: (j, i, 0)),
               out_shape=(2, s, f), out_dtype=F32, acc_shape=None, name=name)


def _mm_up_nt(dhh, w_up, *, name, tm=512, norm_bwd=None):
    _, s, f = dhh.shape
    d = w_up.shape[0]
    tm = _tile(s, tm)
    return _mm(dhh, w_up, dims="nt", grid=(s // tm, 1, 2),
               a_spec=pl.BlockSpec((None, tm, f), lambda i, j, kk: (kk, i, 0)),
               b_spec=pl.BlockSpec((d, f), lambda i, j, kk: (0, kk)),
               o_spec=pl.BlockSpec((tm, d), lambda i, j, kk: (i, 0)),
               out_shape=(s, d), out_dtype=F32, acc_shape=(tm, d), name=name,
               norm_bwd=None if norm_bwd is None else _norm_bwd_arg(*norm_bwd, tm))


def _mm_up_tn(xn, dhh, *, name, ts=_TN_ROWS):
    s, d = xn.shape
    f = dhh.shape[2]
    tn = f // 2
    ts = _tile(s, ts)
    return _mm(xn, dhh, dims="tn", grid=(1, 4, s // ts),
               a_spec=pl.BlockSpec((ts, d), lambda i, j, kk: (kk, 0)),
               b_spec=pl.BlockSpec((None, ts, tn), lambda i, j, kk: (j // 2, kk, j % 2)),
               o_spec=pl.BlockSpec((d, tn), lambda i, j, kk: (0, j)),
               out_shape=(d, 2 * f), out_dtype=BF16, acc_shape=(d, tn), name=name)


def _rms_fwd(h, g, *, name, tr=512):
    s, d = h.shape
    tr = _tile(s, tr)

    def body(h_ref, g_ref, o_ref):
        x = h_ref[...]
        r = lax.rsqrt(jnp.mean(x * x, axis=-1, keepdims=True) + EPS)
        o_ref[...] = ((x * r) * g_ref[...]).astype(o_ref.dtype)

    row = pl.BlockSpec((tr, d), lambda i: (i, 0))
    return pl.pallas_call(
        body, grid=(s // tr,), in_specs=[row, pl.BlockSpec((1, d), lambda i: (0, 0))], out_specs=row,
        out_shape=jax.ShapeDtypeStruct((s, d), BF16), name=name, compiler_params=_params("parallel"),
    )(h, g.reshape(1, d))


def _rms_bwd(h, dxn, g, dres, *, name, tr=512):
    s, d = h.shape
    tr = _tile(s, tr)
    has_res = dres is not None

    def body(*refs):
        if has_res:
            h_ref, dxn_ref, g_ref, dres_ref, dh_ref, dhb_ref, dg_ref = refs
        else:
            h_ref, dxn_ref, g_ref, dh_ref, dhb_ref, dg_ref = refs
            dres_ref = None
        _rms_bwd_math(dxn_ref[...].astype(F32), h_ref, g_ref, dres_ref, dh_ref, dhb_ref, dg_ref,
                      first=pl.program_id(0) == 0)

    row = pl.BlockSpec((tr, d), lambda i: (i, 0))
    vec = pl.BlockSpec((1, d), lambda i: (0, 0))
    in_specs = [row, row, vec] + ([row] if has_res else [])
    args = (h, dxn, g.reshape(1, d)) + ((dres,) if has_res else ())
    return pl.pallas_call(
        body, grid=(s // tr,), in_specs=in_specs, out_specs=[row, row, vec],
        out_shape=[jax.ShapeDtypeStruct((s, d), F32), jax.ShapeDtypeStruct((s, d), BF16),
                   jax.ShapeDtypeStruct((1, d), F32)],
        name=name, compiler_params=_params("arbitrary"),
    )(*args)


def _loss_head(h, g, target, *, name, tr=512):
    s, d = h.shape
    tr = _tile(s, tr)
    nt = s // tr

    def body(h_ref, g_ref, t_ref, dh_ref, dhb_ref, dg_ref, loss_ref, sq_ref):
        i = pl.program_id(0)
        x = h_ref[...]
        gain = g_ref[...]
        r = lax.rsqrt(jnp.mean(x * x, axis=-1, keepdims=True) + EPS)
        xh = x * r
        err = xh * gain - t_ref[...]
        dy = err * (1.0 / d)
        a = dy * gain
        m = jnp.mean(a * x, axis=-1, keepdims=True)
        dh = r * a - x * (r * r * r * m)
        dh_ref[...] = dh
        dhb_ref[...] = dh.astype(BF16)
        dg_part = jnp.sum(dy * xh, axis=0, keepdims=True)
        sq_part = jnp.sum(err * err, axis=0, keepdims=True)

        @pl.when(i == 0)
        def _():
            dg_ref[...] = dg_part
            sq_ref[...] = sq_part

        @pl.when(i > 0)
        def _():
            dg_ref[...] += dg_part
            sq_ref[...] += sq_part

        @pl.when(i == nt - 1)
        def _():
            total = jnp.sum(sq_ref[...], axis=1, keepdims=True) * (0.5 / d)
            loss_ref[...] = jnp.broadcast_to(total, loss_ref.shape)

    row = pl.BlockSpec((tr, d), lambda i: (i, 0))
    vec = pl.BlockSpec((1, d), lambda i: (0, 0))
    return pl.pallas_call(
        body, grid=(nt,), in_specs=[row, vec, row],
        out_specs=[row, row, vec, pl.BlockSpec((1, 128), lambda i: (0, 0))],
        out_shape=[jax.ShapeDtypeStruct((s, d), F32), jax.ShapeDtypeStruct((s, d), BF16),
                   jax.ShapeDtypeStruct((1, d), F32), jax.ShapeDtypeStruct((1, 128), F32)],
        scratch_shapes=[pltpu.VMEM((1, d), F32)], name=name, compiler_params=_params("arbitrary"),
    )(h, g.reshape(1, d), target)


_SQRT_HALF = 0.7071067811865476
_INV_SQRT_2PI = 0.3989422804014327


def _gelu(x):
    return 0.5 * x * (1.0 + lax.erf(x * _SQRT_HALF))


def _gelu_and_grad(x):
    cdf = 0.5 * (1.0 + lax.erf(x * _SQRT_HALF))
    return x * cdf, cdf + x * (jnp.exp(-0.5 * x * x) * _INV_SQRT_2PI)


def _trailing_sums(xe, win):
    s = xe
    sh = 1
    while sh < win:
        s = s + pltpu.roll(s, sh, 0)
        sh *= 2
    return s


def _leading_sums(xe, win):
    n = xe.shape[0]
    s = xe
    sh = 1
    while sh < win:
        s = s + pltpu.roll(s, n - sh, 0)
        sh *= 2
    return s


def _tril_mask():
    return lax.broadcasted_iota(jnp.int32, (GROUP, GROUP), 0) >= lax.broadcasted_iota(jnp.int32, (GROUP, GROUP), 1)


def _layernorm_stats(v):
    mu = jnp.mean(v, axis=-1, keepdims=True)
    xc = v - mu
    rstd = lax.rsqrt(jnp.mean(xc * xc, axis=-1, keepdims=True) + EPS)
    return xc * rstd, rstd


def _mixer_specs(s, t):
    halo_blocks = t // POOL_HALO
    tile = lambda w: pl.BlockSpec((t, w), lambda i: (i, 0))
    prev = pl.BlockSpec((POOL_HALO, POOL_WIDTH), lambda i: (jnp.maximum(i * halo_blocks - 1, 0), 0))
    nxt = pl.BlockSpec((POOL_HALO, POOL_WIDTH),
                       lambda i: (jnp.minimum((i + 1) * halo_blocks, s // POOL_HALO - 1), 0))
    const3 = pl.BlockSpec((HEADS, GROUP, GROUP), lambda i: (0, 0, 0))
    vec = pl.BlockSpec((1, POOL_WIDTH), lambda i: (0, 0))
    bias = pl.BlockSpec((GROUP, SGU_WIDTH), lambda i: (0, 0))
    return tile, prev, nxt, const3, vec, bias


def _mixer_fwd(proj, pool_w, pool_scale, sgu_g, sgu_w, sgu_bias, *, name, t=256):
    s = proj.shape[0]
    t = _tile(s, t)
    tile, prev, _, const3, vec, bias = _mixer_specs(s, t)

    def body(proj_ref, halo_ref, pw_ref, ps_ref, sg_ref, sw_ref, sb_ref, cat_ref):
        i = pl.program_id(0)
        row = i * t + lax.broadcasted_iota(jnp.int32, (t, 1), 0)
        p = proj_ref[:, 0:POOL_WIDTH]
        pe = jnp.concatenate([jnp.where(i > 0, halo_ref[...], 0.0), p], axis=0)
        for gi, win in enumerate(POOL_WINDOWS):
            cols = slice(gi * GROUP, (gi + 1) * GROUP)
            count = jnp.minimum(row + 1, win).astype(F32)
            d = _trailing_sums(pe[:, cols], win)[POOL_HALO:] / count - p[:, cols]
            y = jnp.dot(d.astype(BF16), pw_ref[gi].astype(BF16), preferred_element_type=F32) * ps_ref[:, cols]
            cat_ref[:, cols] = y.astype(BF16)

        u = _gelu(proj_ref[:, POOL_WIDTH:POOL_WIDTH + SGU_WIDTH])
        xhat, _ = _layernorm_stats(_gelu(proj_ref[:, POOL_WIDTH + SGU_WIDTH:]))
        vn = (xhat * sg_ref[...]).astype(BF16)
        tri = _tril_mask()
        for h in range(HEADS):
            cols = slice(h * GROUP, (h + 1) * GROUP)
            w = jnp.where(tri, sw_ref[h], 0.0).astype(BF16)
            for c in range(t // GROUP):
                rows = slice(c * GROUP, (c + 1) * GROUP)
                z = jnp.dot(w, vn[rows, cols], preferred_element_type=F32) + sb_ref[:, cols]
                cat_ref[rows, POOL_WIDTH + h * GROUP:POOL_WIDTH + (h + 1) * GROUP] = (u[rows, cols] * z).astype(BF16)

    return pl.pallas_call(
        body, grid=(s // t,),
        in_specs=[tile(POOL_WIDTH + 2 * SGU_WIDTH), prev, const3, vec, vec, const3, bias],
        out_specs=tile(POOL_WIDTH + SGU_WIDTH),
        out_shape=jax.ShapeDtypeStruct((s, POOL_WIDTH + SGU_WIDTH), BF16), name=name,
        compiler_params=_params("parallel"),
    )(proj, proj, pool_w, pool_scale, sgu_g, sgu_w, sgu_bias)


def _mixer_bwd(proj, dcat, pool_w, pool_scale, sgu_g, sgu_w, sgu_bias, *, name, t=256):
    s = proj.shape[0]
    t = _tile(s, t)
    nt = s // t
    tile, prev, nxt, const3, vec, bias = _mixer_specs(s, t)

    def body(proj_ref, halo_ref, dcat_ref, dnext_ref, pw_ref, ps_ref, sg_ref, sw_ref, sb_ref,
             dproj_ref, dpw_ref, dps_ref, dsg_ref, dsw_ref, dsb_ref, du_ref, dvn_ref, dz_ref):
        i = pl.program_id(0)

        @pl.when(i == 0)
        def _():
            dpw_ref[...] = jnp.zeros_like(dpw_ref)
            dps_ref[...] = jnp.zeros_like(dps_ref)
            dsg_ref[...] = jnp.zeros_like(dsg_ref)
            dsw_ref[...] = jnp.zeros_like(dsw_ref)
            dz_ref[...] = jnp.zeros_like(dz_ref)

        row = i * t + lax.broadcasted_iota(jnp.int32, (t, 1), 0)
        row_e = i * t + lax.broadcasted_iota(jnp.int32, (t + POOL_HALO, 1), 0)
        p = proj_ref[:, 0:POOL_WIDTH]
        pe = jnp.concatenate([jnp.where(i > 0, halo_ref[...], 0.0), p], axis=0)
        dyp = dcat_ref[:, 0:POOL_WIDTH]
        dye = jnp.concatenate([dyp, jnp.where(i < nt - 1, dnext_ref[...], 0.0)], axis=0)
        for gi, win in enumerate(POOL_WINDOWS):
            cols = slice(gi * GROUP, (gi + 1) * GROUP)
            count = jnp.minimum(row + 1, win).astype(F32)
            d = (_trailing_sums(pe[:, cols], win)[POOL_HALO:] / count - p[:, cols]).astype(BF16)
            pw = pw_ref[gi].astype(BF16)
            pre = jnp.dot(d, pw, preferred_element_type=F32)
            dps_ref[:, cols] += jnp.sum(dyp[:, cols] * pre, axis=0, keepdims=True)
            ys = (dye[:, cols] * ps_ref[:, cols]).astype(BF16)
            dpw_ref[gi] += lax.dot_general(d, ys[:t], _DOT_DIMS["tn"], preferred_element_type=F32)
            dd = lax.dot_general(ys, pw, _DOT_DIMS["nt"], preferred_element_type=F32)
            count_e = jnp.minimum(row_e + 1, win).astype(F32)
            dp = _leading_sums(dd / count_e, win)[:t] - dd[:t]
            dproj_ref[:, cols] = dp.astype(BF16)

        xu = proj_ref[:, POOL_WIDTH:POOL_WIDTH + SGU_WIDTH]
        xv = proj_ref[:, POOL_WIDTH + SGU_WIDTH:]
        u, gelu_grad_u = _gelu_and_grad(xu)
        v, gelu_grad_v = _gelu_and_grad(xv)
        xhat, rstd = _layernorm_stats(v)
        gain = sg_ref[...]
        vn = (xhat * gain).astype(BF16)
        tri = _tril_mask()
        for h in range(HEADS):
            cols = slice(h * GROUP, (h + 1) * GROUP)
            wf = jnp.where(tri, sw_ref[h], 0.0)
            w, wt = wf.astype(BF16), wf.T.astype(BF16)
            for c in range(t // GROUP):
                rows = slice(c * GROUP, (c + 1) * GROUP)
                vch = vn[rows, cols]
                z = jnp.dot(w, vch, preferred_element_type=F32) + sb_ref[:, cols]
                dy = dcat_ref[rows, POOL_WIDTH + h * GROUP:POOL_WIDTH + (h + 1) * GROUP]
                du_ref[rows, cols] = dy * z
                dz = dy * u[rows, cols]
                dz_ref[:, cols] += dz
                dzb = dz.astype(BF16)
                dsw_ref[h] += lax.dot_general(dzb, vch, _DOT_DIMS["nt"], preferred_element_type=F32)
                dvn_ref[rows, cols] = jnp.dot(wt, dzb, preferred_element_type=F32)
        dvn = dvn_ref[...]
        dsg_ref[...] += jnp.sum(dvn * xhat, axis=0, keepdims=True)
        dxh = dvn * gain
        dv = rstd * (dxh - jnp.mean(dxh, axis=-1, keepdims=True)
                     - xhat * jnp.mean(dxh * xhat, axis=-1, keepdims=True))
        dproj_ref[:, POOL_WIDTH:POOL_WIDTH + SGU_WIDTH] = (du_ref[...] * gelu_grad_u).astype(BF16)
        dproj_ref[:, POOL_WIDTH + SGU_WIDTH:] = (dv * gelu_grad_v).astype(BF16)

        @pl.when(i == nt - 1)
        def _():
            for h in range(HEADS):
                dsw_ref[h] = jnp.where(tri, dsw_ref[h], 0.0)
            lane = lax.broadcasted_iota(jnp.int32, (GROUP, GROUP), 1)
            out = jnp.zeros((GROUP, GROUP), F32)
            for h in range(HEADS):
                sh = jnp.sum(dz_ref[:, h * GROUP:(h + 1) * GROUP], axis=1, keepdims=True)
                out = jnp.where(lane == h, sh, out)
            dsb_ref[...] = out

    outs = pl.pallas_call(
        body, grid=(nt,),
        in_specs=[tile(POOL_WIDTH + 2 * SGU_WIDTH), prev, tile(POOL_WIDTH + SGU_WIDTH), nxt,
                  const3, vec, vec, const3, bias],
        out_specs=[tile(POOL_WIDTH + 2 * SGU_WIDTH), const3, vec, vec, const3,
                   pl.BlockSpec((GROUP, GROUP), lambda i: (0, 0))],
        out_shape=[jax.ShapeDtypeStruct((s, POOL_WIDTH + 2 * SGU_WIDTH), BF16),
                   jax.ShapeDtypeStruct((HEADS, GROUP, GROUP), F32),
                   jax.ShapeDtypeStruct((1, POOL_WIDTH), F32),
                   jax.ShapeDtypeStruct((1, SGU_WIDTH), F32),
                   jax.ShapeDtypeStruct((HEADS, GROUP, GROUP), F32),
                   jax.ShapeDtypeStruct((GROUP, GROUP), F32)],
        scratch_shapes=[pltpu.VMEM((t, SGU_WIDTH), F32), pltpu.VMEM((t, SGU_WIDTH), F32),
                        pltpu.VMEM((GROUP, SGU_WIDTH), F32)],
        name=name, compiler_params=_params("arbitrary"),
    )(proj, proj, dcat, dcat, pool_w, pool_scale, sgu_g, sgu_w, sgu_bias)
    dproj, dpw, dps, dsg, dsw, dsb = outs
    return dproj, dpw, dps, dsg, dsw, dsb[:, :HEADS].T


def _attn_probs(q, k, scale):
    sc = lax.dot_general(q, k, _DOT_DIMS["nt"], preferred_element_type=F32) * scale
    sc = sc - jnp.max(sc, axis=-1, keepdims=True)
    e = jnp.exp(sc)
    return e / jnp.sum(e, axis=-1, keepdims=True)


def _attn_fwd(q, k, v, *, name, t=512):
    s, d = q.shape
    nm = k.shape[0]
    t = _tile(s, t)
    scale = HEAD_DIM ** -0.5

    def body(q_ref, k_ref, v_ref, o_ref):
        for h in range(HEADS):
            cols = slice(h * HEAD_DIM, (h + 1) * HEAD_DIM)
            pr = _attn_probs(q_ref[:, cols], k_ref[:, cols], scale)
            o_ref[:, cols] = jnp.dot(pr.astype(BF16), v_ref[:, cols], preferred_element_type=F32).astype(BF16)

    row = pl.BlockSpec((t, d), lambda i: (i, 0))
    kv = pl.BlockSpec((nm, d), lambda i: (0, 0))
    return pl.pallas_call(
        body, grid=(s // t,), in_specs=[row, kv, kv], out_specs=row,
        out_shape=jax.ShapeDtypeStruct((s, d), BF16), name=name, compiler_params=_params("parallel"),
    )(q, k, v)


def _attn_bwd(q, k, v, do, *, name, t=512):
    s, d = q.shape
    nm = k.shape[0]
    t = _tile(s, t)
    scale = HEAD_DIM ** -0.5

    def body(q_ref, k_ref, v_ref, do_ref, dq_ref, dk_ref, dv_ref):
        i = pl.program_id(0)

        @pl.when(i == 0)
        def _():
            dk_ref[...] = jnp.zeros_like(dk_ref)
            dv_ref[...] = jnp.zeros_like(dv_ref)

        for h in range(HEADS):
            cols = slice(h * HEAD_DIM, (h + 1) * HEAD_DIM)
            qh, kh, vh, doh = q_ref[:, cols], k_ref[:, cols], v_ref[:, cols], do_ref[:, cols]
            pr = _attn_probs(qh, kh, scale)
            dpr = lax.dot_general(doh, vh, _DOT_DIMS["nt"], preferred_element_type=F32)
            ds = (pr * (dpr - jnp.sum(dpr * pr, axis=-1, keepdims=True)) * scale).astype(BF16)
            dv_ref[:, cols] += lax.dot_general(pr.astype(BF16), doh, _DOT_DIMS["tn"], preferred_element_type=F32)
            dk_ref[:, cols] += lax.dot_general(ds, qh, _DOT_DIMS["tn"], preferred_element_type=F32)
            dq_ref[:, cols] = jnp.dot(ds, kh, preferred_element_type=F32).astype(BF16)

    row = pl.BlockSpec((t, d), lambda i: (i, 0))
    kv = pl.BlockSpec((nm, d), lambda i: (0, 0))
    return pl.pallas_call(
        body, grid=(s // t,), in_specs=[row, kv, kv, row], out_specs=[row, kv, kv],
        out_shape=[jax.ShapeDtypeStruct((s, d), BF16), jax.ShapeDtypeStruct((nm, d), F32),
                   jax.ShapeDtypeStruct((nm, d), F32)],
        name=name, compiler_params=_params("arbitrary"),
    )(q, k, v, do)


def _conv_specs(s, f, t, tc, swap):
    hb = t // CONV_HALO
    order = (lambda fn: (lambda j, i: fn(i, j))) if swap else (lambda fn: fn)
    tile3 = pl.BlockSpec((2, t, tc), order(lambda i, j: (0, i, j)))
    prev3 = pl.BlockSpec((2, CONV_HALO, tc), order(lambda i, j: (0, jnp.maximum(i * hb - 1, 0), j)))
    next3 = pl.BlockSpec((2, CONV_HALO, tc),
                         order(lambda i, j: (0, jnp.minimum((i + 1) * hb, s // CONV_HALO - 1), j)))
    tile2 = pl.BlockSpec((t, tc), order(lambda i, j: (i, j)))
    next2 = pl.BlockSpec((CONV_HALO, tc), order(lambda i, j: (jnp.minimum((i + 1) * hb, s // CONV_HALO - 1), j)))
    wspec = pl.BlockSpec((2, 3, tc), order(lambda i, j: (0, 0, j)))
    bspec = pl.BlockSpec((2, 1, tc), order(lambda i, j: (0, 0, j)))
    return tile3, prev3, next3, tile2, next2, wspec, bspec


def _conv3(w_ref, p, x2, x1, x0, b):
    return (w_ref[p, 0:1, :] * x2 + w_ref[p, 1:2, :] * x1 + w_ref[p, 2:3, :] * x0) + b


def _convgate_fwd(hh, cw, cb, *, name, t=256, tc=1408):
    _, s, f = hh.shape
    t = _tile(s, t)
    tile3, prev3, _, tile2, _, wspec, bspec = _conv_specs(s, f, t, tc, swap=False)

    def body(hh_ref, prev_ref, cw_ref, cb_ref, act_ref):
        i = pl.program_id(0)
        hc = []
        for p in range(2):
            xe = jnp.concatenate([jnp.where(i > 0, prev_ref[p], 0.0), hh_ref[p]], axis=0)
            hc.append(_conv3(cw_ref, p, pltpu.roll(xe, 2, 0), pltpu.roll(xe, 1, 0), xe, cb_ref[p])[CONV_HALO:])
        gate, val = hc
        act_ref[...] = ((gate * jax.nn.sigmoid(gate)) * val).astype(BF16)

    return pl.pallas_call(
        body, grid=(s // t, f // tc), in_specs=[tile3, prev3, wspec, bspec], out_specs=tile2,
        out_shape=jax.ShapeDtypeStruct((s, f), BF16), name=name, compiler_params=_params("parallel", "parallel"),
    )(hh, hh, cw, cb)


def _convgate_bwd(hh, dact, cw, cb, *, name, t=128, tc=1408):
    _, s, f = hh.shape
    t = _tile(s, t)
    nt = s // t
    tile3, prev3, next3, tile2, next2, wspec, bspec = _conv_specs(s, f, t, tc, swap=True)

    def body(hh_ref, prev_ref, next_ref, da_ref, danext_ref, cw_ref, cb_ref, dhh_ref, dcw_ref, dcb_ref):
        i = pl.program_id(1)
        is_last = i == nt - 1

        @pl.when(i == 0)
        def _():
            dcw_ref[...] = jnp.zeros_like(dcw_ref)
            dcb_ref[...] = jnp.zeros_like(dcb_ref)

        taps, hc = [], []
        for p in range(2):
            xe = jnp.concatenate([jnp.where(i > 0, prev_ref[p], 0.0), hh_ref[p],
                                  jnp.where(is_last, 0.0, next_ref[p])], axis=0)
            x2, x1 = pltpu.roll(xe, 2, 0), pltpu.roll(xe, 1, 0)
            hc.append(_conv3(cw_ref, p, x2, x1, xe, cb_ref[p])[CONV_HALO:])
            taps.append((x2[CONV_HALO:CONV_HALO + t], x1[CONV_HALO:CONV_HALO + t], xe[CONV_HALO:CONV_HALO + t]))
        gate, val = hc
        dae = jnp.concatenate([da_ref[...], jnp.where(is_last, 0.0, danext_ref[...])], axis=0)
        sg = jax.nn.sigmoid(gate)
        dval = dae * (gate * sg)
        dgate = dae * val * (sg * (1.0 + gate * (1.0 - sg)))
        m = t + CONV_HALO
        for p, dhc in enumerate((dgate, dval)):
            dh = (cw_ref[p, 2:3, :] * dhc + cw_ref[p, 1:2, :] * pltpu.roll(dhc, m - 1, 0)
                  + cw_ref[p, 0:1, :] * pltpu.roll(dhc, m - 2, 0))
            dhh_ref[p] = dh[:t].astype(BF16)
            d0 = dhc[:t]
            for kk, tap in enumerate(taps[p]):
                dcw_ref[p, kk:kk + 1, :] += jnp.sum(d0 * tap, axis=0, keepdims=True)
            dcb_ref[p] += jnp.sum(d0, axis=0, keepdims=True)

    return pl.pallas_call(
        body, grid=(f // tc, nt), in_specs=[tile3, prev3, next3, tile2, next2, wspec, bspec],
        out_specs=[tile3, wspec, bspec],
        out_shape=[jax.ShapeDtypeStruct((2, s, f), BF16), jax.ShapeDtypeStruct((2, 3, f), F32),
                   jax.ShapeDtypeStruct((2, 1, f), F32)],
        name=name, compiler_params=_params("parallel", "arbitrary"),
    )(hh, hh, hh, dact, dact, cw, cb)


def _position():
    return lax.axis_index("x"), lax.axis_index("y"), lax.axis_index("c")


def _linear(px, py, pc):
    return 4 * px + 2 * py + pc


def _peers_of(x, y, c):
    peers = []
    for mask in range(1, N_DEV):
        peers.append((1 - x if mask & 4 else x, 1 - y if mask & 2 else y, 1 - c if mask & 1 else c))
    return peers


def _exchange_copy(src_ref, land_ref, send_sem, recv_sem, peer, mine, scatter, arriving):
    src = src_ref.at[_linear(*peer)] if scatter else src_ref
    dst = land_ref.at[_linear(*peer) if arriving else mine]
    return pltpu.make_async_remote_copy(src_ref=src, dst_ref=dst, send_sem=send_sem, recv_sem=recv_sem,
                                        device_id=peer, device_id_type=MESH)


_EXCHANGE_COLLECTIVE_ID = 7


def _sequencer_exchange(srcs, *, scatter, name):
    n = len(srcs)
    src_refs = [jax.new_ref(a, memory_space=pltpu.MemorySpace.HBM) for a in srcs]
    land_refs = [jax.empty_ref(jax.ShapeDtypeStruct(a.shape if scatter else (N_DEV,) + a.shape, a.dtype),
                               memory_space=pltpu.MemorySpace.HBM) for a in srcs]

    @pl.kernel(mesh=plsc.ScalarSubcoreMesh(axis_name="sequencer", num_cores=1), name=name,
               scratch_types=(pltpu.SemaphoreType.DMA((7 * n,)), pltpu.SemaphoreType.DMA((7 * n,)),
                              pltpu.SemaphoreType.DMA((n,))),
               compiler_params=pltpu.CompilerParams(collective_id=_EXCHANGE_COLLECTIVE_ID))
    def launch(send_sems, recv_sems, local_sems):
        x, y, c = _position()
        mine = _linear(x, y, c)
        peers = _peers_of(x, y, c)
        barrier = pltpu.get_barrier_semaphore()
        for peer in peers:
            pl.semaphore_signal(barrier, inc=1, device_id=peer, device_id_type=MESH)
        pl.semaphore_wait(barrier, N_DEV - 1)
        local = [pltpu.make_async_copy(src_refs[t].at[mine] if scatter else src_refs[t], land_refs[t].at[mine],
                                       local_sems.at[t]) for t in range(n)]
        for cp in local:
            cp.start()
        sends = []
        for t in range(n):
            for k, peer in enumerate(peers):
                cp = _exchange_copy(src_refs[t], land_refs[t], send_sems.at[7 * t + k], recv_sems.at[7 * t + k],
                                    peer, mine, scatter, arriving=False)
                cp.start()
                sends.append(cp)
        for cp in local:
            cp.wait()
        for cp in sends:
            cp.wait_send()
        for t in range(n):
            for k, peer in enumerate(peers):
                _exchange_copy(src_refs[t], land_refs[t], send_sems.at[7 * t + k], recv_sems.at[7 * t + k],
                               peer, mine, scatter, arriving=True).wait_recv()

    launch()
    return [r[...] for r in land_refs]


def _adamw_math(g, w, m, v):
    m2 = ADAM_B1 * m + (1.0 - ADAM_B1) * g
    v2 = ADAM_B2 * v + (1.0 - ADAM_B2) * (g * g)
    m_hat = m2 / (1.0 - ADAM_B1 ** ADAM_STEP)
    v_hat = v2 / (1.0 - ADAM_B2 ** ADAM_STEP)
    delta = -ADAM_LR * (m_hat / (jnp.sqrt(v_hat) + ADAM_EPS) + ADAM_WD * w)
    return delta, m2, v2


def _adamw(slots, w, m, v, *, name, tr=256):
    depth = len(slots)
    _, r, c = slots[0].shape
    tr = next((cand for cand in range(min(r, tr), 15, -1) if r % cand == 0 and cand % 16 == 0), r)

    def body(*refs):
        s_refs = refs[:depth]
        w_ref, m_ref, v_ref, g_ref, d_ref, m2_ref, v2_ref = refs[depth:]
        layer = pl.program_id(0)
        for l in range(depth):
            @pl.when(layer == l)
            def _():
                g = s_refs[l][0].astype(F32)
                for d in range(1, N_DEV):
                    g = g + s_refs[l][d].astype(F32)
                delta, m2, v2 = _adamw_math(g, w_ref[...], m_ref[...], v_ref[...])
                g_ref[...] = g
                d_ref[...] = delta
                m2_ref[...] = m2
                v2_ref[...] = v2

    blk = pl.BlockSpec((None, tr, c), lambda layer, i: (layer, i, 0))
    sblks = [pl.BlockSpec((N_DEV, tr, c), lambda layer, i, l=l: (0, jnp.where(layer == l, i, 0), 0))
             for l in range(depth)]
    shape = jax.ShapeDtypeStruct((depth, r, c), F32)
    return pl.pallas_call(
        body, grid=(depth, r // tr), in_specs=sblks + [blk, blk, blk], out_specs=[blk] * 4,
        out_shape=[shape] * 4, name=name, compiler_params=_params("arbitrary", "arbitrary"),
    )(*slots, w, m, v)


_SHARDED = ("w_in", "w_out", "wq", "wk", "wv", "wo", "w_up", "conv_w", "w_down")
_SMALL = ("norm_mix_g", "pool_w", "pool_scale", "sgu_g", "sgu_w", "sgu_b", "norm_xattn_g", "mem_norm_g",
          "norm_ffn_g", "conv_b", "final_norm_g")
_WEIGHTS = ("norm_mix_g", "w_in", "pool_w", "pool_scale", "sgu_g", "sgu_w", "sgu_b", "w_out", "norm_xattn_g",
            "mem_norm_g", "wq", "wk", "wv", "wo", "norm_ffn_g", "w_up", "conv_w", "conv_b", "w_down",
            "final_norm_g")
_PACK_LANES = 128
_GATHER_GROUPS = (("w_in",), ("w_out",), ("wq", "wk", "wv", "wo"), ("w_up", "conv_w", "w_down"))


def _cols_to_blocks(a, *, name, tr=256):
    r, c8 = a.shape
    c = c8 // N_DEV
    tr = _tile(r, tr)

    def body(a_ref, o_ref):
        for dev in range(N_DEV):
            o_ref[dev] = a_ref[:, dev * c:(dev + 1) * c]

    return pl.pallas_call(
        body, grid=(r // tr,), in_specs=[pl.BlockSpec((tr, c8), lambda i: (i, 0))],
        out_specs=pl.BlockSpec((N_DEV, tr, c), lambda i: (0, i, 0)),
        out_shape=jax.ShapeDtypeStruct((N_DEV, r, c), a.dtype), name=name, compiler_params=_params("parallel"),
    )(a)


def _blocks_to_cols(a, *, name, tr=256):
    n, r, c = a.shape
    tr = _tile(r, tr)

    def body(a_ref, o_ref):
        for dev in range(n):
            o_ref[:, dev * c:(dev + 1) * c] = a_ref[dev]

    return pl.pallas_call(
        body, grid=(r // tr,), in_specs=[pl.BlockSpec((n, tr, c), lambda i: (0, i, 0))],
        out_specs=pl.BlockSpec((tr, n * c), lambda i: (i, 0)),
        out_shape=jax.ShapeDtypeStruct((r, n * c), a.dtype), name=name, compiler_params=_params("parallel"),
    )(a)


def _pin(x, *deps):
    return lax.optimization_barrier((x, *deps))[0]


def _pack(arrays):
    flat = jnp.concatenate([a.reshape(-1) for a in arrays])
    assert flat.shape[0] % (8 * _PACK_LANES) == 0
    return flat.reshape(-1, _PACK_LANES)


def _unpack(packed, like):
    flat = packed.reshape(-1)
    out, off = [], 0
    for a in like:
        out.append(flat[off:off + a.size].reshape(a.shape))
        off += a.size
    return out


def kernel(x, mem, norm_mix_g, w_in, pool_w, pool_scale, sgu_g, sgu_w, sgu_b, w_out, norm_xattn_g, mem_norm_g, wq, wk, wv, wo, norm_ffn_g, w_up, conv_w, conv_b, w_down, final_norm_g, loss_target, m_norm_mix_g, m_w_in, m_pool_w, m_pool_scale, m_sgu_g, m_sgu_w, m_sgu_b, m_w_out, m_norm_xattn_g, m_mem_norm_g, m_wq, m_wk, m_wv, m_wo, m_norm_ffn_g, m_w_up, m_conv_w, m_conv_b, m_w_down, m_final_norm_g, v_norm_mix_g, v_w_in, v_pool_w, v_pool_scale, v_sgu_g, v_sgu_w, v_sgu_b, v_w_out, v_norm_xattn_g, v_mem_norm_g, v_wq, v_wk, v_wv, v_wo, v_norm_ffn_g, v_w_up, v_conv_w, v_conv_b, v_w_down, v_final_norm_g):
    W = dict(norm_mix_g=norm_mix_g, w_in=w_in, pool_w=pool_w, pool_scale=pool_scale, sgu_g=sgu_g, sgu_w=sgu_w,
             sgu_b=sgu_b, w_out=w_out, norm_xattn_g=norm_xattn_g, mem_norm_g=mem_norm_g, wq=wq, wk=wk, wv=wv, wo=wo,
             norm_ffn_g=norm_ffn_g, w_up=w_up, conv_w=conv_w, conv_b=conv_b, w_down=w_down,
             final_norm_g=final_norm_g)
    M = dict(norm_mix_g=m_norm_mix_g, w_in=m_w_in, pool_w=m_pool_w, pool_scale=m_pool_scale, sgu_g=m_sgu_g,
             sgu_w=m_sgu_w, sgu_b=m_sgu_b, w_out=m_w_out, norm_xattn_g=m_norm_xattn_g, mem_norm_g=m_mem_norm_g,
             wq=m_wq, wk=m_wk, wv=m_wv, wo=m_wo, norm_ffn_g=m_norm_ffn_g, w_up=m_w_up, conv_w=m_conv_w,
             conv_b=m_conv_b, w_down=m_w_down, final_norm_g=m_final_norm_g)
    V = dict(norm_mix_g=v_norm_mix_g, w_in=v_w_in, pool_w=v_pool_w, pool_scale=v_pool_scale, sgu_g=v_sgu_g,
             sgu_w=v_sgu_w, sgu_b=v_sgu_b, w_out=v_w_out, norm_xattn_g=v_norm_xattn_g, mem_norm_g=v_mem_norm_g,
             wq=v_wq, wk=v_wk, wv=v_wv, wo=v_wo, norm_ffn_g=v_norm_ffn_g, w_up=v_w_up, conv_w=v_conv_w,
             conv_b=v_conv_b, w_down=v_w_down, final_norm_g=v_final_norm_g)

    s, d = x.shape[1], x.shape[2]
    f = w_down.shape[1] * N_DEV
    h = x.reshape(s, d)
    memx = mem.reshape(mem.shape[1], d)
    target = loss_target.reshape(s, d)

    gathered = {}

    def launch_gather(l, gi, after):
        if l >= DEPTH:
            return
        names = _GATHER_GROUPS[gi]
        shards = [W[nme][l] if nme == "conv_w" else W[nme][l].astype(BF16) for nme in names]
        if after is not None:
            shards[0], _ = lax.optimization_barrier((shards[0], after))
        gathered[l, gi] = dict(zip(names, _sequencer_exchange(shards, scatter=False, name=f"gather_{l}_{gi}")))

    launch_gather(0, 0, None)

    saved, full = [], []
    for l in range(DEPTH):
        sgu_bias = jnp.repeat(sgu_b[l].T, GROUP, axis=1)
        cb = conv_b[l].reshape(2, 1, f)
        if l == 0:
            xn1 = _rms_fwd(h, norm_mix_g[l], name=f"norm_mix_{l}")
            launch_gather(0, 1, xn1)
        w_in_f = _blocks_to_cols(gathered[l, 0]["w_in"], name=f"w_in_cols_{l}")
        proj = _mm_nn(xn1, w_in_f, out_dtype=F32, name=f"proj_in_{l}")
        if l == 0:
            launch_gather(0, 2, proj)
        cat = _mixer_fwd(proj, pool_w[l], pool_scale[l].reshape(1, -1), sgu_g[l].reshape(1, -1), sgu_w[l], sgu_bias,
                         name=f"mixer_{l}")
        if l == 0:
            launch_gather(0, 3, cat)
        w_out_f = gathered[l, 1]["w_out"].reshape(-1, d)
        h1, xn2 = _mm_nn(cat, w_out_f, out_dtype=F32, res=h, norm_gain=norm_xattn_g[l], name=f"proj_out_{l}")
        launch_gather(l + 1, 0, h1)
        g = gathered[l, 2]
        wq_f, wk_f, wv_f, wo_f = (g[nme].reshape(-1, d) for nme in ("wq", "wk", "wv", "wo"))
        q = _mm_nn(xn2, wq_f, out_dtype=BF16, name=f"q_{l}")
        launch_gather(l + 1, 1, q)
        memn = _rms_fwd(memx, mem_norm_g[l], name=f"norm_mem_{l}")
        k = _mm_nn(memn, wk_f, out_dtype=BF16, name=f"k_{l}")
        v = _mm_nn(memn, wv_f, out_dtype=BF16, name=f"v_{l}")
        o = _attn_fwd(q, k, v, name=f"attn_{l}")
        h2, xn3 = _mm_nn(o, wo_f, out_dtype=F32, res=h1, norm_gain=norm_ffn_g[l], name=f"attn_out_{l}")
        launch_gather(l + 1, 2, h2)
        g = gathered[l, 3]
        w_up_f = _blocks_to_cols(g["w_up"], name=f"w_up_cols_{l}")
        conv_w_f = _blocks_to_cols(g["conv_w"], name=f"conv_w_cols_{l}").reshape(3, 2, f).transpose(1, 0, 2)
        w_down_f = g["w_down"].reshape(-1, d)
        hh = _mm_up(xn3, w_up_f, name=f"ffn_up_{l}")
        launch_gather(l + 1, 3, hh)
        act = _convgate_fwd(hh, conv_w_f, cb, name=f"convgate_{l}")
        if l + 1 < DEPTH:
            h3, xn1_next = _mm_nn(act, w_down_f, out_dtype=F32, res=h2, tm=512, norm_gain=norm_mix_g[l + 1],
                                  name=f"ffn_down_{l}")
        else:
            h3, xn1_next = _mm_nn(act, w_down_f, out_dtype=F32, res=h2, tm=512, name=f"ffn_down_{l}"), None
        full.append(dict(w_in=w_in_f, w_out=w_out_f, wq=wq_f, wk=wk_f, wv=wv_f, wo=wo_f, w_up=w_up_f,
                         conv_w=conv_w_f, w_down=w_down_f))
        saved.append(dict(h0=h, xn1=xn1, proj=proj, cat=cat, h1=h1, xn2=xn2, q=q, memn=memn, k=k, v=v, o=o, h2=h2,
                          xn3=xn3, hh=hh, act=act, sgu_bias=sgu_bias, cb=cb))
        h, xn1 = h3, xn1_next

    dh, dhb, dg_final, loss_row = _loss_head(h, final_norm_g, target, name="loss_head")

    slots = {nme: [None] * DEPTH for nme in _SHARDED}
    small = [None] * DEPTH

    previous = []

    def scatter(l, tag, names, parts):
        parts = [_pin(parts[0], *previous)] + parts[1:]
        arrived = _sequencer_exchange(parts, scatter=True, name=f"scatter_{tag}_{l}")
        previous[:] = arrived[:1]
        for nme, land in zip(names, arrived):
            slots[nme][l] = land
        return parts

    for l in reversed(range(DEPTH)):
        fw, sv = full[l], saved[l]
        dact = _mm_nt(dhb, fw["w_down"], out_dtype=F32, tm=512, name=f"d_act_{l}")
        g_w_down = _mm_tn(sv["act"], dhb, tm=f // 2, name=f"g_w_down_{l}")
        dhh, g_conv_w, g_conv_b = _convgate_bwd(sv["hh"], dact, fw["conv_w"], sv["cb"], name=f"d_convgate_{l}")
        g_w_up = _mm_up_tn(sv["xn3"], dhh, name=f"g_w_up_{l}")
        g_conv_w_cols = g_conv_w.transpose(1, 0, 2).reshape(3, 2 * f)
        parts = [_cols_to_blocks(g_w_up, name=f"g_w_up_blocks_{l}"),
                 _cols_to_blocks(g_conv_w_cols, name=f"g_conv_w_blocks_{l}"), g_w_down.reshape(N_DEV, -1, d)]
        parts = scatter(l, "ffn", ("w_up", "conv_w", "w_down"), parts)
        dh2, dh2b, g_norm_ffn = _mm_up_nt(_pin(dhh, *parts), fw["w_up"], name=f"d_norm_ffn_{l}",
                                          norm_bwd=(sv["h2"], norm_ffn_g[l], dh))

        do = _mm_nt(dh2b, fw["wo"], out_dtype=BF16, name=f"d_o_{l}")
        g_wo = _mm_tn(sv["o"], dh2b, name=f"g_wo_{l}")
        dq, dk, dv = _attn_bwd(sv["q"], sv["k"], sv["v"], do, name=f"d_attn_{l}")
        dkb, dvb = dk.astype(BF16), dv.astype(BF16)
        g_wq = _mm_tn(sv["xn2"], dq, name=f"g_wq_{l}")
        g_wk = _mm_tn(sv["memn"], dkb, name=f"g_wk_{l}")
        g_wv = _mm_tn(sv["memn"], dvb, name=f"g_wv_{l}")
        parts = [g.reshape(N_DEV, -1, d) for g in (g_wq, g_wk, g_wv, g_wo)]
        parts = scatter(l, "attn", ("wq", "wk", "wv", "wo"), parts)
        dq = _pin(dq, *parts)
        dmemn = _mm_nt(dkb, fw["wk"], out_dtype=F32, name=f"d_memn_k_{l}")
        dmemn = _mm_nt(dvb, fw["wv"], out_dtype=F32, res=dmemn, name=f"d_memn_v_{l}")
        _, _, g_mem_norm = _rms_bwd(memx, dmemn, mem_norm_g[l], None, name=f"d_norm_mem_{l}")
        dh1, dh1b, g_norm_xattn = _mm_nt(dq, fw["wq"], name=f"d_norm_xattn_{l}",
                                         norm_bwd=(sv["h1"], norm_xattn_g[l], dh2))

        dcat = _mm_nt(dh1b, fw["w_out"], out_dtype=F32, name=f"d_cat_{l}")
        g_w_out = _mm_tn(sv["cat"], dh1b, name=f"g_w_out_{l}")
        dproj, g_pool_w, g_pool_scale, g_sgu_g, g_sgu_w, g_sgu_b = _mixer_bwd(
            sv["proj"], dcat, pool_w[l], pool_scale[l].reshape(1, -1), sgu_g[l].reshape(1, -1), sgu_w[l],
            sv["sgu_bias"], name=f"d_mixer_{l}")
        g_w_in = _mm_tn(sv["xn1"], dproj, name=f"g_w_in_{l}")
        parts = [_cols_to_blocks(g_w_in, name=f"g_w_in_blocks_{l}"), g_w_out.reshape(N_DEV, -1, d)]
        parts = scatter(l, "mix", ("w_in", "w_out"), parts)
        dh, dhb, g_norm_mix = _mm_nt(_pin(dproj, *parts), fw["w_in"], name=f"d_norm_mix_{l}",
                                     norm_bwd=(sv["h0"], norm_mix_g[l], dh1))

        small[l] = dict(norm_mix_g=g_norm_mix.reshape(-1), pool_w=g_pool_w, pool_scale=g_pool_scale.reshape(-1),
                        sgu_g=g_sgu_g.reshape(-1), sgu_w=g_sgu_w, sgu_b=g_sgu_b, norm_xattn_g=g_norm_xattn.reshape(-1),
                        mem_norm_g=g_mem_norm.reshape(-1), norm_ffn_g=g_norm_ffn.reshape(-1),
                        conv_b=g_conv_b.reshape(-1))
    grad_x = dh.reshape(x.shape)

    out = {}
    for nme in _SHARDED:
        w3 = W[nme].reshape(DEPTH, -1, W[nme].shape[-1])
        res = _adamw([sl.reshape((N_DEV,) + w3.shape[1:]) for sl in slots[nme]], w3, M[nme].reshape(w3.shape),
                     V[nme].reshape(w3.shape), name=f"adamw_{nme}")
        out[nme] = [r.reshape(W[nme].shape) for r in res]

    small_names = [n for n in _SMALL]
    contrib = []
    for nme in small_names:
        if nme == "final_norm_g":
            contrib.append(dg_final.reshape(-1))
        else:
            contrib.append(jnp.stack([small[l][nme] for l in range(DEPTH)]))
    tail = 8 * _PACK_LANES
    packed_g = _pack(contrib + [jnp.pad(loss_row[0, :1], (0, tail - 1))])
    (all_g,) = _sequencer_exchange([_pin(packed_g, *previous)], scatter=False, name="gather_small_grads")
    rows = packed_g.shape[0]
    loss = jnp.sum(all_g[:, rows - 8, 0])
    zeros = jnp.zeros((tail,), F32)
    res = _adamw([all_g], _pack([W[n] for n in small_names] + [zeros]).reshape(1, rows, -1),
                 _pack([M[n] for n in small_names] + [zeros]).reshape(1, rows, -1),
                 _pack([V[n] for n in small_names] + [zeros]).reshape(1, rows, -1), name="adamw_small", tr=rows // 2)
    unpacked = [_unpack(r, [W[n] for n in small_names]) for r in res]
    for i, nme in enumerate(small_names):
        out[nme] = [unpacked[j][i] for j in range(4)]

    grads = [out[n][0] for n in _WEIGHTS]
    deltas = [out[n][1] for n in _WEIGHTS]
    new_m = [out[n][2] for n in _WEIGHTS]
    new_v = [out[n][3] for n in _WEIGHTS]
    return (loss, grad_x, *grads, *deltas, *new_m, *new_v)
```

```python
import jax
import jax.numpy as jnp
from jax import lax
from jax.experimental import pallas as pl
from jax.experimental.pallas import tpu as pltpu
from jax.experimental.pallas import tpu_sc as plsc

F32 = jnp.float32
BF16 = jnp.bfloat16
MESH = pl.DeviceIdType.MESH

EPS = 1e-6
N_DEV = 8
DEPTH = 2
POOL_WINDOWS = (2, 4, 8, 16)
GROUP = 128
POOL_WIDTH = 512
SGU_WIDTH = 512
HEADS = 4
HEAD_DIM = 256
POOL_HALO = 16
CONV_HALO = 8

ADAM_LR = 0.001
ADAM_B1 = 0.9
ADAM_B2 = 0.999
ADAM_EPS = 1e-08
ADAM_WD = 0.01
ADAM_STEP = 10

VMEM_LIMIT_BYTES = 52 * 1024 * 1024


def _params(*semantics):
    return pltpu.CompilerParams(dimension_semantics=semantics, vmem_limit_bytes=VMEM_LIMIT_BYTES)


def _tile(n, want):
    t = min(n, want)
    assert n % t == 0, (n, want)
    return t


_DOT_DIMS = {
    "nn": (((1,), (0,)), ((), ())),
    "nt": (((1,), (1,)), ((), ())),
    "tn": (((0,), (0,)), ((), ())),
}


def _mm(a, b, *, dims, grid, a_spec, b_spec, o_spec, out_shape, out_dtype, acc_shape, name, res=None, res_spec=None,
        norm_bwd=None, norm_out=None):
    nk = grid[2]
    dn = _DOT_DIMS[dims]
    extras, extra_specs = [], []
    if res is not None:
        extras, extra_specs = [res], [res_spec]
    if norm_bwd is not None:
        h, gain, dres, row_spec, gain_spec = norm_bwd
        extras = [h, gain] + ([dres] if dres is not None else [])
        extra_specs = [row_spec, gain_spec] + ([row_spec] if dres is not None else [])
        out_specs = [row_spec, row_spec, gain_spec]
        out_shapes = [jax.ShapeDtypeStruct(h.shape, F32), jax.ShapeDtypeStruct(h.shape, BF16),
                      jax.ShapeDtypeStruct(gain.shape, F32)]
    elif norm_out is not None:
        extras, extra_specs = extras + [norm_out[0]], extra_specs + [norm_out[1]]
        out_specs = [o_spec, o_spec]
        out_shapes = [jax.ShapeDtypeStruct(out_shape, out_dtype), jax.ShapeDtypeStruct(out_shape, BF16)]
    else:
        out_specs, out_shapes = o_spec, jax.ShapeDtypeStruct(out_shape, out_dtype)
    n_extra = len(extras)

    def body(*refs):
        a_ref, b_ref = refs[:2]
        extra_refs = refs[2:2 + n_extra]
        out_refs = refs[2 + n_extra:len(refs) - (1 if nk > 1 else 0)]
        p = lax.dot_general(a_ref[...], b_ref[...], dn, preferred_element_type=F32)

        def finish(r):
            if norm_bwd is not None:
                _rms_bwd_math(r, extra_refs[0], extra_refs[1], extra_refs[2] if n_extra == 3 else None,
                              *out_refs, first=pl.program_id(0) == 0)
                return
            if res is not None:
                r = r + extra_refs[0][...]
            out_refs[0][...] = r.astype(out_refs[0].dtype)
            if norm_out is not None:
                scale = lax.rsqrt(jnp.mean(r * r, axis=-1, keepdims=True) + EPS)
                out_refs[1][...] = ((r * scale) * extra_refs[-1][...]).astype(BF16)

        if nk == 1:
            finish(p)
        else:
            acc_ref = refs[-1]
            k = pl.program_id(2)

            @pl.when(k == 0)
            def _():
                acc_ref[...] = p

            @pl.when(k > 0)
            def _():
                acc_ref[...] += p

            @pl.when(k == nk - 1)
            def _():
                finish(acc_ref[...])

    scratch = [pltpu.VMEM(acc_shape, F32)] if nk > 1 else []
    return pl.pallas_call(
        body, grid=grid, in_specs=[a_spec, b_spec] + extra_specs, out_specs=out_specs,
        out_shape=out_shapes, scratch_shapes=scratch, name=name,
        compiler_params=_params("arbitrary" if norm_bwd is not None else "parallel", "parallel", "arbitrary"),
    )(a, b, *extras)


def _rms_bwd_math(dy, h_ref, g_ref, dres_ref, dh_ref, dhb_ref, dg_ref, *, first):
    x = h_ref[...]
    r = lax.rsqrt(jnp.mean(x * x, axis=-1, keepdims=True) + EPS)
    a = dy * g_ref[...]
    m = jnp.mean(a * x, axis=-1, keepdims=True)
    dh = r * a - x * (r * r * r * m)
    if dres_ref is not None:
        dh = dh + dres_ref[...]
    dh_ref[...] = dh
    dhb_ref[...] = dh.astype(BF16)
    part = jnp.sum(dy * (x * r), axis=0, keepdims=True)

    @pl.when(first)
    def _():
        dg_ref[...] = part

    @pl.when(jnp.logical_not(first))
    def _():
        dg_ref[...] += part


def _mm_nn(a, b, *, out_dtype, name, res=None, tm=1024, norm_gain=None):
    m, k = a.shape
    n = b.shape[1]
    tm = _tile(m, tm)
    spec_o = pl.BlockSpec((tm, n), lambda i, j, kk: (i, 0))
    norm_out = None if norm_gain is None else (norm_gain.reshape(1, n), pl.BlockSpec((1, n), lambda i, j, kk: (0, 0)))
    return _mm(a, b, dims="nn", grid=(m // tm, 1, 1),
               a_spec=pl.BlockSpec((tm, k), lambda i, j, kk: (i, 0)),
               b_spec=pl.BlockSpec((k, n), lambda i, j, kk: (0, 0)),
               o_spec=spec_o, out_shape=(m, n), out_dtype=out_dtype, acc_shape=None, name=name,
               res=res, res_spec=spec_o if res is not None else None, norm_out=norm_out)


def _norm_bwd_arg(h, gain, dres, tm):
    d = h.shape[1]
    return (h, gain.reshape(1, d), dres, pl.BlockSpec((tm, d), lambda i, j, kk: (i, 0)),
            pl.BlockSpec((1, d), lambda i, j, kk: (0, 0)))


def _mm_nt(a, b, *, out_dtype=F32, name, res=None, tm=1024, norm_bwd=None):
    m, k = a.shape
    n = b.shape[0]
    tm = _tile(m, tm)
    spec_o = pl.BlockSpec((tm, n), lambda i, j, kk: (i, 0))
    return _mm(a, b, dims="nt", grid=(m // tm, 1, 1),
               a_spec=pl.BlockSpec((tm, k), lambda i, j, kk: (i, 0)),
               b_spec=pl.BlockSpec((n, k), lambda i, j, kk: (0, 0)),
               o_spec=spec_o, out_shape=(m, n), out_dtype=out_dtype, acc_shape=None, name=name,
               res=res, res_spec=spec_o if res is not None else None,
               norm_bwd=None if norm_bwd is None else _norm_bwd_arg(*norm_bwd, tm))


_TN_ROWS = 2048


def _mm_tn(a, b, *, name, tm=None, tn=None, ts=_TN_ROWS, out_dtype=BF16):
    s, m = a.shape
    n = b.shape[1]
    tm = m if tm is None else tm
    tn = n if tn is None else tn
    ts = _tile(s, ts)
    return _mm(a, b, dims="tn", grid=(m // tm, n // tn, s // ts),
               a_spec=pl.BlockSpec((ts, tm), lambda i, j, kk: (kk, i)),
               b_spec=pl.BlockSpec((ts, tn), lambda i, j, kk: (kk, j)),
               o_spec=pl.BlockSpec((tm, tn), lambda i, j, kk: (i, j)),
               out_shape=(m, n), out_dtype=out_dtype, acc_shape=(tm, tn), name=name)


def _mm_up(xn, w_up, *, name, tm=512):
    s, d = xn.shape
    f = w_up.shape[1] // 2
    tm = _tile(s, tm)
    return _mm(xn, w_up, dims="nn", grid=(2, s // tm, 1),
               a_spec=pl.BlockSpec((tm, d), lambda j, i, kk: (i, 0)),
               b_spec=pl.BlockSpec((d, f), lambda j, i, kk: (0, j)),
               o_spec=pl.BlockSpec((None, tm, f), lambda j, i, kk: (j, i, 0)),
               out_shape=(2, s, f), out_dtype=F32, acc_shape=None, name=name)


def _mm_up_nt(dhh, w_up, *, name, tm=512, norm_bwd=None):
    _, s, f = dhh.shape
    d = w_up.shape[0]
    tm = _tile(s, tm)
    return _mm(dhh, w_up, dims="nt", grid=(s // tm, 1, 2),
               a_spec=pl.BlockSpec((None, tm, f), lambda i, j, kk: (kk, i, 0)),
               b_spec=pl.BlockSpec((d, f), lambda i, j, kk: (0, kk)),
               o_spec=pl.BlockSpec((tm, d), lambda i, j, kk: (i, 0)),
               out_shape=(s, d), out_dtype=F32, acc_shape=(tm, d), name=name,
               norm_bwd=None if norm_bwd is None else _norm_bwd_arg(*norm_bwd, tm))


def _mm_up_tn(xn, dhh, *, name, ts=_TN_ROWS):
    s, d = xn.shape
    f = dhh.shape[2]
    tn = f // 2
    ts = _tile(s, ts)
    return _mm(xn, dhh, dims="tn", grid=(1, 4, s // ts),
               a_spec=pl.BlockSpec((ts, d), lambda i, j, kk: (kk, 0)),
               b_spec=pl.BlockSpec((None, ts, tn), lambda i, j, kk: (j // 2, kk, j % 2)),
               o_spec=pl.BlockSpec((d, tn), lambda i, j, kk: (0, j)),
               out_shape=(d, 2 * f), out_dtype=BF16, acc_shape=(d, tn), name=name)


def _rms_fwd(h, g, *, name, tr=512):
    s, d = h.shape
    tr = _tile(s, tr)

    def body(h_ref, g_ref, o_ref):
        x = h_ref[...]
        r = lax.rsqrt(jnp.mean(x * x, axis=-1, keepdims=True) + EPS)
        o_ref[...] = ((x * r) * g_ref[...]).astype(o_ref.dtype)

    row = pl.BlockSpec((tr, d), lambda i: (i, 0))
    return pl.pallas_call(
        body, grid=(s // tr,), in_specs=[row, pl.BlockSpec((1, d), lambda i: (0, 0))], out_specs=row,
        out_shape=jax.ShapeDtypeStruct((s, d), BF16), name=name, compiler_params=_params("parallel"),
    )(h, g.reshape(1, d))


def _rms_bwd(h, dxn, g, dres, *, name, tr=512):
    s, d = h.shape
    tr = _tile(s, tr)
    has_res = dres is not None

    def body(*refs):
        if has_res:
            h_ref, dxn_ref, g_ref, dres_ref, dh_ref, dhb_ref, dg_ref = refs
        else:
            h_ref, dxn_ref, g_ref, dh_ref, dhb_ref, dg_ref = refs
            dres_ref = None
        _rms_bwd_math(dxn_ref[...].astype(F32), h_ref, g_ref, dres_ref, dh_ref, dhb_ref, dg_ref,
                      first=pl.program_id(0) == 0)

    row = pl.BlockSpec((tr, d), lambda i: (i, 0))
    vec = pl.BlockSpec((1, d), lambda i: (0, 0))
    in_specs = [row, row, vec] + ([row] if has_res else [])
    args = (h, dxn, g.reshape(1, d)) + ((dres,) if has_res else ())
    return pl.pallas_call(
        body, grid=(s // tr,), in_specs=in_specs, out_specs=[row, row, vec],
        out_shape=[jax.ShapeDtypeStruct((s, d), F32), jax.ShapeDtypeStruct((s, d), BF16),
                   jax.ShapeDtypeStruct((1, d), F32)],
        name=name, compiler_params=_params("arbitrary"),
    )(*args)


def _loss_head(h, g, target, *, name, tr=512):
    s, d = h.shape
    tr = _tile(s, tr)
    nt = s // tr

    def body(h_ref, g_ref, t_ref, dh_ref, dhb_ref, dg_ref, loss_ref, sq_ref):
        i = pl.program_id(0)
        x = h_ref[...]
        gain = g_ref[...]
        r = lax.rsqrt(jnp.mean(x * x, axis=-1, keepdims=True) + EPS)
        xh = x * r
        err = xh * gain - t_ref[...]
        dy = err * (1.0 / d)
        a = dy * gain
        m = jnp.mean(a * x, axis=-1, keepdims=True)
        dh = r * a - x * (r * r * r * m)
        dh_ref[...] = dh
        dhb_ref[...] = dh.astype(BF16)
        dg_part = jnp.sum(dy * xh, axis=0, keepdims=True)
        sq_part = jnp.sum(err * err, axis=0, keepdims=True)

        @pl.when(i == 0)
        def _():
            dg_ref[...] = dg_part
            sq_ref[...] = sq_part

        @pl.when(i > 0)
        def _():
            dg_ref[...] += dg_part
            sq_ref[...] += sq_part

        @pl.when(i == nt - 1)
        def _():
            total = jnp.sum(sq_ref[...], axis=1, keepdims=True) * (0.5 / d)
            loss_ref[...] = jnp.broadcast_to(total, loss_ref.shape)

    row = pl.BlockSpec((tr, d), lambda i: (i, 0))
    vec = pl.BlockSpec((1, d), lambda i: (0, 0))
    return pl.pallas_call(
        body, grid=(nt,), in_specs=[row, vec, row],
        out_specs=[row, row, vec, pl.BlockSpec((1, 128), lambda i: (0, 0))],
        out_shape=[jax.ShapeDtypeStruct((s, d), F32), jax.ShapeDtypeStruct((s, d), BF16),
                   jax.ShapeDtypeStruct((1, d), F32), jax.ShapeDtypeStruct((1, 128), F32)],
        scratch_shapes=[pltpu.VMEM((1, d), F32)], name=name, compiler_params=_params("arbitrary"),
    )(h, g.reshape(1, d), target)


_SQRT_HALF = 0.7071067811865476
_INV_SQRT_2PI = 0.3989422804014327


def _gelu(x):
    return 0.5 * x * (1.0 + lax.erf(x * _SQRT_HALF))


def _gelu_and_grad(x):
    cdf = 0.5 * (1.0 + lax.erf(x * _SQRT_HALF))
    return x * cdf, cdf + x * (jnp.exp(-0.5 * x * x) * _INV_SQRT_2PI)


def _trailing_sums(xe, win):
    s = xe
    sh = 1
    while sh < win:
        s = s + pltpu.roll(s, sh, 0)
        sh *= 2
    return s


def _leading_sums(xe, win):
    n = xe.shape[0]
    s = xe
    sh = 1
    while sh < win:
        s = s + pltpu.roll(s, n - sh, 0)
        sh *= 2
    return s


def _tril_mask():
    return lax.broadcasted_iota(jnp.int32, (GROUP, GROUP), 0) >= lax.broadcasted_iota(jnp.int32, (GROUP, GROUP), 1)


def _layernorm_stats(v):
    mu = jnp.mean(v, axis=-1, keepdims=True)
    xc = v - mu
    rstd = lax.rsqrt(jnp.mean(xc * xc, axis=-1, keepdims=True) + EPS)
    return xc * rstd, rstd


def _mixer_specs(s, t):
    halo_blocks = t // POOL_HALO
    tile = lambda w: pl.BlockSpec((t, w), lambda i: (i, 0))
    prev = pl.BlockSpec((POOL_HALO, POOL_WIDTH), lambda i: (jnp.maximum(i * halo_blocks - 1, 0), 0))
    nxt = pl.BlockSpec((POOL_HALO, POOL_WIDTH),
                       lambda i: (jnp.minimum((i + 1) * halo_blocks, s // POOL_HALO - 1), 0))
    const3 = pl.BlockSpec((HEADS, GROUP, GROUP), lambda i: (0, 0, 0))
    vec = pl.BlockSpec((1, POOL_WIDTH), lambda i: (0, 0))
    bias = pl.BlockSpec((GROUP, SGU_WIDTH), lambda i: (0, 0))
    return tile, prev, nxt, const3, vec, bias


def _mixer_fwd(proj, pool_w, pool_scale, sgu_g, sgu_w, sgu_bias, *, name, t=256):
    s = proj.shape[0]
    t = _tile(s, t)
    tile, prev, _, const3, vec, bias = _mixer_specs(s, t)

    def body(proj_ref, halo_ref, pw_ref, ps_ref, sg_ref, sw_ref, sb_ref, cat_ref):
        i = pl.program_id(0)
        row = i * t + lax.broadcasted_iota(jnp.int32, (t, 1), 0)
        p = proj_ref[:, 0:POOL_WIDTH]
        pe = jnp.concatenate([jnp.where(i > 0, halo_ref[...], 0.0), p], axis=0)
        for gi, win in enumerate(POOL_WINDOWS):
            cols = slice(gi * GROUP, (gi + 1) * GROUP)
            count = jnp.minimum(row + 1, win).astype(F32)
            d = _trailing_sums(pe[:, cols], win)[POOL_HALO:] / count - p[:, cols]
            y = jnp.dot(d.astype(BF16), pw_ref[gi].astype(BF16), preferred_element_type=F32) * ps_ref[:, cols]
            cat_ref[:, cols] = y.astype(BF16)

        u = _gelu(proj_ref[:, POOL_WIDTH:POOL_WIDTH + SGU_WIDTH])
        xhat, _ = _layernorm_stats(_gelu(proj_ref[:, POOL_WIDTH + SGU_WIDTH:]))
        vn = (xhat * sg_ref[...]).astype(BF16)
        tri = _tril_mask()
        for h in range(HEADS):
            cols = slice(h * GROUP, (h + 1) * GROUP)
            w = jnp.where(tri, sw_ref[h], 0.0).astype(BF16)
            for c in range(t // GROUP):
                rows = slice(c * GROUP, (c + 1) * GROUP)
                z = jnp.dot(w, vn[rows, cols], preferred_element_type=F32) + sb_ref[:, cols]
                cat_ref[rows, POOL_WIDTH + h * GROUP:POOL_WIDTH + (h + 1) * GROUP] = (u[rows, cols] * z).astype(BF16)

    return pl.pallas_call(
        body, grid=(s // t,),
        in_specs=[tile(POOL_WIDTH + 2 * SGU_WIDTH), prev, const3, vec, vec, const3, bias],
        out_specs=tile(POOL_WIDTH + SGU_WIDTH),
        out_shape=jax.ShapeDtypeStruct((s, POOL_WIDTH + SGU_WIDTH), BF16), name=name,
        compiler_params=_params("parallel"),
    )(proj, proj, pool_w, pool_scale, sgu_g, sgu_w, sgu_bias)


def _mixer_bwd(proj, dcat, pool_w, pool_scale, sgu_g, sgu_w, sgu_bias, *, name, t=256):
    s = proj.shape[0]
    t = _tile(s, t)
    nt = s // t
    tile, prev, nxt, const3, vec, bias = _mixer_specs(s, t)

    def body(proj_ref, halo_ref, dcat_ref, dnext_ref, pw_ref, ps_ref, sg_ref, sw_ref, sb_ref,
             dproj_ref, dpw_ref, dps_ref, dsg_ref, dsw_ref, dsb_ref, du_ref, dvn_ref, dz_ref):
        i = pl.program_id(0)

        @pl.when(i == 0)
        def _():
            dpw_ref[...] = jnp.zeros_like(dpw_ref)
            dps_ref[...] = jnp.zeros_like(dps_ref)
            dsg_ref[...] = jnp.zeros_like(dsg_ref)
            dsw_ref[...] = jnp.zeros_like(dsw_ref)
            dz_ref[...] = jnp.zeros_like(dz_ref)

        row = i * t + lax.broadcasted_iota(jnp.int32, (t, 1), 0)
        row_e = i * t + lax.broadcasted_iota(jnp.int32, (t + POOL_HALO, 1), 0)
        p = proj_ref[:, 0:POOL_WIDTH]
        pe = jnp.concatenate([jnp.where(i > 0, halo_ref[...], 0.0), p], axis=0)
        dyp = dcat_ref[:, 0:POOL_WIDTH]
        dye = jnp.concatenate([dyp, jnp.where(i < nt - 1, dnext_ref[...], 0.0)], axis=0)
        for gi, win in enumerate(POOL_WINDOWS):
            cols = slice(gi * GROUP, (gi + 1) * GROUP)
            count = jnp.minimum(row + 1, win).astype(F32)
            d = (_trailing_sums(pe[:, cols], win)[POOL_HALO:] / count - p[:, cols]).astype(BF16)
            pw = pw_ref[gi].astype(BF16)
            pre = jnp.dot(d, pw, preferred_element_type=F32)
            dps_ref[:, cols] += jnp.sum(dyp[:, cols] * pre, axis=0, keepdims=True)
            ys = (dye[:, cols] * ps_ref[:, cols]).astype(BF16)
            dpw_ref[gi] += lax.dot_general(d, ys[:t], _DOT_DIMS["tn"], preferred_element_type=F32)
            dd = lax.dot_general(ys, pw, _DOT_DIMS["nt"], preferred_element_type=F32)
            count_e = jnp.minimum(row_e + 1, win).astype(F32)
            dp = _leading_sums(dd / count_e, win)[:t] - dd[:t]
            dproj_ref[:, cols] = dp.astype(BF16)

        xu = proj_ref[:, POOL_WIDTH:POOL_WIDTH + SGU_WIDTH]
        xv = proj_ref[:, POOL_WIDTH + SGU_WIDTH:]
        u, gelu_grad_u = _gelu_and_grad(xu)
        v, gelu_grad_v = _gelu_and_grad(xv)
        xhat, rstd = _layernorm_stats(v)
        gain = sg_ref[...]
        vn = (xhat * gain).astype(BF16)
        tri = _tril_mask()
        for h in range(HEADS):
            cols = slice(h * GROUP, (h + 1) * GROUP)
            wf = jnp.where(tri, sw_ref[h], 0.0)
            w, wt = wf.astype(BF16), wf.T.astype(BF16)
            for c in range(t // GROUP):
                rows = slice(c * GROUP, (c + 1) * GROUP)
                vch = vn[rows, cols]
                z = jnp.dot(w, vch, preferred_element_type=F32) + sb_ref[:, cols]
                dy = dcat_ref[rows, POOL_WIDTH + h * GROUP:POOL_WIDTH + (h + 1) * GROUP]
                du_ref[rows, cols] = dy * z
                dz = dy * u[rows, cols]
                dz_ref[:, cols] += dz
                dzb = dz.astype(BF16)
                dsw_ref[h] += lax.dot_general(dzb, vch, _DOT_DIMS["nt"], preferred_element_type=F32)
                dvn_ref[rows, cols] = jnp.dot(wt, dzb, preferred_element_type=F32)
        dvn = dvn_ref[...]
        dsg_ref[...] += jnp.sum(dvn * xhat, axis=0, keepdims=True)
        dxh = dvn * gain
        dv = rstd * (dxh - jnp.mean(dxh, axis=-1, keepdims=True)
                     - xhat * jnp.mean(dxh * xhat, axis=-1, keepdims=True))
        dproj_ref[:, POOL_WIDTH:POOL_WIDTH + SGU_WIDTH] = (du_ref[...] * gelu_grad_u).astype(BF16)
        dproj_ref[:, POOL_WIDTH + SGU_WIDTH:] = (dv * gelu_grad_v).astype(BF16)

        @pl.when(i == nt - 1)
        def _():
            for h in range(HEADS):
                dsw_ref[h] = jnp.where(tri, dsw_ref[h], 0.0)
            lane = lax.broadcasted_iota(jnp.int32, (GROUP, GROUP), 1)
            out = jnp.zeros((GROUP, GROUP), F32)
            for h in range(HEADS):
                sh = jnp.sum(dz_ref[:, h * GROUP:(h + 1) * GROUP], axis=1, keepdims=True)
                out = jnp.where(lane == h, sh, out)
            dsb_ref[...] = out

    outs = pl.pallas_call(
        body, grid=(nt,),
        in_specs=[tile(POOL_WIDTH + 2 * SGU_WIDTH), prev, tile(POOL_WIDTH + SGU_WIDTH), nxt,
                  const3, vec, vec, const3, bias],
        out_specs=[tile(POOL_WIDTH + 2 * SGU_WIDTH), const3, vec, vec, const3,
                   pl.BlockSpec((GROUP, GROUP), lambda i: (0, 0))],
        out_shape=[jax.ShapeDtypeStruct((s, POOL_WIDTH + 2 * SGU_WIDTH), BF16),
                   jax.ShapeDtypeStruct((HEADS, GROUP, GROUP), F32),
                   jax.ShapeDtypeStruct((1, POOL_WIDTH), F32),
                   jax.ShapeDtypeStruct((1, SGU_WIDTH), F32),
                   jax.ShapeDtypeStruct((HEADS, GROUP, GROUP), F32),
                   jax.ShapeDtypeStruct((GROUP, GROUP), F32)],
        scratch_shapes=[pltpu.VMEM((t, SGU_WIDTH), F32), pltpu.VMEM((t, SGU_WIDTH), F32),
                        pltpu.VMEM((GROUP, SGU_WIDTH), F32)],
        name=name, compiler_params=_params("arbitrary"),
    )(proj, proj, dcat, dcat, pool_w, pool_scale, sgu_g, sgu_w, sgu_bias)
    dproj, dpw, dps, dsg, dsw, dsb = outs
    return dproj, dpw, dps, dsg, dsw, dsb[:, :HEADS].T


def _attn_probs(q, k, scale):
    sc = lax.dot_general(q, k, _DOT_DIMS["nt"], preferred_element_type=F32) * scale
    sc = sc - jnp.max(sc, axis=-1, keepdims=True)
    e = jnp.exp(sc)
    return e / jnp.sum(e, axis=-1, keepdims=True)


def _attn_fwd(q, k, v, *, name, t=512):
    s, d = q.shape
    nm = k.shape[0]
    t = _tile(s, t)
    scale = HEAD_DIM ** -0.5

    def body(q_ref, k_ref, v_ref, o_ref):
        for h in range(HEADS):
            cols = slice(h * HEAD_DIM, (h + 1) * HEAD_DIM)
            pr = _attn_probs(q_ref[:, cols], k_ref[:, cols], scale)
            o_ref[:, cols] = jnp.dot(pr.astype(BF16), v_ref[:, cols], preferred_element_type=F32).astype(BF16)

    row = pl.BlockSpec((t, d), lambda i: (i, 0))
    kv = pl.BlockSpec((nm, d), lambda i: (0, 0))
    return pl.pallas_call(
        body, grid=(s // t,), in_specs=[row, kv, kv], out_specs=row,
        out_shape=jax.ShapeDtypeStruct((s, d), BF16), name=name, compiler_params=_params("parallel"),
    )(q, k, v)


def _attn_bwd(q, k, v, do, *, name, t=512):
    s, d = q.shape
    nm = k.shape[0]
    t = _tile(s, t)
    scale = HEAD_DIM ** -0.5

    def body(q_ref, k_ref, v_ref, do_ref, dq_ref, dk_ref, dv_ref):
        i = pl.program_id(0)

        @pl.when(i == 0)
        def _():
            dk_ref[...] = jnp.zeros_like(dk_ref)
            dv_ref[...] = jnp.zeros_like(dv_ref)

        for h in range(HEADS):
            cols = slice(h * HEAD_DIM, (h + 1) * HEAD_DIM)
            qh, kh, vh, doh = q_ref[:, cols], k_ref[:, cols], v_ref[:, cols], do_ref[:, cols]
            pr = _attn_probs(qh, kh, scale)
            dpr = lax.dot_general(doh, vh, _DOT_DIMS["nt"], preferred_element_type=F32)
            ds = (pr * (dpr - jnp.sum(dpr * pr, axis=-1, keepdims=True)) * scale).astype(BF16)
            dv_ref[:, cols] += lax.dot_general(pr.astype(BF16), doh, _DOT_DIMS["tn"], preferred_element_type=F32)
            dk_ref[:, cols] += lax.dot_general(ds, qh, _DOT_DIMS["tn"], preferred_element_type=F32)
            dq_ref[:, cols] = jnp.dot(ds, kh, preferred_element_type=F32).astype(BF16)

    row = pl.BlockSpec((t, d), lambda i: (i, 0))
    kv = pl.BlockSpec((nm, d), lambda i: (0, 0))
    return pl.pallas_call(
        body, grid=(s // t,), in_specs=[row, kv, kv, row], out_specs=[row, kv, kv],
        out_shape=[jax.ShapeDtypeStruct((s, d), BF16), jax.ShapeDtypeStruct((nm, d), F32),
                   jax.ShapeDtypeStruct((nm, d), F32)],
        name=name, compiler_params=_params("arbitrary"),
    )(q, k, v, do)


def _conv_specs(s, f, t, tc, swap):
    hb = t // CONV_HALO
    order = (lambda fn: (lambda j, i: fn(i, j))) if swap else (lambda fn: fn)
    tile3 = pl.BlockSpec((2, t, tc), order(lambda i, j: (0, i, j)))
    prev3 = pl.BlockSpec((2, CONV_HALO, tc), order(lambda i, j: (0, jnp.maximum(i * hb - 1, 0), j)))
    next3 = pl.BlockSpec((2, CONV_HALO, tc),
                         order(lambda i, j: (0, jnp.minimum((i + 1) * hb, s // CONV_HALO - 1), j)))
    tile2 = pl.BlockSpec((t, tc), order(lambda i, j: (i, j)))
    next2 = pl.BlockSpec((CONV_HALO, tc), order(lambda i, j: (jnp.minimum((i + 1) * hb, s // CONV_HALO - 1), j)))
    wspec = pl.BlockSpec((2, 3, tc), order(lambda i, j: (0, 0, j)))
    bspec = pl.BlockSpec((2, 1, tc), order(lambda i, j: (0, 0, j)))
    return tile3, prev3, next3, tile2, next2, wspec, bspec


def _conv3(w_ref, p, x2, x1, x0, b):
    return (w_ref[p, 0:1, :] * x2 + w_ref[p, 1:2, :] * x1 + w_ref[p, 2:3, :] * x0) + b


def _convgate_fwd(hh, cw, cb, *, name, t=256, tc=1408):
    _, s, f = hh.shape
    t = _tile(s, t)
    tile3, prev3, _, tile2, _, wspec, bspec = _conv_specs(s, f, t, tc, swap=False)

    def body(hh_ref, prev_ref, cw_ref, cb_ref, act_ref):
        i = pl.program_id(0)
        hc = []
        for p in range(2):
            xe = jnp.concatenate([jnp.where(i > 0, prev_ref[p], 0.0), hh_ref[p]], axis=0)
            hc.append(_conv3(cw_ref, p, pltpu.roll(xe, 2, 0), pltpu.roll(xe, 1, 0), xe, cb_ref[p])[CONV_HALO:])
        gate, val = hc
        act_ref[...] = ((gate * jax.nn.sigmoid(gate)) * val).astype(BF16)

    return pl.pallas_call(
        body, grid=(s // t, f // tc), in_specs=[tile3, prev3, wspec, bspec], out_specs=tile2,
        out_shape=jax.ShapeDtypeStruct((s, f), BF16), name=name, compiler_params=_params("parallel", "parallel"),
    )(hh, hh, cw, cb)


def _convgate_bwd(hh, dact, cw, cb, *, name, t=128, tc=1408):
    _, s, f = hh.shape
    t = _tile(s, t)
    nt = s // t
    tile3, prev3, next3, tile2, next2, wspec, bspec = _conv_specs(s, f, t, tc, swap=True)

    def body(hh_ref, prev_ref, next_ref, da_ref, danext_ref, cw_ref, cb_ref, dhh_ref, dcw_ref, dcb_ref):
        i = pl.program_id(1)
        is_last = i == nt - 1

        @pl.when(i == 0)
        def _():
            dcw_ref[...] = jnp.zeros_like(dcw_ref)
            dcb_ref[...] = jnp.zeros_like(dcb_ref)

        taps, hc = [], []
        for p in range(2):
            xe = jnp.concatenate([jnp.where(i > 0, prev_ref[p], 0.0), hh_ref[p],
                                  jnp.where(is_last, 0.0, next_ref[p])], axis=0)
            x2, x1 = pltpu.roll(xe, 2, 0), pltpu.roll(xe, 1, 0)
            hc.append(_conv3(cw_ref, p, x2, x1, xe, cb_ref[p])[CONV_HALO:])
            taps.append((x2[CONV_HALO:CONV_HALO + t], x1[CONV_HALO:CONV_HALO + t], xe[CONV_HALO:CONV_HALO + t]))
        gate, val = hc
        dae = jnp.concatenate([da_ref[...], jnp.where(is_last, 0.0, danext_ref[...])], axis=0)
        sg = jax.nn.sigmoid(gate)
        dval = dae * (gate * sg)
        dgate = dae * val * (sg * (1.0 + gate * (1.0 - sg)))
        m = t + CONV_HALO
        for p, dhc in enumerate((dgate, dval)):
            dh = (cw_ref[p, 2:3, :] * dhc + cw_ref[p, 1:2, :] * pltpu.roll(dhc, m - 1, 0)
                  + cw_ref[p, 0:1, :] * pltpu.roll(dhc, m - 2, 0))
            dhh_ref[p] = dh[:t].astype(BF16)
            d0 = dhc[:t]
            for kk, tap in enumerate(taps[p]):
                dcw_ref[p, kk:kk + 1, :] += jnp.sum(d0 * tap, axis=0, keepdims=True)
            dcb_ref[p] += jnp.sum(d0, axis=0, keepdims=True)

    return pl.pallas_call(
        body, grid=(f // tc, nt), in_specs=[tile3, prev3, next3, tile2, next2, wspec, bspec],
        out_specs=[tile3, wspec, bspec],
        out_shape=[jax.ShapeDtypeStruct((2, s, f), BF16), jax.ShapeDtypeStruct((2, 3, f), F32),
                   jax.ShapeDtypeStruct((2, 1, f), F32)],
        name=name, compiler_params=_params("parallel", "arbitrary"),
    )(hh, hh, hh, dact, dact, cw, cb)


def _position():
    return lax.axis_index("x"), lax.axis_index("y"), lax.axis_index("c")


def _linear(px, py, pc):
    return 4 * px + 2 * py + pc


def _peers_of(x, y, c):
    peers = []
    for mask in range(1, N_DEV):
        peers.append((1 - x if mask & 4 else x, 1 - y if mask & 2 else y, 1 - c if mask & 1 else c))
    return peers


def _exchange_copy(src_ref, land_ref, send_sem, recv_sem, peer, mine, scatter, arriving):
    src = src_ref.at[_linear(*peer)] if scatter else src_ref
    dst = land_ref.at[_linear(*peer) if arriving else mine]
    return pltpu.make_async_remote_copy(src_ref=src, dst_ref=dst, send_sem=send_sem, recv_sem=recv_sem,
                                        device_id=peer, device_id_type=MESH)


_EXCHANGE_COLLECTIVE_ID = 7


def _sequencer_exchange(srcs, *, scatter, name):
    n = len(srcs)
    src_refs = [jax.new_ref(a, memory_space=pltpu.MemorySpace.HBM) for a in srcs]
    land_refs = [jax.empty_ref(jax.ShapeDtypeStruct(a.shape if scatter else (N_DEV,) + a.shape, a.dtype),
                               memory_space=pltpu.MemorySpace.HBM) for a in srcs]

    @pl.kernel(mesh=plsc.ScalarSubcoreMesh(axis_name="sequencer", num_cores=1), name=name,
               scratch_types=(pltpu.SemaphoreType.DMA((7 * n,)), pltpu.SemaphoreType.DMA((7 * n,)),
                              pltpu.SemaphoreType.DMA((n,))),
               compiler_params=pltpu.CompilerParams(collective_id=_EXCHANGE_COLLECTIVE_ID))
    def launch(send_sems, recv_sems, local_sems):
        x, y, c = _position()
        mine = _linear(x, y, c)
        peers = _peers_of(x, y, c)
        barrier = pltpu.get_barrier_semaphore()
        for peer in peers:
            pl.semaphore_signal(barrier, inc=1, device_id=peer, device_id_type=MESH)
        pl.semaphore_wait(barrier, N_DEV - 1)
        local = [pltpu.make_async_copy(src_refs[t].at[mine] if scatter else src_refs[t], land_refs[t].at[mine],
                                       local_sems.at[t]) for t in range(n)]
        for cp in local:
            cp.start()
        if scatter:
            sends = []
            for t in range(n):
                for k, peer in enumerate(peers):
                    cp = _exchange_copy(src_refs[t], land_refs[t], send_sems.at[7 * t + k], recv_sems.at[7 * t + k],
                                        peer, mine, scatter, arriving=False)
                    cp.start()
                    sends.append(cp)
            for t in range(n):
                for k, peer in enumerate(peers):
                    _exchange_copy(src_refs[t], land_refs[t], send_sems.at[7 * t + k], recv_sems.at[7 * t + k],
                                   peer, mine, scatter, arriving=True).wait_recv()
        else:
            me, sibling = (x, y, c), (x, y, 1 - c)
            chips = [(1 - x, y), (x, 1 - y), (1 - x, 1 - y)]

            def copy(t, k, block, to, src=None):
                dst = land_refs[t].at[_linear(*block)]
                return pltpu.make_async_remote_copy(
                    src_ref=dst if src is None else src, dst_ref=dst, send_sem=send_sems.at[7 * t + k],
                    recv_sem=recv_sems.at[7 * t + k], device_id=to, device_id_type=MESH)

            sends = []
            for t in range(n):
                sends.append(copy(t, 0, me, sibling, src=src_refs[t]))
                for j, chip in enumerate(chips):
                    sends.append(copy(t, 1 + j, me, (*chip, c), src=src_refs[t]))
            for cp in sends:
                cp.start()
            for j, chip in enumerate(chips):
                for t in range(n):
                    copy(t, 1 + j, (*chip, c), me).wait_recv()
                    passed = copy(t, 4 + j, (*chip, c), sibling)
                    passed.start()
                    sends.append(passed)
            for t in range(n):
                copy(t, 0, sibling, me).wait_recv()
                for j, chip in enumerate(chips):
                    copy(t, 4 + j, (*chip, 1 - c), me).wait_recv()
        for cp in local:
            cp.wait()
        for cp in sends:
            cp.wait_send()

    launch()
    return [r[...] for r in land_refs]


def _adamw_math(g, w, m, v):
    m2 = ADAM_B1 * m + (1.0 - ADAM_B1) * g
    v2 = ADAM_B2 * v + (1.0 - ADAM_B2) * (g * g)
    m_hat = m2 / (1.0 - ADAM_B1 ** ADAM_STEP)
    v_hat = v2 / (1.0 - ADAM_B2 ** ADAM_STEP)
    delta = -ADAM_LR * (m_hat / (jnp.sqrt(v_hat) + ADAM_EPS) + ADAM_WD * w)
    return delta, m2, v2


def _adamw(slots, w, m, v, *, name, tr=256):
    depth = len(slots)
    _, r, c = slots[0].shape
    tr = next((cand for cand in range(min(r, tr), 15, -1) if r % cand == 0 and cand % 16 == 0), r)

    def body(*refs):
        s_refs = refs[:depth]
        w_ref, m_ref, v_ref, g_ref, d_ref, m2_ref, v2_ref = refs[depth:]
        layer = pl.program_id(0)
        for l in range(depth):
            @pl.when(layer == l)
            def _():
                g = s_refs[l][0].astype(F32)
                for d in range(1, N_DEV):
                    g = g + s_refs[l][d].astype(F32)
                delta, m2, v2 = _adamw_math(g, w_ref[...], m_ref[...], v_ref[...])
                g_ref[...] = g
                d_ref[...] = delta
                m2_ref[...] = m2
                v2_ref[...] = v2

    blk = pl.BlockSpec((None, tr, c), lambda layer, i: (layer, i, 0))
    sblks = [pl.BlockSpec((N_DEV, tr, c), lambda layer, i, l=l: (0, jnp.where(layer == l, i, 0), 0))
             for l in range(depth)]
    shape = jax.ShapeDtypeStruct((depth, r, c), F32)
    return pl.pallas_call(
        body, grid=(depth, r // tr), in_specs=sblks + [blk, blk, blk], out_specs=[blk] * 4,
        out_shape=[shape] * 4, name=name, compiler_params=_params("arbitrary", "arbitrary"),
    )(*slots, w, m, v)


_SHARDED = ("w_in", "w_out", "wq", "wk", "wv", "wo", "w_up", "conv_w", "w_down")
_SMALL = ("norm_mix_g", "pool_w", "pool_scale", "sgu_g", "sgu_w", "sgu_b", "norm_xattn_g", "mem_norm_g",
          "norm_ffn_g", "conv_b", "final_norm_g")
_WEIGHTS = ("norm_mix_g", "w_in", "pool_w", "pool_scale", "sgu_g", "sgu_w", "sgu_b", "w_out", "norm_xattn_g",
            "mem_norm_g", "wq", "wk", "wv", "wo", "norm_ffn_g", "w_up", "conv_w", "conv_b", "w_down",
            "final_norm_g")
_PACK_LANES = 128
_GATHER_GROUPS = (("w_in",), ("w_out",), ("wq", "wk", "wv", "wo"), ("w_up", "conv_w", "w_down"))


def _cols_to_blocks(a, *, name, tr=256):
    r, c8 = a.shape
    c = c8 // N_DEV
    tr = _tile(r, tr)

    def body(a_ref, o_ref):
        for dev in range(N_DEV):
            o_ref[dev] = a_ref[:, dev * c:(dev + 1) * c]

    return pl.pallas_call(
        body, grid=(r // tr,), in_specs=[pl.BlockSpec((tr, c8), lambda i: (i, 0))],
        out_specs=pl.BlockSpec((N_DEV, tr, c), lambda i: (0, i, 0)),
        out_shape=jax.ShapeDtypeStruct((N_DEV, r, c), a.dtype), name=name, compiler_params=_params("parallel"),
    )(a)


def _blocks_to_cols(a, *, name, tr=256):
    n, r, c = a.shape
    tr = _tile(r, tr)

    def body(a_ref, o_ref):
        for dev in range(n):
            o_ref[:, dev * c:(dev + 1) * c] = a_ref[dev]

    return pl.pallas_call(
        body, grid=(r // tr,), in_specs=[pl.BlockSpec((n, tr, c), lambda i: (0, i, 0))],
        out_specs=pl.BlockSpec((tr, n * c), lambda i: (i, 0)),
        out_shape=jax.ShapeDtypeStruct((r, n * c), a.dtype), name=name, compiler_params=_params("parallel"),
    )(a)


def _pin(x, *deps):
    return lax.optimization_barrier((x, *deps))[0]


def _pack(arrays):
    flat = jnp.concatenate([a.reshape(-1) for a in arrays])
    assert flat.shape[0] % (8 * _PACK_LANES) == 0
    return flat.reshape(-1, _PACK_LANES)


def _unpack(packed, like):
    flat = packed.reshape(-1)
    out, off = [], 0
    for a in like:
        out.append(flat[off:off + a.size].reshape(a.shape))
        off += a.size
    return out


def kernel(x, mem, norm_mix_g, w_in, pool_w, pool_scale, sgu_g, sgu_w, sgu_b, w_out, norm_xattn_g, mem_norm_g, wq, wk, wv, wo, norm_ffn_g, w_up, conv_w, conv_b, w_down, final_norm_g, loss_target, m_norm_mix_g, m_w_in, m_pool_w, m_pool_scale, m_sgu_g, m_sgu_w, m_sgu_b, m_w_out, m_norm_xattn_g, m_mem_norm_g, m_wq, m_wk, m_wv, m_wo, m_norm_ffn_g, m_w_up, m_conv_w, m_conv_b, m_w_down, m_final_norm_g, v_norm_mix_g, v_w_in, v_pool_w, v_pool_scale, v_sgu_g, v_sgu_w, v_sgu_b, v_w_out, v_norm_xattn_g, v_mem_norm_g, v_wq, v_wk, v_wv, v_wo, v_norm_ffn_g, v_w_up, v_conv_w, v_conv_b, v_w_down, v_final_norm_g):
    W = dict(norm_mix_g=norm_mix_g, w_in=w_in, pool_w=pool_w, pool_scale=pool_scale, sgu_g=sgu_g, sgu_w=sgu_w,
             sgu_b=sgu_b, w_out=w_out, norm_xattn_g=norm_xattn_g, mem_norm_g=mem_norm_g, wq=wq, wk=wk, wv=wv, wo=wo,
             norm_ffn_g=norm_ffn_g, w_up=w_up, conv_w=conv_w, conv_b=conv_b, w_down=w_down,
             final_norm_g=final_norm_g)
    M = dict(norm_mix_g=m_norm_mix_g, w_in=m_w_in, pool_w=m_pool_w, pool_scale=m_pool_scale, sgu_g=m_sgu_g,
             sgu_w=m_sgu_w, sgu_b=m_sgu_b, w_out=m_w_out, norm_xattn_g=m_norm_xattn_g, mem_norm_g=m_mem_norm_g,
             wq=m_wq, wk=m_wk, wv=m_wv, wo=m_wo, norm_ffn_g=m_norm_ffn_g, w_up=m_w_up, conv_w=m_conv_w,
             conv_b=m_conv_b, w_down=m_w_down, final_norm_g=m_final_norm_g)
    V = dict(norm_mix_g=v_norm_mix_g, w_in=v_w_in, pool_w=v_pool_w, pool_scale=v_pool_scale, sgu_g=v_sgu_g,
             sgu_w=v_sgu_w, sgu_b=v_sgu_b, w_out=v_w_out, norm_xattn_g=v_norm_xattn_g, mem_norm_g=v_mem_norm_g,
             wq=v_wq, wk=v_wk, wv=v_wv, wo=v_wo, norm_ffn_g=v_norm_ffn_g, w_up=v_w_up, conv_w=v_conv_w,
             conv_b=v_conv_b, w_down=v_w_down, final_norm_g=v_final_norm_g)

    s, d = x.shape[1], x.shape[2]
    f = w_down.shape[1] * N_DEV
    h = x.reshape(s, d)
    memx = mem.reshape(mem.shape[1], d)
    target = loss_target.reshape(s, d)

    gathered = {}

    def launch_gather(l, gi, after):
        if l >= DEPTH:
            return
        names = _GATHER_GROUPS[gi]
        shards = [W[nme][l] if nme == "conv_w" else W[nme][l].astype(BF16) for nme in names]
        if after is not None:
            shards[0], _ = lax.optimization_barrier((shards[0], after))
        gathered[l, gi] = dict(zip(names, _sequencer_exchange(shards, scatter=False, name=f"gather_{l}_{gi}")))

    launch_gather(0, 0, None)

    saved, full = [], []
    for l in range(DEPTH):
        sgu_bias = jnp.repeat(sgu_b[l].T, GROUP, axis=1)
        cb = conv_b[l].reshape(2, 1, f)
        if l == 0:
            xn1 = _rms_fwd(h, norm_mix_g[l], name=f"norm_mix_{l}")
            launch_gather(0, 1, xn1)
        w_in_f = _blocks_to_cols(gathered[l, 0]["w_in"], name=f"w_in_cols_{l}")
        proj = _mm_nn(xn1, w_in_f, out_dtype=F32, name=f"proj_in_{l}")
        if l == 0:
            launch_gather(0, 2, proj)
        cat = _mixer_fwd(proj, pool_w[l], pool_scale[l].reshape(1, -1), sgu_g[l].reshape(1, -1), sgu_w[l], sgu_bias,
                         name=f"mixer_{l}")
        if l == 0:
            launch_gather(0, 3, cat)
        w_out_f = gathered[l, 1]["w_out"].reshape(-1, d)
        h1, xn2 = _mm_nn(cat, w_out_f, out_dtype=F32, res=h, norm_gain=norm_xattn_g[l], name=f"proj_out_{l}")
        launch_gather(l + 1, 0, h1)
        g = gathered[l, 2]
        wq_f, wk_f, wv_f, wo_f = (g[nme].reshape(-1, d) for nme in ("wq", "wk", "wv", "wo"))
        q = _mm_nn(xn2, wq_f, out_dtype=BF16, name=f"q_{l}")
        launch_gather(l + 1, 1, q)
        memn = _rms_fwd(memx, mem_norm_g[l], name=f"norm_mem_{l}")
        k = _mm_nn(memn, wk_f, out_dtype=BF16, name=f"k_{l}")
        v = _mm_nn(memn, wv_f, out_dtype=BF16, name=f"v_{l}")
        o = _attn_fwd(q, k, v, name=f"attn_{l}")
        h2, xn3 = _mm_nn(o, wo_f, out_dtype=F32, res=h1, norm_gain=norm_ffn_g[l], name=f"attn_out_{l}")
        launch_gather(l + 1, 2, h2)
        g = gathered[l, 3]
        w_up_f = _blocks_to_cols(g["w_up"], name=f"w_up_cols_{l}")
        conv_w_f = _blocks_to_cols(g["conv_w"], name=f"conv_w_cols_{l}").reshape(3, 2, f).transpose(1, 0, 2)
        w_down_f = g["w_down"].reshape(-1, d)
        hh = _mm_up(xn3, w_up_f, name=f"ffn_up_{l}")
        launch_gather(l + 1, 3, hh)
        act = _convgate_fwd(hh, conv_w_f, cb, name=f"convgate_{l}")
        if l + 1 < DEPTH:
            h3, xn1_next = _mm_nn(act, w_down_f, out_dtype=F32, res=h2, tm=512, norm_gain=norm_mix_g[l + 1],
                                  name=f"ffn_down_{l}")
        else:
            h3, xn1_next = _mm_nn(act, w_down_f, out_dtype=F32, res=h2, tm=512, name=f"ffn_down_{l}"), None
        full.append(dict(w_in=w_in_f, w_out=w_out_f, wq=wq_f, wk=wk_f, wv=wv_f, wo=wo_f, w_up=w_up_f,
                         conv_w=conv_w_f, w_down=w_down_f))
        saved.append(dict(h0=h, xn1=xn1, proj=proj, cat=cat, h1=h1, xn2=xn2, q=q, memn=memn, k=k, v=v, o=o, h2=h2,
                          xn3=xn3, hh=hh, act=act, sgu_bias=sgu_bias, cb=cb))
        h, xn1 = h3, xn1_next

    dh, dhb, dg_final, loss_row = _loss_head(h, final_norm_g, target, name="loss_head")

    slots = {nme: [None] * DEPTH for nme in _SHARDED}
    small = [None] * DEPTH

    previous = []

    def scatter(l, tag, names, parts):
        parts = [_pin(parts[0], *previous)] + parts[1:]
        arrived = _sequencer_exchange(parts, scatter=True, name=f"scatter_{tag}_{l}")
        previous[:] = arrived[:1]
        for nme, land in zip(names, arrived):
            slots[nme][l] = land
        return parts

    for l in reversed(range(DEPTH)):
        fw, sv = full[l], saved[l]
        dact = _mm_nt(dhb, fw["w_down"], out_dtype=F32, tm=512, name=f"d_act_{l}")
        g_w_down = _mm_tn(sv["act"], dhb, tm=f // 2, name=f"g_w_down_{l}")
        dhh, g_conv_w, g_conv_b = _convgate_bwd(sv["hh"], dact, fw["conv_w"], sv["cb"], name=f"d_convgate_{l}")
        g_w_up = _mm_up_tn(sv["xn3"], dhh, name=f"g_w_up_{l}")
        g_conv_w_cols = g_conv_w.transpose(1, 0, 2).reshape(3, 2 * f)
        parts = [_cols_to_blocks(g_w_up, name=f"g_w_up_blocks_{l}"),
                 _cols_to_blocks(g_conv_w_cols, name=f"g_conv_w_blocks_{l}"), g_w_down.reshape(N_DEV, -1, d)]
        parts = scatter(l, "ffn", ("w_up", "conv_w", "w_down"), parts)
        dh2, dh2b, g_norm_ffn = _mm_up_nt(_pin(dhh, *parts), fw["w_up"], name=f"d_norm_ffn_{l}",
                                          norm_bwd=(sv["h2"], norm_ffn_g[l], dh))

        do = _mm_nt(dh2b, fw["wo"], out_dtype=BF16, name=f"d_o_{l}")
        g_wo = _mm_tn(sv["o"], dh2b, name=f"g_wo_{l}")
        dq, dk, dv = _attn_bwd(sv["q"], sv["k"], sv["v"], do, name=f"d_attn_{l}")
        dkb, dvb = dk.astype(BF16), dv.astype(BF16)
        g_wq = _mm_tn(sv["xn2"], dq, name=f"g_wq_{l}")
        g_wk = _mm_tn(sv["memn"], dkb, name=f"g_wk_{l}")
        g_wv = _mm_tn(sv["memn"], dvb, name=f"g_wv_{l}")
        parts = [g.reshape(N_DEV, -1, d) for g in (g_wq, g_wk, g_wv, g_wo)]
        parts = scatter(l, "attn", ("wq", "wk", "wv", "wo"), parts)
        dq = _pin(dq, *parts)
        dmemn = _mm_nt(dkb, fw["wk"], out_dtype=F32, name=f"d_memn_k_{l}")
        dmemn = _mm_nt(dvb, fw["wv"], out_dtype=F32, res=dmemn, name=f"d_memn_v_{l}")
        _, _, g_mem_norm = _rms_bwd(memx, dmemn, mem_norm_g[l], None, name=f"d_norm_mem_{l}")
        dh1, dh1b, g_norm_xattn = _mm_nt(dq, fw["wq"], name=f"d_norm_xattn_{l}",
                                         norm_bwd=(sv["h1"], norm_xattn_g[l], dh2))

        dcat = _mm_nt(dh1b, fw["w_out"], out_dtype=F32, name=f"d_cat_{l}")
        g_w_out = _mm_tn(sv["cat"], dh1b, name=f"g_w_out_{l}")
        dproj, g_pool_w, g_pool_scale, g_sgu_g, g_sgu_w, g_sgu_b = _mixer_bwd(
            sv["proj"], dcat, pool_w[l], pool_scale[l].reshape(1, -1), sgu_g[l].reshape(1, -1), sgu_w[l],
            sv["sgu_bias"], name=f"d_mixer_{l}")
        g_w_in = _mm_tn(sv["xn1"], dproj, name=f"g_w_in_{l}")
        parts = [_cols_to_blocks(g_w_in, name=f"g_w_in_blocks_{l}"), g_w_out.reshape(N_DEV, -1, d)]
        parts = scatter(l, "mix", ("w_in", "w_out"), parts)
        dh, dhb, g_norm_mix = _mm_nt(_pin(dproj, *parts), fw["w_in"], name=f"d_norm_mix_{l}",
                                     norm_bwd=(sv["h0"], norm_mix_g[l], dh1))

        small[l] = dict(norm_mix_g=g_norm_mix.reshape(-1), pool_w=g_pool_w, pool_scale=g_pool_scale.reshape(-1),
                        sgu_g=g_sgu_g.reshape(-1), sgu_w=g_sgu_w, sgu_b=g_sgu_b, norm_xattn_g=g_norm_xattn.reshape(-1),
                        mem_norm_g=g_mem_norm.reshape(-1), norm_ffn_g=g_norm_ffn.reshape(-1),
                        conv_b=g_conv_b.reshape(-1))
    grad_x = dh.reshape(x.shape)

    out = {}
    for nme in _SHARDED:
        w3 = W[nme].reshape(DEPTH, -1, W[nme].shape[-1])
        res = _adamw([sl.reshape((N_DEV,) + w3.shape[1:]) for sl in slots[nme]], w3, M[nme].reshape(w3.shape),
                     V[nme].reshape(w3.shape), name=f"adamw_{nme}")
        out[nme] = [r.reshape(W[nme].shape) for r in res]

    small_names = [n for n in _SMALL]
    contrib = []
    for nme in small_names:
        if nme == "final_norm_g":
            contrib.append(dg_final.reshape(-1))
        else:
            contrib.append(jnp.stack([small[l][nme] for l in range(DEPTH)]))
    tail = 8 * _PACK_LANES
    packed_g = _pack(contrib + [jnp.pad(loss_row[0, :1], (0, tail - 1))])
    (all_g,) = _sequencer_exchange([_pin(packed_g, *previous)], scatter=False, name="gather_small_grads")
    rows = packed_g.shape[0]
    loss = jnp.sum(all_g[:, rows - 8, 0])
    zeros = jnp.zeros((tail,), F32)
    res = _adamw([all_g], _pack([W[n] for n in small_names] + [zeros]).reshape(1, rows, -1),
                 _pack([M[n] for n in small_names] + [zeros]).reshape(1, rows, -1),
                 _pack([V[n] for n in small_names] + [zeros]).reshape(1, rows, -1), name="adamw_small", tr=rows // 2)
    unpacked = [_unpack(r, [W[n] for n in small_names]) for r in res]
    for i, nme in enumerate(small_names):
        out[nme] = [unpacked[j][i] for j in range(4)]

    grads = [out[n][0] for n in _WEIGHTS]
    deltas = [out[n][1] for n in _WEIGHTS]
    new_m = [out[n][2] for n in _WEIGHTS]
    new_v = [out[n][3] for n in _WEIGHTS]
    return (loss, grad_x, *grads, *deltas, *new_m, *new_v)
```

```python
import jax
import jax.numpy as jnp
from jax import lax
from jax.experimental import pallas as pl
from jax.experimental.pallas import tpu as pltpu
from jax.experimental.pallas import tpu_sc as plsc

F32 = jnp.float32
BF16 = jnp.bfloat16
MESH = pl.DeviceIdType.MESH

EPS = 1e-6
N_DEV = 8
DEPTH = 2
POOL_WINDOWS = (2, 4, 8, 16)
GROUP = 128
POOL_WIDTH = 512
SGU_WIDTH = 512
HEADS = 4
HEAD_DIM = 256
POOL_HALO = 16
CONV_HALO = 8

ADAM_LR = 0.001
ADAM_B1 = 0.9
ADAM_B2 = 0.999
ADAM_EPS = 1e-08
ADAM_WD = 0.01
ADAM_STEP = 10

VMEM_LIMIT_BYTES = 52 * 1024 * 1024


def _params(*semantics):
    return pltpu.CompilerParams(dimension_semantics=semantics, vmem_limit_bytes=VMEM_LIMIT_BYTES)


def _tile(n, want):
    t = min(n, want)
    assert n % t == 0, (n, want)
    return t


_DOT_DIMS = {
    "nn": (((1,), (0,)), ((), ())),
    "nt": (((1,), (1,)), ((), ())),
    "tn": (((0,), (0,)), ((), ())),
}


def _mm(a, b, *, dims, grid, a_spec, b_spec, o_spec, out_shape, out_dtype, acc_shape, name, res=None, res_spec=None,
        norm_bwd=None, norm_out=None):
    nk = grid[2]
    dn = _DOT_DIMS[dims]
    extras, extra_specs = [], []
    if res is not None:
        extras, extra_specs = [res], [res_spec]
    if norm_bwd is not None:
        h, gain, dres, row_spec, gain_spec = norm_bwd
        extras = [h, gain] + ([dres] if dres is not None else [])
        extra_specs = [row_spec, gain_spec] + ([row_spec] if dres is not None else [])
        out_specs = [row_spec, row_spec, gain_spec]
        out_shapes = [jax.ShapeDtypeStruct(h.shape, F32), jax.ShapeDtypeStruct(h.shape, BF16),
                      jax.ShapeDtypeStruct(gain.shape, F32)]
    elif norm_out is not None:
        extras, extra_specs = extras + [norm_out[0]], extra_specs + [norm_out[1]]
        out_specs = [o_spec, o_spec]
        out_shapes = [jax.ShapeDtypeStruct(out_shape, out_dtype), jax.ShapeDtypeStruct(out_shape, BF16)]
    else:
        out_specs, out_shapes = o_spec, jax.ShapeDtypeStruct(out_shape, out_dtype)
    n_extra = len(extras)

    def body(*refs):
        a_ref, b_ref = refs[:2]
        extra_refs = refs[2:2 + n_extra]
        out_refs = refs[2 + n_extra:len(refs) - (1 if nk > 1 else 0)]
        p = lax.dot_general(a_ref[...], b_ref[...], dn, preferred_element_type=F32)

        def finish(r):
            if norm_bwd is not None:
                _rms_bwd_math(r, extra_refs[0], extra_refs[1], extra_refs[2] if n_extra == 3 else None,
                              *out_refs, first=pl.program_id(0) == 0)
                return
            if res is not None:
                r = r + extra_refs[0][...]
            out_refs[0][...] = r.astype(out_refs[0].dtype)
            if norm_out is not None:
                scale = lax.rsqrt(jnp.mean(r * r, axis=-1, keepdims=True) + EPS)
                out_refs[1][...] = ((r * scale) * extra_refs[-1][...]).astype(BF16)

        if nk == 1:
            finish(p)
        else:
            acc_ref = refs[-1]
            k = pl.program_id(2)

            @pl.when(k == 0)
            def _():
                acc_ref[...] = p

            @pl.when(k > 0)
            def _():
                acc_ref[...] += p

            @pl.when(k == nk - 1)
            def _():
                finish(acc_ref[...])

    scratch = [pltpu.VMEM(acc_shape, F32)] if nk > 1 else []
    return pl.pallas_call(
        body, grid=grid, in_specs=[a_spec, b_spec] + extra_specs, out_specs=out_specs,
        out_shape=out_shapes, scratch_shapes=scratch, name=name,
        compiler_params=_params("arbitrary" if norm_bwd is not None else "parallel", "parallel", "arbitrary"),
    )(a, b, *extras)


def _rms_bwd_math(dy, h_ref, g_ref, dres_ref, dh_ref, dhb_ref, dg_ref, *, first):
    x = h_ref[...]
    r = lax.rsqrt(jnp.mean(x * x, axis=-1, keepdims=True) + EPS)
    a = dy * g_ref[...]
    m = jnp.mean(a * x, axis=-1, keepdims=True)
    dh = r * a - x * (r * r * r * m)
    if dres_ref is not None:
        dh = dh + dres_ref[...]
    dh_ref[...] = dh
    dhb_ref[...] = dh.astype(BF16)
    part = jnp.sum(dy * (x * r), axis=0, keepdims=True)

    @pl.when(first)
    def _():
        dg_ref[...] = part

    @pl.when(jnp.logical_not(first))
    def _():
        dg_ref[...] += part


def _mm_nn(a, b, *, out_dtype, name, res=None, tm=1024, norm_gain=None):
    m, k = a.shape
    n = b.shape[1]
    tm = _tile(m, tm)
    spec_o = pl.BlockSpec((tm, n), lambda i, j, kk: (i, 0))
    norm_out = None if norm_gain is None else (norm_gain.reshape(1, n), pl.BlockSpec((1, n), lambda i, j, kk: (0, 0)))
    return _mm(a, b, dims="nn", grid=(m // tm, 1, 1),
               a_spec=pl.BlockSpec((tm, k), lambda i, j, kk: (i, 0)),
               b_spec=pl.BlockSpec((k, n), lambda i, j, kk: (0, 0)),
               o_spec=spec_o, out_shape=(m, n), out_dtype=out_dtype, acc_shape=None, name=name,
               res=res, res_spec=spec_o if res is not None else None, norm_out=norm_out)


def _norm_bwd_arg(h, gain, dres, tm):
    d = h.shape[1]
    return (h, gain.reshape(1, d), dres, pl.BlockSpec((tm, d), lambda i, j, kk: (i, 0)),
            pl.BlockSpec((1, d), lambda i, j, kk: (0, 0)))


def _mm_nt(a, b, *, out_dtype=F32, name, res=None, tm=1024, norm_bwd=None):
    m, k = a.shape
    n = b.shape[0]
    tm = _tile(m, tm)
    spec_o = pl.BlockSpec((tm, n), lambda i, j, kk: (i, 0))
    return _mm(a, b, dims="nt", grid=(m // tm, 1, 1),
               a_spec=pl.BlockSpec((tm, k), lambda i, j, kk: (i, 0)),
               b_spec=pl.BlockSpec((n, k), lambda i, j, kk: (0, 0)),
               o_spec=spec_o, out_shape=(m, n), out_dtype=out_dtype, acc_shape=None, name=name,
               res=res, res_spec=spec_o if res is not None else None,
               norm_bwd=None if norm_bwd is None else _norm_bwd_arg(*norm_bwd, tm))


_TN_ROWS = 2048


def _mm_tn(a, b, *, name, tm=None, tn=None, ts=_TN_ROWS, out_dtype=BF16):
    s, m = a.shape
    n = b.shape[1]
    tm = m if tm is None else tm
    tn = n if tn is None else tn
    ts = _tile(s, ts)
    return _mm(a, b, dims="tn", grid=(m // tm, n // tn, s // ts),
               a_spec=pl.BlockSpec((ts, tm), lambda i, j, kk: (kk, i)),
               b_spec=pl.BlockSpec((ts, tn), lambda i, j, kk: (kk, j)),
               o_spec=pl.BlockSpec((tm, tn), lambda i, j, kk: (i, j)),
               out_shape=(m, n), out_dtype=out_dtype, acc_shape=(tm, tn), name=name)


def _mm_up(xn, w_up, *, name, tm=512):
    s, d = xn.shape
    f = w_up.shape[1] // 2
    tm = _tile(s, tm)
    return _mm(xn, w_up, dims="nn", grid=(2, s // tm, 1),
               a_spec=pl.BlockSpec((tm, d), lambda j, i, kk: (i, 0)),
               b_spec=pl.BlockSpec((d, f), lambda j, i, kk: (0, j)),
               o_spec=pl.BlockSpec((None, tm, f), lambda j, i, kk: (j, i, 0)),
               out_shape=(2, s, f), out_dtype=F32, acc_shape=None, name=name)


def _mm_up_nt(dhh, w_up, *, name, tm=512, norm_bwd=None):
    _, s, f = dhh.shape
    d = w_up.shape[0]
    tm = _tile(s, tm)
    return _mm(dhh, w_up, dims="nt", grid=(s // tm, 1, 2),
               a_spec=pl.BlockSpec((None, tm, f), lambda i, j, kk: (kk, i, 0)),
               b_spec=pl.BlockSpec((d, f), lambda i, j, kk: (0, kk)),
               o_spec=pl.BlockSpec((tm, d), lambda i, j, kk: (i, 0)),
               out_shape=(s, d), out_dtype=F32, acc_shape=(tm, d), name=name,
               norm_bwd=None if norm_bwd is None else _norm_bwd_arg(*norm_bwd, tm))


def _mm_up_tn(xn, dhh, *, name, ts=_TN_ROWS):
    s, d = xn.shape
    f = dhh.shape[2]
    tn = f // 2
    ts = _tile(s, ts)
    return _mm(xn, dhh, dims="tn", grid=(1, 4, s // ts),
               a_spec=pl.BlockSpec((ts, d), lambda i, j, kk: (kk, 0)),
               b_spec=pl.BlockSpec((None, ts, tn), lambda i, j, kk: (j // 2, kk, j % 2)),
               o_spec=pl.BlockSpec((d, tn), lambda i, j, kk: (0, j)),
               out_shape=(d, 2 * f), out_dtype=BF16, acc_shape=(d, tn), name=name)


def _rms_fwd(h, g, *, name, tr=512):
    s, d = h.shape
    tr = _tile(s, tr)

    def body(h_ref, g_ref, o_ref):
        x = h_ref[...]
        r = lax.rsqrt(jnp.mean(x * x, axis=-1, keepdims=True) + EPS)
        o_ref[...] = ((x * r) * g_ref[...]).astype(o_ref.dtype)

    row = pl.BlockSpec((tr, d), lambda i: (i, 0))
    return pl.pallas_call(
        body, grid=(s // tr,), in_specs=[row, pl.BlockSpec((1, d), lambda i: (0, 0))], out_specs=row,
        out_shape=jax.ShapeDtypeStruct((s, d), BF16), name=name, compiler_params=_params("parallel"),
    )(h, g.reshape(1, d))


def _rms_bwd(h, dxn, g, dres, *, name, tr=512):
    s, d = h.shape
    tr = _tile(s, tr)
    has_res = dres is not None

    def body(*refs):
        if has_res:
            h_ref, dxn_ref, g_ref, dres_ref, dh_ref, dhb_ref, dg_ref = refs
        else:
            h_ref, dxn_ref, g_ref, dh_ref, dhb_ref, dg_ref = refs
            dres_ref = None
        _rms_bwd_math(dxn_ref[...].astype(F32), h_ref, g_ref, dres_ref, dh_ref, dhb_ref, dg_ref,
                      first=pl.program_id(0) == 0)

    row = pl.BlockSpec((tr, d), lambda i: (i, 0))
    vec = pl.BlockSpec((1, d), lambda i: (0, 0))
    in_specs = [row, row, vec] + ([row] if has_res else [])
    args = (h, dxn, g.reshape(1, d)) + ((dres,) if has_res else ())
    return pl.pallas_call(
        body, grid=(s // tr,), in_specs=in_specs, out_specs=[row, row, vec],
        out_shape=[jax.ShapeDtypeStruct((s, d), F32), jax.ShapeDtypeStruct((s, d), BF16),
                   jax.ShapeDtypeStruct((1, d), F32)],
        name=name, compiler_params=_params("arbitrary"),
    )(*args)


def _loss_head(h, g, target, *, name, tr=512):
    s, d = h.shape
    tr = _tile(s, tr)
    nt = s // tr

    def body(h_ref, g_ref, t_ref, dh_ref, dhb_ref, dg_ref, loss_ref, sq_ref):
        i = pl.program_id(0)
        x = h_ref[...]
        gain = g_ref[...]
        r = lax.rsqrt(jnp.mean(x * x, axis=-1, keepdims=True) + EPS)
        xh = x * r
        err = xh * gain - t_ref[...]
        dy = err * (1.0 / d)
        a = dy * gain
        m = jnp.mean(a * x, axis=-1, keepdims=True)
        dh = r * a - x * (r * r * r * m)
        dh_ref[...] = dh
        dhb_ref[...] = dh.astype(BF16)
        dg_part = jnp.sum(dy * xh, axis=0, keepdims=True)
        sq_part = jnp.sum(err * err, axis=0, keepdims=True)

        @pl.when(i == 0)
        def _():
            dg_ref[...] = dg_part
            sq_ref[...] = sq_part

        @pl.when(i > 0)
        def _():
            dg_ref[...] += dg_part
            sq_ref[...] += sq_part

        @pl.when(i == nt - 1)
        def _():
            total = jnp.sum(sq_ref[...], axis=1, keepdims=True) * (0.5 / d)
            loss_ref[...] = jnp.broadcast_to(total, loss_ref.shape)

    row = pl.BlockSpec((tr, d), lambda i: (i, 0))
    vec = pl.BlockSpec((1, d), lambda i: (0, 0))
    return pl.pallas_call(
        body, grid=(nt,), in_specs=[row, vec, row],
        out_specs=[row, row, vec, pl.BlockSpec((1, 128), lambda i: (0, 0))],
        out_shape=[jax.ShapeDtypeStruct((s, d), F32), jax.ShapeDtypeStruct((s, d), BF16),
                   jax.ShapeDtypeStruct((1, d), F32), jax.ShapeDtypeStruct((1, 128), F32)],
        scratch_shapes=[pltpu.VMEM((1, d), F32)], name=name, compiler_params=_params("arbitrary"),
    )(h, g.reshape(1, d), target)


_SQRT_HALF = 0.7071067811865476
_INV_SQRT_2PI = 0.3989422804014327


def _gelu(x):
    return 0.5 * x * (1.0 + lax.erf(x * _SQRT_HALF))


def _gelu_and_grad(x):
    cdf = 0.5 * (1.0 + lax.erf(x * _SQRT_HALF))
    return x * cdf, cdf + x * (jnp.exp(-0.5 * x * x) * _INV_SQRT_2PI)


def _trailing_sums(xe, win):
    s = xe
    sh = 1
    while sh < win:
        s = s + pltpu.roll(s, sh, 0)
        sh *= 2
    return s


def _leading_sums(xe, win):
    n = xe.shape[0]
    s = xe
    sh = 1
    while sh < win:
        s = s + pltpu.roll(s, n - sh, 0)
        sh *= 2
    return s


def _tril_mask():
    return lax.broadcasted_iota(jnp.int32, (GROUP, GROUP), 0) >= lax.broadcasted_iota(jnp.int32, (GROUP, GROUP), 1)


def _layernorm_stats(v):
    mu = jnp.mean(v, axis=-1, keepdims=True)
    xc = v - mu
    rstd = lax.rsqrt(jnp.mean(xc * xc, axis=-1, keepdims=True) + EPS)
    return xc * rstd, rstd


def _mixer_specs(s, t):
    halo_blocks = t // POOL_HALO
    tile = lambda w: pl.BlockSpec((t, w), lambda i: (i, 0))
    prev = pl.BlockSpec((POOL_HALO, POOL_WIDTH), lambda i: (jnp.maximum(i * halo_blocks - 1, 0), 0))
    nxt = pl.BlockSpec((POOL_HALO, POOL_WIDTH),
                       lambda i: (jnp.minimum((i + 1) * halo_blocks, s // POOL_HALO - 1), 0))
    const3 = pl.BlockSpec((HEADS, GROUP, GROUP), lambda i: (0, 0, 0))
    vec = pl.BlockSpec((1, POOL_WIDTH), lambda i: (0, 0))
    bias = pl.BlockSpec((GROUP, SGU_WIDTH), lambda i: (0, 0))
    return tile, prev, nxt, const3, vec, bias


def _mixer_fwd(proj, pool_w, pool_scale, sgu_g, sgu_w, sgu_bias, *, name, t=256):
    s = proj.shape[0]
    t = _tile(s, t)
    tile, prev, _, const3, vec, bias = _mixer_specs(s, t)

    def body(proj_ref, halo_ref, pw_ref, ps_ref, sg_ref, sw_ref, sb_ref, cat_ref):
        i = pl.program_id(0)
        row = i * t + lax.broadcasted_iota(jnp.int32, (t, 1), 0)
        p = proj_ref[:, 0:POOL_WIDTH]
        pe = jnp.concatenate([jnp.where(i > 0, halo_ref[...], 0.0), p], axis=0)
        for gi, win in enumerate(POOL_WINDOWS):
            cols = slice(gi * GROUP, (gi + 1) * GROUP)
            count = jnp.minimum(row + 1, win).astype(F32)
            d = _trailing_sums(pe[:, cols], win)[POOL_HALO:] / count - p[:, cols]
            y = jnp.dot(d.astype(BF16), pw_ref[gi].astype(BF16), preferred_element_type=F32) * ps_ref[:, cols]
            cat_ref[:, cols] = y.astype(BF16)

        u = _gelu(proj_ref[:, POOL_WIDTH:POOL_WIDTH + SGU_WIDTH])
        xhat, _ = _layernorm_stats(_gelu(proj_ref[:, POOL_WIDTH + SGU_WIDTH:]))
        vn = (xhat * sg_ref[...]).astype(BF16)
        tri = _tril_mask()
        for h in range(HEADS):
            cols = slice(h * GROUP, (h + 1) * GROUP)
            w = jnp.where(tri, sw_ref[h], 0.0).astype(BF16)
            for c in range(t // GROUP):
                rows = slice(c * GROUP, (c + 1) * GROUP)
                z = jnp.dot(w, vn[rows, cols], preferred_element_type=F32) + sb_ref[:, cols]
                cat_ref[rows, POOL_WIDTH + h * GROUP:POOL_WIDTH + (h + 1) * GROUP] = (u[rows, cols] * z).astype(BF16)

    return pl.pallas_call(
        body, grid=(s // t,),
        in_specs=[tile(POOL_WIDTH + 2 * SGU_WIDTH), prev, const3, vec, vec, const3, bias],
        out_specs=tile(POOL_WIDTH + SGU_WIDTH),
        out_shape=jax.ShapeDtypeStruct((s, POOL_WIDTH + SGU_WIDTH), BF16), name=name,
        compiler_params=_params("parallel"),
    )(proj, proj, pool_w, pool_scale, sgu_g, sgu_w, sgu_bias)


def _mixer_bwd(proj, dcat, pool_w, pool_scale, sgu_g, sgu_w, sgu_bias, *, name, t=256):
    s = proj.shape[0]
    t = _tile(s, t)
    nt = s // t
    tile, prev, nxt, const3, vec, bias = _mixer_specs(s, t)

    def body(proj_ref, halo_ref, dcat_ref, dnext_ref, pw_ref, ps_ref, sg_ref, sw_ref, sb_ref,
             dproj_ref, dpw_ref, dps_ref, dsg_ref, dsw_ref, dsb_ref, du_ref, dvn_ref, dz_ref):
        i = pl.program_id(0)

        @pl.when(i == 0)
        def _():
            dpw_ref[...] = jnp.zeros_like(dpw_ref)
            dps_ref[...] = jnp.zeros_like(dps_ref)
            dsg_ref[...] = jnp.zeros_like(dsg_ref)
            dsw_ref[...] = jnp.zeros_like(dsw_ref)
            dz_ref[...] = jnp.zeros_like(dz_ref)

        row = i * t + lax.broadcasted_iota(jnp.int32, (t, 1), 0)
        row_e = i * t + lax.broadcasted_iota(jnp.int32, (t + POOL_HALO, 1), 0)
        p = proj_ref[:, 0:POOL_WIDTH]
        pe = jnp.concatenate([jnp.where(i > 0, halo_ref[...], 0.0), p], axis=0)
        dyp = dcat_ref[:, 0:POOL_WIDTH]
        dye = jnp.concatenate([dyp, jnp.where(i < nt - 1, dnext_ref[...], 0.0)], axis=0)
        for gi, win in enumerate(POOL_WINDOWS):
            cols = slice(gi * GROUP, (gi + 1) * GROUP)
            count = jnp.minimum(row + 1, win).astype(F32)
            d = (_trailing_sums(pe[:, cols], win)[POOL_HALO:] / count - p[:, cols]).astype(BF16)
            pw = pw_ref[gi].astype(BF16)
            pre = jnp.dot(d, pw, preferred_element_type=F32)
            dps_ref[:, cols] += jnp.sum(dyp[:, cols] * pre, axis=0, keepdims=True)
            ys = (dye[:, cols] * ps_ref[:, cols]).astype(BF16)
            dpw_ref[gi] += lax.dot_general(d, ys[:t], _DOT_DIMS["tn"], preferred_element_type=F32)
            dd = lax.dot_general(ys, pw, _DOT_DIMS["nt"], preferred_element_type=F32)
            count_e = jnp.minimum(row_e + 1, win).astype(F32)
            dp = _leading_sums(dd / count_e, win)[:t] - dd[:t]
            dproj_ref[:, cols] = dp.astype(BF16)

        xu = proj_ref[:, POOL_WIDTH:POOL_WIDTH + SGU_WIDTH]
        xv = proj_ref[:, POOL_WIDTH + SGU_WIDTH:]
        u, gelu_grad_u = _gelu_and_grad(xu)
        v, gelu_grad_v = _gelu_and_grad(xv)
        xhat, rstd = _layernorm_stats(v)
        gain = sg_ref[...]
        vn = (xhat * gain).astype(BF16)
        tri = _tril_mask()
        for h in range(HEADS):
            cols = slice(h * GROUP, (h + 1) * GROUP)
            wf = jnp.where(tri, sw_ref[h], 0.0)
            w, wt = wf.astype(BF16), wf.T.astype(BF16)
            for c in range(t // GROUP):
                rows = slice(c * GROUP, (c + 1) * GROUP)
                vch = vn[rows, cols]
                z = jnp.dot(w, vch, preferred_element_type=F32) + sb_ref[:, cols]
                dy = dcat_ref[rows, POOL_WIDTH + h * GROUP:POOL_WIDTH + (h + 1) * GROUP]
                du_ref[rows, cols] = dy * z
                dz = dy * u[rows, cols]
                dz_ref[:, cols] += dz
                dzb = dz.astype(BF16)
                dsw_ref[h] += lax.dot_general(dzb, vch, _DOT_DIMS["nt"], preferred_element_type=F32)
                dvn_ref[rows, cols] = jnp.dot(wt, dzb, preferred_element_type=F32)
        dvn = dvn_ref[...]
        dsg_ref[...] += jnp.sum(dvn * xhat, axis=0, keepdims=True)
        dxh = dvn * gain
        dv = rstd * (dxh - jnp.mean(dxh, axis=-1, keepdims=True)
                     - xhat * jnp.mean(dxh * xhat, axis=-1, keepdims=True))
        dproj_ref[:, POOL_WIDTH:POOL_WIDTH + SGU_WIDTH] = (du_ref[...] * gelu_grad_u).astype(BF16)
        dproj_ref[:, POOL_WIDTH + SGU_WIDTH:] = (dv * gelu_grad_v).astype(BF16)

        @pl.when(i == nt - 1)
        def _():
            for h in range(HEADS):
                dsw_ref[h] = jnp.where(tri, dsw_ref[h], 0.0)
            lane = lax.broadcasted_iota(jnp.int32, (GROUP, GROUP), 1)
            out = jnp.zeros((GROUP, GROUP), F32)
            for h in range(HEADS):
                sh = jnp.sum(dz_ref[:, h * GROUP:(h + 1) * GROUP], axis=1, keepdims=True)
                out = jnp.where(lane == h, sh, out)
            dsb_ref[...] = out

    outs = pl.pallas_call(
        body, grid=(nt,),
        in_specs=[tile(POOL_WIDTH + 2 * SGU_WIDTH), prev, tile(POOL_WIDTH + SGU_WIDTH), nxt,
                  const3, vec, vec, const3, bias],
        out_specs=[tile(POOL_WIDTH + 2 * SGU_WIDTH), const3, vec, vec, const3,
                   pl.BlockSpec((GROUP, GROUP), lambda i: (0, 0))],
        out_shape=[jax.ShapeDtypeStruct((s, POOL_WIDTH + 2 * SGU_WIDTH), BF16),
                   jax.ShapeDtypeStruct((HEADS, GROUP, GROUP), F32),
                   jax.ShapeDtypeStruct((1, POOL_WIDTH), F32),
                   jax.ShapeDtypeStruct((1, SGU_WIDTH), F32),
                   jax.ShapeDtypeStruct((HEADS, GROUP, GROUP), F32),
                   jax.ShapeDtypeStruct((GROUP, GROUP), F32)],
        scratch_shapes=[pltpu.VMEM((t, SGU_WIDTH), F32), pltpu.VMEM((t, SGU_WIDTH), F32),
                        pltpu.VMEM((GROUP, SGU_WIDTH), F32)],
        name=name, compiler_params=_params("arbitrary"),
    )(proj, proj, dcat, dcat, pool_w, pool_scale, sgu_g, sgu_w, sgu_bias)
    dproj, dpw, dps, dsg, dsw, dsb = outs
    return dproj, dpw, dps, dsg, dsw, dsb[:, :HEADS].T


def _attn_probs(q, k, scale):
    sc = lax.dot_general(q, k, _DOT_DIMS["nt"], preferred_element_type=F32) * scale
    sc = sc - jnp.max(sc, axis=-1, keepdims=True)
    e = jnp.exp(sc)
    return e / jnp.sum(e, axis=-1, keepdims=True)


def _attn_fwd(q, k, v, *, name, t=512):
    s, d = q.shape
    nm = k.shape[0]
    t = _tile(s, t)
    scale = HEAD_DIM ** -0.5

    def body(q_ref, k_ref, v_ref, o_ref):
        for h in range(HEADS):
            cols = slice(h * HEAD_DIM, (h + 1) * HEAD_DIM)
            pr = _attn_probs(q_ref[:, cols], k_ref[:, cols], scale)
            o_ref[:, cols] = jnp.dot(pr.astype(BF16), v_ref[:, cols], preferred_element_type=F32).astype(BF16)

    row = pl.BlockSpec((t, d), lambda i: (i, 0))
    kv = pl.BlockSpec((nm, d), lambda i: (0, 0))
    return pl.pallas_call(
        body, grid=(s // t,), in_specs=[row, kv, kv], out_specs=row,
        out_shape=jax.ShapeDtypeStruct((s, d), BF16), name=name, compiler_params=_params("parallel"),
    )(q, k, v)


def _attn_bwd(q, k, v, do, *, name, t=512):
    s, d = q.shape
    nm = k.shape[0]
    t = _tile(s, t)
    scale = HEAD_DIM ** -0.5

    def body(q_ref, k_ref, v_ref, do_ref, dq_ref, dk_ref, dv_ref):
        i = pl.program_id(0)

        @pl.when(i == 0)
        def _():
            dk_ref[...] = jnp.zeros_like(dk_ref)
            dv_ref[...] = jnp.zeros_like(dv_ref)

        for h in range(HEADS):
            cols = slice(h * HEAD_DIM, (h + 1) * HEAD_DIM)
            qh, kh, vh, doh = q_ref[:, cols], k_ref[:, cols], v_ref[:, cols], do_ref[:, cols]
            pr = _attn_probs(qh, kh, scale)
            dpr = lax.dot_general(doh, vh, _DOT_DIMS["nt"], preferred_element_type=F32)
            ds = (pr * (dpr - jnp.sum(dpr * pr, axis=-1, keepdims=True)) * scale).astype(BF16)
            dv_ref[:, cols] += lax.dot_general(pr.astype(BF16), doh, _DOT_DIMS["tn"], preferred_element_type=F32)
            dk_ref[:, cols] += lax.dot_general(ds, qh, _DOT_DIMS["tn"], preferred_element_type=F32)
            dq_ref[:, cols] = jnp.dot(ds, kh, preferred_element_type=F32).astype(BF16)

    row = pl.BlockSpec((t, d), lambda i: (i, 0))
    kv = pl.BlockSpec((nm, d), lambda i: (0, 0))
    return pl.pallas_call(
        body, grid=(s // t,), in_specs=[row, kv, kv, row], out_specs=[row, kv, kv],
        out_shape=[jax.ShapeDtypeStruct((s, d), BF16), jax.ShapeDtypeStruct((nm, d), F32),
                   jax.ShapeDtypeStruct((nm, d), F32)],
        name=name, compiler_params=_params("arbitrary"),
    )(q, k, v, do)


def _conv_specs(s, f, t, tc, swap):
    hb = t // CONV_HALO
    order = (lambda fn: (lambda j, i: fn(i, j))) if swap else (lambda fn: fn)
    tile3 = pl.BlockSpec((2, t, tc), order(lambda i, j: (0, i, j)))
    prev3 = pl.BlockSpec((2, CONV_HALO, tc), order(lambda i, j: (0, jnp.maximum(i * hb - 1, 0), j)))
    next3 = pl.BlockSpec((2, CONV_HALO, tc),
                         order(lambda i, j: (0, jnp.minimum((i + 1) * hb, s // CONV_HALO - 1), j)))
    tile2 = pl.BlockSpec((t, tc), order(lambda i, j: (i, j)))
    next2 = pl.BlockSpec((CONV_HALO, tc), order(lambda i, j: (jnp.minimum((i + 1) * hb, s // CONV_HALO - 1), j)))
    wspec = pl.BlockSpec((2, 3, tc), order(lambda i, j: (0, 0, j)))
    bspec = pl.BlockSpec((2, 1, tc), order(lambda i, j: (0, 0, j)))
    return tile3, prev3, next3, tile2, next2, wspec, bspec


def _conv3(w_ref, p, x2, x1, x0, b):
    return (w_ref[p, 0:1, :] * x2 + w_ref[p, 1:2, :] * x1 + w_ref[p, 2:3, :] * x0) + b


def _convgate_fwd(hh, cw, cb, *, name, t=256, tc=1408):
    _, s, f = hh.shape
    t = _tile(s, t)
    tile3, prev3, _, tile2, _, wspec, bspec = _conv_specs(s, f, t, tc, swap=False)

    def body(hh_ref, prev_ref, cw_ref, cb_ref, act_ref):
        i = pl.program_id(0)
        hc = []
        for p in range(2):
            xe = jnp.concatenate([jnp.where(i > 0, prev_ref[p], 0.0), hh_ref[p]], axis=0)
            hc.append(_conv3(cw_ref, p, pltpu.roll(xe, 2, 0), pltpu.roll(xe, 1, 0), xe, cb_ref[p])[CONV_HALO:])
        gate, val = hc
        act_ref[...] = ((gate * jax.nn.sigmoid(gate)) * val).astype(BF16)

    return pl.pallas_call(
        body, grid=(s // t, f // tc), in_specs=[tile3, prev3, wspec, bspec], out_specs=tile2,
        out_shape=jax.ShapeDtypeStruct((s, f), BF16), name=name, compiler_params=_params("parallel", "parallel"),
    )(hh, hh, cw, cb)


def _ffn_up_gate(xn, w_up, cw, cb, *, name, t=256, tc=1408):
    s, d = xn.shape
    f = w_up.shape[1] // 2
    t = _tile(s, t)
    nt, nj = s // t, f // tc

    def body(xn_ref, wg_ref, wv_ref, cw_ref, cb_ref, hh_ref, act_ref, held_ref, above_ref):
        i = pl.program_id(1)

        @pl.when(i == 0)
        def _():
            held_ref[...] = jnp.zeros_like(held_ref)
            above_ref[...] = jnp.zeros_like(above_ref)

        hc = []
        for p in range(2):
            xe = jnp.concatenate([above_ref[p], held_ref[p]], axis=0)
            hc.append(_conv3(cw_ref, p, pltpu.roll(xe, 2, 0), pltpu.roll(xe, 1, 0), xe, cb_ref[p])[CONV_HALO:])
            above_ref[p] = held_ref[p, t - CONV_HALO:t, :]
        gate, val = hc
        act_ref[...] = ((gate * jax.nn.sigmoid(gate)) * val).astype(BF16)

        x = xn_ref[...]
        for p, w_ref in enumerate((wg_ref, wv_ref)):
            y = jnp.dot(x, w_ref[...], preferred_element_type=F32)
            hh_ref[p] = y
            held_ref[p] = y

    row = lambda i: jnp.minimum(i, nt - 1)
    return pl.pallas_call(
        body, grid=(nj, nt + 1),
        in_specs=[pl.BlockSpec((t, d), lambda j, i: (row(i), 0)),
                  pl.BlockSpec((d, tc), lambda j, i: (0, j)),
                  pl.BlockSpec((d, tc), lambda j, i: (0, nj + j)),
                  pl.BlockSpec((2, 3, tc), lambda j, i: (0, 0, j)),
                  pl.BlockSpec((2, 1, tc), lambda j, i: (0, 0, j))],
        out_specs=[pl.BlockSpec((2, t, tc), lambda j, i: (0, row(i), j)),
                   pl.BlockSpec((t, tc), lambda j, i: (jnp.maximum(i - 1, 0), j))],
        out_shape=[jax.ShapeDtypeStruct((2, s, f), F32), jax.ShapeDtypeStruct((s, f), BF16)],
        scratch_shapes=[pltpu.VMEM((2, t, tc), F32), pltpu.VMEM((2, CONV_HALO, tc), F32)],
        name=name, compiler_params=_params("arbitrary", "arbitrary"),
    )(xn, w_up, w_up, cw, cb)


def _convgate_bwd(hh, dact, cw, cb, *, name, t=256, tc=1408):
    _, s, f = hh.shape
    t = _tile(s, t)
    nt = s // t
    tile3, prev3, next3, tile2, next2, wspec, bspec = _conv_specs(s, f, t, tc, swap=True)

    def body(hh_ref, prev_ref, next_ref, da_ref, danext_ref, cw_ref, cb_ref, dhh_ref, dcw_ref, dcb_ref):
        i = pl.program_id(1)
        is_last = i == nt - 1

        @pl.when(i == 0)
        def _():
            dcw_ref[...] = jnp.zeros_like(dcw_ref)
            dcb_ref[...] = jnp.zeros_like(dcb_ref)

        taps, hc = [], []
        for p in range(2):
            xe = jnp.concatenate([jnp.where(i > 0, prev_ref[p], 0.0), hh_ref[p],
                                  jnp.where(is_last, 0.0, next_ref[p])], axis=0)
            x2, x1 = pltpu.roll(xe, 2, 0), pltpu.roll(xe, 1, 0)
            hc.append(_conv3(cw_ref, p, x2, x1, xe, cb_ref[p])[CONV_HALO:])
            taps.append((x2[CONV_HALO:CONV_HALO + t], x1[CONV_HALO:CONV_HALO + t], xe[CONV_HALO:CONV_HALO + t]))
        gate, val = hc
        dae = jnp.concatenate([da_ref[...], jnp.where(is_last, 0.0, danext_ref[...])], axis=0)
        sg = jax.nn.sigmoid(gate)
        dval = dae * (gate * sg)
        dgate = dae * val * (sg * (1.0 + gate * (1.0 - sg)))
        m = t + CONV_HALO
        for p, dhc in enumerate((dgate, dval)):
            dh = (cw_ref[p, 2:3, :] * dhc + cw_ref[p, 1:2, :] * pltpu.roll(dhc, m - 1, 0)
                  + cw_ref[p, 0:1, :] * pltpu.roll(dhc, m - 2, 0))
            dhh_ref[p] = dh[:t].astype(BF16)
            d0 = dhc[:t]
            for kk, tap in enumerate(taps[p]):
                dcw_ref[p, kk:kk + 1, :] += jnp.sum(d0 * tap, axis=0, keepdims=True)
            dcb_ref[p] += jnp.sum(d0, axis=0, keepdims=True)

    return pl.pallas_call(
        body, grid=(f // tc, nt), in_specs=[tile3, prev3, next3, tile2, next2, wspec, bspec],
        out_specs=[tile3, wspec, bspec],
        out_shape=[jax.ShapeDtypeStruct((2, s, f), BF16), jax.ShapeDtypeStruct((2, 3, f), F32),
                   jax.ShapeDtypeStruct((2, 1, f), F32)],
        name=name, compiler_params=_params("parallel", "arbitrary"),
    )(hh, hh, hh, dact, dact, cw, cb)


def _position():
    return lax.axis_index("x"), lax.axis_index("y"), lax.axis_index("c")


def _linear(px, py, pc):
    return 4 * px + 2 * py + pc


def _peers_of(x, y, c):
    peers = []
    for mask in range(1, N_DEV):
        peers.append((1 - x if mask & 4 else x, 1 - y if mask & 2 else y, 1 - c if mask & 1 else c))
    return peers


def _exchange_copy(src_ref, land_ref, send_sem, recv_sem, peer, mine, scatter, arriving):
    src = src_ref.at[_linear(*peer)] if scatter else src_ref
    dst = land_ref.at[_linear(*peer) if arriving else mine]
    return pltpu.make_async_remote_copy(src_ref=src, dst_ref=dst, send_sem=send_sem, recv_sem=recv_sem,
                                        device_id=peer, device_id_type=MESH)


_EXCHANGE_COLLECTIVE_ID = 7


def _sequencer_exchange(srcs, *, scatter, name):
    n = len(srcs)
    src_refs = [jax.new_ref(a, memory_space=pltpu.MemorySpace.HBM) for a in srcs]
    land_refs = [jax.empty_ref(jax.ShapeDtypeStruct(a.shape if scatter else (N_DEV,) + a.shape, a.dtype),
                               memory_space=pltpu.MemorySpace.HBM) for a in srcs]

    @pl.kernel(mesh=plsc.ScalarSubcoreMesh(axis_name="sequencer", num_cores=1), name=name,
               scratch_types=(pltpu.SemaphoreType.DMA((7 * n,)), pltpu.SemaphoreType.DMA((7 * n,)),
                              pltpu.SemaphoreType.DMA((n,))),
               compiler_params=pltpu.CompilerParams(collective_id=_EXCHANGE_COLLECTIVE_ID))
    def launch(send_sems, recv_sems, local_sems):
        x, y, c = _position()
        mine = _linear(x, y, c)
        peers = _peers_of(x, y, c)
        barrier = pltpu.get_barrier_semaphore()
        for peer in peers:
            pl.semaphore_signal(barrier, inc=1, device_id=peer, device_id_type=MESH)
        pl.semaphore_wait(barrier, N_DEV - 1)
        local = [pltpu.make_async_copy(src_refs[t].at[mine] if scatter else src_refs[t], land_refs[t].at[mine],
                                       local_sems.at[t]) for t in range(n)]
        for cp in local:
            cp.start()
        if scatter:
            sends = []
            for t in range(n):
                for k, peer in enumerate(peers):
                    cp = _exchange_copy(src_refs[t], land_refs[t], send_sems.at[7 * t + k], recv_sems.at[7 * t + k],
                                        peer, mine, scatter, arriving=False)
                    cp.start()
                    sends.append(cp)
            for t in range(n):
                for k, peer in enumerate(peers):
                    _exchange_copy(src_refs[t], land_refs[t], send_sems.at[7 * t + k], recv_sems.at[7 * t + k],
                                   peer, mine, scatter, arriving=True).wait_recv()
        else:
            me, sibling = (x, y, c), (x, y, 1 - c)
            chips = [(1 - x, y), (x, 1 - y), (1 - x, 1 - y)]

            def copy(t, k, block, to, src=None):
                dst = land_refs[t].at[_linear(*block)]
                return pltpu.make_async_remote_copy(
                    src_ref=dst if src is None else src, dst_ref=dst, send_sem=send_sems.at[7 * t + k],
                    recv_sem=recv_sems.at[7 * t + k], device_id=to, device_id_type=MESH)

            sends = []
            for t in range(n):
                sends.append(copy(t, 0, me, sibling, src=src_refs[t]))
                for j, chip in enumerate(chips):
                    sends.append(copy(t, 1 + j, me, (*chip, c), src=src_refs[t]))
            for cp in sends:
                cp.start()
            for j, chip in enumerate(chips):
                for t in range(n):
                    copy(t, 1 + j, (*chip, c), me).wait_recv()
                    passed = copy(t, 4 + j, (*chip, c), sibling)
                    passed.start()
                    sends.append(passed)
            for t in range(n):
                copy(t, 0, sibling, me).wait_recv()
                for j, chip in enumerate(chips):
                    copy(t, 4 + j, (*chip, 1 - c), me).wait_recv()
        for cp in local:
            cp.wait()
        for cp in sends:
            cp.wait_send()

    launch()
    return [r[...] for r in land_refs]


def _adamw_math(g, w, m, v):
    m2 = ADAM_B1 * m + (1.0 - ADAM_B1) * g
    v2 = ADAM_B2 * v + (1.0 - ADAM_B2) * (g * g)
    m_hat = m2 / (1.0 - ADAM_B1 ** ADAM_STEP)
    v_hat = v2 / (1.0 - ADAM_B2 ** ADAM_STEP)
    delta = -ADAM_LR * (m_hat / (jnp.sqrt(v_hat) + ADAM_EPS) + ADAM_WD * w)
    return delta, m2, v2


def _adamw(slots, w, m, v, *, name, tr=256):
    depth = len(slots)
    _, r, c = slots[0].shape
    tr = next((cand for cand in range(min(r, tr), 15, -1) if r % cand == 0 and cand % 16 == 0), r)

    def body(*refs):
        s_refs = refs[:depth]
        w_ref, m_ref, v_ref, g_ref, d_ref, m2_ref, v2_ref = refs[depth:]
        layer = pl.program_id(0)
        for l in range(depth):
            @pl.when(layer == l)
            def _():
                g = s_refs[l][0].astype(F32)
                for d in range(1, N_DEV):
                    g = g + s_refs[l][d].astype(F32)
                delta, m2, v2 = _adamw_math(g, w_ref[...], m_ref[...], v_ref[...])
                g_ref[...] = g
                d_ref[...] = delta
                m2_ref[...] = m2
                v2_ref[...] = v2

    blk = pl.BlockSpec((None, tr, c), lambda layer, i: (layer, i, 0))
    sblks = [pl.BlockSpec((N_DEV, tr, c), lambda layer, i, l=l: (0, jnp.where(layer == l, i, 0), 0))
             for l in range(depth)]
    shape = jax.ShapeDtypeStruct((depth, r, c), F32)
    return pl.pallas_call(
        body, grid=(depth, r // tr), in_specs=sblks + [blk, blk, blk], out_specs=[blk] * 4,
        out_shape=[shape] * 4, name=name, compiler_params=_params("arbitrary", "arbitrary"),
    )(*slots, w, m, v)


_SHARDED = ("w_in", "w_out", "wq", "wk", "wv", "wo", "w_up", "conv_w", "w_down")
_SMALL = ("norm_mix_g", "pool_w", "pool_scale", "sgu_g", "sgu_w", "sgu_b", "norm_xattn_g", "mem_norm_g",
          "norm_ffn_g", "conv_b", "final_norm_g")
_WEIGHTS = ("norm_mix_g", "w_in", "pool_w", "pool_scale", "sgu_g", "sgu_w", "sgu_b", "w_out", "norm_xattn_g",
            "mem_norm_g", "wq", "wk", "wv", "wo", "norm_ffn_g", "w_up", "conv_w", "conv_b", "w_down",
            "final_norm_g")
_PACK_LANES = 128
_GATHER_GROUPS = (("w_in",), ("w_out",), ("wq", "wk", "wv", "wo"), ("w_up", "conv_w", "w_down"))


def _cols_to_blocks(a, *, name, tr=256):
    r, c8 = a.shape
    c = c8 // N_DEV
    tr = _tile(r, tr)

    def body(a_ref, o_ref):
        for dev in range(N_DEV):
            o_ref[dev] = a_ref[:, dev * c:(dev + 1) * c]

    return pl.pallas_call(
        body, grid=(r // tr,), in_specs=[pl.BlockSpec((tr, c8), lambda i: (i, 0))],
        out_specs=pl.BlockSpec((N_DEV, tr, c), lambda i: (0, i, 0)),
        out_shape=jax.ShapeDtypeStruct((N_DEV, r, c), a.dtype), name=name, compiler_params=_params("parallel"),
    )(a)


def _blocks_to_cols(a, *, name, tr=256):
    n, r, c = a.shape
    tr = _tile(r, tr)

    def body(a_ref, o_ref):
        for dev in range(n):
            o_ref[:, dev * c:(dev + 1) * c] = a_ref[dev]

    return pl.pallas_call(
        body, grid=(r // tr,), in_specs=[pl.BlockSpec((n, tr, c), lambda i: (0, i, 0))],
        out_specs=pl.BlockSpec((tr, n * c), lambda i: (i, 0)),
        out_shape=jax.ShapeDtypeStruct((r, n * c), a.dtype), name=name, compiler_params=_params("parallel"),
    )(a)


def _pin(x, *deps):
    return lax.optimization_barrier((x, *deps))[0]


def _pack(arrays):
    flat = jnp.concatenate([a.reshape(-1) for a in arrays])
    assert flat.shape[0] % (8 * _PACK_LANES) == 0
    return flat.reshape(-1, _PACK_LANES)


def _unpack(packed, like):
    flat = packed.reshape(-1)
    out, off = [], 0
    for a in like:
        out.append(flat[off:off + a.size].reshape(a.shape))
        off += a.size
    return out


def kernel(x, mem, norm_mix_g, w_in, pool_w, pool_scale, sgu_g, sgu_w, sgu_b, w_out, norm_xattn_g, mem_norm_g, wq, wk, wv, wo, norm_ffn_g, w_up, conv_w, conv_b, w_down, final_norm_g, loss_target, m_norm_mix_g, m_w_in, m_pool_w, m_pool_scale, m_sgu_g, m_sgu_w, m_sgu_b, m_w_out, m_norm_xattn_g, m_mem_norm_g, m_wq, m_wk, m_wv, m_wo, m_norm_ffn_g, m_w_up, m_conv_w, m_conv_b, m_w_down, m_final_norm_g, v_norm_mix_g, v_w_in, v_pool_w, v_pool_scale, v_sgu_g, v_sgu_w, v_sgu_b, v_w_out, v_norm_xattn_g, v_mem_norm_g, v_wq, v_wk, v_wv, v_wo, v_norm_ffn_g, v_w_up, v_conv_w, v_conv_b, v_w_down, v_final_norm_g):
    W = dict(norm_mix_g=norm_mix_g, w_in=w_in, pool_w=pool_w, pool_scale=pool_scale, sgu_g=sgu_g, sgu_w=sgu_w,
             sgu_b=sgu_b, w_out=w_out, norm_xattn_g=norm_xattn_g, mem_norm_g=mem_norm_g, wq=wq, wk=wk, wv=wv, wo=wo,
             norm_ffn_g=norm_ffn_g, w_up=w_up, conv_w=conv_w, conv_b=conv_b, w_down=w_down,
             final_norm_g=final_norm_g)
    M = dict(norm_mix_g=m_norm_mix_g, w_in=m_w_in, pool_w=m_pool_w, pool_scale=m_pool_scale, sgu_g=m_sgu_g,
             sgu_w=m_sgu_w, sgu_b=m_sgu_b, w_out=m_w_out, norm_xattn_g=m_norm_xattn_g, mem_norm_g=m_mem_norm_g,
             wq=m_wq, wk=m_wk, wv=m_wv, wo=m_wo, norm_ffn_g=m_norm_ffn_g, w_up=m_w_up, conv_w=m_conv_w,
             conv_b=m_conv_b, w_down=m_w_down, final_norm_g=m_final_norm_g)
    V = dict(norm_mix_g=v_norm_mix_g, w_in=v_w_in, pool_w=v_pool_w, pool_scale=v_pool_scale, sgu_g=v_sgu_g,
             sgu_w=v_sgu_w, sgu_b=v_sgu_b, w_out=v_w_out, norm_xattn_g=v_norm_xattn_g, mem_norm_g=v_mem_norm_g,
             wq=v_wq, wk=v_wk, wv=v_wv, wo=v_wo, norm_ffn_g=v_norm_ffn_g, w_up=v_w_up, conv_w=v_conv_w,
             conv_b=v_conv_b, w_down=v_w_down, final_norm_g=v_final_norm_g)

    s, d = x.shape[1], x.shape[2]
    f = w_down.shape[1] * N_DEV
    h = x.reshape(s, d)
    memx = mem.reshape(mem.shape[1], d)
    target = loss_target.reshape(s, d)

    gathered = {}

    def launch_gather(l, gi, after):
        if l >= DEPTH:
            return
        names = _GATHER_GROUPS[gi]
        shards = [W[nme][l] if nme == "conv_w" else W[nme][l].astype(BF16) for nme in names]
        if after is not None:
            shards[0], _ = lax.optimization_barrier((shards[0], after))
        gathered[l, gi] = dict(zip(names, _sequencer_exchange(shards, scatter=False, name=f"gather_{l}_{gi}")))

    launch_gather(0, 0, None)

    saved, full = [], []
    for l in range(DEPTH):
        sgu_bias = jnp.repeat(sgu_b[l].T, GROUP, axis=1)
        cb = conv_b[l].reshape(2, 1, f)
        if l == 0:
            xn1 = _rms_fwd(h, norm_mix_g[l], name=f"norm_mix_{l}")
            launch_gather(0, 1, xn1)
        w_in_f = _blocks_to_cols(gathered[l, 0]["w_in"], name=f"w_in_cols_{l}")
        proj = _mm_nn(xn1, w_in_f, out_dtype=F32, name=f"proj_in_{l}")
        if l == 0:
            launch_gather(0, 2, proj)
        cat = _mixer_fwd(proj, pool_w[l], pool_scale[l].reshape(1, -1), sgu_g[l].reshape(1, -1), sgu_w[l], sgu_bias,
                         name=f"mixer_{l}")
        if l == 0:
            launch_gather(0, 3, cat)
        w_out_f = gathered[l, 1]["w_out"].reshape(-1, d)
        h1, xn2 = _mm_nn(cat, w_out_f, out_dtype=F32, res=h, norm_gain=norm_xattn_g[l], name=f"proj_out_{l}")
        launch_gather(l + 1, 0, h1)
        g = gathered[l, 2]
        wq_f, wk_f, wv_f, wo_f = (g[nme].reshape(-1, d) for nme in ("wq", "wk", "wv", "wo"))
        q = _mm_nn(xn2, wq_f, out_dtype=BF16, name=f"q_{l}")
        launch_gather(l + 1, 1, q)
        memn = _rms_fwd(memx, mem_norm_g[l], name=f"norm_mem_{l}")
        k = _mm_nn(memn, wk_f, out_dtype=BF16, name=f"k_{l}")
        v = _mm_nn(memn, wv_f, out_dtype=BF16, name=f"v_{l}")
        o = _attn_fwd(q, k, v, name=f"attn_{l}")
        h2, xn3 = _mm_nn(o, wo_f, out_dtype=F32, res=h1, norm_gain=norm_ffn_g[l], name=f"attn_out_{l}")
        launch_gather(l + 1, 2, h2)
        g = gathered[l, 3]
        w_up_f = _blocks_to_cols(g["w_up"], name=f"w_up_cols_{l}")
        conv_w_f = _blocks_to_cols(g["conv_w"], name=f"conv_w_cols_{l}").reshape(3, 2, f).transpose(1, 0, 2)
        w_down_f = g["w_down"].reshape(-1, d)
        hh, act = _ffn_up_gate(xn3, w_up_f, conv_w_f, cb, name=f"ffn_up_gate_{l}")
        launch_gather(l + 1, 3, hh)
        if l + 1 < DEPTH:
            h3, xn1_next = _mm_nn(act, w_down_f, out_dtype=F32, res=h2, tm=512, norm_gain=norm_mix_g[l + 1],
                                  name=f"ffn_down_{l}")
        else:
            h3, xn1_next = _mm_nn(act, w_down_f, out_dtype=F32, res=h2, tm=512, name=f"ffn_down_{l}"), None
        full.append(dict(w_in=w_in_f, w_out=w_out_f, wq=wq_f, wk=wk_f, wv=wv_f, wo=wo_f, w_up=w_up_f,
                         conv_w=conv_w_f, w_down=w_down_f))
        saved.append(dict(h0=h, xn1=xn1, proj=proj, cat=cat, h1=h1, xn2=xn2, q=q, memn=memn, k=k, v=v, o=o, h2=h2,
                          xn3=xn3, hh=hh, act=act, sgu_bias=sgu_bias, cb=cb))
        h, xn1 = h3, xn1_next

    dh, dhb, dg_final, loss_row = _loss_head(h, final_norm_g, target, name="loss_head")

    slots = {nme: [None] * DEPTH for nme in _SHARDED}
    small = [None] * DEPTH

    previous = []

    def scatter(l, tag, names, parts):
        parts = [_pin(parts[0], *previous)] + parts[1:]
        arrived = _sequencer_exchange(parts, scatter=True, name=f"scatter_{tag}_{l}")
        previous[:] = arrived[:1]
        for nme, land in zip(names, arrived):
            slots[nme][l] = land
        return parts

    for l in reversed(range(DEPTH)):
        fw, sv = full[l], saved[l]
        dact = _mm_nt(dhb, fw["w_down"], out_dtype=F32, tm=512, name=f"d_act_{l}")
        g_w_down = _mm_tn(sv["act"], dhb, tm=f // 2, name=f"g_w_down_{l}")
        dhh, g_conv_w, g_conv_b = _convgate_bwd(sv["hh"], dact, fw["conv_w"], sv["cb"], name=f"d_convgate_{l}")
        g_w_up = _mm_up_tn(sv["xn3"], dhh, name=f"g_w_up_{l}")
        g_conv_w_cols = g_conv_w.transpose(1, 0, 2).reshape(3, 2 * f)
        parts = [_cols_to_blocks(g_w_up, name=f"g_w_up_blocks_{l}"),
                 _cols_to_blocks(g_conv_w_cols, name=f"g_conv_w_blocks_{l}"), g_w_down.reshape(N_DEV, -1, d)]
        parts = scatter(l, "ffn", ("w_up", "conv_w", "w_down"), parts)
        dh2, dh2b, g_norm_ffn = _mm_up_nt(_pin(dhh, *parts), fw["w_up"], name=f"d_norm_ffn_{l}",
                                          norm_bwd=(sv["h2"], norm_ffn_g[l], dh))

        do = _mm_nt(dh2b, fw["wo"], out_dtype=BF16, name=f"d_o_{l}")
        g_wo = _mm_tn(sv["o"], dh2b, name=f"g_wo_{l}")
        dq, dk, dv = _attn_bwd(sv["q"], sv["k"], sv["v"], do, name=f"d_attn_{l}")
        dkb, dvb = dk.astype(BF16), dv.astype(BF16)
        g_wq = _mm_tn(sv["xn2"], dq, name=f"g_wq_{l}")
        g_wk = _mm_tn(sv["memn"], dkb, name=f"g_wk_{l}")
        g_wv = _mm_tn(sv["memn"], dvb, name=f"g_wv_{l}")
        parts = [g.reshape(N_DEV, -1, d) for g in (g_wq, g_wk, g_wv, g_wo)]
        parts = scatter(l, "attn", ("wq", "wk", "wv", "wo"), parts)
        dq = _pin(dq, *parts)
        dmemn = _mm_nt(dkb, fw["wk"], out_dtype=F32, name=f"d_memn_k_{l}")
        dmemn = _mm_nt(dvb, fw["wv"], out_dtype=F32, res=dmemn, name=f"d_memn_v_{l}")
        _, _, g_mem_norm = _rms_bwd(memx, dmemn, mem_norm_g[l], None, name=f"d_norm_mem_{l}")
        dh1, dh1b, g_norm_xattn = _mm_nt(dq, fw["wq"], name=f"d_norm_xattn_{l}",
                                         norm_bwd=(sv["h1"], norm_xattn_g[l], dh2))

        dcat = _mm_nt(dh1b, fw["w_out"], out_dtype=F32, name=f"d_cat_{l}")
        g_w_out = _mm_tn(sv["cat"], dh1b, name=f"g_w_out_{l}")
        dproj, g_pool_w, g_pool_scale, g_sgu_g, g_sgu_w, g_sgu_b = _mixer_bwd(
            sv["proj"], dcat, pool_w[l], pool_scale[l].reshape(1, -1), sgu_g[l].reshape(1, -1), sgu_w[l],
            sv["sgu_bias"], name=f"d_mixer_{l}")
        g_w_in = _mm_tn(sv["xn1"], dproj, name=f"g_w_in_{l}")
        parts = [_cols_to_blocks(g_w_in, name=f"g_w_in_blocks_{l}"), g_w_out.reshape(N_DEV, -1, d)]
        parts = scatter(l, "mix", ("w_in", "w_out"), parts)
        dh, dhb, g_norm_mix = _mm_nt(_pin(dproj, *parts), fw["w_in"], name=f"d_norm_mix_{l}",
                                     norm_bwd=(sv["h0"], norm_mix_g[l], dh1))

        small[l] = dict(norm_mix_g=g_norm_mix.reshape(-1), pool_w=g_pool_w, pool_scale=g_pool_scale.reshape(-1),
                        sgu_g=g_sgu_g.reshape(-1), sgu_w=g_sgu_w, sgu_b=g_sgu_b, norm_xattn_g=g_norm_xattn.reshape(-1),
                        mem_norm_g=g_mem_norm.reshape(-1), norm_ffn_g=g_norm_ffn.reshape(-1),
                        conv_b=g_conv_b.reshape(-1))
    grad_x = dh.reshape(x.shape)

    out = {}
    for nme in _SHARDED:
        w3 = W[nme].reshape(DEPTH, -1, W[nme].shape[-1])
        res = _adamw([sl.reshape((N_DEV,) + w3.shape[1:]) for sl in slots[nme]], w3, M[nme].reshape(w3.shape),
                     V[nme].reshape(w3.shape), name=f"adamw_{nme}")
        out[nme] = [r.reshape(W[nme].shape) for r in res]

    small_names = [n for n in _SMALL]
    contrib = []
    for nme in small_names:
        if nme == "final_norm_g":
            contrib.append(dg_final.reshape(-1))
        else:
            contrib.append(jnp.stack([small[l][nme] for l in range(DEPTH)]))
    tail = 8 * _PACK_LANES
    packed_g = _pack(contrib + [jnp.pad(loss_row[0, :1], (0, tail - 1))])
    (all_g,) = _sequencer_exchange([_pin(packed_g, *previous)], scatter=False, name="gather_small_grads")
    rows = packed_g.shape[0]
    loss = jnp.sum(all_g[:, rows - 8, 0])
    zeros = jnp.zeros((tail,), F32)
    res = _adamw([all_g], _pack([W[n] for n in small_names] + [zeros]).reshape(1, rows, -1),
                 _pack([M[n] for n in small_names] + [zeros]).reshape(1, rows, -1),
                 _pack([V[n] for n in small_names] + [zeros]).reshape(1, rows, -1), name="adamw_small", tr=rows // 2)
    unpacked = [_unpack(r, [W[n] for n in small_names]) for r in res]
    for i, nme in enumerate(small_names):
        out[nme] = [unpacked[j][i] for j in range(4)]

    grads = [out[n][0] for n in _WEIGHTS]
    deltas = [out[n][1] for n in _WEIGHTS]
    new_m = [out[n][2] for n in _WEIGHTS]
    new_v = [out[n][3] for n in _WEIGHTS]
    return (loss, grad_x, *grads, *deltas, *new_m, *new_v)
```

```python
import jax
import jax.numpy as jnp
from jax import lax
from jax.experimental import pallas as pl
from jax.experimental.pallas import tpu as pltpu
from jax.experimental.pallas import tpu_sc as plsc

F32 = jnp.float32
BF16 = jnp.bfloat16
MESH = pl.DeviceIdType.MESH

EPS = 1e-6
N_DEV = 8
DEPTH = 2
POOL_WINDOWS = (2, 4, 8, 16)
GROUP = 128
POOL_WIDTH = 512
SGU_WIDTH = 512
HEADS = 4
HEAD_DIM = 256
POOL_HALO = 16
CONV_HALO = 8

ADAM_LR = 0.001
ADAM_B1 = 0.9
ADAM_B2 = 0.999
ADAM_EPS = 1e-08
ADAM_WD = 0.01
ADAM_STEP = 10

VMEM_LIMIT_BYTES = 52 * 1024 * 1024


def _params(*semantics):
    return pltpu.CompilerParams(dimension_semantics=semantics, vmem_limit_bytes=VMEM_LIMIT_BYTES)


def _tile(n, want):
    t = min(n, want)
    assert n % t == 0, (n, want)
    return t


_DOT_DIMS = {
    "nn": (((1,), (0,)), ((), ())),
    "nt": (((1,), (1,)), ((), ())),
    "tn": (((0,), (0,)), ((), ())),
}


def _mm(a, b, *, dims, grid, a_spec, b_spec, o_spec, out_shape, out_dtype, acc_shape, name, res=None, res_spec=None,
        norm_bwd=None, norm_out=None):
    nk = grid[2]
    dn = _DOT_DIMS[dims]
    extras, extra_specs = [], []
    if res is not None:
        extras, extra_specs = [res], [res_spec]
    if norm_bwd is not None:
        h, gain, dres, row_spec, gain_spec = norm_bwd
        extras = [h, gain] + ([dres] if dres is not None else [])
        extra_specs = [row_spec, gain_spec] + ([row_spec] if dres is not None else [])
        out_specs = [row_spec, row_spec, gain_spec]
        out_shapes = [jax.ShapeDtypeStruct(h.shape, F32), jax.ShapeDtypeStruct(h.shape, BF16),
                      jax.ShapeDtypeStruct(gain.shape, F32)]
    elif norm_out is not None:
        extras, extra_specs = extras + [norm_out[0]], extra_specs + [norm_out[1]]
        out_specs = [o_spec, o_spec]
        out_shapes = [jax.ShapeDtypeStruct(out_shape, out_dtype), jax.ShapeDtypeStruct(out_shape, BF16)]
    else:
        out_specs, out_shapes = o_spec, jax.ShapeDtypeStruct(out_shape, out_dtype)
    n_extra = len(extras)

    def body(*refs):
        a_ref, b_ref = refs[:2]
        extra_refs = refs[2:2 + n_extra]
        out_refs = refs[2 + n_extra:len(refs) - (1 if nk > 1 else 0)]
        p = lax.dot_general(a_ref[...], b_ref[...], dn, preferred_element_type=F32)

        def finish(r):
            if norm_bwd is not None:
                _rms_bwd_math(r, extra_refs[0], extra_refs[1], extra_refs[2] if n_extra == 3 else None,
                              *out_refs, first=pl.program_id(0) == 0)
                return
            if res is not None:
                r = r + extra_refs[0][...]
            out_refs[0][...] = r.astype(out_refs[0].dtype)
            if norm_out is not None:
                scale = lax.rsqrt(jnp.mean(r * r, axis=-1, keepdims=True) + EPS)
                out_refs[1][...] = ((r * scale) * extra_refs[-1][...]).astype(BF16)

        if nk == 1:
            finish(p)
        else:
            acc_ref = refs[-1]
            k = pl.program_id(2)

            @pl.when(k == 0)
            def _():
                acc_ref[...] = p

            @pl.when(k > 0)
            def _():
                acc_ref[...] += p

            @pl.when(k == nk - 1)
            def _():
                finish(acc_ref[...])

    scratch = [pltpu.VMEM(acc_shape, F32)] if nk > 1 else []
    return pl.pallas_call(
        body, grid=grid, in_specs=[a_spec, b_spec] + extra_specs, out_specs=out_specs,
        out_shape=out_shapes, scratch_shapes=scratch, name=name,
        compiler_params=_params("arbitrary" if norm_bwd is not None else "parallel", "parallel", "arbitrary"),
    )(a, b, *extras)


def _rms_bwd_math(dy, h_ref, g_ref, dres_ref, dh_ref, dhb_ref, dg_ref, *, first):
    x = h_ref[...]
    r = lax.rsqrt(jnp.mean(x * x, axis=-1, keepdims=True) + EPS)
    a = dy * g_ref[...]
    m = jnp.mean(a * x, axis=-1, keepdims=True)
    dh = r * a - x * (r * r * r * m)
    if dres_ref is not None:
        dh = dh + dres_ref[...]
    dh_ref[...] = dh
    dhb_ref[...] = dh.astype(BF16)
    part = jnp.sum(dy * (x * r), axis=0, keepdims=True)

    @pl.when(first)
    def _():
        dg_ref[...] = part

    @pl.when(jnp.logical_not(first))
    def _():
        dg_ref[...] += part


def _mm_nn(a, b, *, out_dtype=F32, name, res=None, tm=1024, norm_gain=None, norm_bwd=None):
    m, k = a.shape
    n = b.shape[1]
    tm = _tile(m, tm)
    spec_o = pl.BlockSpec((tm, n), lambda i, j, kk: (i, 0))
    norm_out = None if norm_gain is None else (norm_gain.reshape(1, n), pl.BlockSpec((1, n), lambda i, j, kk: (0, 0)))
    return _mm(a, b, dims="nn", grid=(m // tm, 1, 1),
               a_spec=pl.BlockSpec((tm, k), lambda i, j, kk: (i, 0)),
               b_spec=pl.BlockSpec((k, n), lambda i, j, kk: (0, 0)),
               o_spec=spec_o, out_shape=(m, n), out_dtype=out_dtype, acc_shape=None, name=name,
               res=res, res_spec=spec_o if res is not None else None, norm_out=norm_out,
               norm_bwd=None if norm_bwd is None else _norm_bwd_arg(*norm_bwd, tm))


def _norm_bwd_arg(h, gain, dres, tm):
    d = h.shape[1]
    return (h, gain.reshape(1, d), dres, pl.BlockSpec((tm, d), lambda i, j, kk: (i, 0)),
            pl.BlockSpec((1, d), lambda i, j, kk: (0, 0)))


def _mm_nt(a, b, *, out_dtype=F32, name, res=None, tm=1024, norm_bwd=None):
    m, k = a.shape
    n = b.shape[0]
    tm = _tile(m, tm)
    spec_o = pl.BlockSpec((tm, n), lambda i, j, kk: (i, 0))
    return _mm(a, b, dims="nt", grid=(m // tm, 1, 1),
               a_spec=pl.BlockSpec((tm, k), lambda i, j, kk: (i, 0)),
               b_spec=pl.BlockSpec((n, k), lambda i, j, kk: (0, 0)),
               o_spec=spec_o, out_shape=(m, n), out_dtype=out_dtype, acc_shape=None, name=name,
               res=res, res_spec=spec_o if res is not None else None,
               norm_bwd=None if norm_bwd is None else _norm_bwd_arg(*norm_bwd, tm))


_TN_ROWS = 2048


def _mm_tn(a, b, *, name, tm=None, tn=None, ts=_TN_ROWS, out_dtype=BF16):
    s, m = a.shape
    n = b.shape[1]
    tm = m if tm is None else tm
    tn = n if tn is None else tn
    ts = _tile(s, ts)
    return _mm(a, b, dims="tn", grid=(m // tm, n // tn, s // ts),
               a_spec=pl.BlockSpec((ts, tm), lambda i, j, kk: (kk, i)),
               b_spec=pl.BlockSpec((ts, tn), lambda i, j, kk: (kk, j)),
               o_spec=pl.BlockSpec((tm, tn), lambda i, j, kk: (i, j)),
               out_shape=(m, n), out_dtype=out_dtype, acc_shape=(tm, tn), name=name)


def _mm_up_back(dhh, w_up_t, *, name, tm=512, norm_bwd=None):
    _, s, f = dhh.shape
    d = w_up_t.shape[1]
    tm = _tile(s, tm)
    return _mm(dhh, w_up_t, dims="nn", grid=(s // tm, 1, 2),
               a_spec=pl.BlockSpec((None, tm, f), lambda i, j, kk: (kk, i, 0)),
               b_spec=pl.BlockSpec((f, d), lambda i, j, kk: (kk, 0)),
               o_spec=pl.BlockSpec((tm, d), lambda i, j, kk: (i, 0)),
               out_shape=(s, d), out_dtype=F32, acc_shape=(tm, d), name=name,
               norm_bwd=None if norm_bwd is None else _norm_bwd_arg(*norm_bwd, tm))


def _mm_up_grad(dhh, xn, *, name, ts=_TN_ROWS):
    s, d = xn.shape
    f = dhh.shape[2]
    tm = f // 2
    ts = _tile(s, ts)
    return _mm(dhh, xn, dims="tn", grid=(4, 1, s // ts),
               a_spec=pl.BlockSpec((None, ts, tm), lambda i, j, kk: (i // 2, kk, i % 2)),
               b_spec=pl.BlockSpec((ts, d), lambda i, j, kk: (kk, 0)),
               o_spec=pl.BlockSpec((tm, d), lambda i, j, kk: (i, 0)),
               out_shape=(2 * f, d), out_dtype=BF16, acc_shape=(tm, d), name=name)


def _rms_fwd(h, g, *, name, tr=512):
    s, d = h.shape
    tr = _tile(s, tr)

    def body(h_ref, g_ref, o_ref):
        x = h_ref[...]
        r = lax.rsqrt(jnp.mean(x * x, axis=-1, keepdims=True) + EPS)
        o_ref[...] = ((x * r) * g_ref[...]).astype(o_ref.dtype)

    row = pl.BlockSpec((tr, d), lambda i: (i, 0))
    return pl.pallas_call(
        body, grid=(s // tr,), in_specs=[row, pl.BlockSpec((1, d), lambda i: (0, 0))], out_specs=row,
        out_shape=jax.ShapeDtypeStruct((s, d), BF16), name=name, compiler_params=_params("parallel"),
    )(h, g.reshape(1, d))


def _rms_bwd(h, dxn, g, dres, *, name, tr=512):
    s, d = h.shape
    tr = _tile(s, tr)
    has_res = dres is not None

    def body(*refs):
        if has_res:
            h_ref, dxn_ref, g_ref, dres_ref, dh_ref, dhb_ref, dg_ref = refs
        else:
            h_ref, dxn_ref, g_ref, dh_ref, dhb_ref, dg_ref = refs
            dres_ref = None
        _rms_bwd_math(dxn_ref[...].astype(F32), h_ref, g_ref, dres_ref, dh_ref, dhb_ref, dg_ref,
                      first=pl.program_id(0) == 0)

    row = pl.BlockSpec((tr, d), lambda i: (i, 0))
    vec = pl.BlockSpec((1, d), lambda i: (0, 0))
    in_specs = [row, row, vec] + ([row] if has_res else [])
    args = (h, dxn, g.reshape(1, d)) + ((dres,) if has_res else ())
    return pl.pallas_call(
        body, grid=(s // tr,), in_specs=in_specs, out_specs=[row, row, vec],
        out_shape=[jax.ShapeDtypeStruct((s, d), F32), jax.ShapeDtypeStruct((s, d), BF16),
                   jax.ShapeDtypeStruct((1, d), F32)],
        name=name, compiler_params=_params("arbitrary"),
    )(*args)


def _loss_head(h, g, target, *, name, tr=512):
    s, d = h.shape
    tr = _tile(s, tr)
    nt = s // tr

    def body(h_ref, g_ref, t_ref, dh_ref, dhb_ref, dg_ref, loss_ref, sq_ref):
        i = pl.program_id(0)
        x = h_ref[...]
        gain = g_ref[...]
        r = lax.rsqrt(jnp.mean(x * x, axis=-1, keepdims=True) + EPS)
        xh = x * r
        err = xh * gain - t_ref[...]
        dy = err * (1.0 / d)
        a = dy * gain
        m = jnp.mean(a * x, axis=-1, keepdims=True)
        dh = r * a - x * (r * r * r * m)
        dh_ref[...] = dh
        dhb_ref[...] = dh.astype(BF16)
        dg_part = jnp.sum(dy * xh, axis=0, keepdims=True)
        sq_part = jnp.sum(err * err, axis=0, keepdims=True)

        @pl.when(i == 0)
        def _():
            dg_ref[...] = dg_part
            sq_ref[...] = sq_part

        @pl.when(i > 0)
        def _():
            dg_ref[...] += dg_part
            sq_ref[...] += sq_part

        @pl.when(i == nt - 1)
        def _():
            total = jnp.sum(sq_ref[...], axis=1, keepdims=True) * (0.5 / d)
            loss_ref[...] = jnp.broadcast_to(total, loss_ref.shape)

    row = pl.BlockSpec((tr, d), lambda i: (i, 0))
    vec = pl.BlockSpec((1, d), lambda i: (0, 0))
    return pl.pallas_call(
        body, grid=(nt,), in_specs=[row, vec, row],
        out_specs=[row, row, vec, pl.BlockSpec((1, 128), lambda i: (0, 0))],
        out_shape=[jax.ShapeDtypeStruct((s, d), F32), jax.ShapeDtypeStruct((s, d), BF16),
                   jax.ShapeDtypeStruct((1, d), F32), jax.ShapeDtypeStruct((1, 128), F32)],
        scratch_shapes=[pltpu.VMEM((1, d), F32)], name=name, compiler_params=_params("arbitrary"),
    )(h, g.reshape(1, d), target)


_SQRT_HALF = 0.7071067811865476
_INV_SQRT_2PI = 0.3989422804014327


def _gelu(x):
    return 0.5 * x * (1.0 + lax.erf(x * _SQRT_HALF))


def _gelu_and_grad(x):
    cdf = 0.5 * (1.0 + lax.erf(x * _SQRT_HALF))
    return x * cdf, cdf + x * (jnp.exp(-0.5 * x * x) * _INV_SQRT_2PI)


def _trailing_sums(xe, win):
    s = xe
    sh = 1
    while sh < win:
        s = s + pltpu.roll(s, sh, 0)
        sh *= 2
    return s


def _leading_sums(xe, win):
    n = xe.shape[0]
    s = xe
    sh = 1
    while sh < win:
        s = s + pltpu.roll(s, n - sh, 0)
        sh *= 2
    return s


def _tril_mask():
    return lax.broadcasted_iota(jnp.int32, (GROUP, GROUP), 0) >= lax.broadcasted_iota(jnp.int32, (GROUP, GROUP), 1)


def _layernorm_stats(v):
    mu = jnp.mean(v, axis=-1, keepdims=True)
    xc = v - mu
    rstd = lax.rsqrt(jnp.mean(xc * xc, axis=-1, keepdims=True) + EPS)
    return xc * rstd, rstd


def _mixer_specs(s, t):
    halo_blocks = t // POOL_HALO
    tile = lambda w: pl.BlockSpec((t, w), lambda i: (i, 0))
    prev = pl.BlockSpec((POOL_HALO, POOL_WIDTH), lambda i: (jnp.maximum(i * halo_blocks - 1, 0), 0))
    nxt = pl.BlockSpec((POOL_HALO, POOL_WIDTH),
                       lambda i: (jnp.minimum((i + 1) * halo_blocks, s // POOL_HALO - 1), 0))
    const3 = pl.BlockSpec((HEADS, GROUP, GROUP), lambda i: (0, 0, 0))
    vec = pl.BlockSpec((1, POOL_WIDTH), lambda i: (0, 0))
    bias = pl.BlockSpec((GROUP, SGU_WIDTH), lambda i: (0, 0))
    return tile, prev, nxt, const3, vec, bias


def _mixer_fwd(proj, pool_w, pool_scale, sgu_g, sgu_w, sgu_bias, *, name, t=256):
    s = proj.shape[0]
    t = _tile(s, t)
    tile, prev, _, const3, vec, bias = _mixer_specs(s, t)

    def body(proj_ref, halo_ref, pw_ref, ps_ref, sg_ref, sw_ref, sb_ref, cat_ref):
        i = pl.program_id(0)
        row = i * t + lax.broadcasted_iota(jnp.int32, (t, 1), 0)
        p = proj_ref[:, 0:POOL_WIDTH]
        pe = jnp.concatenate([jnp.where(i > 0, halo_ref[...], 0.0), p], axis=0)
        for gi, win in enumerate(POOL_WINDOWS):
            cols = slice(gi * GROUP, (gi + 1) * GROUP)
            count = jnp.minimum(row + 1, win).astype(F32)
            d = _trailing_sums(pe[:, cols], win)[POOL_HALO:] / count - p[:, cols]
            y = jnp.dot(d.astype(BF16), pw_ref[gi].astype(BF16), preferred_element_type=F32) * ps_ref[:, cols]
            cat_ref[:, cols] = y.astype(BF16)

        u = _gelu(proj_ref[:, POOL_WIDTH:POOL_WIDTH + SGU_WIDTH])
        xhat, _ = _layernorm_stats(_gelu(proj_ref[:, POOL_WIDTH + SGU_WIDTH:]))
        vn = (xhat * sg_ref[...]).astype(BF16)
        tri = _tril_mask()
        for h in range(HEADS):
            cols = slice(h * GROUP, (h + 1) * GROUP)
            w = jnp.where(tri, sw_ref[h], 0.0).astype(BF16)
            for c in range(t // GROUP):
                rows = slice(c * GROUP, (c + 1) * GROUP)
                z = jnp.dot(w, vn[rows, cols], preferred_element_type=F32) + sb_ref[:, cols]
                cat_ref[rows, POOL_WIDTH + h * GROUP:POOL_WIDTH + (h + 1) * GROUP] = (u[rows, cols] * z).astype(BF16)

    return pl.pallas_call(
        body, grid=(s // t,),
        in_specs=[tile(POOL_WIDTH + 2 * SGU_WIDTH), prev, const3, vec, vec, const3, bias],
        out_specs=tile(POOL_WIDTH + SGU_WIDTH),
        out_shape=jax.ShapeDtypeStruct((s, POOL_WIDTH + SGU_WIDTH), BF16), name=name,
        compiler_params=_params("parallel"),
    )(proj, proj, pool_w, pool_scale, sgu_g, sgu_w, sgu_bias)


def _mixer_bwd(proj, dcat, pool_w, pool_scale, sgu_g, sgu_w, sgu_bias, *, name, t=256):
    s = proj.shape[0]
    t = _tile(s, t)
    nt = s // t
    tile, prev, nxt, const3, vec, bias = _mixer_specs(s, t)

    def body(proj_ref, halo_ref, dcat_ref, dnext_ref, pw_ref, ps_ref, sg_ref, sw_ref, sb_ref,
             dproj_ref, dpw_ref, dps_ref, dsg_ref, dsw_ref, dsb_ref, du_ref, dvn_ref, dz_ref):
        i = pl.program_id(0)

        @pl.when(i == 0)
        def _():
            dpw_ref[...] = jnp.zeros_like(dpw_ref)
            dps_ref[...] = jnp.zeros_like(dps_ref)
            dsg_ref[...] = jnp.zeros_like(dsg_ref)
            dsw_ref[...] = jnp.zeros_like(dsw_ref)
            dz_ref[...] = jnp.zeros_like(dz_ref)

        row = i * t + lax.broadcasted_iota(jnp.int32, (t, 1), 0)
        row_e = i * t + lax.broadcasted_iota(jnp.int32, (t + POOL_HALO, 1), 0)
        p = proj_ref[:, 0:POOL_WIDTH]
        pe = jnp.concatenate([jnp.where(i > 0, halo_ref[...], 0.0), p], axis=0)
        dyp = dcat_ref[:, 0:POOL_WIDTH]
        dye = jnp.concatenate([dyp, jnp.where(i < nt - 1, dnext_ref[...], 0.0)], axis=0)
        for gi, win in enumerate(POOL_WINDOWS):
            cols = slice(gi * GROUP, (gi + 1) * GROUP)
            count = jnp.minimum(row + 1, win).astype(F32)
            d = (_trailing_sums(pe[:, cols], win)[POOL_HALO:] / count - p[:, cols]).astype(BF16)
            pw = pw_ref[gi].astype(BF16)
            pre = jnp.dot(d, pw, preferred_element_type=F32)
            dps_ref[:, cols] += jnp.sum(dyp[:, cols] * pre, axis=0, keepdims=True)
            ys = (dye[:, cols] * ps_ref[:, cols]).astype(BF16)
            dpw_ref[gi] += lax.dot_general(d, ys[:t], _DOT_DIMS["tn"], preferred_element_type=F32)
            dd = lax.dot_general(ys, pw, _DOT_DIMS["nt"], preferred_element_type=F32)
            count_e = jnp.minimum(row_e + 1, win).astype(F32)
            dp = _leading_sums(dd / count_e, win)[:t] - dd[:t]
            dproj_ref[:, cols] = dp.astype(BF16)

        xu = proj_ref[:, POOL_WIDTH:POOL_WIDTH + SGU_WIDTH]
        xv = proj_ref[:, POOL_WIDTH + SGU_WIDTH:]
        u, gelu_grad_u = _gelu_and_grad(xu)
        v, gelu_grad_v = _gelu_and_grad(xv)
        xhat, rstd = _layernorm_stats(v)
        gain = sg_ref[...]
        vn = (xhat * gain).astype(BF16)
        tri = _tril_mask()
        for h in range(HEADS):
            cols = slice(h * GROUP, (h + 1) * GROUP)
            wf = jnp.where(tri, sw_ref[h], 0.0)
            w, wt = wf.astype(BF16), wf.T.astype(BF16)
            for c in range(t // GROUP):
                rows = slice(c * GROUP, (c + 1) * GROUP)
                vch = vn[rows, cols]
                z = jnp.dot(w, vch, preferred_element_type=F32) + sb_ref[:, cols]
                dy = dcat_ref[rows, POOL_WIDTH + h * GROUP:POOL_WIDTH + (h + 1) * GROUP]
                du_ref[rows, cols] = dy * z
                dz = dy * u[rows, cols]
                dz_ref[:, cols] += dz
                dzb = dz.astype(BF16)
                dsw_ref[h] += lax.dot_general(dzb, vch, _DOT_DIMS["nt"], preferred_element_type=F32)
                dvn_ref[rows, cols] = jnp.dot(wt, dzb, preferred_element_type=F32)
        dvn = dvn_ref[...]
        dsg_ref[...] += jnp.sum(dvn * xhat, axis=0, keepdims=True)
        dxh = dvn * gain
        dv = rstd * (dxh - jnp.mean(dxh, axis=-1, keepdims=True)
                     - xhat * jnp.mean(dxh * xhat, axis=-1, keepdims=True))
        dproj_ref[:, POOL_WIDTH:POOL_WIDTH + SGU_WIDTH] = (du_ref[...] * gelu_grad_u).astype(BF16)
        dproj_ref[:, POOL_WIDTH + SGU_WIDTH:] = (dv * gelu_grad_v).astype(BF16)

        @pl.when(i == nt - 1)
        def _():
            for h in range(HEADS):
                dsw_ref[h] = jnp.where(tri, dsw_ref[h], 0.0)
            lane = lax.broadcasted_iota(jnp.int32, (GROUP, GROUP), 1)
            out = jnp.zeros((GROUP, GROUP), F32)
            for h in range(HEADS):
                sh = jnp.sum(dz_ref[:, h * GROUP:(h + 1) * GROUP], axis=1, keepdims=True)
                out = jnp.where(lane == h, sh, out)
            dsb_ref[...] = out

    outs = pl.pallas_call(
        body, grid=(nt,),
        in_specs=[tile(POOL_WIDTH + 2 * SGU_WIDTH), prev, tile(POOL_WIDTH + SGU_WIDTH), nxt,
                  const3, vec, vec, const3, bias],
        out_specs=[tile(POOL_WIDTH + 2 * SGU_WIDTH), const3, vec, vec, const3,
                   pl.BlockSpec((GROUP, GROUP), lambda i: (0, 0))],
        out_shape=[jax.ShapeDtypeStruct((s, POOL_WIDTH + 2 * SGU_WIDTH), BF16),
                   jax.ShapeDtypeStruct((HEADS, GROUP, GROUP), F32),
                   jax.ShapeDtypeStruct((1, POOL_WIDTH), F32),
                   jax.ShapeDtypeStruct((1, SGU_WIDTH), F32),
                   jax.ShapeDtypeStruct((HEADS, GROUP, GROUP), F32),
                   jax.ShapeDtypeStruct((GROUP, GROUP), F32)],
        scratch_shapes=[pltpu.VMEM((t, SGU_WIDTH), F32), pltpu.VMEM((t, SGU_WIDTH), F32),
                        pltpu.VMEM((GROUP, SGU_WIDTH), F32)],
        name=name, compiler_params=_params("arbitrary"),
    )(proj, proj, dcat, dcat, pool_w, pool_scale, sgu_g, sgu_w, sgu_bias)
    dproj, dpw, dps, dsg, dsw, dsb = outs
    return dproj, dpw, dps, dsg, dsw, dsb[:, :HEADS].T


def _attn_probs(q, k, scale):
    sc = lax.dot_general(q, k, _DOT_DIMS["nt"], preferred_element_type=F32) * scale
    sc = sc - jnp.max(sc, axis=-1, keepdims=True)
    e = jnp.exp(sc)
    return e / jnp.sum(e, axis=-1, keepdims=True)


def _attn_fwd(q, k, v, *, name, t=512):
    s, d = q.shape
    nm = k.shape[0]
    t = _tile(s, t)
    scale = HEAD_DIM ** -0.5

    def body(q_ref, k_ref, v_ref, o_ref):
        for h in range(HEADS):
            cols = slice(h * HEAD_DIM, (h + 1) * HEAD_DIM)
            pr = _attn_probs(q_ref[:, cols], k_ref[:, cols], scale)
            o_ref[:, cols] = jnp.dot(pr.astype(BF16), v_ref[:, cols], preferred_element_type=F32).astype(BF16)

    row = pl.BlockSpec((t, d), lambda i: (i, 0))
    kv = pl.BlockSpec((nm, d), lambda i: (0, 0))
    return pl.pallas_call(
        body, grid=(s // t,), in_specs=[row, kv, kv], out_specs=row,
        out_shape=jax.ShapeDtypeStruct((s, d), BF16), name=name, compiler_params=_params("parallel"),
    )(q, k, v)


def _attn_bwd(q, k, v, do, *, name, t=512):
    s, d = q.shape
    nm = k.shape[0]
    t = _tile(s, t)
    scale = HEAD_DIM ** -0.5

    def body(q_ref, k_ref, v_ref, do_ref, dq_ref, dk_ref, dv_ref):
        i = pl.program_id(0)

        @pl.when(i == 0)
        def _():
            dk_ref[...] = jnp.zeros_like(dk_ref)
            dv_ref[...] = jnp.zeros_like(dv_ref)

        for h in range(HEADS):
            cols = slice(h * HEAD_DIM, (h + 1) * HEAD_DIM)
            qh, kh, vh, doh = q_ref[:, cols], k_ref[:, cols], v_ref[:, cols], do_ref[:, cols]
            pr = _attn_probs(qh, kh, scale)
            dpr = lax.dot_general(doh, vh, _DOT_DIMS["nt"], preferred_element_type=F32)
            ds = (pr * (dpr - jnp.sum(dpr * pr, axis=-1, keepdims=True)) * scale).astype(BF16)
            dv_ref[:, cols] += lax.dot_general(pr.astype(BF16), doh, _DOT_DIMS["tn"], preferred_element_type=F32)
            dk_ref[:, cols] += lax.dot_general(ds, qh, _DOT_DIMS["tn"], preferred_element_type=F32)
            dq_ref[:, cols] = jnp.dot(ds, kh, preferred_element_type=F32).astype(BF16)

    row = pl.BlockSpec((t, d), lambda i: (i, 0))
    kv = pl.BlockSpec((nm, d), lambda i: (0, 0))
    return pl.pallas_call(
        body, grid=(s // t,), in_specs=[row, kv, kv, row], out_specs=[row, kv, kv],
        out_shape=[jax.ShapeDtypeStruct((s, d), BF16), jax.ShapeDtypeStruct((nm, d), F32),
                   jax.ShapeDtypeStruct((nm, d), F32)],
        name=name, compiler_params=_params("arbitrary"),
    )(q, k, v, do)


def _conv3(w_ref, p, x2, x1, x0, b):
    return (w_ref[p, 0:1, :] * x2 + w_ref[p, 1:2, :] * x1 + w_ref[p, 2:3, :] * x0) + b


def _ffn_up_gate(xn, w_up_t, cw, cb, *, name, t=256, tc=1408):
    s, d = xn.shape
    f = w_up_t.shape[0] // 2
    t = _tile(s, t)
    nt, nj = s // t, f // tc

    def body(xn_ref, wg_ref, wv_ref, cw_ref, cb_ref, hh_ref, act_ref, held_ref, above_ref, w_ref):
        i = pl.program_id(1)

        @pl.when(i == 0)
        def _():
            held_ref[...] = jnp.zeros_like(held_ref)
            above_ref[...] = jnp.zeros_like(above_ref)
            w_ref[0] = wg_ref[...].astype(F32).T.astype(BF16)
            w_ref[1] = wv_ref[...].astype(F32).T.astype(BF16)

        hc = []
        for p in range(2):
            xe = jnp.concatenate([above_ref[p], held_ref[p]], axis=0)
            hc.append(_conv3(cw_ref, p, pltpu.roll(xe, 2, 0), pltpu.roll(xe, 1, 0), xe, cb_ref[p])[CONV_HALO:])
            above_ref[p] = held_ref[p, t - CONV_HALO:t, :]
        gate, val = hc
        act_ref[...] = ((gate * jax.nn.sigmoid(gate)) * val).astype(BF16)

        x = xn_ref[...]
        for p in range(2):
            y = jnp.dot(x, w_ref[p], preferred_element_type=F32)
            hh_ref[p] = y
            held_ref[p] = y

    row = lambda i: jnp.minimum(i, nt - 1)
    return pl.pallas_call(
        body, grid=(nj, nt + 1),
        in_specs=[pl.BlockSpec((t, d), lambda j, i: (row(i), 0)),
                  pl.BlockSpec((tc, d), lambda j, i: (j, 0)),
                  pl.BlockSpec((tc, d), lambda j, i: (nj + j, 0)),
                  pl.BlockSpec((2, 3, tc), lambda j, i: (0, 0, j)),
                  pl.BlockSpec((2, 1, tc), lambda j, i: (0, 0, j))],
        out_specs=[pl.BlockSpec((2, t, tc), lambda j, i: (0, row(i), j)),
                   pl.BlockSpec((t, tc), lambda j, i: (jnp.maximum(i - 1, 0), j))],
        out_shape=[jax.ShapeDtypeStruct((2, s, f), F32), jax.ShapeDtypeStruct((s, f), BF16)],
        scratch_shapes=[pltpu.VMEM((2, t, tc), F32), pltpu.VMEM((2, CONV_HALO, tc), F32), pltpu.VMEM((2, d, tc), BF16)],
        name=name, compiler_params=_params("arbitrary", "arbitrary"),
    )(xn, w_up_t, w_up_t, cw, cb)


def _convgate_bwd(hh, dact, cw, cb, *, name, t=256, tc=1408):
    _, s, f = hh.shape
    t = _tile(s, t)
    nt, nj = s // t, f // tc
    hb = t // CONV_HALO
    m = t + CONV_HALO

    def body(hh_ref, prev_ref, next_ref, da_ref, danext_ref, cw_ref, cb_ref, dhh_ref, dcw_ref, dcb_ref):
        i = pl.program_id(1)
        is_last = i == nt - 1

        @pl.when(i == 0)
        def _():
            dcw_ref[...] = jnp.zeros_like(dcw_ref)
            dcb_ref[...] = jnp.zeros_like(dcb_ref)

        taps, hc = [], []
        for p in range(2):
            xe = jnp.concatenate([jnp.where(i > 0, prev_ref[p], 0.0), hh_ref[p],
                                  jnp.where(is_last, 0.0, next_ref[p])], axis=0)
            x2, x1 = pltpu.roll(xe, 2, 0), pltpu.roll(xe, 1, 0)
            hc.append(_conv3(cw_ref, p, x2, x1, xe, cb_ref[p])[CONV_HALO:])
            taps.append((x2[CONV_HALO:CONV_HALO + t], x1[CONV_HALO:CONV_HALO + t], xe[CONV_HALO:CONV_HALO + t]))
        gate, val = hc
        dae = jnp.concatenate([da_ref[...], jnp.where(is_last, 0.0, danext_ref[...])], axis=0)
        sg = jax.nn.sigmoid(gate)
        dval = dae * (gate * sg)
        dgate = dae * val * (sg * (1.0 + gate * (1.0 - sg)))
        for p, dhc in enumerate((dgate, dval)):
            dh = (cw_ref[p, 2:3, :] * dhc + cw_ref[p, 1:2, :] * pltpu.roll(dhc, m - 1, 0)
                  + cw_ref[p, 0:1, :] * pltpu.roll(dhc, m - 2, 0))
            dhh_ref[p] = dh[:t].astype(BF16)
            d0 = dhc[:t]
            for kk, tap in enumerate(taps[p]):
                dcw_ref[p, kk:kk + 1, :] += jnp.sum(d0 * tap, axis=0, keepdims=True)
            dcb_ref[p] += jnp.sum(d0, axis=0, keepdims=True)

    below = lambda i: jnp.minimum((i + 1) * hb, s // CONV_HALO - 1)
    return pl.pallas_call(
        body, grid=(nj, nt),
        in_specs=[pl.BlockSpec((2, t, tc), lambda j, i: (0, i, j)),
                  pl.BlockSpec((2, CONV_HALO, tc), lambda j, i: (0, jnp.maximum(i * hb - 1, 0), j)),
                  pl.BlockSpec((2, CONV_HALO, tc), lambda j, i: (0, below(i), j)),
                  pl.BlockSpec((t, tc), lambda j, i: (i, j)),
                  pl.BlockSpec((CONV_HALO, tc), lambda j, i: (below(i), j)),
                  pl.BlockSpec((2, 3, tc), lambda j, i: (0, 0, j)),
                  pl.BlockSpec((2, 1, tc), lambda j, i: (0, 0, j))],
        out_specs=[pl.BlockSpec((2, t, tc), lambda j, i: (0, i, j)),
                   pl.BlockSpec((2, 3, tc), lambda j, i: (0, 0, j)),
                   pl.BlockSpec((2, 1, tc), lambda j, i: (0, 0, j))],
        out_shape=[jax.ShapeDtypeStruct((2, s, f), BF16), jax.ShapeDtypeStruct((2, 3, f), F32),
                   jax.ShapeDtypeStruct((2, 1, f), F32)],
        name=name, compiler_params=_params("parallel", "arbitrary"),
    )(hh, hh, hh, dact, dact, cw, cb)


def _position():
    return lax.axis_index("x"), lax.axis_index("y"), lax.axis_index("c")


def _linear(px, py, pc):
    return 4 * px + 2 * py + pc


def _peers_of(x, y, c):
    peers = []
    for mask in range(1, N_DEV):
        peers.append((1 - x if mask & 4 else x, 1 - y if mask & 2 else y, 1 - c if mask & 1 else c))
    return peers


def _exchange_copy(src_ref, land_ref, send_sem, recv_sem, peer, mine, scatter, arriving):
    src = src_ref.at[_linear(*peer)] if scatter else src_ref
    dst = land_ref.at[_linear(*peer) if arriving else mine]
    return pltpu.make_async_remote_copy(src_ref=src, dst_ref=dst, send_sem=send_sem, recv_sem=recv_sem,
                                        device_id=peer, device_id_type=MESH)


_EXCHANGE_COLLECTIVE_ID = 7


def _sequencer_exchange(srcs, *, scatter, name):
    n = len(srcs)
    src_refs = [jax.new_ref(a, memory_space=pltpu.MemorySpace.HBM) for a in srcs]
    land_refs = [jax.empty_ref(jax.ShapeDtypeStruct(a.shape if scatter else (N_DEV,) + a.shape, a.dtype),
                               memory_space=pltpu.MemorySpace.HBM) for a in srcs]

    @pl.kernel(mesh=plsc.ScalarSubcoreMesh(axis_name="sequencer", num_cores=1), name=name,
               scratch_types=(pltpu.SemaphoreType.DMA((7 * n,)), pltpu.SemaphoreType.DMA((7 * n,)),
                              pltpu.SemaphoreType.DMA((n,))),
               compiler_params=pltpu.CompilerParams(collective_id=_EXCHANGE_COLLECTIVE_ID))
    def launch(send_sems, recv_sems, local_sems):
        x, y, c = _position()
        mine = _linear(x, y, c)
        peers = _peers_of(x, y, c)
        barrier = pltpu.get_barrier_semaphore()
        for peer in peers:
            pl.semaphore_signal(barrier, inc=1, device_id=peer, device_id_type=MESH)
        pl.semaphore_wait(barrier, N_DEV - 1)
        local = [pltpu.make_async_copy(src_refs[t].at[mine] if scatter else src_refs[t], land_refs[t].at[mine],
                                       local_sems.at[t]) for t in range(n)]
        for cp in local:
            cp.start()
        if scatter:
            sends = []
            for t in range(n):
                for k, peer in enumerate(peers):
                    cp = _exchange_copy(src_refs[t], land_refs[t], send_sems.at[7 * t + k], recv_sems.at[7 * t + k],
                                        peer, mine, scatter, arriving=False)
                    cp.start()
                    sends.append(cp)
            for t in range(n):
                for k, peer in enumerate(peers):
                    _exchange_copy(src_refs[t], land_refs[t], send_sems.at[7 * t + k], recv_sems.at[7 * t + k],
                                   peer, mine, scatter, arriving=True).wait_recv()
        else:
            me, sibling = (x, y, c), (x, y, 1 - c)
            chips = [(1 - x, y), (x, 1 - y), (1 - x, 1 - y)]

            def copy(t, k, block, to, src=None):
                dst = land_refs[t].at[_linear(*block)]
                return pltpu.make_async_remote_copy(
                    src_ref=dst if src is None else src, dst_ref=dst, send_sem=send_sems.at[7 * t + k],
                    recv_sem=recv_sems.at[7 * t + k], device_id=to, device_id_type=MESH)

            sends = []
            for t in range(n):
                sends.append(copy(t, 0, me, sibling, src=src_refs[t]))
                for j, chip in enumerate(chips):
                    sends.append(copy(t, 1 + j, me, (*chip, c), src=src_refs[t]))
            for cp in sends:
                cp.start()
            for j, chip in enumerate(chips):
                for t in range(n):
                    copy(t, 1 + j, (*chip, c), me).wait_recv()
                    passed = copy(t, 4 + j, (*chip, c), sibling)
                    passed.start()
                    sends.append(passed)
            for t in range(n):
                copy(t, 0, sibling, me).wait_recv()
                for j, chip in enumerate(chips):
                    copy(t, 4 + j, (*chip, 1 - c), me).wait_recv()
        for cp in local:
            cp.wait()
        for cp in sends:
            cp.wait_send()

    launch()
    return [r[...] for r in land_refs]


def _adamw_math(g, w, m, v):
    m2 = ADAM_B1 * m + (1.0 - ADAM_B1) * g
    v2 = ADAM_B2 * v + (1.0 - ADAM_B2) * (g * g)
    m_hat = m2 / (1.0 - ADAM_B1 ** ADAM_STEP)
    v_hat = v2 / (1.0 - ADAM_B2 ** ADAM_STEP)
    delta = -ADAM_LR * (m_hat / (jnp.sqrt(v_hat) + ADAM_EPS) + ADAM_WD * w)
    return delta, m2, v2


def _adamw(slots, w, m, v, *, name, tr=256):
    depth = len(slots)
    _, r, c = slots[0].shape
    tr = next((cand for cand in range(min(r, tr), 15, -1) if r % cand == 0 and cand % 16 == 0), r)

    def body(*refs):
        s_refs = refs[:depth]
        w_ref, m_ref, v_ref, g_ref, d_ref, m2_ref, v2_ref = refs[depth:]
        layer = pl.program_id(0)
        for l in range(depth):
            @pl.when(layer == l)
            def _():
                g = s_refs[l][0].astype(F32)
                for d in range(1, N_DEV):
                    g = g + s_refs[l][d].astype(F32)
                delta, m2, v2 = _adamw_math(g, w_ref[...], m_ref[...], v_ref[...])
                g_ref[...] = g
                d_ref[...] = delta
                m2_ref[...] = m2
                v2_ref[...] = v2

    blk = pl.BlockSpec((None, tr, c), lambda layer, i: (layer, i, 0))
    sblks = [pl.BlockSpec((N_DEV, tr, c), lambda layer, i, l=l: (0, jnp.where(layer == l, i, 0), 0))
             for l in range(depth)]
    shape = jax.ShapeDtypeStruct((depth, r, c), F32)
    return pl.pallas_call(
        body, grid=(depth, r // tr), in_specs=sblks + [blk, blk, blk], out_specs=[blk] * 4,
        out_shape=[shape] * 4, name=name, compiler_params=_params("arbitrary", "arbitrary"),
    )(*slots, w, m, v)


_SHARDED = ("w_in", "w_out", "wq", "wk", "wv", "wo", "w_up", "conv_w", "w_down")
_SMALL = ("norm_mix_g", "pool_w", "pool_scale", "sgu_g", "sgu_w", "sgu_b", "norm_xattn_g", "mem_norm_g",
          "norm_ffn_g", "conv_b", "final_norm_g")
_WEIGHTS = ("norm_mix_g", "w_in", "pool_w", "pool_scale", "sgu_g", "sgu_w", "sgu_b", "w_out", "norm_xattn_g",
            "mem_norm_g", "wq", "wk", "wv", "wo", "norm_ffn_g", "w_up", "conv_w", "conv_b", "w_down",
            "final_norm_g")
_PACK_LANES = 128
_GATHER_GROUPS = (("w_in",), ("w_out",), ("wq", "wk", "wv", "wo"), ("w_up", "conv_w", "w_down"))
_COLUMN_SHARDED = ("w_in", "w_up")


def _cols_to_blocks(a, *, name, tr=256):
    r, c8 = a.shape
    c = c8 // N_DEV
    tr = _tile(r, tr)

    def body(a_ref, o_ref):
        for dev in range(N_DEV):
            o_ref[dev] = a_ref[:, dev * c:(dev + 1) * c]

    return pl.pallas_call(
        body, grid=(r // tr,), in_specs=[pl.BlockSpec((tr, c8), lambda i: (i, 0))],
        out_specs=pl.BlockSpec((N_DEV, tr, c), lambda i: (0, i, 0)),
        out_shape=jax.ShapeDtypeStruct((N_DEV, r, c), a.dtype), name=name, compiler_params=_params("parallel"),
    )(a)


def _blocks_to_cols(a, *, name, tr=256):
    n, r, c = a.shape
    tr = _tile(r, tr)

    def body(a_ref, o_ref):
        for dev in range(n):
            o_ref[:, dev * c:(dev + 1) * c] = a_ref[dev]

    return pl.pallas_call(
        body, grid=(r // tr,), in_specs=[pl.BlockSpec((n, tr, c), lambda i: (0, i, 0))],
        out_specs=pl.BlockSpec((tr, n * c), lambda i: (i, 0)),
        out_shape=jax.ShapeDtypeStruct((r, n * c), a.dtype), name=name, compiler_params=_params("parallel"),
    )(a)


def _pin(x, *deps):
    return lax.optimization_barrier((x, *deps))[0]


def _pack(arrays):
    flat = jnp.concatenate([a.reshape(-1) for a in arrays])
    assert flat.shape[0] % (8 * _PACK_LANES) == 0
    return flat.reshape(-1, _PACK_LANES)


def _unpack(packed, like):
    flat = packed.reshape(-1)
    out, off = [], 0
    for a in like:
        out.append(flat[off:off + a.size].reshape(a.shape))
        off += a.size
    return out


def kernel(x, mem, norm_mix_g, w_in, pool_w, pool_scale, sgu_g, sgu_w, sgu_b, w_out, norm_xattn_g, mem_norm_g, wq, wk, wv, wo, norm_ffn_g, w_up, conv_w, conv_b, w_down, final_norm_g, loss_target, m_norm_mix_g, m_w_in, m_pool_w, m_pool_scale, m_sgu_g, m_sgu_w, m_sgu_b, m_w_out, m_norm_xattn_g, m_mem_norm_g, m_wq, m_wk, m_wv, m_wo, m_norm_ffn_g, m_w_up, m_conv_w, m_conv_b, m_w_down, m_final_norm_g, v_norm_mix_g, v_w_in, v_pool_w, v_pool_scale, v_sgu_g, v_sgu_w, v_sgu_b, v_w_out, v_norm_xattn_g, v_mem_norm_g, v_wq, v_wk, v_wv, v_wo, v_norm_ffn_g, v_w_up, v_conv_w, v_conv_b, v_w_down, v_final_norm_g):
    W = dict(norm_mix_g=norm_mix_g, w_in=w_in, pool_w=pool_w, pool_scale=pool_scale, sgu_g=sgu_g, sgu_w=sgu_w,
             sgu_b=sgu_b, w_out=w_out, norm_xattn_g=norm_xattn_g, mem_norm_g=mem_norm_g, wq=wq, wk=wk, wv=wv, wo=wo,
             norm_ffn_g=norm_ffn_g, w_up=w_up, conv_w=conv_w, conv_b=conv_b, w_down=w_down,
             final_norm_g=final_norm_g)
    M = dict(norm_mix_g=m_norm_mix_g, w_in=m_w_in, pool_w=m_pool_w, pool_scale=m_pool_scale, sgu_g=m_sgu_g,
             sgu_w=m_sgu_w, sgu_b=m_sgu_b, w_out=m_w_out, norm_xattn_g=m_norm_xattn_g, mem_norm_g=m_mem_norm_g,
             wq=m_wq, wk=m_wk, wv=m_wv, wo=m_wo, norm_ffn_g=m_norm_ffn_g, w_up=m_w_up, conv_w=m_conv_w,
             conv_b=m_conv_b, w_down=m_w_down, final_norm_g=m_final_norm_g)
    V = dict(norm_mix_g=v_norm_mix_g, w_in=v_w_in, pool_w=v_pool_w, pool_scale=v_pool_scale, sgu_g=v_sgu_g,
             sgu_w=v_sgu_w, sgu_b=v_sgu_b, w_out=v_w_out, norm_xattn_g=v_norm_xattn_g, mem_norm_g=v_mem_norm_g,
             wq=v_wq, wk=v_wk, wv=v_wv, wo=v_wo, norm_ffn_g=v_norm_ffn_g, w_up=v_w_up, conv_w=v_conv_w,
             conv_b=v_conv_b, w_down=v_w_down, final_norm_g=v_final_norm_g)

    s, d = x.shape[1], x.shape[2]
    f = w_down.shape[1] * N_DEV
    h = x.reshape(s, d)
    memx = mem.reshape(mem.shape[1], d)
    target = loss_target.reshape(s, d)

    gathered = {}

    def launch_gather(l, gi, after):
        if l >= DEPTH:
            return
        names = _GATHER_GROUPS[gi]
        shards = [W[nme][l] if nme == "conv_w" else
                  (W[nme][l].T if nme in _COLUMN_SHARDED else W[nme][l]).astype(BF16) for nme in names]
        if after is not None:
            shards[0], _ = lax.optimization_barrier((shards[0], after))
        gathered[l, gi] = dict(zip(names, _sequencer_exchange(shards, scatter=False, name=f"gather_{l}_{gi}")))

    launch_gather(0, 0, None)

    saved, full = [], []
    for l in range(DEPTH):
        sgu_bias = jnp.repeat(sgu_b[l].T, GROUP, axis=1)
        cb = conv_b[l].reshape(2, 1, f)
        if l == 0:
            xn1 = _rms_fwd(h, norm_mix_g[l], name=f"norm_mix_{l}")
            launch_gather(0, 1, xn1)
        w_in_t = gathered[l, 0]["w_in"].reshape(-1, d)
        proj = _mm_nt(xn1, w_in_t, out_dtype=F32, name=f"proj_in_{l}")
        if l == 0:
            launch_gather(0, 2, proj)
        cat = _mixer_fwd(proj, pool_w[l], pool_scale[l].reshape(1, -1), sgu_g[l].reshape(1, -1), sgu_w[l], sgu_bias,
                         name=f"mixer_{l}")
        if l == 0:
            launch_gather(0, 3, cat)
        w_out_f = gathered[l, 1]["w_out"].reshape(-1, d)
        h1, xn2 = _mm_nn(cat, w_out_f, out_dtype=F32, res=h, norm_gain=norm_xattn_g[l], name=f"proj_out_{l}")
        launch_gather(l + 1, 0, h1)
        g = gathered[l, 2]
        wq_f, wk_f, wv_f, wo_f = (g[nme].reshape(-1, d) for nme in ("wq", "wk", "wv", "wo"))
        q = _mm_nn(xn2, wq_f, out_dtype=BF16, name=f"q_{l}")
        launch_gather(l + 1, 1, q)
        memn = _rms_fwd(memx, mem_norm_g[l], name=f"norm_mem_{l}")
        k = _mm_nn(memn, wk_f, out_dtype=BF16, name=f"k_{l}")
        v = _mm_nn(memn, wv_f, out_dtype=BF16, name=f"v_{l}")
        o = _attn_fwd(q, k, v, name=f"attn_{l}")
        h2, xn3 = _mm_nn(o, wo_f, out_dtype=F32, res=h1, norm_gain=norm_ffn_g[l], name=f"attn_out_{l}")
        launch_gather(l + 1, 2, h2)
        g = gathered[l, 3]
        w_up_t = g["w_up"].reshape(-1, d)
        conv_w_f = _blocks_to_cols(g["conv_w"], name=f"conv_w_cols_{l}").reshape(3, 2, f).transpose(1, 0, 2)
        w_down_f = g["w_down"].reshape(-1, d)
        hh, act = _ffn_up_gate(xn3, w_up_t, conv_w_f, cb, name=f"ffn_up_gate_{l}")
        launch_gather(l + 1, 3, hh)
        if l + 1 < DEPTH:
            h3, xn1_next = _mm_nn(act, w_down_f, out_dtype=F32, res=h2, tm=512, norm_gain=norm_mix_g[l + 1],
                                  name=f"ffn_down_{l}")
        else:
            h3, xn1_next = _mm_nn(act, w_down_f, out_dtype=F32, res=h2, tm=512, name=f"ffn_down_{l}"), None
        full.append(dict(w_in_t=w_in_t, w_out=w_out_f, wq=wq_f, wk=wk_f, wv=wv_f, wo=wo_f, w_up_t=w_up_t,
                         conv_w=conv_w_f, w_down=w_down_f))
        saved.append(dict(h0=h, xn1=xn1, proj=proj, cat=cat, h1=h1, xn2=xn2, q=q, memn=memn, k=k, v=v, o=o, h2=h2,
                          xn3=xn3, hh=hh, act=act, sgu_bias=sgu_bias, cb=cb))
        h, xn1 = h3, xn1_next

    dh, dhb, dg_final, loss_row = _loss_head(h, final_norm_g, target, name="loss_head")

    slots = {nme: [None] * DEPTH for nme in _SHARDED}
    small = [None] * DEPTH

    previous = []

    def scatter(l, tag, names, parts):
        parts = [_pin(parts[0], *previous)] + parts[1:]
        arrived = _sequencer_exchange(parts, scatter=True, name=f"scatter_{tag}_{l}")
        previous[:] = arrived[:1]
        for nme, land in zip(names, arrived):
            slots[nme][l] = land
        return parts

    for l in reversed(range(DEPTH)):
        fw, sv = full[l], saved[l]
        dact = _mm_nt(dhb, fw["w_down"], out_dtype=F32, tm=512, name=f"d_act_{l}")
        g_w_down = _mm_tn(sv["act"], dhb, tm=f // 2, name=f"g_w_down_{l}")
        dhh, g_conv_w, g_conv_b = _convgate_bwd(sv["hh"], dact, fw["conv_w"], sv["cb"], name=f"d_convgate_{l}")
        g_w_up_t = _mm_up_grad(dhh, sv["xn3"], name=f"g_w_up_{l}")
        g_conv_w_cols = g_conv_w.transpose(1, 0, 2).reshape(3, 2 * f)
        parts = [g_w_up_t.reshape(N_DEV, -1, d),
                 _cols_to_blocks(g_conv_w_cols, name=f"g_conv_w_blocks_{l}"), g_w_down.reshape(N_DEV, -1, d)]
        parts = scatter(l, "ffn", ("w_up", "conv_w", "w_down"), parts)
        dh2, dh2b, g_norm_ffn = _mm_up_back(_pin(dhh, *parts), fw["w_up_t"], name=f"d_norm_ffn_{l}",
                                            norm_bwd=(sv["h2"], norm_ffn_g[l], dh))

        do = _mm_nt(dh2b, fw["wo"], out_dtype=BF16, name=f"d_o_{l}")
        g_wo = _mm_tn(sv["o"], dh2b, name=f"g_wo_{l}")
        dq, dk, dv = _attn_bwd(sv["q"], sv["k"], sv["v"], do, name=f"d_attn_{l}")
        dkb, dvb = dk.astype(BF16), dv.astype(BF16)
        g_wq = _mm_tn(sv["xn2"], dq, name=f"g_wq_{l}")
        g_wk = _mm_tn(sv["memn"], dkb, name=f"g_wk_{l}")
        g_wv = _mm_tn(sv["memn"], dvb, name=f"g_wv_{l}")
        parts = [g.reshape(N_DEV, -1, d) for g in (g_wq, g_wk, g_wv, g_wo)]
        parts = scatter(l, "attn", ("wq", "wk", "wv", "wo"), parts)
        dq = _pin(dq, *parts)
        dmemn = _mm_nt(dkb, fw["wk"], out_dtype=F32, name=f"d_memn_k_{l}")
        dmemn = _mm_nt(dvb, fw["wv"], out_dtype=F32, res=dmemn, name=f"d_memn_v_{l}")
        _, _, g_mem_norm = _rms_bwd(memx, dmemn, mem_norm_g[l], None, name=f"d_norm_mem_{l}")
        dh1, dh1b, g_norm_xattn = _mm_nt(dq, fw["wq"], name=f"d_norm_xattn_{l}",
                                         norm_bwd=(sv["h1"], norm_xattn_g[l], dh2))

        dcat = _mm_nt(dh1b, fw["w_out"], out_dtype=F32, name=f"d_cat_{l}")
        g_w_out = _mm_tn(sv["cat"], dh1b, name=f"g_w_out_{l}")
        dproj, g_pool_w, g_pool_scale, g_sgu_g, g_sgu_w, g_sgu_b = _mixer_bwd(
            sv["proj"], dcat, pool_w[l], pool_scale[l].reshape(1, -1), sgu_g[l].reshape(1, -1), sgu_w[l],
            sv["sgu_bias"], name=f"d_mixer_{l}")
        g_w_in_t = _mm_tn(dproj, sv["xn1"], name=f"g_w_in_{l}")
        parts = [g_w_in_t.reshape(N_DEV, -1, d), g_w_out.reshape(N_DEV, -1, d)]
        parts = scatter(l, "mix", ("w_in", "w_out"), parts)
        dh, dhb, g_norm_mix = _mm_nn(_pin(dproj, *parts), fw["w_in_t"], name=f"d_norm_mix_{l}",
                                     norm_bwd=(sv["h0"], norm_mix_g[l], dh1))

        small[l] = dict(norm_mix_g=g_norm_mix.reshape(-1), pool_w=g_pool_w, pool_scale=g_pool_scale.reshape(-1),
                        sgu_g=g_sgu_g.reshape(-1), sgu_w=g_sgu_w, sgu_b=g_sgu_b, norm_xattn_g=g_norm_xattn.reshape(-1),
                        mem_norm_g=g_mem_norm.reshape(-1), norm_ffn_g=g_norm_ffn.reshape(-1),
                        conv_b=g_conv_b.reshape(-1))
    grad_x = dh.reshape(x.shape)

    out = {}
    for nme in _SHARDED:
        view = (lambda a: jnp.swapaxes(a, 1, 2)) if nme in _COLUMN_SHARDED else (lambda a: a)
        w3 = view(W[nme])
        res = _adamw([sl.reshape((N_DEV,) + w3.shape[1:]) for sl in slots[nme]], w3, view(M[nme]), view(V[nme]),
                     name=f"adamw_{nme}")
        out[nme] = [view(r) for r in res]

    small_names = [n for n in _SMALL]
    contrib = []
    for nme in small_names:
        if nme == "final_norm_g":
            contrib.append(dg_final.reshape(-1))
        else:
            contrib.append(jnp.stack([small[l][nme] for l in range(DEPTH)]))
    tail = 8 * _PACK_LANES
    packed_g = _pack(contrib + [jnp.pad(loss_row[0, :1], (0, tail - 1))])
    (all_g,) = _sequencer_exchange([_pin(packed_g, *previous)], scatter=False, name="gather_small_grads")
    rows = packed_g.shape[0]
    loss = jnp.sum(all_g[:, rows - 8, 0])
    zeros = jnp.zeros((tail,), F32)
    res = _adamw([all_g], _pack([W[n] for n in small_names] + [zeros]).reshape(1, rows, -1),
                 _pack([M[n] for n in small_names] + [zeros]).reshape(1, rows, -1),
                 _pack([V[n] for n in small_names] + [zeros]).reshape(1, rows, -1), name="adamw_small", tr=rows // 2)
    unpacked = [_unpack(r, [W[n] for n in small_names]) for r in res]
    for i, nme in enumerate(small_names):
        out[nme] = [unpacked[j][i] for j in range(4)]

    grads = [out[n][0] for n in _WEIGHTS]
    deltas = [out[n][1] for n in _WEIGHTS]
    new_m = [out[n][2] for n in _WEIGHTS]
    new_v = [out[n][3] for n in _WEIGHTS]
    return (loss, grad_x, *grads, *deltas, *new_m, *new_v)
```

```python
import jax
import jax.numpy as jnp
from jax import lax
from jax.experimental import pallas as pl
from jax.experimental.pallas import tpu as pltpu
from jax.experimental.pallas import tpu_sc as plsc

F32 = jnp.float32
BF16 = jnp.bfloat16
MESH = pl.DeviceIdType.MESH

EPS = 1e-6
N_DEV = 8
DEPTH = 2
POOL_WINDOWS = (2, 4, 8, 16)
GROUP = 128
POOL_WIDTH = 512
SGU_WIDTH = 512
HEADS = 4
HEAD_DIM = 256
POOL_HALO = 16
CONV_HALO = 8

ADAM_LR = 0.001
ADAM_B1 = 0.9
ADAM_B2 = 0.999
ADAM_EPS = 1e-08
ADAM_WD = 0.01
ADAM_STEP = 10

VMEM_LIMIT_BYTES = 52 * 1024 * 1024


def _params(*semantics):
    return pltpu.CompilerParams(dimension_semantics=semantics, vmem_limit_bytes=VMEM_LIMIT_BYTES)


def _tile(n, want):
    t = min(n, want)
    assert n % t == 0, (n, want)
    return t


_DOT_DIMS = {
    "nn": (((1,), (0,)), ((), ())),
    "nt": (((1,), (1,)), ((), ())),
    "tn": (((0,), (0,)), ((), ())),
}


def _mm(a, b, *, dims, grid, a_spec, b_spec, o_spec, out_shape, out_dtype, acc_shape, name, res=None, res_spec=None,
        norm_bwd=None, norm_out=None):
    nk = grid[2]
    dn = _DOT_DIMS[dims]
    extras, extra_specs = [], []
    if res is not None:
        extras, extra_specs = [res], [res_spec]
    if norm_bwd is not None:
        h, gain, dres, row_spec, gain_spec = norm_bwd
        extras = [h, gain] + ([dres] if dres is not None else [])
        extra_specs = [row_spec, gain_spec] + ([row_spec] if dres is not None else [])
        out_specs = [row_spec, row_spec, gain_spec]
        out_shapes = [jax.ShapeDtypeStruct(h.shape, F32), jax.ShapeDtypeStruct(h.shape, BF16),
                      jax.ShapeDtypeStruct(gain.shape, F32)]
    elif norm_out is not None:
        extras, extra_specs = extras + [norm_out[0]], extra_specs + [norm_out[1]]
        out_specs = [o_spec, o_spec]
        out_shapes = [jax.ShapeDtypeStruct(out_shape, out_dtype), jax.ShapeDtypeStruct(out_shape, BF16)]
    else:
        out_specs, out_shapes = o_spec, jax.ShapeDtypeStruct(out_shape, out_dtype)
    n_extra = len(extras)

    def body(*refs):
        a_ref, b_ref = refs[:2]
        extra_refs = refs[2:2 + n_extra]
        out_refs = refs[2 + n_extra:len(refs) - (1 if nk > 1 else 0)]
        p = lax.dot_general(a_ref[...], b_ref[...], dn, preferred_element_type=F32)

        def finish(r):
            if norm_bwd is not None:
                _rms_bwd_math(r, extra_refs[0], extra_refs[1], extra_refs[2] if n_extra == 3 else None,
                              *out_refs, first=pl.program_id(0) == 0)
                return
            if res is not None:
                r = r + extra_refs[0][...]
            out_refs[0][...] = r.astype(out_refs[0].dtype)
            if norm_out is not None:
                scale = lax.rsqrt(jnp.mean(r * r, axis=-1, keepdims=True) + EPS)
                out_refs[1][...] = ((r * scale) * extra_refs[-1][...]).astype(BF16)

        if nk == 1:
            finish(p)
        else:
            acc_ref = refs[-1]
            k = pl.program_id(2)

            @pl.when(k == 0)
            def _():
                acc_ref[...] = p

            @pl.when(k > 0)
            def _():
                acc_ref[...] += p

            @pl.when(k == nk - 1)
            def _():
                finish(acc_ref[...])

    scratch = [pltpu.VMEM(acc_shape, F32)] if nk > 1 else []
    return pl.pallas_call(
        body, grid=grid, in_specs=[a_spec, b_spec] + extra_specs, out_specs=out_specs,
        out_shape=out_shapes, scratch_shapes=scratch, name=name,
        compiler_params=_params("arbitrary" if norm_bwd is not None else "parallel", "parallel", "arbitrary"),
    )(a, b, *extras)


def _rms_bwd_math(dy, h_ref, g_ref, dres_ref, dh_ref, dhb_ref, dg_ref, *, first):
    x = h_ref[...]
    r = lax.rsqrt(jnp.mean(x * x, axis=-1, keepdims=True) + EPS)
    a = dy * g_ref[...]
    m = jnp.mean(a * x, axis=-1, keepdims=True)
    dh = r * a - x * (r * r * r * m)
    if dres_ref is not None:
        dh = dh + dres_ref[...]
    dh_ref[...] = dh
    dhb_ref[...] = dh.astype(BF16)
    part = jnp.sum(dy * (x * r), axis=0, keepdims=True)

    @pl.when(first)
    def _():
        dg_ref[...] = part

    @pl.when(jnp.logical_not(first))
    def _():
        dg_ref[...] += part


def _mm_nn(a, b, *, out_dtype=F32, name, res=None, tm=1024, norm_gain=None, norm_bwd=None):
    m, k = a.shape
    n = b.shape[1]
    tm = _tile(m, tm)
    spec_o = pl.BlockSpec((tm, n), lambda i, j, kk: (i, 0))
    norm_out = None if norm_gain is None else (norm_gain.reshape(1, n), pl.BlockSpec((1, n), lambda i, j, kk: (0, 0)))
    return _mm(a, b, dims="nn", grid=(m // tm, 1, 1),
               a_spec=pl.BlockSpec((tm, k), lambda i, j, kk: (i, 0)),
               b_spec=pl.BlockSpec((k, n), lambda i, j, kk: (0, 0)),
               o_spec=spec_o, out_shape=(m, n), out_dtype=out_dtype, acc_shape=None, name=name,
               res=res, res_spec=spec_o if res is not None else None, norm_out=norm_out,
               norm_bwd=None if norm_bwd is None else _norm_bwd_arg(*norm_bwd, tm))


def _norm_bwd_arg(h, gain, dres, tm):
    d = h.shape[1]
    return (h, gain.reshape(1, d), dres, pl.BlockSpec((tm, d), lambda i, j, kk: (i, 0)),
            pl.BlockSpec((1, d), lambda i, j, kk: (0, 0)))


def _mm_nt(a, b, *, out_dtype=F32, name, res=None, tm=1024, norm_bwd=None):
    m, k = a.shape
    n = b.shape[0]
    tm = _tile(m, tm)
    spec_o = pl.BlockSpec((tm, n), lambda i, j, kk: (i, 0))
    return _mm(a, b, dims="nt", grid=(m // tm, 1, 1),
               a_spec=pl.BlockSpec((tm, k), lambda i, j, kk: (i, 0)),
               b_spec=pl.BlockSpec((n, k), lambda i, j, kk: (0, 0)),
               o_spec=spec_o, out_shape=(m, n), out_dtype=out_dtype, acc_shape=None, name=name,
               res=res, res_spec=spec_o if res is not None else None,
               norm_bwd=None if norm_bwd is None else _norm_bwd_arg(*norm_bwd, tm))


_TN_ROWS = 2048


def _mm_tn(a, b, *, name, tm=None, tn=None, ts=_TN_ROWS, out_dtype=BF16):
    s, m = a.shape
    n = b.shape[1]
    tm = m if tm is None else tm
    tn = n if tn is None else tn
    ts = _tile(s, ts)
    return _mm(a, b, dims="tn", grid=(m // tm, n // tn, s // ts),
               a_spec=pl.BlockSpec((ts, tm), lambda i, j, kk: (kk, i)),
               b_spec=pl.BlockSpec((ts, tn), lambda i, j, kk: (kk, j)),
               o_spec=pl.BlockSpec((tm, tn), lambda i, j, kk: (i, j)),
               out_shape=(m, n), out_dtype=out_dtype, acc_shape=(tm, tn), name=name)


def _mm_up_back(dhh, w_up_t, *, name, tm=512, norm_bwd=None):
    _, s, f = dhh.shape
    d = w_up_t.shape[1]
    tm = _tile(s, tm)
    return _mm(dhh, w_up_t, dims="nn", grid=(s // tm, 1, 2),
               a_spec=pl.BlockSpec((None, tm, f), lambda i, j, kk: (kk, i, 0)),
               b_spec=pl.BlockSpec((f, d), lambda i, j, kk: (kk, 0)),
               o_spec=pl.BlockSpec((tm, d), lambda i, j, kk: (i, 0)),
               out_shape=(s, d), out_dtype=F32, acc_shape=(tm, d), name=name,
               norm_bwd=None if norm_bwd is None else _norm_bwd_arg(*norm_bwd, tm))


def _mm_up_grad(dhh, xn, *, name, ts=_TN_ROWS):
    s, d = xn.shape
    f = dhh.shape[2]
    tm = f // 2
    ts = _tile(s, ts)
    return _mm(dhh, xn, dims="tn", grid=(4, 1, s // ts),
               a_spec=pl.BlockSpec((None, ts, tm), lambda i, j, kk: (i // 2, kk, i % 2)),
               b_spec=pl.BlockSpec((ts, d), lambda i, j, kk: (kk, 0)),
               o_spec=pl.BlockSpec((tm, d), lambda i, j, kk: (i, 0)),
               out_shape=(2 * f, d), out_dtype=BF16, acc_shape=(tm, d), name=name)


def _rms_fwd(h, g, *, name, tr=512):
    s, d = h.shape
    tr = _tile(s, tr)

    def body(h_ref, g_ref, o_ref):
        x = h_ref[...]
        r = lax.rsqrt(jnp.mean(x * x, axis=-1, keepdims=True) + EPS)
        o_ref[...] = ((x * r) * g_ref[...]).astype(o_ref.dtype)

    row = pl.BlockSpec((tr, d), lambda i: (i, 0))
    return pl.pallas_call(
        body, grid=(s // tr,), in_specs=[row, pl.BlockSpec((1, d), lambda i: (0, 0))], out_specs=row,
        out_shape=jax.ShapeDtypeStruct((s, d), BF16), name=name, compiler_params=_params("parallel"),
    )(h, g.reshape(1, d))


def _rms_bwd(h, dxn, g, dres, *, name, tr=512):
    s, d = h.shape
    tr = _tile(s, tr)
    has_res = dres is not None

    def body(*refs):
        if has_res:
            h_ref, dxn_ref, g_ref, dres_ref, dh_ref, dhb_ref, dg_ref = refs
        else:
            h_ref, dxn_ref, g_ref, dh_ref, dhb_ref, dg_ref = refs
            dres_ref = None
        _rms_bwd_math(dxn_ref[...].astype(F32), h_ref, g_ref, dres_ref, dh_ref, dhb_ref, dg_ref,
                      first=pl.program_id(0) == 0)

    row = pl.BlockSpec((tr, d), lambda i: (i, 0))
    vec = pl.BlockSpec((1, d), lambda i: (0, 0))
    in_specs = [row, row, vec] + ([row] if has_res else [])
    args = (h, dxn, g.reshape(1, d)) + ((dres,) if has_res else ())
    return pl.pallas_call(
        body, grid=(s // tr,), in_specs=in_specs, out_specs=[row, row, vec],
        out_shape=[jax.ShapeDtypeStruct((s, d), F32), jax.ShapeDtypeStruct((s, d), BF16),
                   jax.ShapeDtypeStruct((1, d), F32)],
        name=name, compiler_params=_params("arbitrary"),
    )(*args)


def _loss_head(h, g, target, *, name, tr=1024):
    s, d = h.shape
    tr = _tile(s, tr)
    nt = s // tr

    def body(h_ref, g_ref, t_ref, dh_ref, dhb_ref, dg_ref, loss_ref, sq_ref):
        i = pl.program_id(0)
        x = h_ref[...]
        gain = g_ref[...]
        r = lax.rsqrt(jnp.mean(x * x, axis=-1, keepdims=True) + EPS)
        xh = x * r
        err = xh * gain - t_ref[...]
        dy = err * (1.0 / d)
        a = dy * gain
        m = jnp.mean(a * x, axis=-1, keepdims=True)
        dh = r * a - x * (r * r * r * m)
        dh_ref[...] = dh
        dhb_ref[...] = dh.astype(BF16)
        dg_part = jnp.sum(dy * xh, axis=0, keepdims=True)
        sq_part = jnp.sum(err * err, axis=0, keepdims=True)

        @pl.when(i == 0)
        def _():
            dg_ref[...] = dg_part
            sq_ref[...] = sq_part

        @pl.when(i > 0)
        def _():
            dg_ref[...] += dg_part
            sq_ref[...] += sq_part

        @pl.when(i == nt - 1)
        def _():
            total = jnp.sum(sq_ref[...], axis=1, keepdims=True) * (0.5 / d)
            loss_ref[...] = jnp.broadcast_to(total, loss_ref.shape)

    row = pl.BlockSpec((tr, d), lambda i: (i, 0))
    vec = pl.BlockSpec((1, d), lambda i: (0, 0))
    return pl.pallas_call(
        body, grid=(nt,), in_specs=[row, vec, row],
        out_specs=[row, row, vec, pl.BlockSpec((1, 128), lambda i: (0, 0))],
        out_shape=[jax.ShapeDtypeStruct((s, d), F32), jax.ShapeDtypeStruct((s, d), BF16),
                   jax.ShapeDtypeStruct((1, d), F32), jax.ShapeDtypeStruct((1, 128), F32)],
        scratch_shapes=[pltpu.VMEM((1, d), F32)], name=name, compiler_params=_params("arbitrary"),
    )(h, g.reshape(1, d), target)


_SQRT_HALF = 0.7071067811865476
_INV_SQRT_2PI = 0.3989422804014327


def _gelu(x):
    return 0.5 * x * (1.0 + lax.erf(x * _SQRT_HALF))


def _gelu_and_grad(x):
    cdf = 0.5 * (1.0 + lax.erf(x * _SQRT_HALF))
    return x * cdf, cdf + x * (jnp.exp(-0.5 * x * x) * _INV_SQRT_2PI)


def _trailing_sums(xe, win):
    s = xe
    sh = 1
    while sh < win:
        s = s + pltpu.roll(s, sh, 0)
        sh *= 2
    return s


def _leading_sums(xe, win):
    n = xe.shape[0]
    s = xe
    sh = 1
    while sh < win:
        s = s + pltpu.roll(s, n - sh, 0)
        sh *= 2
    return s


def _tril_mask():
    return lax.broadcasted_iota(jnp.int32, (GROUP, GROUP), 0) >= lax.broadcasted_iota(jnp.int32, (GROUP, GROUP), 1)


def _layernorm_stats(v):
    mu = jnp.mean(v, axis=-1, keepdims=True)
    xc = v - mu
    rstd = lax.rsqrt(jnp.mean(xc * xc, axis=-1, keepdims=True) + EPS)
    return xc * rstd, rstd


def _mixer_specs(s, t):
    halo_blocks = t // POOL_HALO
    tile = lambda w: pl.BlockSpec((t, w), lambda i: (i, 0))
    prev = pl.BlockSpec((POOL_HALO, POOL_WIDTH), lambda i: (jnp.maximum(i * halo_blocks - 1, 0), 0))
    nxt = pl.BlockSpec((POOL_HALO, POOL_WIDTH),
                       lambda i: (jnp.minimum((i + 1) * halo_blocks, s // POOL_HALO - 1), 0))
    const3 = pl.BlockSpec((HEADS, GROUP, GROUP), lambda i: (0, 0, 0))
    vec = pl.BlockSpec((1, POOL_WIDTH), lambda i: (0, 0))
    bias = pl.BlockSpec((GROUP, SGU_WIDTH), lambda i: (0, 0))
    return tile, prev, nxt, const3, vec, bias


def _mixer_fwd(proj, pool_w, pool_scale, sgu_g, sgu_w, sgu_bias, *, name, t=1024):
    s = proj.shape[0]
    t = _tile(s, t)
    tile, prev, _, const3, vec, bias = _mixer_specs(s, t)

    def body(proj_ref, halo_ref, pw_ref, ps_ref, sg_ref, sw_ref, sb_ref, cat_ref):
        i = pl.program_id(0)
        row = i * t + lax.broadcasted_iota(jnp.int32, (t, 1), 0)
        p = proj_ref[:, 0:POOL_WIDTH]
        pe = jnp.concatenate([jnp.where(i > 0, halo_ref[...], 0.0), p], axis=0)
        for gi, win in enumerate(POOL_WINDOWS):
            cols = slice(gi * GROUP, (gi + 1) * GROUP)
            count = jnp.minimum(row + 1, win).astype(F32)
            d = _trailing_sums(pe[:, cols], win)[POOL_HALO:] / count - p[:, cols]
            y = jnp.dot(d.astype(BF16), pw_ref[gi].astype(BF16), preferred_element_type=F32) * ps_ref[:, cols]
            cat_ref[:, cols] = y.astype(BF16)

        u = _gelu(proj_ref[:, POOL_WIDTH:POOL_WIDTH + SGU_WIDTH])
        xhat, _ = _layernorm_stats(_gelu(proj_ref[:, POOL_WIDTH + SGU_WIDTH:]))
        vn = (xhat * sg_ref[...]).astype(BF16)
        tri = _tril_mask()
        for h in range(HEADS):
            cols = slice(h * GROUP, (h + 1) * GROUP)
            w = jnp.where(tri, sw_ref[h], 0.0).astype(BF16)
            for c in range(t // GROUP):
                rows = slice(c * GROUP, (c + 1) * GROUP)
                z = jnp.dot(w, vn[rows, cols], preferred_element_type=F32) + sb_ref[:, cols]
                cat_ref[rows, POOL_WIDTH + h * GROUP:POOL_WIDTH + (h + 1) * GROUP] = (u[rows, cols] * z).astype(BF16)

    return pl.pallas_call(
        body, grid=(s // t,),
        in_specs=[tile(POOL_WIDTH + 2 * SGU_WIDTH), prev, const3, vec, vec, const3, bias],
        out_specs=tile(POOL_WIDTH + SGU_WIDTH),
        out_shape=jax.ShapeDtypeStruct((s, POOL_WIDTH + SGU_WIDTH), BF16), name=name,
        compiler_params=_params("parallel"),
    )(proj, proj, pool_w, pool_scale, sgu_g, sgu_w, sgu_bias)


def _mixer_bwd(proj, dcat, pool_w, pool_scale, sgu_g, sgu_w, sgu_bias, *, name, t=512):
    s = proj.shape[0]
    t = _tile(s, t)
    nt = s // t
    tile, prev, nxt, const3, vec, bias = _mixer_specs(s, t)

    def body(proj_ref, halo_ref, dcat_ref, dnext_ref, pw_ref, ps_ref, sg_ref, sw_ref, sb_ref,
             dproj_ref, dpw_ref, dps_ref, dsg_ref, dsw_ref, dsb_ref, du_ref, dvn_ref, dz_ref):
        i = pl.program_id(0)

        @pl.when(i == 0)
        def _():
            dpw_ref[...] = jnp.zeros_like(dpw_ref)
            dps_ref[...] = jnp.zeros_like(dps_ref)
            dsg_ref[...] = jnp.zeros_like(dsg_ref)
            dsw_ref[...] = jnp.zeros_like(dsw_ref)
            dz_ref[...] = jnp.zeros_like(dz_ref)

        row = i * t + lax.broadcasted_iota(jnp.int32, (t, 1), 0)
        row_e = i * t + lax.broadcasted_iota(jnp.int32, (t + POOL_HALO, 1), 0)
        p = proj_ref[:, 0:POOL_WIDTH]
        pe = jnp.concatenate([jnp.where(i > 0, halo_ref[...], 0.0), p], axis=0)
        dyp = dcat_ref[:, 0:POOL_WIDTH]
        dye = jnp.concatenate([dyp, jnp.where(i < nt - 1, dnext_ref[...], 0.0)], axis=0)
        for gi, win in enumerate(POOL_WINDOWS):
            cols = slice(gi * GROUP, (gi + 1) * GROUP)
            count = jnp.minimum(row + 1, win).astype(F32)
            d = (_trailing_sums(pe[:, cols], win)[POOL_HALO:] / count - p[:, cols]).astype(BF16)
            pw = pw_ref[gi].astype(BF16)
            pre = jnp.dot(d, pw, preferred_element_type=F32)
            dps_ref[:, cols] += jnp.sum(dyp[:, cols] * pre, axis=0, keepdims=True)
            ys = (dye[:, cols] * ps_ref[:, cols]).astype(BF16)
            dpw_ref[gi] += lax.dot_general(d, ys[:t], _DOT_DIMS["tn"], preferred_element_type=F32)
            dd = lax.dot_general(ys, pw, _DOT_DIMS["nt"], preferred_element_type=F32)
            count_e = jnp.minimum(row_e + 1, win).astype(F32)
            dp = _leading_sums(dd / count_e, win)[:t] - dd[:t]
            dproj_ref[:, cols] = dp.astype(BF16)

        xu = proj_ref[:, POOL_WIDTH:POOL_WIDTH + SGU_WIDTH]
        xv = proj_ref[:, POOL_WIDTH + SGU_WIDTH:]
        u, gelu_grad_u = _gelu_and_grad(xu)
        v, gelu_grad_v = _gelu_and_grad(xv)
        xhat, rstd = _layernorm_stats(v)
        gain = sg_ref[...]
        vn = (xhat * gain).astype(BF16)
        tri = _tril_mask()
        for h in range(HEADS):
            cols = slice(h * GROUP, (h + 1) * GROUP)
            wf = jnp.where(tri, sw_ref[h], 0.0)
            w, wt = wf.astype(BF16), wf.T.astype(BF16)
            for c in range(t // GROUP):
                rows = slice(c * GROUP, (c + 1) * GROUP)
                vch = vn[rows, cols]
                z = jnp.dot(w, vch, preferred_element_type=F32) + sb_ref[:, cols]
                dy = dcat_ref[rows, POOL_WIDTH + h * GROUP:POOL_WIDTH + (h + 1) * GROUP]
                du_ref[rows, cols] = dy * z
                dz = dy * u[rows, cols]
                dz_ref[:, cols] += dz
                dzb = dz.astype(BF16)
                dsw_ref[h] += lax.dot_general(dzb, vch, _DOT_DIMS["nt"], preferred_element_type=F32)
                dvn_ref[rows, cols] = jnp.dot(wt, dzb, preferred_element_type=F32)
        dvn = dvn_ref[...]
        dsg_ref[...] += jnp.sum(dvn * xhat, axis=0, keepdims=True)
        dxh = dvn * gain
        dv = rstd * (dxh - jnp.mean(dxh, axis=-1, keepdims=True)
                     - xhat * jnp.mean(dxh * xhat, axis=-1, keepdims=True))
        dproj_ref[:, POOL_WIDTH:POOL_WIDTH + SGU_WIDTH] = (du_ref[...] * gelu_grad_u).astype(BF16)
        dproj_ref[:, POOL_WIDTH + SGU_WIDTH:] = (dv * gelu_grad_v).astype(BF16)

        @pl.when(i == nt - 1)
        def _():
            for h in range(HEADS):
                dsw_ref[h] = jnp.where(tri, dsw_ref[h], 0.0)
            lane = lax.broadcasted_iota(jnp.int32, (GROUP, GROUP), 1)
            out = jnp.zeros((GROUP, GROUP), F32)
            for h in range(HEADS):
                sh = jnp.sum(dz_ref[:, h * GROUP:(h + 1) * GROUP], axis=1, keepdims=True)
                out = jnp.where(lane == h, sh, out)
            dsb_ref[...] = out

    outs = pl.pallas_call(
        body, grid=(nt,),
        in_specs=[tile(POOL_WIDTH + 2 * SGU_WIDTH), prev, tile(POOL_WIDTH + SGU_WIDTH), nxt,
                  const3, vec, vec, const3, bias],
        out_specs=[tile(POOL_WIDTH + 2 * SGU_WIDTH), const3, vec, vec, const3,
                   pl.BlockSpec((GROUP, GROUP), lambda i: (0, 0))],
        out_shape=[jax.ShapeDtypeStruct((s, POOL_WIDTH + 2 * SGU_WIDTH), BF16),
                   jax.ShapeDtypeStruct((HEADS, GROUP, GROUP), F32),
                   jax.ShapeDtypeStruct((1, POOL_WIDTH), F32),
                   jax.ShapeDtypeStruct((1, SGU_WIDTH), F32),
                   jax.ShapeDtypeStruct((HEADS, GROUP, GROUP), F32),
                   jax.ShapeDtypeStruct((GROUP, GROUP), F32)],
        scratch_shapes=[pltpu.VMEM((t, SGU_WIDTH), F32), pltpu.VMEM((t, SGU_WIDTH), F32),
                        pltpu.VMEM((GROUP, SGU_WIDTH), F32)],
        name=name, compiler_params=_params("arbitrary"),
    )(proj, proj, dcat, dcat, pool_w, pool_scale, sgu_g, sgu_w, sgu_bias)
    dproj, dpw, dps, dsg, dsw, dsb = outs
    return dproj, dpw, dps, dsg, dsw, dsb[:, :HEADS].T


def _attn_probs(q, k, scale):
    sc = lax.dot_general(q, k, _DOT_DIMS["nt"], preferred_element_type=F32) * scale
    sc = sc - jnp.max(sc, axis=-1, keepdims=True)
    e = jnp.exp(sc)
    return e / jnp.sum(e, axis=-1, keepdims=True)


def _attn_fwd(q, k, v, *, name, t=2048):
    s, d = q.shape
    nm = k.shape[0]
    t = _tile(s, t)
    scale = HEAD_DIM ** -0.5

    def body(q_ref, k_ref, v_ref, o_ref):
        for h in range(HEADS):
            cols = slice(h * HEAD_DIM, (h + 1) * HEAD_DIM)
            pr = _attn_probs(q_ref[:, cols], k_ref[:, cols], scale)
            o_ref[:, cols] = jnp.dot(pr.astype(BF16), v_ref[:, cols], preferred_element_type=F32).astype(BF16)

    row = pl.BlockSpec((t, d), lambda i: (i, 0))
    kv = pl.BlockSpec((nm, d), lambda i: (0, 0))
    return pl.pallas_call(
        body, grid=(s // t,), in_specs=[row, kv, kv], out_specs=row,
        out_shape=jax.ShapeDtypeStruct((s, d), BF16), name=name, compiler_params=_params("parallel"),
    )(q, k, v)


def _attn_bwd(q, k, v, do, *, name, t=2048):
    s, d = q.shape
    nm = k.shape[0]
    t = _tile(s, t)
    scale = HEAD_DIM ** -0.5

    def body(q_ref, k_ref, v_ref, do_ref, dq_ref, dk_ref, dv_ref):
        i = pl.program_id(0)

        @pl.when(i == 0)
        def _():
            dk_ref[...] = jnp.zeros_like(dk_ref)
            dv_ref[...] = jnp.zeros_like(dv_ref)

        for h in range(HEADS):
            cols = slice(h * HEAD_DIM, (h + 1) * HEAD_DIM)
            qh, kh, vh, doh = q_ref[:, cols], k_ref[:, cols], v_ref[:, cols], do_ref[:, cols]
            pr = _attn_probs(qh, kh, scale)
            dpr = lax.dot_general(doh, vh, _DOT_DIMS["nt"], preferred_element_type=F32)
            ds = (pr * (dpr - jnp.sum(dpr * pr, axis=-1, keepdims=True)) * scale).astype(BF16)
            dv_ref[:, cols] += lax.dot_general(pr.astype(BF16), doh, _DOT_DIMS["tn"], preferred_element_type=F32)
            dk_ref[:, cols] += lax.dot_general(ds, qh, _DOT_DIMS["tn"], preferred_element_type=F32)
            dq_ref[:, cols] = jnp.dot(ds, kh, preferred_element_type=F32).astype(BF16)

    row = pl.BlockSpec((t, d), lambda i: (i, 0))
    kv = pl.BlockSpec((nm, d), lambda i: (0, 0))
    return pl.pallas_call(
        body, grid=(s // t,), in_specs=[row, kv, kv, row], out_specs=[row, kv, kv],
        out_shape=[jax.ShapeDtypeStruct((s, d), BF16), jax.ShapeDtypeStruct((nm, d), F32),
                   jax.ShapeDtypeStruct((nm, d), F32)],
        name=name, compiler_params=_params("arbitrary"),
    )(q, k, v, do)


def _conv3(w_ref, p, x2, x1, x0, b):
    return (w_ref[p, 0:1, :] * x2 + w_ref[p, 1:2, :] * x1 + w_ref[p, 2:3, :] * x0) + b


def _ffn_up_gate(xn, w_up_t, cw, cb, *, name, t=256, tc=1408):
    s, d = xn.shape
    f = w_up_t.shape[0] // 2
    t = _tile(s, t)
    nt, nj = s // t, f // tc

    def body(xn_ref, wg_ref, wv_ref, cw_ref, cb_ref, hh_ref, act_ref, held_ref, above_ref, w_ref):
        i = pl.program_id(1)

        @pl.when(i == 0)
        def _():
            held_ref[...] = jnp.zeros_like(held_ref)
            above_ref[...] = jnp.zeros_like(above_ref)
            w_ref[0] = wg_ref[...].astype(F32).T.astype(BF16)
            w_ref[1] = wv_ref[...].astype(F32).T.astype(BF16)

        hc = []
        for p in range(2):
            xe = jnp.concatenate([above_ref[p], held_ref[p]], axis=0)
            hc.append(_conv3(cw_ref, p, pltpu.roll(xe, 2, 0), pltpu.roll(xe, 1, 0), xe, cb_ref[p])[CONV_HALO:])
            above_ref[p] = held_ref[p, t - CONV_HALO:t, :]
        gate, val = hc
        act_ref[...] = ((gate * jax.nn.sigmoid(gate)) * val).astype(BF16)

        x = xn_ref[...]
        for p in range(2):
            y = jnp.dot(x, w_ref[p], preferred_element_type=F32)
            hh_ref[p] = y
            held_ref[p] = y

    row = lambda i: jnp.minimum(i, nt - 1)
    return pl.pallas_call(
        body, grid=(nj, nt + 1),
        in_specs=[pl.BlockSpec((t, d), lambda j, i: (row(i), 0)),
                  pl.BlockSpec((tc, d), lambda j, i: (j, 0)),
                  pl.BlockSpec((tc, d), lambda j, i: (nj + j, 0)),
                  pl.BlockSpec((2, 3, tc), lambda j, i: (0, 0, j)),
                  pl.BlockSpec((2, 1, tc), lambda j, i: (0, 0, j))],
        out_specs=[pl.BlockSpec((2, t, tc), lambda j, i: (0, row(i), j)),
                   pl.BlockSpec((t, tc), lambda j, i: (jnp.maximum(i - 1, 0), j))],
        out_shape=[jax.ShapeDtypeStruct((2, s, f), F32), jax.ShapeDtypeStruct((s, f), BF16)],
        scratch_shapes=[pltpu.VMEM((2, t, tc), F32), pltpu.VMEM((2, CONV_HALO, tc), F32), pltpu.VMEM((2, d, tc), BF16)],
        name=name, compiler_params=_params("arbitrary", "arbitrary"),
    )(xn, w_up_t, w_up_t, cw, cb)


def _convgate_bwd(hh, dact, cw, cb, *, name, t=256, tc=1408):
    _, s, f = hh.shape
    t = _tile(s, t)
    nt, nj = s // t, f // tc
    hb = t // CONV_HALO
    m = t + CONV_HALO

    def body(hh_ref, prev_ref, next_ref, da_ref, danext_ref, cw_ref, cb_ref, dhh_ref, dcw_ref, dcb_ref):
        i = pl.program_id(1)
        is_last = i == nt - 1

        @pl.when(i == 0)
        def _():
            dcw_ref[...] = jnp.zeros_like(dcw_ref)
            dcb_ref[...] = jnp.zeros_like(dcb_ref)

        taps, hc = [], []
        for p in range(2):
            xe = jnp.concatenate([jnp.where(i > 0, prev_ref[p], 0.0), hh_ref[p],
                                  jnp.where(is_last, 0.0, next_ref[p])], axis=0)
            x2, x1 = pltpu.roll(xe, 2, 0), pltpu.roll(xe, 1, 0)
            hc.append(_conv3(cw_ref, p, x2, x1, xe, cb_ref[p])[CONV_HALO:])
            taps.append((x2[CONV_HALO:CONV_HALO + t], x1[CONV_HALO:CONV_HALO + t], xe[CONV_HALO:CONV_HALO + t]))
        gate, val = hc
        dae = jnp.concatenate([da_ref[...], jnp.where(is_last, 0.0, danext_ref[...])], axis=0)
        sg = jax.nn.sigmoid(gate)
        dval = dae * (gate * sg)
        dgate = dae * val * (sg * (1.0 + gate * (1.0 - sg)))
        for p, dhc in enumerate((dgate, dval)):
            dh = (cw_ref[p, 2:3, :] * dhc + cw_ref[p, 1:2, :] * pltpu.roll(dhc, m - 1, 0)
                  + cw_ref[p, 0:1, :] * pltpu.roll(dhc, m - 2, 0))
            dhh_ref[p] = dh[:t].astype(BF16)
            d0 = dhc[:t]
            for kk, tap in enumerate(taps[p]):
                dcw_ref[p, kk:kk + 1, :] += jnp.sum(d0 * tap, axis=0, keepdims=True)
            dcb_ref[p] += jnp.sum(d0, axis=0, keepdims=True)

    below = lambda i: jnp.minimum((i + 1) * hb, s // CONV_HALO - 1)
    return pl.pallas_call(
        body, grid=(nj, nt),
        in_specs=[pl.BlockSpec((2, t, tc), lambda j, i: (0, i, j)),
                  pl.BlockSpec((2, CONV_HALO, tc), lambda j, i: (0, jnp.maximum(i * hb - 1, 0), j)),
                  pl.BlockSpec((2, CONV_HALO, tc), lambda j, i: (0, below(i), j)),
                  pl.BlockSpec((t, tc), lambda j, i: (i, j)),
                  pl.BlockSpec((CONV_HALO, tc), lambda j, i: (below(i), j)),
                  pl.BlockSpec((2, 3, tc), lambda j, i: (0, 0, j)),
                  pl.BlockSpec((2, 1, tc), lambda j, i: (0, 0, j))],
        out_specs=[pl.BlockSpec((2, t, tc), lambda j, i: (0, i, j)),
                   pl.BlockSpec((2, 3, tc), lambda j, i: (0, 0, j)),
                   pl.BlockSpec((2, 1, tc), lambda j, i: (0, 0, j))],
        out_shape=[jax.ShapeDtypeStruct((2, s, f), BF16), jax.ShapeDtypeStruct((2, 3, f), F32),
                   jax.ShapeDtypeStruct((2, 1, f), F32)],
        name=name, compiler_params=_params("parallel", "arbitrary"),
    )(hh, hh, hh, dact, dact, cw, cb)


def _position():
    return lax.axis_index("x"), lax.axis_index("y"), lax.axis_index("c")


def _linear(px, py, pc):
    return 4 * px + 2 * py + pc


def _peers_of(x, y, c):
    peers = []
    for mask in range(1, N_DEV):
        peers.append((1 - x if mask & 4 else x, 1 - y if mask & 2 else y, 1 - c if mask & 1 else c))
    return peers


def _exchange_copy(src_ref, land_ref, send_sem, recv_sem, peer, mine, scatter, arriving):
    src = src_ref.at[_linear(*peer)] if scatter else src_ref
    dst = land_ref.at[_linear(*peer) if arriving else mine]
    return pltpu.make_async_remote_copy(src_ref=src, dst_ref=dst, send_sem=send_sem, recv_sem=recv_sem,
                                        device_id=peer, device_id_type=MESH)


_EXCHANGE_COLLECTIVE_ID = 7


def _sequencer_exchange(srcs, *, scatter, name):
    n = len(srcs)
    src_refs = [jax.new_ref(a, memory_space=pltpu.MemorySpace.HBM) for a in srcs]
    land_refs = [jax.empty_ref(jax.ShapeDtypeStruct(a.shape if scatter else (N_DEV,) + a.shape, a.dtype),
                               memory_space=pltpu.MemorySpace.HBM) for a in srcs]

    @pl.kernel(mesh=plsc.ScalarSubcoreMesh(axis_name="sequencer", num_cores=1), name=name,
               scratch_types=(pltpu.SemaphoreType.DMA((7 * n,)), pltpu.SemaphoreType.DMA((7 * n,)),
                              pltpu.SemaphoreType.DMA((n,))),
               compiler_params=pltpu.CompilerParams(collective_id=_EXCHANGE_COLLECTIVE_ID))
    def launch(send_sems, recv_sems, local_sems):
        x, y, c = _position()
        mine = _linear(x, y, c)
        peers = _peers_of(x, y, c)
        barrier = pltpu.get_barrier_semaphore()
        for peer in peers:
            pl.semaphore_signal(barrier, inc=1, device_id=peer, device_id_type=MESH)
        pl.semaphore_wait(barrier, N_DEV - 1)
        local = [pltpu.make_async_copy(src_refs[t].at[mine] if scatter else src_refs[t], land_refs[t].at[mine],
                                       local_sems.at[t]) for t in range(n)]
        for cp in local:
            cp.start()
        if scatter:
            sends = []
            for t in range(n):
                for k, peer in enumerate(peers):
                    cp = _exchange_copy(src_refs[t], land_refs[t], send_sems.at[7 * t + k], recv_sems.at[7 * t + k],
                                        peer, mine, scatter, arriving=False)
                    cp.start()
                    sends.append(cp)
            for t in range(n):
                for k, peer in enumerate(peers):
                    _exchange_copy(src_refs[t], land_refs[t], send_sems.at[7 * t + k], recv_sems.at[7 * t + k],
                                   peer, mine, scatter, arriving=True).wait_recv()
        else:
            me, sibling = (x, y, c), (x, y, 1 - c)
            chips = [(1 - x, y), (x, 1 - y), (1 - x, 1 - y)]

            def copy(t, k, block, to, src=None):
                dst = land_refs[t].at[_linear(*block)]
                return pltpu.make_async_remote_copy(
                    src_ref=dst if src is None else src, dst_ref=dst, send_sem=send_sems.at[7 * t + k],
                    recv_sem=recv_sems.at[7 * t + k], device_id=to, device_id_type=MESH)

            sends = []
            for t in range(n):
                sends.append(copy(t, 0, me, sibling, src=src_refs[t]))
                for j, chip in enumerate(chips):
                    sends.append(copy(t, 1 + j, me, (*chip, c), src=src_refs[t]))
            for cp in sends:
                cp.start()
            for j, chip in enumerate(chips):
                for t in range(n):
                    copy(t, 1 + j, (*chip, c), me).wait_recv()
                    passed = copy(t, 4 + j, (*chip, c), sibling)
                    passed.start()
                    sends.append(passed)
            for t in range(n):
                copy(t, 0, sibling, me).wait_recv()
                for j, chip in enumerate(chips):
                    copy(t, 4 + j, (*chip, 1 - c), me).wait_recv()
        for cp in local:
            cp.wait()
        for cp in sends:
            cp.wait_send()

    launch()
    return [r[...] for r in land_refs]


def _adamw_math(g, w, m, v):
    m2 = ADAM_B1 * m + (1.0 - ADAM_B1) * g
    v2 = ADAM_B2 * v + (1.0 - ADAM_B2) * (g * g)
    m_hat = m2 / (1.0 - ADAM_B1 ** ADAM_STEP)
    v_hat = v2 / (1.0 - ADAM_B2 ** ADAM_STEP)
    delta = -ADAM_LR * (m_hat / (jnp.sqrt(v_hat) + ADAM_EPS) + ADAM_WD * w)
    return delta, m2, v2


def _adamw(slots, w, m, v, *, name, tr=256):
    depth = len(slots)
    _, r, c = slots[0].shape
    tr = next((cand for cand in range(min(r, tr), 15, -1) if r % cand == 0 and cand % 16 == 0), r)

    def body(*refs):
        s_refs = refs[:depth]
        w_ref, m_ref, v_ref, g_ref, d_ref, m2_ref, v2_ref = refs[depth:]
        layer = pl.program_id(0)
        for l in range(depth):
            @pl.when(layer == l)
            def _():
                g = s_refs[l][0].astype(F32)
                for d in range(1, N_DEV):
                    g = g + s_refs[l][d].astype(F32)
                delta, m2, v2 = _adamw_math(g, w_ref[...], m_ref[...], v_ref[...])
                g_ref[...] = g
                d_ref[...] = delta
                m2_ref[...] = m2
                v2_ref[...] = v2

    blk = pl.BlockSpec((None, tr, c), lambda layer, i: (layer, i, 0))
    sblks = [pl.BlockSpec((N_DEV, tr, c), lambda layer, i, l=l: (0, jnp.where(layer == l, i, 0), 0))
             for l in range(depth)]
    shape = jax.ShapeDtypeStruct((depth, r, c), F32)
    return pl.pallas_call(
        body, grid=(depth, r // tr), in_specs=sblks + [blk, blk, blk], out_specs=[blk] * 4,
        out_shape=[shape] * 4, name=name, compiler_params=_params("arbitrary", "arbitrary"),
    )(*slots, w, m, v)


_SHARDED = ("w_in", "w_out", "wq", "wk", "wv", "wo", "w_up", "conv_w", "w_down")
_SMALL = ("norm_mix_g", "pool_w", "pool_scale", "sgu_g", "sgu_w", "sgu_b", "norm_xattn_g", "mem_norm_g",
          "norm_ffn_g", "conv_b", "final_norm_g")
_WEIGHTS = ("norm_mix_g", "w_in", "pool_w", "pool_scale", "sgu_g", "sgu_w", "sgu_b", "w_out", "norm_xattn_g",
            "mem_norm_g", "wq", "wk", "wv", "wo", "norm_ffn_g", "w_up", "conv_w", "conv_b", "w_down",
            "final_norm_g")
_PACK_LANES = 128
_GATHER_GROUPS = (("w_in",), ("w_out",), ("wq", "wk", "wv", "wo"), ("w_up", "conv_w", "w_down"))
_COLUMN_SHARDED = ("w_in", "w_up")


def _cols_to_blocks(a, *, name, tr=256):
    r, c8 = a.shape
    c = c8 // N_DEV
    tr = _tile(r, tr)

    def body(a_ref, o_ref):
        for dev in range(N_DEV):
            o_ref[dev] = a_ref[:, dev * c:(dev + 1) * c]

    return pl.pallas_call(
        body, grid=(r // tr,), in_specs=[pl.BlockSpec((tr, c8), lambda i: (i, 0))],
        out_specs=pl.BlockSpec((N_DEV, tr, c), lambda i: (0, i, 0)),
        out_shape=jax.ShapeDtypeStruct((N_DEV, r, c), a.dtype), name=name, compiler_params=_params("parallel"),
    )(a)


def _blocks_to_cols(a, *, name, tr=256):
    n, r, c = a.shape
    tr = _tile(r, tr)

    def body(a_ref, o_ref):
        for dev in range(n):
            o_ref[:, dev * c:(dev + 1) * c] = a_ref[dev]

    return pl.pallas_call(
        body, grid=(r // tr,), in_specs=[pl.BlockSpec((n, tr, c), lambda i: (0, i, 0))],
        out_specs=pl.BlockSpec((tr, n * c), lambda i: (i, 0)),
        out_shape=jax.ShapeDtypeStruct((r, n * c), a.dtype), name=name, compiler_params=_params("parallel"),
    )(a)


def _pin(x, *deps):
    return lax.optimization_barrier((x, *deps))[0]


def _pack(arrays):
    flat = jnp.concatenate([a.reshape(-1) for a in arrays])
    assert flat.shape[0] % (8 * _PACK_LANES) == 0
    return flat.reshape(-1, _PACK_LANES)


def _unpack(packed, like):
    flat = packed.reshape(-1)
    out, off = [], 0
    for a in like:
        out.append(flat[off:off + a.size].reshape(a.shape))
        off += a.size
    return out


def kernel(x, mem, norm_mix_g, w_in, pool_w, pool_scale, sgu_g, sgu_w, sgu_b, w_out, norm_xattn_g, mem_norm_g, wq, wk, wv, wo, norm_ffn_g, w_up, conv_w, conv_b, w_down, final_norm_g, loss_target, m_norm_mix_g, m_w_in, m_pool_w, m_pool_scale, m_sgu_g, m_sgu_w, m_sgu_b, m_w_out, m_norm_xattn_g, m_mem_norm_g, m_wq, m_wk, m_wv, m_wo, m_norm_ffn_g, m_w_up, m_conv_w, m_conv_b, m_w_down, m_final_norm_g, v_norm_mix_g, v_w_in, v_pool_w, v_pool_scale, v_sgu_g, v_sgu_w, v_sgu_b, v_w_out, v_norm_xattn_g, v_mem_norm_g, v_wq, v_wk, v_wv, v_wo, v_norm_ffn_g, v_w_up, v_conv_w, v_conv_b, v_w_down, v_final_norm_g):
    W = dict(norm_mix_g=norm_mix_g, w_in=w_in, pool_w=pool_w, pool_scale=pool_scale, sgu_g=sgu_g, sgu_w=sgu_w,
             sgu_b=sgu_b, w_out=w_out, norm_xattn_g=norm_xattn_g, mem_norm_g=mem_norm_g, wq=wq, wk=wk, wv=wv, wo=wo,
             norm_ffn_g=norm_ffn_g, w_up=w_up, conv_w=conv_w, conv_b=conv_b, w_down=w_down,
             final_norm_g=final_norm_g)
    M = dict(norm_mix_g=m_norm_mix_g, w_in=m_w_in, pool_w=m_pool_w, pool_scale=m_pool_scale, sgu_g=m_sgu_g,
             sgu_w=m_sgu_w, sgu_b=m_sgu_b, w_out=m_w_out, norm_xattn_g=m_norm_xattn_g, mem_norm_g=m_mem_norm_g,
             wq=m_wq, wk=m_wk, wv=m_wv, wo=m_wo, norm_ffn_g=m_norm_ffn_g, w_up=m_w_up, conv_w=m_conv_w,
             conv_b=m_conv_b, w_down=m_w_down, final_norm_g=m_final_norm_g)
    V = dict(norm_mix_g=v_norm_mix_g, w_in=v_w_in, pool_w=v_pool_w, pool_scale=v_pool_scale, sgu_g=v_sgu_g,
             sgu_w=v_sgu_w, sgu_b=v_sgu_b, w_out=v_w_out, norm_xattn_g=v_norm_xattn_g, mem_norm_g=v_mem_norm_g,
             wq=v_wq, wk=v_wk, wv=v_wv, wo=v_wo, norm_ffn_g=v_norm_ffn_g, w_up=v_w_up, conv_w=v_conv_w,
             conv_b=v_conv_b, w_down=v_w_down, final_norm_g=v_final_norm_g)

    s, d = x.shape[1], x.shape[2]
    f = w_down.shape[1] * N_DEV
    h = x.reshape(s, d)
    memx = mem.reshape(mem.shape[1], d)
    target = loss_target.reshape(s, d)

    gathered = {}

    def launch_gather(l, gi, after):
        if l >= DEPTH:
            return
        names = _GATHER_GROUPS[gi]
        shards = [W[nme][l] if nme == "conv_w" else
                  (W[nme][l].T if nme in _COLUMN_SHARDED else W[nme][l]).astype(BF16) for nme in names]
        if after is not None:
            shards[0], _ = lax.optimization_barrier((shards[0], after))
        gathered[l, gi] = dict(zip(names, _sequencer_exchange(shards, scatter=False, name=f"gather_{l}_{gi}")))

    launch_gather(0, 0, None)

    saved, full = [], []
    for l in range(DEPTH):
        sgu_bias = jnp.repeat(sgu_b[l].T, GROUP, axis=1)
        cb = conv_b[l].reshape(2, 1, f)
        if l == 0:
            xn1 = _rms_fwd(h, norm_mix_g[l], name=f"norm_mix_{l}")
            launch_gather(0, 1, xn1)
        w_in_t = gathered[l, 0]["w_in"].reshape(-1, d)
        proj = _mm_nt(xn1, w_in_t, out_dtype=F32, name=f"proj_in_{l}")
        if l == 0:
            launch_gather(0, 2, proj)
        cat = _mixer_fwd(proj, pool_w[l], pool_scale[l].reshape(1, -1), sgu_g[l].reshape(1, -1), sgu_w[l], sgu_bias,
                         name=f"mixer_{l}")
        if l == 0:
            launch_gather(0, 3, cat)
        w_out_f = gathered[l, 1]["w_out"].reshape(-1, d)
        h1, xn2 = _mm_nn(cat, w_out_f, out_dtype=F32, res=h, norm_gain=norm_xattn_g[l], name=f"proj_out_{l}")
        launch_gather(l + 1, 0, h1)
        g = gathered[l, 2]
        wq_f, wk_f, wv_f, wo_f = (g[nme].reshape(-1, d) for nme in ("wq", "wk", "wv", "wo"))
        q = _mm_nn(xn2, wq_f, out_dtype=BF16, name=f"q_{l}")
        launch_gather(l + 1, 1, q)
        memn = _rms_fwd(memx, mem_norm_g[l], name=f"norm_mem_{l}")
        k = _mm_nn(memn, wk_f, out_dtype=BF16, name=f"k_{l}")
        v = _mm_nn(memn, wv_f, out_dtype=BF16, name=f"v_{l}")
        o = _attn_fwd(q, k, v, name=f"attn_{l}")
        h2, xn3 = _mm_nn(o, wo_f, out_dtype=F32, res=h1, norm_gain=norm_ffn_g[l], name=f"attn_out_{l}")
        launch_gather(l + 1, 2, h2)
        g = gathered[l, 3]
        w_up_t = g["w_up"].reshape(-1, d)
        conv_w_f = _blocks_to_cols(g["conv_w"], name=f"conv_w_cols_{l}").reshape(3, 2, f).transpose(1, 0, 2)
        w_down_f = g["w_down"].reshape(-1, d)
        hh, act = _ffn_up_gate(xn3, w_up_t, conv_w_f, cb, name=f"ffn_up_gate_{l}")
        launch_gather(l + 1, 3, hh)
        if l + 1 < DEPTH:
            h3, xn1_next = _mm_nn(act, w_down_f, out_dtype=F32, res=h2, tm=512, norm_gain=norm_mix_g[l + 1],
                                  name=f"ffn_down_{l}")
        else:
            h3, xn1_next = _mm_nn(act, w_down_f, out_dtype=F32, res=h2, tm=512, name=f"ffn_down_{l}"), None
        full.append(dict(w_in_t=w_in_t, w_out=w_out_f, wq=wq_f, wk=wk_f, wv=wv_f, wo=wo_f, w_up_t=w_up_t,
                         conv_w=conv_w_f, w_down=w_down_f))
        saved.append(dict(h0=h, xn1=xn1, proj=proj, cat=cat, h1=h1, xn2=xn2, q=q, memn=memn, k=k, v=v, o=o, h2=h2,
                          xn3=xn3, hh=hh, act=act, sgu_bias=sgu_bias, cb=cb))
        h, xn1 = h3, xn1_next

    dh, dhb, dg_final, loss_row = _loss_head(h, final_norm_g, target, name="loss_head")

    slots = {nme: [None] * DEPTH for nme in _SHARDED}
    small = [None] * DEPTH

    previous = []

    def scatter(l, tag, names, parts):
        parts = [_pin(parts[0], *previous)] + parts[1:]
        arrived = _sequencer_exchange(parts, scatter=True, name=f"scatter_{tag}_{l}")
        previous[:] = arrived[:1]
        for nme, land in zip(names, arrived):
            slots[nme][l] = land
        return parts

    for l in reversed(range(DEPTH)):
        fw, sv = full[l], saved[l]
        dact = _mm_nt(dhb, fw["w_down"], out_dtype=F32, tm=512, name=f"d_act_{l}")
        g_w_down = _mm_tn(sv["act"], dhb, tm=f // 2, name=f"g_w_down_{l}")
        dhh, g_conv_w, g_conv_b = _convgate_bwd(sv["hh"], dact, fw["conv_w"], sv["cb"], name=f"d_convgate_{l}")
        g_w_up_t = _mm_up_grad(dhh, sv["xn3"], name=f"g_w_up_{l}")
        g_conv_w_cols = g_conv_w.transpose(1, 0, 2).reshape(3, 2 * f)
        parts = [g_w_up_t.reshape(N_DEV, -1, d),
                 _cols_to_blocks(g_conv_w_cols, name=f"g_conv_w_blocks_{l}"), g_w_down.reshape(N_DEV, -1, d)]
        parts = scatter(l, "ffn", ("w_up", "conv_w", "w_down"), parts)
        dh2, dh2b, g_norm_ffn = _mm_up_back(_pin(dhh, *parts), fw["w_up_t"], name=f"d_norm_ffn_{l}",
                                            norm_bwd=(sv["h2"], norm_ffn_g[l], dh))

        do = _mm_nt(dh2b, fw["wo"], out_dtype=BF16, name=f"d_o_{l}")
        g_wo = _mm_tn(sv["o"], dh2b, name=f"g_wo_{l}")
        dq, dk, dv = _attn_bwd(sv["q"], sv["k"], sv["v"], do, name=f"d_attn_{l}")
        dkb, dvb = dk.astype(BF16), dv.astype(BF16)
        g_wq = _mm_tn(sv["xn2"], dq, name=f"g_wq_{l}")
        g_wk = _mm_tn(sv["memn"], dkb, name=f"g_wk_{l}")
        g_wv = _mm_tn(sv["memn"], dvb, name=f"g_wv_{l}")
        parts = [g.reshape(N_DEV, -1, d) for g in (g_wq, g_wk, g_wv, g_wo)]
        parts = scatter(l, "attn", ("wq", "wk", "wv", "wo"), parts)
        dq = _pin(dq, *parts)
        dmemn = _mm_nt(dkb, fw["wk"], out_dtype=F32, name=f"d_memn_k_{l}")
        dmemn = _mm_nt(dvb, fw["wv"], out_dtype=F32, res=dmemn, name=f"d_memn_v_{l}")
        _, _, g_mem_norm = _rms_bwd(memx, dmemn, mem_norm_g[l], None, name=f"d_norm_mem_{l}")
        dh1, dh1b, g_norm_xattn = _mm_nt(dq, fw["wq"], name=f"d_norm_xattn_{l}",
                                         norm_bwd=(sv["h1"], norm_xattn_g[l], dh2))

        dcat = _mm_nt(dh1b, fw["w_out"], out_dtype=F32, name=f"d_cat_{l}")
        g_w_out = _mm_tn(sv["cat"], dh1b, name=f"g_w_out_{l}")
        dproj, g_pool_w, g_pool_scale, g_sgu_g, g_sgu_w, g_sgu_b = _mixer_bwd(
            sv["proj"], dcat, pool_w[l], pool_scale[l].reshape(1, -1), sgu_g[l].reshape(1, -1), sgu_w[l],
            sv["sgu_bias"], name=f"d_mixer_{l}")
        g_w_in_t = _mm_tn(dproj, sv["xn1"], name=f"g_w_in_{l}")
        parts = [g_w_in_t.reshape(N_DEV, -1, d), g_w_out.reshape(N_DEV, -1, d)]
        parts = scatter(l, "mix", ("w_in", "w_out"), parts)
        dh, dhb, g_norm_mix = _mm_nn(_pin(dproj, *parts), fw["w_in_t"], name=f"d_norm_mix_{l}",
                                     norm_bwd=(sv["h0"], norm_mix_g[l], dh1))

        small[l] = dict(norm_mix_g=g_norm_mix.reshape(-1), pool_w=g_pool_w, pool_scale=g_pool_scale.reshape(-1),
                        sgu_g=g_sgu_g.reshape(-1), sgu_w=g_sgu_w, sgu_b=g_sgu_b, norm_xattn_g=g_norm_xattn.reshape(-1),
                        mem_norm_g=g_mem_norm.reshape(-1), norm_ffn_g=g_norm_ffn.reshape(-1),
                        conv_b=g_conv_b.reshape(-1))
    grad_x = dh.reshape(x.shape)

    out = {}
    for nme in _SHARDED:
        view = (lambda a: jnp.swapaxes(a, 1, 2)) if nme in _COLUMN_SHARDED else (lambda a: a)
        w3 = view(W[nme])
        res = _adamw([sl.reshape((N_DEV,) + w3.shape[1:]) for sl in slots[nme]], w3, view(M[nme]), view(V[nme]),
                     name=f"adamw_{nme}")
        out[nme] = [view(r) for r in res]

    small_names = [n for n in _SMALL]
    contrib = []
    for nme in small_names:
        if nme == "final_norm_g":
            contrib.append(dg_final.reshape(-1))
        else:
            contrib.append(jnp.stack([small[l][nme] for l in range(DEPTH)]))
    tail = 8 * _PACK_LANES
    packed_g = _pack(contrib + [jnp.pad(loss_row[0, :1], (0, tail - 1))])
    (all_g,) = _sequencer_exchange([_pin(packed_g, *previous)], scatter=False, name="gather_small_grads")
    rows = packed_g.shape[0]
    loss = jnp.sum(all_g[:, rows - 8, 0])
    zeros = jnp.zeros((tail,), F32)
    res = _adamw([all_g], _pack([W[n] for n in small_names] + [zeros]).reshape(1, rows, -1),
                 _pack([M[n] for n in small_names] + [zeros]).reshape(1, rows, -1),
                 _pack([V[n] for n in small_names] + [zeros]).reshape(1, rows, -1), name="adamw_small", tr=rows // 2)
    unpacked = [_unpack(r, [W[n] for n in small_names]) for r in res]
    for i, nme in enumerate(small_names):
        out[nme] = [unpacked[j][i] for j in range(4)]

    grads = [out[n][0] for n in _WEIGHTS]
    deltas = [out[n][1] for n in _WEIGHTS]
    new_m = [out[n][2] for n in _WEIGHTS]
    new_v = [out[n][3] for n in _WEIGHTS]
    return (loss, grad_x, *grads, *deltas, *new_m, *new_v)
```

```python
import jax
import jax.numpy as jnp
from jax import lax
from jax.experimental import pallas as pl
from jax.experimental.pallas import tpu as pltpu
from jax.experimental.pallas import tpu_sc as plsc

F32 = jnp.float32
BF16 = jnp.bfloat16
MESH = pl.DeviceIdType.MESH

EPS = 1e-6
N_DEV = 8
DEPTH = 2
POOL_WINDOWS = (2, 4, 8, 16)
GROUP = 128
POOL_WIDTH = 512
SGU_WIDTH = 512
HEADS = 4
HEAD_DIM = 256
POOL_HALO = 16
CONV_HALO = 8

ADAM_LR = 0.001
ADAM_B1 = 0.9
ADAM_B2 = 0.999
ADAM_EPS = 1e-08
ADAM_WD = 0.01
ADAM_STEP = 10

VMEM_LIMIT_BYTES = 52 * 1024 * 1024


def _params(*semantics):
    return pltpu.CompilerParams(dimension_semantics=semantics, vmem_limit_bytes=VMEM_LIMIT_BYTES)


def _tile(n, want):
    t = min(n, want)
    assert n % t == 0, (n, want)
    return t


_DOT_DIMS = {
    "nn": (((1,), (0,)), ((), ())),
    "nt": (((1,), (1,)), ((), ())),
    "tn": (((0,), (0,)), ((), ())),
}


def _mm(a, b, *, dims, grid, a_spec, b_spec, o_spec, out_shape, out_dtype, acc_shape, name, res=None, res_spec=None,
        norm_bwd=None, norm_out=None):
    nk = grid[2]
    dn = _DOT_DIMS[dims]
    extras, extra_specs = [], []
    if res is not None:
        extras, extra_specs = [res], [res_spec]
    if norm_bwd is not None:
        h, gain, dres, row_spec, gain_spec = norm_bwd
        extras = [h, gain] + ([dres] if dres is not None else [])
        extra_specs = [row_spec, gain_spec] + ([row_spec] if dres is not None else [])
        out_specs = [row_spec, row_spec, gain_spec]
        out_shapes = [jax.ShapeDtypeStruct(h.shape, F32), jax.ShapeDtypeStruct(h.shape, BF16),
                      jax.ShapeDtypeStruct(gain.shape, F32)]
    elif norm_out is not None:
        extras, extra_specs = extras + [norm_out[0]], extra_specs + [norm_out[1]]
        out_specs = [o_spec, o_spec]
        out_shapes = [jax.ShapeDtypeStruct(out_shape, out_dtype), jax.ShapeDtypeStruct(out_shape, BF16)]
    else:
        out_specs, out_shapes = o_spec, jax.ShapeDtypeStruct(out_shape, out_dtype)
    n_extra = len(extras)

    def body(*refs):
        a_ref, b_ref = refs[:2]
        extra_refs = refs[2:2 + n_extra]
        out_refs = refs[2 + n_extra:len(refs) - (1 if nk > 1 else 0)]
        p = lax.dot_general(a_ref[...], b_ref[...], dn, preferred_element_type=F32)

        def finish(r):
            if norm_bwd is not None:
                _rms_bwd_math(r, extra_refs[0], extra_refs[1], extra_refs[2] if n_extra == 3 else None,
                              *out_refs, first=pl.program_id(0) == 0)
                return
            if res is not None:
                r = r + extra_refs[0][...]
            out_refs[0][...] = r.astype(out_refs[0].dtype)
            if norm_out is not None:
                scale = lax.rsqrt(jnp.mean(r * r, axis=-1, keepdims=True) + EPS)
                out_refs[1][...] = ((r * scale) * extra_refs[-1][...]).astype(BF16)

        if nk == 1:
            finish(p)
        else:
            acc_ref = refs[-1]
            k = pl.program_id(2)

            @pl.when(k == 0)
            def _():
                acc_ref[...] = p

            @pl.when(k > 0)
            def _():
                acc_ref[...] += p

            @pl.when(k == nk - 1)
            def _():
                finish(acc_ref[...])

    scratch = [pltpu.VMEM(acc_shape, F32)] if nk > 1 else []
    return pl.pallas_call(
        body, grid=grid, in_specs=[a_spec, b_spec] + extra_specs, out_specs=out_specs,
        out_shape=out_shapes, scratch_shapes=scratch, name=name,
        compiler_params=_params("arbitrary" if norm_bwd is not None else "parallel", "parallel", "arbitrary"),
    )(a, b, *extras)


def _rms_bwd_math(dy, h_ref, g_ref, dres_ref, dh_ref, dhb_ref, dg_ref, *, first):
    x = h_ref[...]
    r = lax.rsqrt(jnp.mean(x * x, axis=-1, keepdims=True) + EPS)
    a = dy * g_ref[...]
    m = jnp.mean(a * x, axis=-1, keepdims=True)
    dh = r * a - x * (r * r * r * m)
    if dres_ref is not None:
        dh = dh + dres_ref[...]
    dh_ref[...] = dh
    dhb_ref[...] = dh.astype(BF16)
    part = jnp.sum(dy * (x * r), axis=0, keepdims=True)

    @pl.when(first)
    def _():
        dg_ref[...] = part

    @pl.when(jnp.logical_not(first))
    def _():
        dg_ref[...] += part


def _mm_nn(a, b, *, out_dtype=F32, name, res=None, tm=1024, norm_gain=None, norm_bwd=None):
    m, k = a.shape
    n = b.shape[1]
    tm = _tile(m, tm)
    spec_o = pl.BlockSpec((tm, n), lambda i, j, kk: (i, 0))
    norm_out = None if norm_gain is None else (norm_gain.reshape(1, n), pl.BlockSpec((1, n), lambda i, j, kk: (0, 0)))
    return _mm(a, b, dims="nn", grid=(m // tm, 1, 1),
               a_spec=pl.BlockSpec((tm, k), lambda i, j, kk: (i, 0)),
               b_spec=pl.BlockSpec((k, n), lambda i, j, kk: (0, 0)),
               o_spec=spec_o, out_shape=(m, n), out_dtype=out_dtype, acc_shape=None, name=name,
               res=res, res_spec=spec_o if res is not None else None, norm_out=norm_out,
               norm_bwd=None if norm_bwd is None else _norm_bwd_arg(*norm_bwd, tm))


def _norm_bwd_arg(h, gain, dres, tm):
    d = h.shape[1]
    return (h, gain.reshape(1, d), dres, pl.BlockSpec((tm, d), lambda i, j, kk: (i, 0)),
            pl.BlockSpec((1, d), lambda i, j, kk: (0, 0)))


def _mm_nt(a, b, *, out_dtype=F32, name, res=None, tm=1024, norm_bwd=None):
    m, k = a.shape
    n = b.shape[0]
    tm = _tile(m, tm)
    spec_o = pl.BlockSpec((tm, n), lambda i, j, kk: (i, 0))
    return _mm(a, b, dims="nt", grid=(m // tm, 1, 1),
               a_spec=pl.BlockSpec((tm, k), lambda i, j, kk: (i, 0)),
               b_spec=pl.BlockSpec((n, k), lambda i, j, kk: (0, 0)),
               o_spec=spec_o, out_shape=(m, n), out_dtype=out_dtype, acc_shape=None, name=name,
               res=res, res_spec=spec_o if res is not None else None,
               norm_bwd=None if norm_bwd is None else _norm_bwd_arg(*norm_bwd, tm))


_TN_ROWS = 2048


def _mm_tn(a, b, *, name, tm=None, tn=None, ts=_TN_ROWS, out_dtype=BF16):
    s, m = a.shape
    n = b.shape[1]
    tm = m if tm is None else tm
    tn = n if tn is None else tn
    ts = _tile(s, ts)
    return _mm(a, b, dims="tn", grid=(m // tm, n // tn, s // ts),
               a_spec=pl.BlockSpec((ts, tm), lambda i, j, kk: (kk, i)),
               b_spec=pl.BlockSpec((ts, tn), lambda i, j, kk: (kk, j)),
               o_spec=pl.BlockSpec((tm, tn), lambda i, j, kk: (i, j)),
               out_shape=(m, n), out_dtype=out_dtype, acc_shape=(tm, tn), name=name)


def _mm_up_back(dhh, w_up_t, *, name, tm=512, norm_bwd=None):
    _, s, f = dhh.shape
    d = w_up_t.shape[1]
    tm = _tile(s, tm)
    return _mm(dhh, w_up_t, dims="nn", grid=(s // tm, 1, 2),
               a_spec=pl.BlockSpec((None, tm, f), lambda i, j, kk: (kk, i, 0)),
               b_spec=pl.BlockSpec((f, d), lambda i, j, kk: (kk, 0)),
               o_spec=pl.BlockSpec((tm, d), lambda i, j, kk: (i, 0)),
               out_shape=(s, d), out_dtype=F32, acc_shape=(tm, d), name=name,
               norm_bwd=None if norm_bwd is None else _norm_bwd_arg(*norm_bwd, tm))


def _mm_up_grad(dhh, xn, *, name, ts=_TN_ROWS):
    s, d = xn.shape
    f = dhh.shape[2]
    tm = f // 2
    ts = _tile(s, ts)
    return _mm(dhh, xn, dims="tn", grid=(4, 1, s // ts),
               a_spec=pl.BlockSpec((None, ts, tm), lambda i, j, kk: (i // 2, kk, i % 2)),
               b_spec=pl.BlockSpec((ts, d), lambda i, j, kk: (kk, 0)),
               o_spec=pl.BlockSpec((tm, d), lambda i, j, kk: (i, 0)),
               out_shape=(2 * f, d), out_dtype=BF16, acc_shape=(tm, d), name=name)


def _rms_fwd(h, g, *, name, tr=512):
    s, d = h.shape
    tr = _tile(s, tr)

    def body(h_ref, g_ref, o_ref):
        x = h_ref[...]
        r = lax.rsqrt(jnp.mean(x * x, axis=-1, keepdims=True) + EPS)
        o_ref[...] = ((x * r) * g_ref[...]).astype(o_ref.dtype)

    row = pl.BlockSpec((tr, d), lambda i: (i, 0))
    return pl.pallas_call(
        body, grid=(s // tr,), in_specs=[row, pl.BlockSpec((1, d), lambda i: (0, 0))], out_specs=row,
        out_shape=jax.ShapeDtypeStruct((s, d), BF16), name=name, compiler_params=_params("parallel"),
    )(h, g.reshape(1, d))


def _rms_bwd(h, dxn, g, dres, *, name, tr=512):
    s, d = h.shape
    tr = _tile(s, tr)
    has_res = dres is not None

    def body(*refs):
        if has_res:
            h_ref, dxn_ref, g_ref, dres_ref, dh_ref, dhb_ref, dg_ref = refs
        else:
            h_ref, dxn_ref, g_ref, dh_ref, dhb_ref, dg_ref = refs
            dres_ref = None
        _rms_bwd_math(dxn_ref[...].astype(F32), h_ref, g_ref, dres_ref, dh_ref, dhb_ref, dg_ref,
                      first=pl.program_id(0) == 0)

    row = pl.BlockSpec((tr, d), lambda i: (i, 0))
    vec = pl.BlockSpec((1, d), lambda i: (0, 0))
    in_specs = [row, row, vec] + ([row] if has_res else [])
    args = (h, dxn, g.reshape(1, d)) + ((dres,) if has_res else ())
    return pl.pallas_call(
        body, grid=(s // tr,), in_specs=in_specs, out_specs=[row, row, vec],
        out_shape=[jax.ShapeDtypeStruct((s, d), F32), jax.ShapeDtypeStruct((s, d), BF16),
                   jax.ShapeDtypeStruct((1, d), F32)],
        name=name, compiler_params=_params("arbitrary"),
    )(*args)


def _loss_head(h, g, target, *, name, tr=1024):
    s, d = h.shape
    tr = _tile(s, tr)
    nt = s // tr

    def body(h_ref, g_ref, t_ref, dh_ref, dhb_ref, dg_ref, loss_ref, sq_ref):
        i = pl.program_id(0)
        x = h_ref[...]
        gain = g_ref[...]
        r = lax.rsqrt(jnp.mean(x * x, axis=-1, keepdims=True) + EPS)
        xh = x * r
        err = xh * gain - t_ref[...]
        dy = err * (1.0 / d)
        a = dy * gain
        m = jnp.mean(a * x, axis=-1, keepdims=True)
        dh = r * a - x * (r * r * r * m)
        dh_ref[...] = dh
        dhb_ref[...] = dh.astype(BF16)
        dg_part = jnp.sum(dy * xh, axis=0, keepdims=True)
        sq_part = jnp.sum(err * err, axis=0, keepdims=True)

        @pl.when(i == 0)
        def _():
            dg_ref[...] = dg_part
            sq_ref[...] = sq_part

        @pl.when(i > 0)
        def _():
            dg_ref[...] += dg_part
            sq_ref[...] += sq_part

        @pl.when(i == nt - 1)
        def _():
            total = jnp.sum(sq_ref[...], axis=1, keepdims=True) * (0.5 / d)
            loss_ref[...] = jnp.broadcast_to(total, loss_ref.shape)

    row = pl.BlockSpec((tr, d), lambda i: (i, 0))
    vec = pl.BlockSpec((1, d), lambda i: (0, 0))
    return pl.pallas_call(
        body, grid=(nt,), in_specs=[row, vec, row],
        out_specs=[row, row, vec, pl.BlockSpec((1, 128), lambda i: (0, 0))],
        out_shape=[jax.ShapeDtypeStruct((s, d), F32), jax.ShapeDtypeStruct((s, d), BF16),
                   jax.ShapeDtypeStruct((1, d), F32), jax.ShapeDtypeStruct((1, 128), F32)],
        scratch_shapes=[pltpu.VMEM((1, d), F32)], name=name, compiler_params=_params("arbitrary"),
    )(h, g.reshape(1, d), target)


_SQRT_HALF = 0.7071067811865476
_INV_SQRT_2PI = 0.3989422804014327


def _gelu(x):
    return 0.5 * x * (1.0 + lax.erf(x * _SQRT_HALF))


def _gelu_and_grad(x):
    cdf = 0.5 * (1.0 + lax.erf(x * _SQRT_HALF))
    return x * cdf, cdf + x * (jnp.exp(-0.5 * x * x) * _INV_SQRT_2PI)


def _trailing_sums(xe, win):
    s = xe
    sh = 1
    while sh < win:
        s = s + pltpu.roll(s, sh, 0)
        sh *= 2
    return s


def _leading_sums(xe, win):
    n = xe.shape[0]
    s = xe
    sh = 1
    while sh < win:
        s = s + pltpu.roll(s, n - sh, 0)
        sh *= 2
    return s


def _tril_mask():
    return lax.broadcasted_iota(jnp.int32, (GROUP, GROUP), 0) >= lax.broadcasted_iota(jnp.int32, (GROUP, GROUP), 1)


def _layernorm_stats(v):
    mu = jnp.mean(v, axis=-1, keepdims=True)
    xc = v - mu
    rstd = lax.rsqrt(jnp.mean(xc * xc, axis=-1, keepdims=True) + EPS)
    return xc * rstd, rstd


def _mixer_specs(s, t):
    halo_blocks = t // POOL_HALO
    tile = lambda w: pl.BlockSpec((t, w), lambda i: (i, 0))
    prev = pl.BlockSpec((POOL_HALO, POOL_WIDTH), lambda i: (jnp.maximum(i * halo_blocks - 1, 0), 0))
    nxt = pl.BlockSpec((POOL_HALO, POOL_WIDTH),
                       lambda i: (jnp.minimum((i + 1) * halo_blocks, s // POOL_HALO - 1), 0))
    const3 = pl.BlockSpec((HEADS, GROUP, GROUP), lambda i: (0, 0, 0))
    vec = pl.BlockSpec((1, POOL_WIDTH), lambda i: (0, 0))
    bias = pl.BlockSpec((GROUP, SGU_WIDTH), lambda i: (0, 0))
    return tile, prev, nxt, const3, vec, bias


def _proj_mixer(xn, w_in_t, pool_w, pool_scale, sgu_g, sgu_w, sgu_bias, *, name, t=512):
    s, d = xn.shape
    n = w_in_t.shape[0]
    t = _tile(s, t)
    nt = s // t

    def body(xn_ref, wt_ref, pw_ref, ps_ref, sg_ref, sw_ref, sb_ref, proj_ref, cat_ref, held_ref, above_ref, w_ref):
        i = pl.program_id(0)

        @pl.when(i == 0)
        def _():
            held_ref[...] = jnp.zeros_like(held_ref)
            above_ref[...] = jnp.zeros_like(above_ref)
            w_ref[...] = wt_ref[...].astype(F32).T.astype(BF16)

        row = jnp.maximum((i - 1) * t + lax.broadcasted_iota(jnp.int32, (t, 1), 0), 0)
        p = held_ref[:, 0:POOL_WIDTH]
        pe = jnp.concatenate([above_ref[...], p], axis=0)
        for gi, win in enumerate(POOL_WINDOWS):
            cols = slice(gi * GROUP, (gi + 1) * GROUP)
            count = jnp.minimum(row + 1, win).astype(F32)
            dev = _trailing_sums(pe[:, cols], win)[POOL_HALO:] / count - p[:, cols]
            y = jnp.dot(dev.astype(BF16), pw_ref[gi].astype(BF16), preferred_element_type=F32) * ps_ref[:, cols]
            cat_ref[:, cols] = y.astype(BF16)
        above_ref[...] = held_ref[t - POOL_HALO:t, 0:POOL_WIDTH]

        u = _gelu(held_ref[:, POOL_WIDTH:POOL_WIDTH + SGU_WIDTH])
        xhat, _ = _layernorm_stats(_gelu(held_ref[:, POOL_WIDTH + SGU_WIDTH:]))
        vn = (xhat * sg_ref[...]).astype(BF16)
        tri = _tril_mask()
        for h in range(HEADS):
            cols = slice(h * GROUP, (h + 1) * GROUP)
            w = jnp.where(tri, sw_ref[h], 0.0).astype(BF16)
            for c in range(t // GROUP):
                rows = slice(c * GROUP, (c + 1) * GROUP)
                z = jnp.dot(w, vn[rows, cols], preferred_element_type=F32) + sb_ref[:, cols]
                cat_ref[rows, POOL_WIDTH + h * GROUP:POOL_WIDTH + (h + 1) * GROUP] = (u[rows, cols] * z).astype(BF16)

        y = jnp.dot(xn_ref[...], w_ref[...], preferred_element_type=F32)
        proj_ref[...] = y
        held_ref[...] = y

    row_of = lambda i: jnp.minimum(i, nt - 1)
    const3 = pl.BlockSpec((HEADS, GROUP, GROUP), lambda i: (0, 0, 0))
    vec = pl.BlockSpec((1, POOL_WIDTH), lambda i: (0, 0))
    return pl.pallas_call(
        body, grid=(nt + 1,),
        in_specs=[pl.BlockSpec((t, d), lambda i: (row_of(i), 0)), pl.BlockSpec((n, d), lambda i: (0, 0)),
                  const3, vec, vec, const3, pl.BlockSpec((GROUP, SGU_WIDTH), lambda i: (0, 0))],
        out_specs=[pl.BlockSpec((t, n), lambda i: (row_of(i), 0)),
                   pl.BlockSpec((t, POOL_WIDTH + SGU_WIDTH), lambda i: (jnp.maximum(i - 1, 0), 0))],
        out_shape=[jax.ShapeDtypeStruct((s, n), F32), jax.ShapeDtypeStruct((s, POOL_WIDTH + SGU_WIDTH), BF16)],
        scratch_shapes=[pltpu.VMEM((t, n), F32), pltpu.VMEM((POOL_HALO, POOL_WIDTH), F32), pltpu.VMEM((d, n), BF16)],
        name=name, compiler_params=_params("arbitrary"),
    )(xn, w_in_t, pool_w, pool_scale, sgu_g, sgu_w, sgu_bias)


def _mixer_bwd(proj, dcat, pool_w, pool_scale, sgu_g, sgu_w, sgu_bias, *, name, t=512):
    s = proj.shape[0]
    t = _tile(s, t)
    nt = s // t
    tile, prev, nxt, const3, vec, bias = _mixer_specs(s, t)

    def body(proj_ref, halo_ref, dcat_ref, dnext_ref, pw_ref, ps_ref, sg_ref, sw_ref, sb_ref,
             dproj_ref, dpw_ref, dps_ref, dsg_ref, dsw_ref, dsb_ref, du_ref, dvn_ref, dz_ref):
        i = pl.program_id(0)

        @pl.when(i == 0)
        def _():
            dpw_ref[...] = jnp.zeros_like(dpw_ref)
            dps_ref[...] = jnp.zeros_like(dps_ref)
            dsg_ref[...] = jnp.zeros_like(dsg_ref)
            dsw_ref[...] = jnp.zeros_like(dsw_ref)
            dz_ref[...] = jnp.zeros_like(dz_ref)

        row = i * t + lax.broadcasted_iota(jnp.int32, (t, 1), 0)
        row_e = i * t + lax.broadcasted_iota(jnp.int32, (t + POOL_HALO, 1), 0)
        p = proj_ref[:, 0:POOL_WIDTH]
        pe = jnp.concatenate([jnp.where(i > 0, halo_ref[...], 0.0), p], axis=0)
        dyp = dcat_ref[:, 0:POOL_WIDTH]
        dye = jnp.concatenate([dyp, jnp.where(i < nt - 1, dnext_ref[...], 0.0)], axis=0)
        for gi, win in enumerate(POOL_WINDOWS):
            cols = slice(gi * GROUP, (gi + 1) * GROUP)
            count = jnp.minimum(row + 1, win).astype(F32)
            d = (_trailing_sums(pe[:, cols], win)[POOL_HALO:] / count - p[:, cols]).astype(BF16)
            pw = pw_ref[gi].astype(BF16)
            pre = jnp.dot(d, pw, preferred_element_type=F32)
            dps_ref[:, cols] += jnp.sum(dyp[:, cols] * pre, axis=0, keepdims=True)
            ys = (dye[:, cols] * ps_ref[:, cols]).astype(BF16)
            dpw_ref[gi] += lax.dot_general(d, ys[:t], _DOT_DIMS["tn"], preferred_element_type=F32)
            dd = lax.dot_general(ys, pw, _DOT_DIMS["nt"], preferred_element_type=F32)
            count_e = jnp.minimum(row_e + 1, win).astype(F32)
            dp = _leading_sums(dd / count_e, win)[:t] - dd[:t]
            dproj_ref[:, cols] = dp.astype(BF16)

        xu = proj_ref[:, POOL_WIDTH:POOL_WIDTH + SGU_WIDTH]
        xv = proj_ref[:, POOL_WIDTH + SGU_WIDTH:]
        u, gelu_grad_u = _gelu_and_grad(xu)
        v, gelu_grad_v = _gelu_and_grad(xv)
        xhat, rstd = _layernorm_stats(v)
        gain = sg_ref[...]
        vn = (xhat * gain).astype(BF16)
        tri = _tril_mask()
        for h in range(HEADS):
            cols = slice(h * GROUP, (h + 1) * GROUP)
            wf = jnp.where(tri, sw_ref[h], 0.0)
            w, wt = wf.astype(BF16), wf.T.astype(BF16)
            for c in range(t // GROUP):
                rows = slice(c * GROUP, (c + 1) * GROUP)
                vch = vn[rows, cols]
                z = jnp.dot(w, vch, preferred_element_type=F32) + sb_ref[:, cols]
                dy = dcat_ref[rows, POOL_WIDTH + h * GROUP:POOL_WIDTH + (h + 1) * GROUP]
                du_ref[rows, cols] = dy * z
                dz = dy * u[rows, cols]
                dz_ref[:, cols] += dz
                dzb = dz.astype(BF16)
                dsw_ref[h] += lax.dot_general(dzb, vch, _DOT_DIMS["nt"], preferred_element_type=F32)
                dvn_ref[rows, cols] = jnp.dot(wt, dzb, preferred_element_type=F32)
        dvn = dvn_ref[...]
        dsg_ref[...] += jnp.sum(dvn * xhat, axis=0, keepdims=True)
        dxh = dvn * gain
        dv = rstd * (dxh - jnp.mean(dxh, axis=-1, keepdims=True)
                     - xhat * jnp.mean(dxh * xhat, axis=-1, keepdims=True))
        dproj_ref[:, POOL_WIDTH:POOL_WIDTH + SGU_WIDTH] = (du_ref[...] * gelu_grad_u).astype(BF16)
        dproj_ref[:, POOL_WIDTH + SGU_WIDTH:] = (dv * gelu_grad_v).astype(BF16)

        @pl.when(i == nt - 1)
        def _():
            for h in range(HEADS):
                dsw_ref[h] = jnp.where(tri, dsw_ref[h], 0.0)
            lane = lax.broadcasted_iota(jnp.int32, (GROUP, GROUP), 1)
            out = jnp.zeros((GROUP, GROUP), F32)
            for h in range(HEADS):
                sh = jnp.sum(dz_ref[:, h * GROUP:(h + 1) * GROUP], axis=1, keepdims=True)
                out = jnp.where(lane == h, sh, out)
            dsb_ref[...] = out

    outs = pl.pallas_call(
        body, grid=(nt,),
        in_specs=[tile(POOL_WIDTH + 2 * SGU_WIDTH), prev, tile(POOL_WIDTH + SGU_WIDTH), nxt,
                  const3, vec, vec, const3, bias],
        out_specs=[tile(POOL_WIDTH + 2 * SGU_WIDTH), const3, vec, vec, const3,
                   pl.BlockSpec((GROUP, GROUP), lambda i: (0, 0))],
        out_shape=[jax.ShapeDtypeStruct((s, POOL_WIDTH + 2 * SGU_WIDTH), BF16),
                   jax.ShapeDtypeStruct((HEADS, GROUP, GROUP), F32),
                   jax.ShapeDtypeStruct((1, POOL_WIDTH), F32),
                   jax.ShapeDtypeStruct((1, SGU_WIDTH), F32),
                   jax.ShapeDtypeStruct((HEADS, GROUP, GROUP), F32),
                   jax.ShapeDtypeStruct((GROUP, GROUP), F32)],
        scratch_shapes=[pltpu.VMEM((t, SGU_WIDTH), F32), pltpu.VMEM((t, SGU_WIDTH), F32),
                        pltpu.VMEM((GROUP, SGU_WIDTH), F32)],
        name=name, compiler_params=_params("arbitrary"),
    )(proj, proj, dcat, dcat, pool_w, pool_scale, sgu_g, sgu_w, sgu_bias)
    dproj, dpw, dps, dsg, dsw, dsb = outs
    return dproj, dpw, dps, dsg, dsw, dsb[:, :HEADS].T


def _attn_probs(q, k, scale):
    sc = lax.dot_general(q, k, _DOT_DIMS["nt"], preferred_element_type=F32) * scale
    sc = sc - jnp.max(sc, axis=-1, keepdims=True)
    e = jnp.exp(sc)
    return e / jnp.sum(e, axis=-1, keepdims=True)


def _attn_fwd(q, k, v, *, name, t=2048):
    s, d = q.shape
    nm = k.shape[0]
    t = _tile(s, t)
    scale = HEAD_DIM ** -0.5

    def body(q_ref, k_ref, v_ref, o_ref):
        for h in range(HEADS):
            cols = slice(h * HEAD_DIM, (h + 1) * HEAD_DIM)
            pr = _attn_probs(q_ref[:, cols], k_ref[:, cols], scale)
            o_ref[:, cols] = jnp.dot(pr.astype(BF16), v_ref[:, cols], preferred_element_type=F32).astype(BF16)

    row = pl.BlockSpec((t, d), lambda i: (i, 0))
    kv = pl.BlockSpec((nm, d), lambda i: (0, 0))
    return pl.pallas_call(
        body, grid=(s // t,), in_specs=[row, kv, kv], out_specs=row,
        out_shape=jax.ShapeDtypeStruct((s, d), BF16), name=name, compiler_params=_params("parallel"),
    )(q, k, v)


def _attn_bwd(q, k, v, do, *, name, t=2048):
    s, d = q.shape
    nm = k.shape[0]
    t = _tile(s, t)
    scale = HEAD_DIM ** -0.5

    def body(q_ref, k_ref, v_ref, do_ref, dq_ref, dk_ref, dv_ref):
        i = pl.program_id(0)

        @pl.when(i == 0)
        def _():
            dk_ref[...] = jnp.zeros_like(dk_ref)
            dv_ref[...] = jnp.zeros_like(dv_ref)

        for h in range(HEADS):
            cols = slice(h * HEAD_DIM, (h + 1) * HEAD_DIM)
            qh, kh, vh, doh = q_ref[:, cols], k_ref[:, cols], v_ref[:, cols], do_ref[:, cols]
            pr = _attn_probs(qh, kh, scale)
            dpr = lax.dot_general(doh, vh, _DOT_DIMS["nt"], preferred_element_type=F32)
            ds = (pr * (dpr - jnp.sum(dpr * pr, axis=-1, keepdims=True)) * scale).astype(BF16)
            dv_ref[:, cols] += lax.dot_general(pr.astype(BF16), doh, _DOT_DIMS["tn"], preferred_element_type=F32)
            dk_ref[:, cols] += lax.dot_general(ds, qh, _DOT_DIMS["tn"], preferred_element_type=F32)
            dq_ref[:, cols] = jnp.dot(ds, kh, preferred_element_type=F32).astype(BF16)

    row = pl.BlockSpec((t, d), lambda i: (i, 0))
    kv = pl.BlockSpec((nm, d), lambda i: (0, 0))
    return pl.pallas_call(
        body, grid=(s // t,), in_specs=[row, kv, kv, row], out_specs=[row, kv, kv],
        out_shape=[jax.ShapeDtypeStruct((s, d), BF16), jax.ShapeDtypeStruct((nm, d), F32),
                   jax.ShapeDtypeStruct((nm, d), F32)],
        name=name, compiler_params=_params("arbitrary"),
    )(q, k, v, do)


def _conv3(w_ref, p, x2, x1, x0, b):
    return (w_ref[p, 0:1, :] * x2 + w_ref[p, 1:2, :] * x1 + w_ref[p, 2:3, :] * x0) + b


def _ffn_up_gate(xn, w_up_t, cw, cb, *, name, t=256, tc=1408):
    s, d = xn.shape
    f = w_up_t.shape[0] // 2
    t = _tile(s, t)
    nt, nj = s // t, f // tc

    def body(xn_ref, wg_ref, wv_ref, cw_ref, cb_ref, hh_ref, act_ref, held_ref, above_ref, w_ref):
        i = pl.program_id(1)

        @pl.when(i == 0)
        def _():
            held_ref[...] = jnp.zeros_like(held_ref)
            above_ref[...] = jnp.zeros_like(above_ref)
            w_ref[0] = wg_ref[...].astype(F32).T.astype(BF16)
            w_ref[1] = wv_ref[...].astype(F32).T.astype(BF16)

        hc = []
        for p in range(2):
            xe = jnp.concatenate([above_ref[p], held_ref[p]], axis=0)
            hc.append(_conv3(cw_ref, p, pltpu.roll(xe, 2, 0), pltpu.roll(xe, 1, 0), xe, cb_ref[p])[CONV_HALO:])
            above_ref[p] = held_ref[p, t - CONV_HALO:t, :]
        gate, val = hc
        act_ref[...] = ((gate * jax.nn.sigmoid(gate)) * val).astype(BF16)

        x = xn_ref[...]
        for p in range(2):
            y = jnp.dot(x, w_ref[p], preferred_element_type=F32)
            hh_ref[p] = y
            held_ref[p] = y

    row = lambda i: jnp.minimum(i, nt - 1)
    return pl.pallas_call(
        body, grid=(nj, nt + 1),
        in_specs=[pl.BlockSpec((t, d), lambda j, i: (row(i), 0)),
                  pl.BlockSpec((tc, d), lambda j, i: (j, 0)),
                  pl.BlockSpec((tc, d), lambda j, i: (nj + j, 0)),
                  pl.BlockSpec((2, 3, tc), lambda j, i: (0, 0, j)),
                  pl.BlockSpec((2, 1, tc), lambda j, i: (0, 0, j))],
        out_specs=[pl.BlockSpec((2, t, tc), lambda j, i: (0, row(i), j)),
                   pl.BlockSpec((t, tc), lambda j, i: (jnp.maximum(i - 1, 0), j))],
        out_shape=[jax.ShapeDtypeStruct((2, s, f), F32), jax.ShapeDtypeStruct((s, f), BF16)],
        scratch_shapes=[pltpu.VMEM((2, t, tc), F32), pltpu.VMEM((2, CONV_HALO, tc), F32), pltpu.VMEM((2, d, tc), BF16)],
        name=name, compiler_params=_params("arbitrary", "arbitrary"),
    )(xn, w_up_t, w_up_t, cw, cb)


def _convgate_bwd(hh, dact, cw, cb, *, name, t=256, tc=1408):
    _, s, f = hh.shape
    t = _tile(s, t)
    nt, nj = s // t, f // tc
    hb = t // CONV_HALO
    m = t + CONV_HALO

    def body(hh_ref, prev_ref, next_ref, da_ref, danext_ref, cw_ref, cb_ref, dhh_ref, dcw_ref, dcb_ref):
        i = pl.program_id(1)
        is_last = i == nt - 1

        @pl.when(i == 0)
        def _():
            dcw_ref[...] = jnp.zeros_like(dcw_ref)
            dcb_ref[...] = jnp.zeros_like(dcb_ref)

        taps, hc = [], []
        for p in range(2):
            xe = jnp.concatenate([jnp.where(i > 0, prev_ref[p], 0.0), hh_ref[p],
                                  jnp.where(is_last, 0.0, next_ref[p])], axis=0)
            x2, x1 = pltpu.roll(xe, 2, 0), pltpu.roll(xe, 1, 0)
            hc.append(_conv3(cw_ref, p, x2, x1, xe, cb_ref[p])[CONV_HALO:])
            taps.append((x2[CONV_HALO:CONV_HALO + t], x1[CONV_HALO:CONV_HALO + t], xe[CONV_HALO:CONV_HALO + t]))
        gate, val = hc
        dae = jnp.concatenate([da_ref[...], jnp.where(is_last, 0.0, danext_ref[...])], axis=0)
        sg = jax.nn.sigmoid(gate)
        dval = dae * (gate * sg)
        dgate = dae * val * (sg * (1.0 + gate * (1.0 - sg)))
        for p, dhc in enumerate((dgate, dval)):
            dh = (cw_ref[p, 2:3, :] * dhc + cw_ref[p, 1:2, :] * pltpu.roll(dhc, m - 1, 0)
                  + cw_ref[p, 0:1, :] * pltpu.roll(dhc, m - 2, 0))
            dhh_ref[p] = dh[:t].astype(BF16)
            d0 = dhc[:t]
            for kk, tap in enumerate(taps[p]):
                dcw_ref[p, kk:kk + 1, :] += jnp.sum(d0 * tap, axis=0, keepdims=True)
            dcb_ref[p] += jnp.sum(d0, axis=0, keepdims=True)

    below = lambda i: jnp.minimum((i + 1) * hb, s // CONV_HALO - 1)
    return pl.pallas_call(
        body, grid=(nj, nt),
        in_specs=[pl.BlockSpec((2, t, tc), lambda j, i: (0, i, j)),
                  pl.BlockSpec((2, CONV_HALO, tc), lambda j, i: (0, jnp.maximum(i * hb - 1, 0), j)),
                  pl.BlockSpec((2, CONV_HALO, tc), lambda j, i: (0, below(i), j)),
                  pl.BlockSpec((t, tc), lambda j, i: (i, j)),
                  pl.BlockSpec((CONV_HALO, tc), lambda j, i: (below(i), j)),
                  pl.BlockSpec((2, 3, tc), lambda j, i: (0, 0, j)),
                  pl.BlockSpec((2, 1, tc), lambda j, i: (0, 0, j))],
        out_specs=[pl.BlockSpec((2, t, tc), lambda j, i: (0, i, j)),
                   pl.BlockSpec((2, 3, tc), lambda j, i: (0, 0, j)),
                   pl.BlockSpec((2, 1, tc), lambda j, i: (0, 0, j))],
        out_shape=[jax.ShapeDtypeStruct((2, s, f), BF16), jax.ShapeDtypeStruct((2, 3, f), F32),
                   jax.ShapeDtypeStruct((2, 1, f), F32)],
        name=name, compiler_params=_params("parallel", "arbitrary"),
    )(hh, hh, hh, dact, dact, cw, cb)


def _position():
    return lax.axis_index("x"), lax.axis_index("y"), lax.axis_index("c")


def _linear(px, py, pc):
    return 4 * px + 2 * py + pc


def _peers_of(x, y, c):
    peers = []
    for mask in range(1, N_DEV):
        peers.append((1 - x if mask & 4 else x, 1 - y if mask & 2 else y, 1 - c if mask & 1 else c))
    return peers


def _exchange_copy(src_ref, land_ref, send_sem, recv_sem, peer, mine, scatter, arriving):
    src = src_ref.at[_linear(*peer)] if scatter else src_ref
    dst = land_ref.at[_linear(*peer) if arriving else mine]
    return pltpu.make_async_remote_copy(src_ref=src, dst_ref=dst, send_sem=send_sem, recv_sem=recv_sem,
                                        device_id=peer, device_id_type=MESH)


_EXCHANGE_COLLECTIVE_ID = 7


def _sequencer_exchange(srcs, *, scatter, name):
    n = len(srcs)
    src_refs = [jax.new_ref(a, memory_space=pltpu.MemorySpace.HBM) for a in srcs]
    land_refs = [jax.empty_ref(jax.ShapeDtypeStruct(a.shape if scatter else (N_DEV,) + a.shape, a.dtype),
                               memory_space=pltpu.MemorySpace.HBM) for a in srcs]

    @pl.kernel(mesh=plsc.ScalarSubcoreMesh(axis_name="sequencer", num_cores=1), name=name,
               scratch_types=(pltpu.SemaphoreType.DMA((7 * n,)), pltpu.SemaphoreType.DMA((7 * n,)),
                              pltpu.SemaphoreType.DMA((n,))),
               compiler_params=pltpu.CompilerParams(collective_id=_EXCHANGE_COLLECTIVE_ID))
    def launch(send_sems, recv_sems, local_sems):
        x, y, c = _position()
        mine = _linear(x, y, c)
        peers = _peers_of(x, y, c)
        barrier = pltpu.get_barrier_semaphore()
        for peer in peers:
            pl.semaphore_signal(barrier, inc=1, device_id=peer, device_id_type=MESH)
        pl.semaphore_wait(barrier, N_DEV - 1)
        local = [pltpu.make_async_copy(src_refs[t].at[mine] if scatter else src_refs[t], land_refs[t].at[mine],
                                       local_sems.at[t]) for t in range(n)]
        for cp in local:
            cp.start()
        if scatter:
            sends = []
            for t in range(n):
                for k, peer in enumerate(peers):
                    cp = _exchange_copy(src_refs[t], land_refs[t], send_sems.at[7 * t + k], recv_sems.at[7 * t + k],
                                        peer, mine, scatter, arriving=False)
                    cp.start()
                    sends.append(cp)
            for t in range(n):
                for k, peer in enumerate(peers):
                    _exchange_copy(src_refs[t], land_refs[t], send_sems.at[7 * t + k], recv_sems.at[7 * t + k],
                                   peer, mine, scatter, arriving=True).wait_recv()
        else:
            me, sibling = (x, y, c), (x, y, 1 - c)
            chips = [(1 - x, y), (x, 1 - y), (1 - x, 1 - y)]

            def copy(t, k, block, to, src=None):
                dst = land_refs[t].at[_linear(*block)]
                return pltpu.make_async_remote_copy(
                    src_ref=dst if src is None else src, dst_ref=dst, send_sem=send_sems.at[7 * t + k],
                    recv_sem=recv_sems.at[7 * t + k], device_id=to, device_id_type=MESH)

            sends = []
            for t in range(n):
                sends.append(copy(t, 0, me, sibling, src=src_refs[t]))
                for j, chip in enumerate(chips):
                    sends.append(copy(t, 1 + j, me, (*chip, c), src=src_refs[t]))
            for cp in sends:
                cp.start()
            for j, chip in enumerate(chips):
                for t in range(n):
                    copy(t, 1 + j, (*chip, c), me).wait_recv()
                    passed = copy(t, 4 + j, (*chip, c), sibling)
                    passed.start()
                    sends.append(passed)
            for t in range(n):
                copy(t, 0, sibling, me).wait_recv()
                for j, chip in enumerate(chips):
                    copy(t, 4 + j, (*chip, 1 - c), me).wait_recv()
        for cp in local:
            cp.wait()
        for cp in sends:
            cp.wait_send()

    launch()
    return [r[...] for r in land_refs]


def _adamw_math(g, w, m, v):
    m2 = ADAM_B1 * m + (1.0 - ADAM_B1) * g
    v2 = ADAM_B2 * v + (1.0 - ADAM_B2) * (g * g)
    m_hat = m2 / (1.0 - ADAM_B1 ** ADAM_STEP)
    v_hat = v2 / (1.0 - ADAM_B2 ** ADAM_STEP)
    delta = -ADAM_LR * (m_hat / (jnp.sqrt(v_hat) + ADAM_EPS) + ADAM_WD * w)
    return delta, m2, v2


def _adamw(slots, w, m, v, *, name, tr=256):
    depth = len(slots)
    _, r, c = slots[0].shape
    tr = next((cand for cand in range(min(r, tr), 15, -1) if r % cand == 0 and cand % 16 == 0), r)

    def body(*refs):
        s_refs = refs[:depth]
        w_ref, m_ref, v_ref, g_ref, d_ref, m2_ref, v2_ref = refs[depth:]
        layer = pl.program_id(0)
        for l in range(depth):
            @pl.when(layer == l)
            def _():
                g = s_refs[l][0].astype(F32)
                for d in range(1, N_DEV):
                    g = g + s_refs[l][d].astype(F32)
                delta, m2, v2 = _adamw_math(g, w_ref[...], m_ref[...], v_ref[...])
                g_ref[...] = g
                d_ref[...] = delta
                m2_ref[...] = m2
                v2_ref[...] = v2

    blk = pl.BlockSpec((None, tr, c), lambda layer, i: (layer, i, 0))
    sblks = [pl.BlockSpec((N_DEV, tr, c), lambda layer, i, l=l: (0, jnp.where(layer == l, i, 0), 0))
             for l in range(depth)]
    shape = jax.ShapeDtypeStruct((depth, r, c), F32)
    return pl.pallas_call(
        body, grid=(depth, r // tr), in_specs=sblks + [blk, blk, blk], out_specs=[blk] * 4,
        out_shape=[shape] * 4, name=name, compiler_params=_params("arbitrary", "arbitrary"),
    )(*slots, w, m, v)


_SHARDED = ("w_in", "w_out", "wq", "wk", "wv", "wo", "w_up", "conv_w", "w_down")
_SMALL = ("norm_mix_g", "pool_w", "pool_scale", "sgu_g", "sgu_w", "sgu_b", "norm_xattn_g", "mem_norm_g",
          "norm_ffn_g", "conv_b", "final_norm_g")
_WEIGHTS = ("norm_mix_g", "w_in", "pool_w", "pool_scale", "sgu_g", "sgu_w", "sgu_b", "w_out", "norm_xattn_g",
            "mem_norm_g", "wq", "wk", "wv", "wo", "norm_ffn_g", "w_up", "conv_w", "conv_b", "w_down",
            "final_norm_g")
_PACK_LANES = 128
_GATHER_GROUPS = (("w_in",), ("w_out",), ("wq", "wk", "wv", "wo"), ("w_up", "conv_w", "w_down"))
_COLUMN_SHARDED = ("w_in", "w_up")


def _cols_to_blocks(a, *, name, tr=256):
    r, c8 = a.shape
    c = c8 // N_DEV
    tr = _tile(r, tr)

    def body(a_ref, o_ref):
        for dev in range(N_DEV):
            o_ref[dev] = a_ref[:, dev * c:(dev + 1) * c]

    return pl.pallas_call(
        body, grid=(r // tr,), in_specs=[pl.BlockSpec((tr, c8), lambda i: (i, 0))],
        out_specs=pl.BlockSpec((N_DEV, tr, c), lambda i: (0, i, 0)),
        out_shape=jax.ShapeDtypeStruct((N_DEV, r, c), a.dtype), name=name, compiler_params=_params("parallel"),
    )(a)


def _blocks_to_cols(a, *, name, tr=256):
    n, r, c = a.shape
    tr = _tile(r, tr)

    def body(a_ref, o_ref):
        for dev in range(n):
            o_ref[:, dev * c:(dev + 1) * c] = a_ref[dev]

    return pl.pallas_call(
        body, grid=(r // tr,), in_specs=[pl.BlockSpec((n, tr, c), lambda i: (0, i, 0))],
        out_specs=pl.BlockSpec((tr, n * c), lambda i: (i, 0)),
        out_shape=jax.ShapeDtypeStruct((r, n * c), a.dtype), name=name, compiler_params=_params("parallel"),
    )(a)


def _pin(x, *deps):
    return lax.optimization_barrier((x, *deps))[0]


def _pack(arrays):
    flat = jnp.concatenate([a.reshape(-1) for a in arrays])
    assert flat.shape[0] % (8 * _PACK_LANES) == 0
    return flat.reshape(-1, _PACK_LANES)


def _unpack(packed, like):
    flat = packed.reshape(-1)
    out, off = [], 0
    for a in like:
        out.append(flat[off:off + a.size].reshape(a.shape))
        off += a.size
    return out


def kernel(x, mem, norm_mix_g, w_in, pool_w, pool_scale, sgu_g, sgu_w, sgu_b, w_out, norm_xattn_g, mem_norm_g, wq, wk, wv, wo, norm_ffn_g, w_up, conv_w, conv_b, w_down, final_norm_g, loss_target, m_norm_mix_g, m_w_in, m_pool_w, m_pool_scale, m_sgu_g, m_sgu_w, m_sgu_b, m_w_out, m_norm_xattn_g, m_mem_norm_g, m_wq, m_wk, m_wv, m_wo, m_norm_ffn_g, m_w_up, m_conv_w, m_conv_b, m_w_down, m_final_norm_g, v_norm_mix_g, v_w_in, v_pool_w, v_pool_scale, v_sgu_g, v_sgu_w, v_sgu_b, v_w_out, v_norm_xattn_g, v_mem_norm_g, v_wq, v_wk, v_wv, v_wo, v_norm_ffn_g, v_w_up, v_conv_w, v_conv_b, v_w_down, v_final_norm_g):
    W = dict(norm_mix_g=norm_mix_g, w_in=w_in, pool_w=pool_w, pool_scale=pool_scale, sgu_g=sgu_g, sgu_w=sgu_w,
             sgu_b=sgu_b, w_out=w_out, norm_xattn_g=norm_xattn_g, mem_norm_g=mem_norm_g, wq=wq, wk=wk, wv=wv, wo=wo,
             norm_ffn_g=norm_ffn_g, w_up=w_up, conv_w=conv_w, conv_b=conv_b, w_down=w_down,
             final_norm_g=final_norm_g)
    M = dict(norm_mix_g=m_norm_mix_g, w_in=m_w_in, pool_w=m_pool_w, pool_scale=m_pool_scale, sgu_g=m_sgu_g,
             sgu_w=m_sgu_w, sgu_b=m_sgu_b, w_out=m_w_out, norm_xattn_g=m_norm_xattn_g, mem_norm_g=m_mem_norm_g,
             wq=m_wq, wk=m_wk, wv=m_wv, wo=m_wo, norm_ffn_g=m_norm_ffn_g, w_up=m_w_up, conv_w=m_conv_w,
             conv_b=m_conv_b, w_down=m_w_down, final_norm_g=m_final_norm_g)
    V = dict(norm_mix_g=v_norm_mix_g, w_in=v_w_in, pool_w=v_pool_w, pool_scale=v_pool_scale, sgu_g=v_sgu_g,
             sgu_w=v_sgu_w, sgu_b=v_sgu_b, w_out=v_w_out, norm_xattn_g=v_norm_xattn_g, mem_norm_g=v_mem_norm_g,
             wq=v_wq, wk=v_wk, wv=v_wv, wo=v_wo, norm_ffn_g=v_norm_ffn_g, w_up=v_w_up, conv_w=v_conv_w,
             conv_b=v_conv_b, w_down=v_w_down, final_norm_g=v_final_norm_g)

    s, d = x.shape[1], x.shape[2]
    f = w_down.shape[1] * N_DEV
    h = x.reshape(s, d)
    memx = mem.reshape(mem.shape[1], d)
    target = loss_target.reshape(s, d)

    gathered = {}

    def launch_gather(l, gi, after):
        if l >= DEPTH:
            return
        names = _GATHER_GROUPS[gi]
        shards = [W[nme][l] if nme == "conv_w" else
                  (W[nme][l].T if nme in _COLUMN_SHARDED else W[nme][l]).astype(BF16) for nme in names]
        if after is not None:
            shards[0], _ = lax.optimization_barrier((shards[0], after))
        gathered[l, gi] = dict(zip(names, _sequencer_exchange(shards, scatter=False, name=f"gather_{l}_{gi}")))

    launch_gather(0, 0, None)

    saved, full = [], []
    for l in range(DEPTH):
        sgu_bias = jnp.repeat(sgu_b[l].T, GROUP, axis=1)
        cb = conv_b[l].reshape(2, 1, f)
        if l == 0:
            xn1 = _rms_fwd(h, norm_mix_g[l], name=f"norm_mix_{l}")
            launch_gather(0, 1, xn1)
        memn = _rms_fwd(_pin(memx, xn1) if l == 0 else memx, mem_norm_g[l], name=f"norm_mem_{l}")
        if l == 0:
            launch_gather(0, 2, memn)
        w_in_t = gathered[l, 0]["w_in"].reshape(-1, d)
        proj, cat = _proj_mixer(xn1, w_in_t, pool_w[l], pool_scale[l].reshape(1, -1), sgu_g[l].reshape(1, -1),
                                sgu_w[l], sgu_bias, name=f"proj_mixer_{l}")
        if l == 0:
            launch_gather(0, 3, cat)
        w_out_f = gathered[l, 1]["w_out"].reshape(-1, d)
        h1, xn2 = _mm_nn(cat, w_out_f, out_dtype=F32, res=h, norm_gain=norm_xattn_g[l], name=f"proj_out_{l}")
        launch_gather(l + 1, 0, h1)
        g = gathered[l, 2]
        wq_f, wk_f, wv_f, wo_f = (g[nme].reshape(-1, d) for nme in ("wq", "wk", "wv", "wo"))
        q = _mm_nn(xn2, wq_f, out_dtype=BF16, name=f"q_{l}")
        launch_gather(l + 1, 1, q)
        k = _mm_nn(memn, wk_f, out_dtype=BF16, name=f"k_{l}")
        v = _mm_nn(memn, wv_f, out_dtype=BF16, name=f"v_{l}")
        o = _attn_fwd(q, k, v, name=f"attn_{l}")
        h2, xn3 = _mm_nn(o, wo_f, out_dtype=F32, res=h1, norm_gain=norm_ffn_g[l], name=f"attn_out_{l}")
        launch_gather(l + 1, 2, h2)
        g = gathered[l, 3]
        w_up_t = g["w_up"].reshape(-1, d)
        conv_w_f = _blocks_to_cols(g["conv_w"], name=f"conv_w_cols_{l}").reshape(3, 2, f).transpose(1, 0, 2)
        w_down_f = g["w_down"].reshape(-1, d)
        hh, act = _ffn_up_gate(xn3, w_up_t, conv_w_f, cb, name=f"ffn_up_gate_{l}")
        launch_gather(l + 1, 3, hh)
        if l + 1 < DEPTH:
            h3, xn1_next = _mm_nn(act, w_down_f, out_dtype=F32, res=h2, tm=512, norm_gain=norm_mix_g[l + 1],
                                  name=f"ffn_down_{l}")
        else:
            h3, xn1_next = _mm_nn(act, w_down_f, out_dtype=F32, res=h2, tm=512, name=f"ffn_down_{l}"), None
        full.append(dict(w_in_t=w_in_t, w_out=w_out_f, wq=wq_f, wk=wk_f, wv=wv_f, wo=wo_f, w_up_t=w_up_t,
                         conv_w=conv_w_f, w_down=w_down_f))
        saved.append(dict(h0=h, xn1=xn1, proj=proj, cat=cat, h1=h1, xn2=xn2, q=q, memn=memn, k=k, v=v, o=o, h2=h2,
                          xn3=xn3, hh=hh, act=act, sgu_bias=sgu_bias, cb=cb))
        h, xn1 = h3, xn1_next

    dh, dhb, dg_final, loss_row = _loss_head(h, final_norm_g, target, name="loss_head")

    slots = {nme: [None] * DEPTH for nme in _SHARDED}
    small = [None] * DEPTH

    previous = []

    def scatter(l, tag, names, parts):
        parts = [_pin(parts[0], *previous)] + parts[1:]
        arrived = _sequencer_exchange(parts, scatter=True, name=f"scatter_{tag}_{l}")
        previous[:] = arrived[:1]
        for nme, land in zip(names, arrived):
            slots[nme][l] = land
        return parts

    for l in reversed(range(DEPTH)):
        fw, sv = full[l], saved[l]
        dact = _mm_nt(dhb, fw["w_down"], out_dtype=F32, tm=512, name=f"d_act_{l}")
        g_w_down = _mm_tn(sv["act"], dhb, tm=f // 2, name=f"g_w_down_{l}")
        dhh, g_conv_w, g_conv_b = _convgate_bwd(sv["hh"], dact, fw["conv_w"], sv["cb"], name=f"d_convgate_{l}")
        g_w_up_t = _mm_up_grad(dhh, sv["xn3"], name=f"g_w_up_{l}")
        g_conv_w_cols = g_conv_w.transpose(1, 0, 2).reshape(3, 2 * f)
        parts = [g_w_up_t.reshape(N_DEV, -1, d),
                 _cols_to_blocks(g_conv_w_cols, name=f"g_conv_w_blocks_{l}"), g_w_down.reshape(N_DEV, -1, d)]
        parts = scatter(l, "ffn", ("w_up", "conv_w", "w_down"), parts)
        dh2, dh2b, g_norm_ffn = _mm_up_back(_pin(dhh, *parts), fw["w_up_t"], name=f"d_norm_ffn_{l}",
                                            norm_bwd=(sv["h2"], norm_ffn_g[l], dh))

        do = _mm_nt(dh2b, fw["wo"], out_dtype=BF16, name=f"d_o_{l}")
        g_wo = _mm_tn(sv["o"], dh2b, name=f"g_wo_{l}")
        dq, dk, dv = _attn_bwd(sv["q"], sv["k"], sv["v"], do, name=f"d_attn_{l}")
        dkb, dvb = dk.astype(BF16), dv.astype(BF16)
        g_wq = _mm_tn(sv["xn2"], dq, name=f"g_wq_{l}")
        g_wk = _mm_tn(sv["memn"], dkb, name=f"g_wk_{l}")
        g_wv = _mm_tn(sv["memn"], dvb, name=f"g_wv_{l}")
        parts = [g.reshape(N_DEV, -1, d) for g in (g_wq, g_wk, g_wv, g_wo)]
        parts = scatter(l, "attn", ("wq", "wk", "wv", "wo"), parts)
        dq = _pin(dq, *parts)
        dmemn = _mm_nt(dkb, fw["wk"], out_dtype=F32, name=f"d_memn_k_{l}")
        dmemn = _mm_nt(dvb, fw["wv"], out_dtype=F32, res=dmemn, name=f"d_memn_v_{l}")
        _, _, g_mem_norm = _rms_bwd(memx, dmemn, mem_norm_g[l], None, name=f"d_norm_mem_{l}")
        dh1, dh1b, g_norm_xattn = _mm_nt(dq, fw["wq"], name=f"d_norm_xattn_{l}",
                                         norm_bwd=(sv["h1"], norm_xattn_g[l], dh2))

        dcat = _mm_nt(dh1b, fw["w_out"], out_dtype=F32, name=f"d_cat_{l}")
        g_w_out = _mm_tn(sv["cat"], dh1b, name=f"g_w_out_{l}")
        dproj, g_pool_w, g_pool_scale, g_sgu_g, g_sgu_w, g_sgu_b = _mixer_bwd(
            sv["proj"], dcat, pool_w[l], pool_scale[l].reshape(1, -1), sgu_g[l].reshape(1, -1), sgu_w[l],
            sv["sgu_bias"], name=f"d_mixer_{l}")
        g_w_in_t = _mm_tn(dproj, sv["xn1"], name=f"g_w_in_{l}")
        parts = [g_w_in_t.reshape(N_DEV, -1, d), g_w_out.reshape(N_DEV, -1, d)]
        parts = scatter(l, "mix", ("w_in", "w_out"), parts)
        dh, dhb, g_norm_mix = _mm_nn(_pin(dproj, *parts), fw["w_in_t"], name=f"d_norm_mix_{l}",
                                     norm_bwd=(sv["h0"], norm_mix_g[l], dh1))

        small[l] = dict(norm_mix_g=g_norm_mix.reshape(-1), pool_w=g_pool_w, pool_scale=g_pool_scale.reshape(-1),
                        sgu_g=g_sgu_g.reshape(-1), sgu_w=g_sgu_w, sgu_b=g_sgu_b, norm_xattn_g=g_norm_xattn.reshape(-1),
                        mem_norm_g=g_mem_norm.reshape(-1), norm_ffn_g=g_norm_ffn.reshape(-1),
                        conv_b=g_conv_b.reshape(-1))
    grad_x = dh.reshape(x.shape)

    out = {}
    for nme in _SHARDED:
        view = (lambda a: jnp.swapaxes(a, 1, 2)) if nme in _COLUMN_SHARDED else (lambda a: a)
        w3 = view(W[nme])
        res = _adamw([sl.reshape((N_DEV,) + w3.shape[1:]) for sl in slots[nme]], w3, view(M[nme]), view(V[nme]),
                     name=f"adamw_{nme}")
        out[nme] = [view(r) for r in res]

    small_names = [n for n in _SMALL]
    contrib = []
    for nme in small_names:
        if nme == "final_norm_g":
            contrib.append(dg_final.reshape(-1))
        else:
            contrib.append(jnp.stack([small[l][nme] for l in range(DEPTH)]))
    tail = 8 * _PACK_LANES
    packed_g = _pack(contrib + [jnp.pad(loss_row[0, :1], (0, tail - 1))])
    (all_g,) = _sequencer_exchange([_pin(packed_g, *previous)], scatter=False, name="gather_small_grads")
    rows = packed_g.shape[0]
    loss = jnp.sum(all_g[:, rows - 8, 0])
    zeros = jnp.zeros((tail,), F32)
    res = _adamw([all_g], _pack([W[n] for n in small_names] + [zeros]).reshape(1, rows, -1),
                 _pack([M[n] for n in small_names] + [zeros]).reshape(1, rows, -1),
                 _pack([V[n] for n in small_names] + [zeros]).reshape(1, rows, -1), name="adamw_small", tr=rows // 2)
    unpacked = [_unpack(r, [W[n] for n in small_names]) for r in res]
    for i, nme in enumerate(small_names):
        out[nme] = [unpacked[j][i] for j in range(4)]

    grads = [out[n][0] for n in _WEIGHTS]
    deltas = [out[n][1] for n in _WEIGHTS]
    new_m = [out[n][2] for n in _WEIGHTS]
    new_v = [out[n][3] for n in _WEIGHTS]
    return (loss, grad_x, *grads, *deltas, *new_m, *new_v)
```

```python
import jax
import jax.numpy as jnp
from jax import lax
from jax.experimental import pallas as pl
from jax.experimental.pallas import tpu as pltpu
from jax.experimental.pallas import tpu_sc as plsc

F32 = jnp.float32
BF16 = jnp.bfloat16
MESH = pl.DeviceIdType.MESH

EPS = 1e-6
N_DEV = 8
DEPTH = 2
POOL_WINDOWS = (2, 4, 8, 16)
GROUP = 128
POOL_WIDTH = 512
SGU_WIDTH = 512
HEADS = 4
HEAD_DIM = 256
POOL_HALO = 16
CONV_HALO = 8

ADAM_LR = 0.001
ADAM_B1 = 0.9
ADAM_B2 = 0.999
ADAM_EPS = 1e-08
ADAM_WD = 0.01
ADAM_STEP = 10

VMEM_LIMIT_BYTES = 52 * 1024 * 1024


def _params(*semantics):
    return pltpu.CompilerParams(dimension_semantics=semantics, vmem_limit_bytes=VMEM_LIMIT_BYTES)


def _tile(n, want):
    t = min(n, want)
    assert n % t == 0, (n, want)
    return t


_DOT_DIMS = {
    "nn": (((1,), (0,)), ((), ())),
    "nt": (((1,), (1,)), ((), ())),
    "tn": (((0,), (0,)), ((), ())),
}


def _mm(a, b, *, dims, grid, a_spec, b_spec, o_spec, out_shape, out_dtype, acc_shape, name, res=None, res_spec=None,
        norm_bwd=None, norm_out=None):
    nk = grid[2]
    dn = _DOT_DIMS[dims]
    extras, extra_specs = [], []
    if res is not None:
        extras, extra_specs = [res], [res_spec]
    if norm_bwd is not None:
        h, gain, dres, row_spec, gain_spec = norm_bwd
        extras = [h, gain] + ([dres] if dres is not None else [])
        extra_specs = [row_spec, gain_spec] + ([row_spec] if dres is not None else [])
        out_specs = [row_spec, row_spec, gain_spec]
        out_shapes = [jax.ShapeDtypeStruct(h.shape, F32), jax.ShapeDtypeStruct(h.shape, BF16),
                      jax.ShapeDtypeStruct(gain.shape, F32)]
    elif norm_out is not None:
        extras, extra_specs = extras + [norm_out[0]], extra_specs + [norm_out[1]]
        out_specs = [o_spec, o_spec]
        out_shapes = [jax.ShapeDtypeStruct(out_shape, out_dtype), jax.ShapeDtypeStruct(out_shape, BF16)]
    else:
        out_specs, out_shapes = o_spec, jax.ShapeDtypeStruct(out_shape, out_dtype)
    n_extra = len(extras)

    def body(*refs):
        a_ref, b_ref = refs[:2]
        extra_refs = refs[2:2 + n_extra]
        out_refs = refs[2 + n_extra:len(refs) - (1 if nk > 1 else 0)]
        p = lax.dot_general(a_ref[...], b_ref[...], dn, preferred_element_type=F32)

        def finish(r):
            if norm_bwd is not None:
                _rms_bwd_math(r, extra_refs[0], extra_refs[1], extra_refs[2] if n_extra == 3 else None,
                              *out_refs, first=pl.program_id(0) == 0)
                return
            if res is not None:
                r = r + extra_refs[0][...]
            out_refs[0][...] = r.astype(out_refs[0].dtype)
            if norm_out is not None:
                scale = lax.rsqrt(jnp.mean(r * r, axis=-1, keepdims=True) + EPS)
                out_refs[1][...] = ((r * scale) * extra_refs[-1][...]).astype(BF16)

        if nk == 1:
            finish(p)
        else:
            acc_ref = refs[-1]
            k = pl.program_id(2)

            @pl.when(k == 0)
            def _():
                acc_ref[...] = p

            @pl.when(k > 0)
            def _():
                acc_ref[...] += p

            @pl.when(k == nk - 1)
            def _():
                finish(acc_ref[...])

    scratch = [pltpu.VMEM(acc_shape, F32)] if nk > 1 else []
    return pl.pallas_call(
        body, grid=grid, in_specs=[a_spec, b_spec] + extra_specs, out_specs=out_specs,
        out_shape=out_shapes, scratch_shapes=scratch, name=name,
        compiler_params=_params("arbitrary" if norm_bwd is not None else "parallel", "parallel", "arbitrary"),
    )(a, b, *extras)


def _rms_bwd_math(dy, h_ref, g_ref, dres_ref, dh_ref, dhb_ref, dg_ref, *, first):
    x = h_ref[...]
    r = lax.rsqrt(jnp.mean(x * x, axis=-1, keepdims=True) + EPS)
    a = dy * g_ref[...]
    m = jnp.mean(a * x, axis=-1, keepdims=True)
    dh = r * a - x * (r * r * r * m)
    if dres_ref is not None:
        dh = dh + dres_ref[...]
    dh_ref[...] = dh
    dhb_ref[...] = dh.astype(BF16)
    part = jnp.sum(dy * (x * r), axis=0, keepdims=True)

    @pl.when(first)
    def _():
        dg_ref[...] = part

    @pl.when(jnp.logical_not(first))
    def _():
        dg_ref[...] += part


def _mm_nn(a, b, *, out_dtype=F32, name, res=None, tm=1024, norm_gain=None, norm_bwd=None):
    m, k = a.shape
    n = b.shape[1]
    tm = _tile(m, tm)
    spec_o = pl.BlockSpec((tm, n), lambda i, j, kk: (i, 0))
    norm_out = None if norm_gain is None else (norm_gain.reshape(1, n), pl.BlockSpec((1, n), lambda i, j, kk: (0, 0)))
    return _mm(a, b, dims="nn", grid=(m // tm, 1, 1),
               a_spec=pl.BlockSpec((tm, k), lambda i, j, kk: (i, 0)),
               b_spec=pl.BlockSpec((k, n), lambda i, j, kk: (0, 0)),
               o_spec=spec_o, out_shape=(m, n), out_dtype=out_dtype, acc_shape=None, name=name,
               res=res, res_spec=spec_o if res is not None else None, norm_out=norm_out,
               norm_bwd=None if norm_bwd is None else _norm_bwd_arg(*norm_bwd, tm))


def _norm_bwd_arg(h, gain, dres, tm):
    d = h.shape[1]
    return (h, gain.reshape(1, d), dres, pl.BlockSpec((tm, d), lambda i, j, kk: (i, 0)),
            pl.BlockSpec((1, d), lambda i, j, kk: (0, 0)))


def _mm_nt(a, b, *, out_dtype=F32, name, res=None, tm=1024, norm_bwd=None):
    m, k = a.shape
    n = b.shape[0]
    tm = _tile(m, tm)
    spec_o = pl.BlockSpec((tm, n), lambda i, j, kk: (i, 0))
    return _mm(a, b, dims="nt", grid=(m // tm, 1, 1),
               a_spec=pl.BlockSpec((tm, k), lambda i, j, kk: (i, 0)),
               b_spec=pl.BlockSpec((n, k), lambda i, j, kk: (0, 0)),
               o_spec=spec_o, out_shape=(m, n), out_dtype=out_dtype, acc_shape=None, name=name,
               res=res, res_spec=spec_o if res is not None else None,
               norm_bwd=None if norm_bwd is None else _norm_bwd_arg(*norm_bwd, tm))


_TN_ROWS = 2048


def _mm_tn(a, b, *, name, tm=None, tn=None, ts=_TN_ROWS, out_dtype=BF16):
    s, m = a.shape
    n = b.shape[1]
    tm = m if tm is None else tm
    tn = n if tn is None else tn
    ts = _tile(s, ts)
    return _mm(a, b, dims="tn", grid=(m // tm, n // tn, s // ts),
               a_spec=pl.BlockSpec((ts, tm), lambda i, j, kk: (kk, i)),
               b_spec=pl.BlockSpec((ts, tn), lambda i, j, kk: (kk, j)),
               o_spec=pl.BlockSpec((tm, tn), lambda i, j, kk: (i, j)),
               out_shape=(m, n), out_dtype=out_dtype, acc_shape=(tm, tn), name=name)


def _mm_up_back(dhh, w_up_t, *, name, tm=512, norm_bwd=None):
    _, s, f = dhh.shape
    d = w_up_t.shape[1]
    tm = _tile(s, tm)
    return _mm(dhh, w_up_t, dims="nn", grid=(s // tm, 1, 2),
               a_spec=pl.BlockSpec((None, tm, f), lambda i, j, kk: (kk, i, 0)),
               b_spec=pl.BlockSpec((f, d), lambda i, j, kk: (kk, 0)),
               o_spec=pl.BlockSpec((tm, d), lambda i, j, kk: (i, 0)),
               out_shape=(s, d), out_dtype=F32, acc_shape=(tm, d), name=name,
               norm_bwd=None if norm_bwd is None else _norm_bwd_arg(*norm_bwd, tm))


def _mm_up_grad(dhh, xn, *, name, ts=_TN_ROWS):
    s, d = xn.shape
    f = dhh.shape[2]
    tm = f // 2
    ts = _tile(s, ts)
    return _mm(dhh, xn, dims="tn", grid=(4, 1, s // ts),
               a_spec=pl.BlockSpec((None, ts, tm), lambda i, j, kk: (i // 2, kk, i % 2)),
               b_spec=pl.BlockSpec((ts, d), lambda i, j, kk: (kk, 0)),
               o_spec=pl.BlockSpec((tm, d), lambda i, j, kk: (i, 0)),
               out_shape=(2 * f, d), out_dtype=BF16, acc_shape=(tm, d), name=name)


def _rms_fwd(h, g, *, name, tr=512):
    s, d = h.shape
    tr = _tile(s, tr)

    def body(h_ref, g_ref, o_ref):
        x = h_ref[...]
        r = lax.rsqrt(jnp.mean(x * x, axis=-1, keepdims=True) + EPS)
        o_ref[...] = ((x * r) * g_ref[...]).astype(o_ref.dtype)

    row = pl.BlockSpec((tr, d), lambda i: (i, 0))
    return pl.pallas_call(
        body, grid=(s // tr,), in_specs=[row, pl.BlockSpec((1, d), lambda i: (0, 0))], out_specs=row,
        out_shape=jax.ShapeDtypeStruct((s, d), BF16), name=name, compiler_params=_params("parallel"),
    )(h, g.reshape(1, d))


def _rms_bwd(h, dxn, g, dres, *, name, tr=512):
    s, d = h.shape
    tr = _tile(s, tr)
    has_res = dres is not None

    def body(*refs):
        if has_res:
            h_ref, dxn_ref, g_ref, dres_ref, dh_ref, dhb_ref, dg_ref = refs
        else:
            h_ref, dxn_ref, g_ref, dh_ref, dhb_ref, dg_ref = refs
            dres_ref = None
        _rms_bwd_math(dxn_ref[...].astype(F32), h_ref, g_ref, dres_ref, dh_ref, dhb_ref, dg_ref,
                      first=pl.program_id(0) == 0)

    row = pl.BlockSpec((tr, d), lambda i: (i, 0))
    vec = pl.BlockSpec((1, d), lambda i: (0, 0))
    in_specs = [row, row, vec] + ([row] if has_res else [])
    args = (h, dxn, g.reshape(1, d)) + ((dres,) if has_res else ())
    return pl.pallas_call(
        body, grid=(s // tr,), in_specs=in_specs, out_specs=[row, row, vec],
        out_shape=[jax.ShapeDtypeStruct((s, d), F32), jax.ShapeDtypeStruct((s, d), BF16),
                   jax.ShapeDtypeStruct((1, d), F32)],
        name=name, compiler_params=_params("arbitrary"),
    )(*args)


def _loss_head(h, g, target, *, name, tr=1024):
    s, d = h.shape
    tr = _tile(s, tr)
    nt = s // tr

    def body(h_ref, g_ref, t_ref, dh_ref, dhb_ref, dg_ref, loss_ref, sq_ref):
        i = pl.program_id(0)
        x = h_ref[...]
        gain = g_ref[...]
        r = lax.rsqrt(jnp.mean(x * x, axis=-1, keepdims=True) + EPS)
        xh = x * r
        err = xh * gain - t_ref[...]
        dy = err * (1.0 / d)
        a = dy * gain
        m = jnp.mean(a * x, axis=-1, keepdims=True)
        dh = r * a - x * (r * r * r * m)
        dh_ref[...] = dh
        dhb_ref[...] = dh.astype(BF16)
        dg_part = jnp.sum(dy * xh, axis=0, keepdims=True)
        sq_part = jnp.sum(err * err, axis=0, keepdims=True)

        @pl.when(i == 0)
        def _():
            dg_ref[...] = dg_part
            sq_ref[...] = sq_part

        @pl.when(i > 0)
        def _():
            dg_ref[...] += dg_part
            sq_ref[...] += sq_part

        @pl.when(i == nt - 1)
        def _():
            total = jnp.sum(sq_ref[...], axis=1, keepdims=True) * (0.5 / d)
            loss_ref[...] = jnp.broadcast_to(total, loss_ref.shape)

    row = pl.BlockSpec((tr, d), lambda i: (i, 0))
    vec = pl.BlockSpec((1, d), lambda i: (0, 0))
    return pl.pallas_call(
        body, grid=(nt,), in_specs=[row, vec, row],
        out_specs=[row, row, vec, pl.BlockSpec((1, 128), lambda i: (0, 0))],
        out_shape=[jax.ShapeDtypeStruct((s, d), F32), jax.ShapeDtypeStruct((s, d), BF16),
                   jax.ShapeDtypeStruct((1, d), F32), jax.ShapeDtypeStruct((1, 128), F32)],
        scratch_shapes=[pltpu.VMEM((1, d), F32)], name=name, compiler_params=_params("arbitrary"),
    )(h, g.reshape(1, d), target)


_SQRT_HALF = 0.7071067811865476
_INV_SQRT_2PI = 0.3989422804014327


def _gelu(x):
    return 0.5 * x * (1.0 + lax.erf(x * _SQRT_HALF))


def _gelu_and_grad(x):
    cdf = 0.5 * (1.0 + lax.erf(x * _SQRT_HALF))
    return x * cdf, cdf + x * (jnp.exp(-0.5 * x * x) * _INV_SQRT_2PI)


def _trailing_sums(xe, win):
    s = xe
    sh = 1
    while sh < win:
        s = s + pltpu.roll(s, sh, 0)
        sh *= 2
    return s


def _leading_sums(xe, win):
    n = xe.shape[0]
    s = xe
    sh = 1
    while sh < win:
        s = s + pltpu.roll(s, n - sh, 0)
        sh *= 2
    return s


def _tril_mask():
    return lax.broadcasted_iota(jnp.int32, (GROUP, GROUP), 0) >= lax.broadcasted_iota(jnp.int32, (GROUP, GROUP), 1)


def _layernorm_stats(v):
    mu = jnp.mean(v, axis=-1, keepdims=True)
    xc = v - mu
    rstd = lax.rsqrt(jnp.mean(xc * xc, axis=-1, keepdims=True) + EPS)
    return xc * rstd, rstd


def _mixer_specs(s, t):
    halo_blocks = t // POOL_HALO
    tile = lambda w: pl.BlockSpec((t, w), lambda i: (i, 0))
    prev = pl.BlockSpec((POOL_HALO, POOL_WIDTH), lambda i: (jnp.maximum(i * halo_blocks - 1, 0), 0))
    nxt = pl.BlockSpec((POOL_HALO, POOL_WIDTH),
                       lambda i: (jnp.minimum((i + 1) * halo_blocks, s // POOL_HALO - 1), 0))
    const3 = pl.BlockSpec((HEADS, GROUP, GROUP), lambda i: (0, 0, 0))
    vec = pl.BlockSpec((1, POOL_WIDTH), lambda i: (0, 0))
    bias = pl.BlockSpec((GROUP, SGU_WIDTH), lambda i: (0, 0))
    return tile, prev, nxt, const3, vec, bias


def _proj_mixer(xn, w_in_t, pool_w, pool_scale, sgu_g, sgu_w, sgu_bias, *, name, t=512):
    s, d = xn.shape
    n = w_in_t.shape[0]
    t = _tile(s, t)
    nt = s // t

    def body(xn_ref, wt_ref, pw_ref, ps_ref, sg_ref, sw_ref, sb_ref, proj_ref, cat_ref, held_ref, above_ref, w_ref):
        i = pl.program_id(0)

        @pl.when(i == 0)
        def _():
            held_ref[...] = jnp.zeros_like(held_ref)
            above_ref[...] = jnp.zeros_like(above_ref)
            w_ref[...] = wt_ref[...].astype(F32).T.astype(BF16)

        row = jnp.maximum((i - 1) * t + lax.broadcasted_iota(jnp.int32, (t, 1), 0), 0)
        p = held_ref[:, 0:POOL_WIDTH]
        pe = jnp.concatenate([above_ref[...], p], axis=0)
        for gi, win in enumerate(POOL_WINDOWS):
            cols = slice(gi * GROUP, (gi + 1) * GROUP)
            count = jnp.minimum(row + 1, win).astype(F32)
            dev = _trailing_sums(pe[:, cols], win)[POOL_HALO:] / count - p[:, cols]
            y = jnp.dot(dev.astype(BF16), pw_ref[gi].astype(BF16), preferred_element_type=F32) * ps_ref[:, cols]
            cat_ref[:, cols] = y.astype(BF16)
        above_ref[...] = held_ref[t - POOL_HALO:t, 0:POOL_WIDTH]

        u = _gelu(held_ref[:, POOL_WIDTH:POOL_WIDTH + SGU_WIDTH])
        xhat, _ = _layernorm_stats(_gelu(held_ref[:, POOL_WIDTH + SGU_WIDTH:]))
        vn = (xhat * sg_ref[...]).astype(BF16)
        tri = _tril_mask()
        for h in range(HEADS):
            cols = slice(h * GROUP, (h + 1) * GROUP)
            w = jnp.where(tri, sw_ref[h], 0.0).astype(BF16)
            for c in range(t // GROUP):
                rows = slice(c * GROUP, (c + 1) * GROUP)
                z = jnp.dot(w, vn[rows, cols], preferred_element_type=F32) + sb_ref[:, cols]
                cat_ref[rows, POOL_WIDTH + h * GROUP:POOL_WIDTH + (h + 1) * GROUP] = (u[rows, cols] * z).astype(BF16)

        y = jnp.dot(xn_ref[...], w_ref[...], preferred_element_type=F32)
        proj_ref[...] = y
        held_ref[...] = y

    row_of = lambda i: jnp.minimum(i, nt - 1)
    const3 = pl.BlockSpec((HEADS, GROUP, GROUP), lambda i: (0, 0, 0))
    vec = pl.BlockSpec((1, POOL_WIDTH), lambda i: (0, 0))
    return pl.pallas_call(
        body, grid=(nt + 1,),
        in_specs=[pl.BlockSpec((t, d), lambda i: (row_of(i), 0)), pl.BlockSpec((n, d), lambda i: (0, 0)),
                  const3, vec, vec, const3, pl.BlockSpec((GROUP, SGU_WIDTH), lambda i: (0, 0))],
        out_specs=[pl.BlockSpec((t, n), lambda i: (row_of(i), 0)),
                   pl.BlockSpec((t, POOL_WIDTH + SGU_WIDTH), lambda i: (jnp.maximum(i - 1, 0), 0))],
        out_shape=[jax.ShapeDtypeStruct((s, n), F32), jax.ShapeDtypeStruct((s, POOL_WIDTH + SGU_WIDTH), BF16)],
        scratch_shapes=[pltpu.VMEM((t, n), F32), pltpu.VMEM((POOL_HALO, POOL_WIDTH), F32), pltpu.VMEM((d, n), BF16)],
        name=name, compiler_params=_params("arbitrary"),
    )(xn, w_in_t, pool_w, pool_scale, sgu_g, sgu_w, sgu_bias)


def _mixer_bwd(proj, dcat, pool_w, pool_scale, sgu_g, sgu_w, sgu_bias, *, name, t=512):
    s = proj.shape[0]
    t = _tile(s, t)
    nt = s // t
    tile, prev, nxt, const3, vec, bias = _mixer_specs(s, t)

    def body(proj_ref, halo_ref, dcat_ref, dnext_ref, pw_ref, ps_ref, sg_ref, sw_ref, sb_ref,
             dproj_ref, dpw_ref, dps_ref, dsg_ref, dsw_ref, dsb_ref, du_ref, dvn_ref, dz_ref):
        i = pl.program_id(0)

        @pl.when(i == 0)
        def _():
            dpw_ref[...] = jnp.zeros_like(dpw_ref)
            dps_ref[...] = jnp.zeros_like(dps_ref)
            dsg_ref[...] = jnp.zeros_like(dsg_ref)
            dsw_ref[...] = jnp.zeros_like(dsw_ref)
            dz_ref[...] = jnp.zeros_like(dz_ref)

        row = i * t + lax.broadcasted_iota(jnp.int32, (t, 1), 0)
        row_e = i * t + lax.broadcasted_iota(jnp.int32, (t + POOL_HALO, 1), 0)
        p = proj_ref[:, 0:POOL_WIDTH]
        pe = jnp.concatenate([jnp.where(i > 0, halo_ref[...], 0.0), p], axis=0)
        dyp = dcat_ref[:, 0:POOL_WIDTH]
        dye = jnp.concatenate([dyp, jnp.where(i < nt - 1, dnext_ref[...], 0.0)], axis=0)
        for gi, win in enumerate(POOL_WINDOWS):
            cols = slice(gi * GROUP, (gi + 1) * GROUP)
            count = jnp.minimum(row + 1, win).astype(F32)
            d = (_trailing_sums(pe[:, cols], win)[POOL_HALO:] / count - p[:, cols]).astype(BF16)
            pw = pw_ref[gi].astype(BF16)
            pre = jnp.dot(d, pw, preferred_element_type=F32)
            dps_ref[:, cols] += jnp.sum(dyp[:, cols] * pre, axis=0, keepdims=True)
            ys = (dye[:, cols] * ps_ref[:, cols]).astype(BF16)
            dpw_ref[gi] += lax.dot_general(d, ys[:t], _DOT_DIMS["tn"], preferred_element_type=F32)
            dd = lax.dot_general(ys, pw, _DOT_DIMS["nt"], preferred_element_type=F32)
            count_e = jnp.minimum(row_e + 1, win).astype(F32)
            dp = _leading_sums(dd / count_e, win)[:t] - dd[:t]
            dproj_ref[:, cols] = dp.astype(BF16)

        xu = proj_ref[:, POOL_WIDTH:POOL_WIDTH + SGU_WIDTH]
        xv = proj_ref[:, POOL_WIDTH + SGU_WIDTH:]
        u, gelu_grad_u = _gelu_and_grad(xu)
        v, gelu_grad_v = _gelu_and_grad(xv)
        xhat, rstd = _layernorm_stats(v)
        gain = sg_ref[...]
        vn = (xhat * gain).astype(BF16)
        tri = _tril_mask()
        for h in range(HEADS):
            cols = slice(h * GROUP, (h + 1) * GROUP)
            wf = jnp.where(tri, sw_ref[h], 0.0)
            w, wt = wf.astype(BF16), wf.T.astype(BF16)
            for c in range(t // GROUP):
                rows = slice(c * GROUP, (c + 1) * GROUP)
                vch = vn[rows, cols]
                z = jnp.dot(w, vch, preferred_element_type=F32) + sb_ref[:, cols]
                dy = dcat_ref[rows, POOL_WIDTH + h * GROUP:POOL_WIDTH + (h + 1) * GROUP]
                du_ref[rows, cols] = dy * z
                dz = dy * u[rows, cols]
                dz_ref[:, cols] += dz
                dzb = dz.astype(BF16)
                dsw_ref[h] += lax.dot_general(dzb, vch, _DOT_DIMS["nt"], preferred_element_type=F32)
                dvn_ref[rows, cols] = jnp.dot(wt, dzb, preferred_element_type=F32)
        dvn = dvn_ref[...]
        dsg_ref[...] += jnp.sum(dvn * xhat, axis=0, keepdims=True)
        dxh = dvn * gain
        dv = rstd * (dxh - jnp.mean(dxh, axis=-1, keepdims=True)
                     - xhat * jnp.mean(dxh * xhat, axis=-1, keepdims=True))
        dproj_ref[:, POOL_WIDTH:POOL_WIDTH + SGU_WIDTH] = (du_ref[...] * gelu_grad_u).astype(BF16)
        dproj_ref[:, POOL_WIDTH + SGU_WIDTH:] = (dv * gelu_grad_v).astype(BF16)

        @pl.when(i == nt - 1)
        def _():
            for h in range(HEADS):
                dsw_ref[h] = jnp.where(tri, dsw_ref[h], 0.0)
            lane = lax.broadcasted_iota(jnp.int32, (GROUP, GROUP), 1)
            out = jnp.zeros((GROUP, GROUP), F32)
            for h in range(HEADS):
                sh = jnp.sum(dz_ref[:, h * GROUP:(h + 1) * GROUP], axis=1, keepdims=True)
                out = jnp.where(lane == h, sh, out)
            dsb_ref[...] = out

    outs = pl.pallas_call(
        body, grid=(nt,),
        in_specs=[tile(POOL_WIDTH + 2 * SGU_WIDTH), prev, tile(POOL_WIDTH + SGU_WIDTH), nxt,
                  const3, vec, vec, const3, bias],
        out_specs=[tile(POOL_WIDTH + 2 * SGU_WIDTH), const3, vec, vec, const3,
                   pl.BlockSpec((GROUP, GROUP), lambda i: (0, 0))],
        out_shape=[jax.ShapeDtypeStruct((s, POOL_WIDTH + 2 * SGU_WIDTH), BF16),
                   jax.ShapeDtypeStruct((HEADS, GROUP, GROUP), F32),
                   jax.ShapeDtypeStruct((1, POOL_WIDTH), F32),
                   jax.ShapeDtypeStruct((1, SGU_WIDTH), F32),
                   jax.ShapeDtypeStruct((HEADS, GROUP, GROUP), F32),
                   jax.ShapeDtypeStruct((GROUP, GROUP), F32)],
        scratch_shapes=[pltpu.VMEM((t, SGU_WIDTH), F32), pltpu.VMEM((t, SGU_WIDTH), F32),
                        pltpu.VMEM((GROUP, SGU_WIDTH), F32)],
        name=name, compiler_params=_params("arbitrary"),
    )(proj, proj, dcat, dcat, pool_w, pool_scale, sgu_g, sgu_w, sgu_bias)
    dproj, dpw, dps, dsg, dsw, dsb = outs
    return dproj, dpw, dps, dsg, dsw, dsb[:, :HEADS].T


def _attn_probs(q, k, scale):
    sc = lax.dot_general(q, k, _DOT_DIMS["nt"], preferred_element_type=F32) * scale
    sc = sc - jnp.max(sc, axis=-1, keepdims=True)
    e = jnp.exp(sc)
    return e / jnp.sum(e, axis=-1, keepdims=True)


def _attn_fwd(q, k, v, *, name, t=2048):
    s, d = q.shape
    nm = k.shape[0]
    t = _tile(s, t)
    scale = HEAD_DIM ** -0.5

    def body(q_ref, k_ref, v_ref, o_ref):
        for h in range(HEADS):
            cols = slice(h * HEAD_DIM, (h + 1) * HEAD_DIM)
            pr = _attn_probs(q_ref[:, cols], k_ref[:, cols], scale)
            o_ref[:, cols] = jnp.dot(pr.astype(BF16), v_ref[:, cols], preferred_element_type=F32).astype(BF16)

    row = pl.BlockSpec((t, d), lambda i: (i, 0))
    kv = pl.BlockSpec((nm, d), lambda i: (0, 0))
    return pl.pallas_call(
        body, grid=(s // t,), in_specs=[row, kv, kv], out_specs=row,
        out_shape=jax.ShapeDtypeStruct((s, d), BF16), name=name, compiler_params=_params("parallel"),
    )(q, k, v)


def _attn_bwd(q, k, v, do, *, name, t=2048):
    s, d = q.shape
    nm = k.shape[0]
    t = _tile(s, t)
    scale = HEAD_DIM ** -0.5

    def body(q_ref, k_ref, v_ref, do_ref, dq_ref, dk_ref, dv_ref):
        i = pl.program_id(0)

        @pl.when(i == 0)
        def _():
            dk_ref[...] = jnp.zeros_like(dk_ref)
            dv_ref[...] = jnp.zeros_like(dv_ref)

        for h in range(HEADS):
            cols = slice(h * HEAD_DIM, (h + 1) * HEAD_DIM)
            qh, kh, vh, doh = q_ref[:, cols], k_ref[:, cols], v_ref[:, cols], do_ref[:, cols]
            pr = _attn_probs(qh, kh, scale)
            dpr = lax.dot_general(doh, vh, _DOT_DIMS["nt"], preferred_element_type=F32)
            ds = (pr * (dpr - jnp.sum(dpr * pr, axis=-1, keepdims=True)) * scale).astype(BF16)
            dv_ref[:, cols] += lax.dot_general(pr.astype(BF16), doh, _DOT_DIMS["tn"], preferred_element_type=F32)
            dk_ref[:, cols] += lax.dot_general(ds, qh, _DOT_DIMS["tn"], preferred_element_type=F32)
            dq_ref[:, cols] = jnp.dot(ds, kh, preferred_element_type=F32).astype(BF16)

    row = pl.BlockSpec((t, d), lambda i: (i, 0))
    kv = pl.BlockSpec((nm, d), lambda i: (0, 0))
    return pl.pallas_call(
        body, grid=(s // t,), in_specs=[row, kv, kv, row], out_specs=[row, kv, kv],
        out_shape=[jax.ShapeDtypeStruct((s, d), BF16), jax.ShapeDtypeStruct((nm, d), F32),
                   jax.ShapeDtypeStruct((nm, d), F32)],
        name=name, compiler_params=_params("arbitrary"),
    )(q, k, v, do)


def _conv3(w_ref, p, x2, x1, x0, b):
    return (w_ref[p, 0:1, :] * x2 + w_ref[p, 1:2, :] * x1 + w_ref[p, 2:3, :] * x0) + b


def _ffn_up_gate(xn, w_up_t, cw, cb, *, name, t=256, tc=1408):
    s, d = xn.shape
    f = w_up_t.shape[0] // 2
    t = _tile(s, t)
    nt, nj = s // t, f // tc

    def body(xn_ref, wg_ref, wv_ref, cw_ref, cb_ref, hh_ref, act_ref, held_ref, above_ref, w_ref):
        i = pl.program_id(1)

        @pl.when(i == 0)
        def _():
            held_ref[...] = jnp.zeros_like(held_ref)
            above_ref[...] = jnp.zeros_like(above_ref)
            w_ref[0] = wg_ref[...].astype(F32).T.astype(BF16)
            w_ref[1] = wv_ref[...].astype(F32).T.astype(BF16)

        hc = []
        for p in range(2):
            xe = jnp.concatenate([above_ref[p], held_ref[p]], axis=0)
            hc.append(_conv3(cw_ref, p, pltpu.roll(xe, 2, 0), pltpu.roll(xe, 1, 0), xe, cb_ref[p])[CONV_HALO:])
            above_ref[p] = held_ref[p, t - CONV_HALO:t, :]
        gate, val = hc
        act_ref[...] = ((gate * jax.nn.sigmoid(gate)) * val).astype(BF16)

        x = xn_ref[...]
        for p in range(2):
            y = jnp.dot(x, w_ref[p], preferred_element_type=F32)
            hh_ref[p] = y
            held_ref[p] = y

    row = lambda i: jnp.minimum(i, nt - 1)
    return pl.pallas_call(
        body, grid=(nj, nt + 1),
        in_specs=[pl.BlockSpec((t, d), lambda j, i: (row(i), 0)),
                  pl.BlockSpec((tc, d), lambda j, i: (j, 0)),
                  pl.BlockSpec((tc, d), lambda j, i: (nj + j, 0)),
                  pl.BlockSpec((2, 3, tc), lambda j, i: (0, 0, j)),
                  pl.BlockSpec((2, 1, tc), lambda j, i: (0, 0, j))],
        out_specs=[pl.BlockSpec((2, t, tc), lambda j, i: (0, row(i), j)),
                   pl.BlockSpec((t, tc), lambda j, i: (jnp.maximum(i - 1, 0), j))],
        out_shape=[jax.ShapeDtypeStruct((2, s, f), F32), jax.ShapeDtypeStruct((s, f), BF16)],
        scratch_shapes=[pltpu.VMEM((2, t, tc), F32), pltpu.VMEM((2, CONV_HALO, tc), F32), pltpu.VMEM((2, d, tc), BF16)],
        name=name, compiler_params=_params("arbitrary", "arbitrary"),
    )(xn, w_up_t, w_up_t, cw, cb)


def _convgate_bwd(hh, dact, cw, cb, *, name, t=256, tc=1408):
    _, s, f = hh.shape
    t = _tile(s, t)
    nt, nj = s // t, f // tc
    hb = t // CONV_HALO
    m = t + CONV_HALO

    def body(hh_ref, prev_ref, next_ref, da_ref, danext_ref, cw_ref, cb_ref, dhh_ref, dcw_ref, dcb_ref):
        i = pl.program_id(1)
        is_last = i == nt - 1

        @pl.when(i == 0)
        def _():
            dcw_ref[...] = jnp.zeros_like(dcw_ref)
            dcb_ref[...] = jnp.zeros_like(dcb_ref)

        taps, hc = [], []
        for p in range(2):
            xe = jnp.concatenate([jnp.where(i > 0, prev_ref[p], 0.0), hh_ref[p],
                                  jnp.where(is_last, 0.0, next_ref[p])], axis=0)
            x2, x1 = pltpu.roll(xe, 2, 0), pltpu.roll(xe, 1, 0)
            hc.append(_conv3(cw_ref, p, x2, x1, xe, cb_ref[p])[CONV_HALO:])
            taps.append((x2[CONV_HALO:CONV_HALO + t], x1[CONV_HALO:CONV_HALO + t], xe[CONV_HALO:CONV_HALO + t]))
        gate, val = hc
        dae = jnp.concatenate([da_ref[...].astype(F32),
                               jnp.where(is_last, 0.0, danext_ref[...].astype(F32)[:CONV_HALO])], axis=0)
        sg = jax.nn.sigmoid(gate)
        dval = dae * (gate * sg)
        dgate = dae * val * (sg * (1.0 + gate * (1.0 - sg)))
        for p, dhc in enumerate((dgate, dval)):
            dh = (cw_ref[p, 2:3, :] * dhc + cw_ref[p, 1:2, :] * pltpu.roll(dhc, m - 1, 0)
                  + cw_ref[p, 0:1, :] * pltpu.roll(dhc, m - 2, 0))
            dhh_ref[p] = dh[:t].astype(BF16)
            d0 = dhc[:t]
            for kk, tap in enumerate(taps[p]):
                dcw_ref[p, kk:kk + 1, :] += jnp.sum(d0 * tap, axis=0, keepdims=True)
            dcb_ref[p] += jnp.sum(d0, axis=0, keepdims=True)

    below = lambda i: jnp.minimum((i + 1) * hb, s // CONV_HALO - 1)
    return pl.pallas_call(
        body, grid=(nj, nt),
        in_specs=[pl.BlockSpec((2, t, tc), lambda j, i: (0, i, j)),
                  pl.BlockSpec((2, CONV_HALO, tc), lambda j, i: (0, jnp.maximum(i * hb - 1, 0), j)),
                  pl.BlockSpec((2, CONV_HALO, tc), lambda j, i: (0, below(i), j)),
                  pl.BlockSpec((t, tc), lambda j, i: (i, j)),
                  pl.BlockSpec((2 * CONV_HALO, tc),
                               lambda j, i: (jnp.minimum((i + 1) * (hb // 2), s // (2 * CONV_HALO) - 1), j)),
                  pl.BlockSpec((2, 3, tc), lambda j, i: (0, 0, j)),
                  pl.BlockSpec((2, 1, tc), lambda j, i: (0, 0, j))],
        out_specs=[pl.BlockSpec((2, t, tc), lambda j, i: (0, i, j)),
                   pl.BlockSpec((2, 3, tc), lambda j, i: (0, 0, j)),
                   pl.BlockSpec((2, 1, tc), lambda j, i: (0, 0, j))],
        out_shape=[jax.ShapeDtypeStruct((2, s, f), BF16), jax.ShapeDtypeStruct((2, 3, f), F32),
                   jax.ShapeDtypeStruct((2, 1, f), F32)],
        name=name, compiler_params=_params("parallel", "arbitrary"),
    )(hh, hh, hh, dact, dact, cw, cb)


def _position():
    return lax.axis_index("x"), lax.axis_index("y"), lax.axis_index("c")


def _linear(px, py, pc):
    return 4 * px + 2 * py + pc


def _peers_of(x, y, c):
    peers = []
    for mask in range(1, N_DEV):
        peers.append((1 - x if mask & 4 else x, 1 - y if mask & 2 else y, 1 - c if mask & 1 else c))
    return peers


def _exchange_copy(src_ref, land_ref, send_sem, recv_sem, peer, mine, scatter, arriving):
    src = src_ref.at[_linear(*peer)] if scatter else src_ref
    dst = land_ref.at[_linear(*peer) if arriving else mine]
    return pltpu.make_async_remote_copy(src_ref=src, dst_ref=dst, send_sem=send_sem, recv_sem=recv_sem,
                                        device_id=peer, device_id_type=MESH)


_EXCHANGE_COLLECTIVE_ID = 7


def _sequencer_exchange(srcs, *, scatter, name):
    n = len(srcs)
    src_refs = [jax.new_ref(a, memory_space=pltpu.MemorySpace.HBM) for a in srcs]
    land_refs = [jax.empty_ref(jax.ShapeDtypeStruct(a.shape if scatter else (N_DEV,) + a.shape, a.dtype),
                               memory_space=pltpu.MemorySpace.HBM) for a in srcs]

    @pl.kernel(mesh=plsc.ScalarSubcoreMesh(axis_name="sequencer", num_cores=1), name=name,
               scratch_types=(pltpu.SemaphoreType.DMA((7 * n,)), pltpu.SemaphoreType.DMA((7 * n,)),
                              pltpu.SemaphoreType.DMA((n,))),
               compiler_params=pltpu.CompilerParams(collective_id=_EXCHANGE_COLLECTIVE_ID))
    def launch(send_sems, recv_sems, local_sems):
        x, y, c = _position()
        mine = _linear(x, y, c)
        peers = _peers_of(x, y, c)
        barrier = pltpu.get_barrier_semaphore()
        for peer in peers:
            pl.semaphore_signal(barrier, inc=1, device_id=peer, device_id_type=MESH)
        pl.semaphore_wait(barrier, N_DEV - 1)
        local = [pltpu.make_async_copy(src_refs[t].at[mine] if scatter else src_refs[t], land_refs[t].at[mine],
                                       local_sems.at[t]) for t in range(n)]
        for cp in local:
            cp.start()
        if scatter:
            sends = []
            for t in range(n):
                for k, peer in enumerate(peers):
                    cp = _exchange_copy(src_refs[t], land_refs[t], send_sems.at[7 * t + k], recv_sems.at[7 * t + k],
                                        peer, mine, scatter, arriving=False)
                    cp.start()
                    sends.append(cp)
            for t in range(n):
                for k, peer in enumerate(peers):
                    _exchange_copy(src_refs[t], land_refs[t], send_sems.at[7 * t + k], recv_sems.at[7 * t + k],
                                   peer, mine, scatter, arriving=True).wait_recv()
        else:
            me, sibling = (x, y, c), (x, y, 1 - c)
            chips = [(1 - x, y), (x, 1 - y), (1 - x, 1 - y)]

            def copy(t, k, block, to, src=None):
                dst = land_refs[t].at[_linear(*block)]
                return pltpu.make_async_remote_copy(
                    src_ref=dst if src is None else src, dst_ref=dst, send_sem=send_sems.at[7 * t + k],
                    recv_sem=recv_sems.at[7 * t + k], device_id=to, device_id_type=MESH)

            sends = []
            for t in range(n):
                sends.append(copy(t, 0, me, sibling, src=src_refs[t]))
                for j, chip in enumerate(chips):
                    sends.append(copy(t, 1 + j, me, (*chip, c), src=src_refs[t]))
            for cp in sends:
                cp.start()
            for j, chip in enumerate(chips):
                for t in range(n):
                    copy(t, 1 + j, (*chip, c), me).wait_recv()
                    passed = copy(t, 4 + j, (*chip, c), sibling)
                    passed.start()
                    sends.append(passed)
            for t in range(n):
                copy(t, 0, sibling, me).wait_recv()
                for j, chip in enumerate(chips):
                    copy(t, 4 + j, (*chip, 1 - c), me).wait_recv()
        for cp in local:
            cp.wait()
        for cp in sends:
            cp.wait_send()

    launch()
    return [r[...] for r in land_refs]


def _adamw_math(g, w, m, v):
    m2 = ADAM_B1 * m + (1.0 - ADAM_B1) * g
    v2 = ADAM_B2 * v + (1.0 - ADAM_B2) * (g * g)
    m_hat = m2 / (1.0 - ADAM_B1 ** ADAM_STEP)
    v_hat = v2 / (1.0 - ADAM_B2 ** ADAM_STEP)
    delta = -ADAM_LR * (m_hat / (jnp.sqrt(v_hat) + ADAM_EPS) + ADAM_WD * w)
    return delta, m2, v2


def _adamw(slots, w, m, v, *, name, tr=256):
    depth = len(slots)
    _, r, c = slots[0].shape
    tr = next((cand for cand in range(min(r, tr), 15, -1) if r % cand == 0 and cand % 16 == 0), r)

    def body(*refs):
        s_refs = refs[:depth]
        w_ref, m_ref, v_ref, g_ref, d_ref, m2_ref, v2_ref = refs[depth:]
        layer = pl.program_id(0)
        for l in range(depth):
            @pl.when(layer == l)
            def _():
                g = s_refs[l][0].astype(F32)
                for d in range(1, N_DEV):
                    g = g + s_refs[l][d].astype(F32)
                delta, m2, v2 = _adamw_math(g, w_ref[...], m_ref[...], v_ref[...])
                g_ref[...] = g
                d_ref[...] = delta
                m2_ref[...] = m2
                v2_ref[...] = v2

    blk = pl.BlockSpec((None, tr, c), lambda layer, i: (layer, i, 0))
    sblks = [pl.BlockSpec((N_DEV, tr, c), lambda layer, i, l=l: (0, jnp.where(layer == l, i, 0), 0))
             for l in range(depth)]
    shape = jax.ShapeDtypeStruct((depth, r, c), F32)
    return pl.pallas_call(
        body, grid=(depth, r // tr), in_specs=sblks + [blk, blk, blk], out_specs=[blk] * 4,
        out_shape=[shape] * 4, name=name, compiler_params=_params("arbitrary", "arbitrary"),
    )(*slots, w, m, v)


_SHARDED = ("w_in", "w_out", "wq", "wk", "wv", "wo", "w_up", "conv_w", "w_down")
_SMALL = ("norm_mix_g", "pool_w", "pool_scale", "sgu_g", "sgu_w", "sgu_b", "norm_xattn_g", "mem_norm_g",
          "norm_ffn_g", "conv_b", "final_norm_g")
_WEIGHTS = ("norm_mix_g", "w_in", "pool_w", "pool_scale", "sgu_g", "sgu_w", "sgu_b", "w_out", "norm_xattn_g",
            "mem_norm_g", "wq", "wk", "wv", "wo", "norm_ffn_g", "w_up", "conv_w", "conv_b", "w_down",
            "final_norm_g")
_PACK_LANES = 128
_GATHER_GROUPS = (("w_in",), ("w_out",), ("wq", "wk", "wv", "wo"), ("w_up", "conv_w", "w_down"))
_COLUMN_SHARDED = ("w_in", "w_up")


def _cols_to_blocks(a, *, name, tr=256):
    r, c8 = a.shape
    c = c8 // N_DEV
    tr = _tile(r, tr)

    def body(a_ref, o_ref):
        for dev in range(N_DEV):
            o_ref[dev] = a_ref[:, dev * c:(dev + 1) * c]

    return pl.pallas_call(
        body, grid=(r // tr,), in_specs=[pl.BlockSpec((tr, c8), lambda i: (i, 0))],
        out_specs=pl.BlockSpec((N_DEV, tr, c), lambda i: (0, i, 0)),
        out_shape=jax.ShapeDtypeStruct((N_DEV, r, c), a.dtype), name=name, compiler_params=_params("parallel"),
    )(a)


def _blocks_to_cols(a, *, name, tr=256):
    n, r, c = a.shape
    tr = _tile(r, tr)

    def body(a_ref, o_ref):
        for dev in range(n):
            o_ref[:, dev * c:(dev + 1) * c] = a_ref[dev]

    return pl.pallas_call(
        body, grid=(r // tr,), in_specs=[pl.BlockSpec((n, tr, c), lambda i: (0, i, 0))],
        out_specs=pl.BlockSpec((tr, n * c), lambda i: (i, 0)),
        out_shape=jax.ShapeDtypeStruct((r, n * c), a.dtype), name=name, compiler_params=_params("parallel"),
    )(a)


def _pin(x, *deps):
    return lax.optimization_barrier((x, *deps))[0]


def _pack(arrays):
    flat = jnp.concatenate([a.reshape(-1) for a in arrays])
    assert flat.shape[0] % (8 * _PACK_LANES) == 0
    return flat.reshape(-1, _PACK_LANES)


def _unpack(packed, like):
    flat = packed.reshape(-1)
    out, off = [], 0
    for a in like:
        out.append(flat[off:off + a.size].reshape(a.shape))
        off += a.size
    return out


def kernel(x, mem, norm_mix_g, w_in, pool_w, pool_scale, sgu_g, sgu_w, sgu_b, w_out, norm_xattn_g, mem_norm_g, wq, wk, wv, wo, norm_ffn_g, w_up, conv_w, conv_b, w_down, final_norm_g, loss_target, m_norm_mix_g, m_w_in, m_pool_w, m_pool_scale, m_sgu_g, m_sgu_w, m_sgu_b, m_w_out, m_norm_xattn_g, m_mem_norm_g, m_wq, m_wk, m_wv, m_wo, m_norm_ffn_g, m_w_up, m_conv_w, m_conv_b, m_w_down, m_final_norm_g, v_norm_mix_g, v_w_in, v_pool_w, v_pool_scale, v_sgu_g, v_sgu_w, v_sgu_b, v_w_out, v_norm_xattn_g, v_mem_norm_g, v_wq, v_wk, v_wv, v_wo, v_norm_ffn_g, v_w_up, v_conv_w, v_conv_b, v_w_down, v_final_norm_g):
    W = dict(norm_mix_g=norm_mix_g, w_in=w_in, pool_w=pool_w, pool_scale=pool_scale, sgu_g=sgu_g, sgu_w=sgu_w,
             sgu_b=sgu_b, w_out=w_out, norm_xattn_g=norm_xattn_g, mem_norm_g=mem_norm_g, wq=wq, wk=wk, wv=wv, wo=wo,
             norm_ffn_g=norm_ffn_g, w_up=w_up, conv_w=conv_w, conv_b=conv_b, w_down=w_down,
             final_norm_g=final_norm_g)
    M = dict(norm_mix_g=m_norm_mix_g, w_in=m_w_in, pool_w=m_pool_w, pool_scale=m_pool_scale, sgu_g=m_sgu_g,
             sgu_w=m_sgu_w, sgu_b=m_sgu_b, w_out=m_w_out, norm_xattn_g=m_norm_xattn_g, mem_norm_g=m_mem_norm_g,
             wq=m_wq, wk=m_wk, wv=m_wv, wo=m_wo, norm_ffn_g=m_norm_ffn_g, w_up=m_w_up, conv_w=m_conv_w,
             conv_b=m_conv_b, w_down=m_w_down, final_norm_g=m_final_norm_g)
    V = dict(norm_mix_g=v_norm_mix_g, w_in=v_w_in, pool_w=v_pool_w, pool_scale=v_pool_scale, sgu_g=v_sgu_g,
             sgu_w=v_sgu_w, sgu_b=v_sgu_b, w_out=v_w_out, norm_xattn_g=v_norm_xattn_g, mem_norm_g=v_mem_norm_g,
             wq=v_wq, wk=v_wk, wv=v_wv, wo=v_wo, norm_ffn_g=v_norm_ffn_g, w_up=v_w_up, conv_w=v_conv_w,
             conv_b=v_conv_b, w_down=v_w_down, final_norm_g=v_final_norm_g)

    s, d = x.shape[1], x.shape[2]
    f = w_down.shape[1] * N_DEV
    h = x.reshape(s, d)
    memx = mem.reshape(mem.shape[1], d)
    target = loss_target.reshape(s, d)

    gathered = {}

    def launch_gather(l, gi, after):
        if l >= DEPTH:
            return
        names = _GATHER_GROUPS[gi]
        shards = [W[nme][l] if nme == "conv_w" else
                  (W[nme][l].T if nme in _COLUMN_SHARDED else W[nme][l]).astype(BF16) for nme in names]
        if after is not None:
            shards[0], _ = lax.optimization_barrier((shards[0], after))
        gathered[l, gi] = dict(zip(names, _sequencer_exchange(shards, scatter=False, name=f"gather_{l}_{gi}")))

    launch_gather(0, 0, None)

    saved, full = [], []
    for l in range(DEPTH):
        sgu_bias = jnp.repeat(sgu_b[l].T, GROUP, axis=1)
        cb = conv_b[l].reshape(2, 1, f)
        if l == 0:
            xn1 = _rms_fwd(h, norm_mix_g[l], name=f"norm_mix_{l}")
            launch_gather(0, 1, xn1)
        memn = _rms_fwd(_pin(memx, xn1) if l == 0 else memx, mem_norm_g[l], name=f"norm_mem_{l}")
        if l == 0:
            launch_gather(0, 2, memn)
        w_in_t = gathered[l, 0]["w_in"].reshape(-1, d)
        proj, cat = _proj_mixer(xn1, w_in_t, pool_w[l], pool_scale[l].reshape(1, -1), sgu_g[l].reshape(1, -1),
                                sgu_w[l], sgu_bias, name=f"proj_mixer_{l}")
        if l == 0:
            launch_gather(0, 3, cat)
        w_out_f = gathered[l, 1]["w_out"].reshape(-1, d)
        h1, xn2 = _mm_nn(cat, w_out_f, out_dtype=F32, res=h, norm_gain=norm_xattn_g[l], name=f"proj_out_{l}")
        launch_gather(l + 1, 0, h1)
        g = gathered[l, 2]
        wq_f, wk_f, wv_f, wo_f = (g[nme].reshape(-1, d) for nme in ("wq", "wk", "wv", "wo"))
        q = _mm_nn(xn2, wq_f, out_dtype=BF16, name=f"q_{l}")
        launch_gather(l + 1, 1, q)
        k = _mm_nn(memn, wk_f, out_dtype=BF16, name=f"k_{l}")
        v = _mm_nn(memn, wv_f, out_dtype=BF16, name=f"v_{l}")
        o = _attn_fwd(q, k, v, name=f"attn_{l}")
        h2, xn3 = _mm_nn(o, wo_f, out_dtype=F32, res=h1, norm_gain=norm_ffn_g[l], name=f"attn_out_{l}")
        launch_gather(l + 1, 2, h2)
        g = gathered[l, 3]
        w_up_t = g["w_up"].reshape(-1, d)
        conv_w_f = _blocks_to_cols(g["conv_w"], name=f"conv_w_cols_{l}").reshape(3, 2, f).transpose(1, 0, 2)
        w_down_f = g["w_down"].reshape(-1, d)
        hh, act = _ffn_up_gate(xn3, w_up_t, conv_w_f, cb, name=f"ffn_up_gate_{l}")
        launch_gather(l + 1, 3, hh)
        if l + 1 < DEPTH:
            h3, xn1_next = _mm_nn(act, w_down_f, out_dtype=F32, res=h2, tm=512, norm_gain=norm_mix_g[l + 1],
                                  name=f"ffn_down_{l}")
        else:
            h3, xn1_next = _mm_nn(act, w_down_f, out_dtype=F32, res=h2, tm=512, name=f"ffn_down_{l}"), None
        full.append(dict(w_in_t=w_in_t, w_out=w_out_f, wq=wq_f, wk=wk_f, wv=wv_f, wo=wo_f, w_up_t=w_up_t,
                         conv_w=conv_w_f, w_down=w_down_f))
        saved.append(dict(h0=h, xn1=xn1, proj=proj, cat=cat, h1=h1, xn2=xn2, q=q, memn=memn, k=k, v=v, o=o, h2=h2,
                          xn3=xn3, hh=hh, act=act, sgu_bias=sgu_bias, cb=cb))
        h, xn1 = h3, xn1_next

    dh, dhb, dg_final, loss_row = _loss_head(h, final_norm_g, target, name="loss_head")

    slots = {nme: [None] * DEPTH for nme in _SHARDED}
    small = [None] * DEPTH

    previous = []

    def scatter(l, tag, names, parts):
        parts = [_pin(parts[0], *previous)] + parts[1:]
        arrived = _sequencer_exchange(parts, scatter=True, name=f"scatter_{tag}_{l}")
        previous[:] = arrived[:1]
        for nme, land in zip(names, arrived):
            slots[nme][l] = land
        return parts

    for l in reversed(range(DEPTH)):
        fw, sv = full[l], saved[l]
        dact = _mm_nt(dhb, fw["w_down"], out_dtype=BF16, tm=512, name=f"d_act_{l}")
        g_w_down = _mm_tn(sv["act"], dhb, tm=f // 2, name=f"g_w_down_{l}")
        dhh, g_conv_w, g_conv_b = _convgate_bwd(sv["hh"], dact, fw["conv_w"], sv["cb"], name=f"d_convgate_{l}")
        g_w_up_t = _mm_up_grad(dhh, sv["xn3"], name=f"g_w_up_{l}")
        g_conv_w_cols = g_conv_w.transpose(1, 0, 2).reshape(3, 2 * f)
        parts = [g_w_up_t.reshape(N_DEV, -1, d),
                 _cols_to_blocks(g_conv_w_cols, name=f"g_conv_w_blocks_{l}"), g_w_down.reshape(N_DEV, -1, d)]
        parts = scatter(l, "ffn", ("w_up", "conv_w", "w_down"), parts)
        dh2, dh2b, g_norm_ffn = _mm_up_back(_pin(dhh, *parts), fw["w_up_t"], name=f"d_norm_ffn_{l}",
                                            norm_bwd=(sv["h2"], norm_ffn_g[l], dh))

        do = _mm_nt(dh2b, fw["wo"], out_dtype=BF16, name=f"d_o_{l}")
        g_wo = _mm_tn(sv["o"], dh2b, name=f"g_wo_{l}")
        dq, dk, dv = _attn_bwd(sv["q"], sv["k"], sv["v"], do, name=f"d_attn_{l}")
        dkb, dvb = dk.astype(BF16), dv.astype(BF16)
        g_wq = _mm_tn(sv["xn2"], dq, name=f"g_wq_{l}")
        g_wk = _mm_tn(sv["memn"], dkb, name=f"g_wk_{l}")
        g_wv = _mm_tn(sv["memn"], dvb, name=f"g_wv_{l}")
        parts = [g.reshape(N_DEV, -1, d) for g in (g_wq, g_wk, g_wv, g_wo)]
        parts = scatter(l, "attn", ("wq", "wk", "wv", "wo"), parts)
        dq = _pin(dq, *parts)
        dmemn = _mm_nt(dkb, fw["wk"], out_dtype=F32, name=f"d_memn_k_{l}")
        dmemn = _mm_nt(dvb, fw["wv"], out_dtype=F32, res=dmemn, name=f"d_memn_v_{l}")
        _, _, g_mem_norm = _rms_bwd(memx, dmemn, mem_norm_g[l], None, name=f"d_norm_mem_{l}")
        dh1, dh1b, g_norm_xattn = _mm_nt(dq, fw["wq"], name=f"d_norm_xattn_{l}",
                                         norm_bwd=(sv["h1"], norm_xattn_g[l], dh2))

        dcat = _mm_nt(dh1b, fw["w_out"], out_dtype=F32, name=f"d_cat_{l}")
        g_w_out = _mm_tn(sv["cat"], dh1b, name=f"g_w_out_{l}")
        dproj, g_pool_w, g_pool_scale, g_sgu_g, g_sgu_w, g_sgu_b = _mixer_bwd(
            sv["proj"], dcat, pool_w[l], pool_scale[l].reshape(1, -1), sgu_g[l].reshape(1, -1), sgu_w[l],
            sv["sgu_bias"], name=f"d_mixer_{l}")
        g_w_in_t = _mm_tn(dproj, sv["xn1"], name=f"g_w_in_{l}")
        parts = [g_w_in_t.reshape(N_DEV, -1, d), g_w_out.reshape(N_DEV, -1, d)]
        parts = scatter(l, "mix", ("w_in", "w_out"), parts)
        dh, dhb, g_norm_mix = _mm_nn(_pin(dproj, *parts), fw["w_in_t"], name=f"d_norm_mix_{l}",
                                     norm_bwd=(sv["h0"], norm_mix_g[l], dh1))

        small[l] = dict(norm_mix_g=g_norm_mix.reshape(-1), pool_w=g_pool_w, pool_scale=g_pool_scale.reshape(-1),
                        sgu_g=g_sgu_g.reshape(-1), sgu_w=g_sgu_w, sgu_b=g_sgu_b, norm_xattn_g=g_norm_xattn.reshape(-1),
                        mem_norm_g=g_mem_norm.reshape(-1), norm_ffn_g=g_norm_ffn.reshape(-1),
                        conv_b=g_conv_b.reshape(-1))
    grad_x = dh.reshape(x.shape)

    out = {}
    for nme in _SHARDED:
        view = (lambda a: jnp.swapaxes(a, 1, 2)) if nme in _COLUMN_SHARDED else (lambda a: a)
        w3 = view(W[nme])
        res = _adamw([sl.reshape((N_DEV,) + w3.shape[1:]) for sl in slots[nme]], w3, view(M[nme]), view(V[nme]),
                     name=f"adamw_{nme}")
        out[nme] = [view(r) for r in res]

    small_names = [n for n in _SMALL]
    contrib = []
    for nme in small_names:
        if nme == "final_norm_g":
            contrib.append(dg_final.reshape(-1))
        else:
            contrib.append(jnp.stack([small[l][nme] for l in range(DEPTH)]))
    tail = 8 * _PACK_LANES
    packed_g = _pack(contrib + [jnp.pad(loss_row[0, :1], (0, tail - 1))])
    (all_g,) = _sequencer_exchange([_pin(packed_g, *previous)], scatter=False, name="gather_small_grads")
    rows = packed_g.shape[0]
    loss = jnp.sum(all_g[:, rows - 8, 0])
    zeros = jnp.zeros((tail,), F32)
    res = _adamw([all_g], _pack([W[n] for n in small_names] + [zeros]).reshape(1, rows, -1),
                 _pack([M[n] for n in small_names] + [zeros]).reshape(1, rows, -1),
                 _pack([V[n] for n in small_names] + [zeros]).reshape(1, rows, -1), name="adamw_small", tr=rows // 2)
    unpacked = [_unpack(r, [W[n] for n in small_names]) for r in res]
    for i, nme in enumerate(small_names):
        out[nme] = [unpacked[j][i] for j in range(4)]

    grads = [out[n][0] for n in _WEIGHTS]
    deltas = [out[n][1] for n in _WEIGHTS]
    new_m = [out[n][2] for n in _WEIGHTS]
    new_v = [out[n][3] for n in _WEIGHTS]
    return (loss, grad_x, *grads, *deltas, *new_m, *new_v)
```
